```python
import math
import jax, jax.numpy as jnp
from jax import lax
import numpy as np

D_MODEL = 1024
BATCH = 8
SEQ = 4096
DEPTH = 1
DEC_BATCH = 128
DEC_SEQ = 8
PAST_LEN = 16384
PAGE_SIZE = 128

N_META = 16
HEAD_DIM = 64
ATTN_WIDTH = D_MODEL // 2
N_Q_HEADS = ATTN_WIDTH // HEAD_DIM
Q_PER_KV = 4
N_KV_HEADS = N_Q_HEADS // Q_PER_KV
KV_WIDTH = N_KV_HEADS * HEAD_DIM
WINDOW = 128
BLOCK = 128
SSM_WIDTH = D_MODEL - ATTN_WIDTH
SSM_GROUP = 16
N_SSM_GROUPS = SSM_WIDTH // SSM_GROUP
SSM_STATE = 64
IN_COLS = ATTN_WIDTH + 2 * KV_WIDTH + SSM_WIDTH
D_FF = -(-8 * D_MODEL // (3 * 256)) * 256
EPS = 1e-6
ATTN_SCALE = HEAD_DIM ** -0.5

kernel_name = "hymba_s5_swa_sink_decoder_step"


def rms_norm(x, g):
    xf = x.astype(jnp.float32)
    y = xf * lax.rsqrt(jnp.mean(xf * xf, axis=-1, keepdims=True) + EPS)
    return (y * g.astype(jnp.float32)).astype(x.dtype)


def mixer_front(x, g_mix, w_in, g_q, g_k):
    b, l = x.shape[:2]
    z = rms_norm(x, g_mix) @ w_in
    q, k, v, u = jnp.split(z, [ATTN_WIDTH, ATTN_WIDTH + KV_WIDTH, ATTN_WIDTH + 2 * KV_WIDTH], axis=-1)
    q = rms_norm(q.reshape(b, l, N_Q_HEADS, HEAD_DIM), g_q)
    k = rms_norm(k.reshape(b, l, N_KV_HEADS, HEAD_DIM), g_k)
    v = v.reshape(b, l, N_KV_HEADS, HEAD_DIM)
    u = u.reshape(b, l, N_SSM_GROUPS, SSM_GROUP)
    return q, k, v, u


def sink_attention(q, k, v, mask, sinks):
    lead = q.shape[:-3]
    tq = q.shape[-3]
    qg = q.reshape(lead + (tq, N_KV_HEADS, Q_PER_KV, HEAD_DIM))
    s = jnp.einsum('...qhgd,...khd->...hgqk', qg, k).astype(jnp.float32) * ATTN_SCALE
    s = jnp.where(mask, s, -jnp.inf)
    sink = sinks.astype(jnp.float32).reshape(N_KV_HEADS, Q_PER_KV, 1, 1)
    m = jnp.maximum(jnp.max(s, axis=-1, keepdims=True), sink)
    p = jnp.exp(s - m)
    p = p / (jnp.sum(p, axis=-1, keepdims=True) + jnp.exp(sink - m))
    o = jnp.einsum('...hgqk,...khd->...qhgd', p.astype(v.dtype), v)
    return o.reshape(lead + (tq, N_Q_HEADS * HEAD_DIM))


def prompt_window_attention(q, k, v, sinks):
    b, l = q.shape[:2]
    lpad = (-N_META) % BLOCK
    rpad = (-(lpad + l)) % BLOCK
    padw = ((0, 0), (lpad, rpad), (0, 0), (0, 0))
    qp, kp, vp = jnp.pad(q, padw), jnp.pad(k, padw), jnp.pad(v, padw)
    lp = lpad + l + rpad
    nb = lp // BLOCK
    qb = qp.reshape(b, nb, BLOCK, N_Q_HEADS, HEAD_DIM)
    kb = kp.reshape(b, nb, BLOCK, N_KV_HEADS, HEAD_DIM)
    vb = vp.reshape(b, nb, BLOCK, N_KV_HEADS, HEAD_DIM)

    def prev(t):
        return jnp.pad(t, ((0, 0), (1, 0), (0, 0), (0, 0), (0, 0)))[:, :-1]

    kk = jnp.concatenate([prev(kb), kb], axis=2)
    vv = jnp.concatenate([prev(vb), vb], axis=2)
    qpos = jnp.arange(nb)[:, None] * BLOCK + jnp.arange(BLOCK)[None, :]
    kpos = jnp.arange(nb)[:, None] * BLOCK - BLOCK + jnp.arange(2 * BLOCK)[None, :]
    dist = qpos[:, :, None] - kpos[:, None, :]
    valid_k = (kpos >= lpad) & (kpos < lpad + l)
    mask = (dist >= 0) & (dist <= WINDOW) & valid_k[:, None, :]
    o = sink_attention(qb, kk, vv, mask[None, :, None, None], sinks)
    return o.reshape(b, lp, ATTN_WIDTH)[:, lpad:lpad + l]


def sample_window_attention(q, k, v, cache_k, cache_v, sinks):
    t = q.shape[1]
    kk = jnp.concatenate([cache_k.astype(k.dtype), k], axis=1)
    vv = jnp.concatenate([cache_v.astype(v.dtype), v], axis=1)
    dist = (jnp.arange(t)[:, None] + WINDOW) - jnp.arange(WINDOW + t)[None, :]
    mask = (dist >= 0) & (dist <= WINDOW)
    o = sink_attention(q, kk, vv, mask[None, None, None], sinks)
    return o, kk[:, t:], vv[:, t:]


def zoh_discretize(a_re, a_im, log_dt, b_re, b_im):
    f32 = jnp.float32
    ar, ai = a_re.astype(f32), a_im.astype(f32)
    dt = jnp.exp(log_dt.astype(f32))[:, None]
    mag = jnp.exp(ar * dt)
    lr, li = mag * jnp.cos(ai * dt), mag * jnp.sin(ai * dt)
    nr, ni = lr - 1.0, li
    den = ar * ar + ai * ai
    fr, fi = (nr * ar + ni * ai) / den, (ni * ar - nr * ai) / den
    br, bi = b_re.astype(f32), b_im.astype(f32)
    bbr = fr[..., None] * br - fi[..., None] * bi
    bbi = fr[..., None] * bi + fi[..., None] * br
    return lr, li, bbr, bbi


def _ssm_combine(e1, e2):
    a1r, a1i, b1r, b1i = e1
    a2r, a2i, b2r, b2i = e2
    return (a2r * a1r - a2i * a1i, a2r * a1i + a2i * a1r,
            a2r * b1r - a2i * b1i + b2r, a2r * b1i + a2i * b1r + b2i)


def ssm_mixer(u, h0_re, h0_im, a_re, a_im, log_dt, b_re, b_im, c_re, c_im, d_skip, w_glu, b_glu):
    f32 = jnp.float32
    b, l = u.shape[:2]
    lr, li, bbr, bbi = zoh_discretize(a_re, a_im, log_dt, b_re, b_im)
    uf = u.astype(f32)
    xr = jnp.einsum('blgh,gph->blgp', uf, bbr)
    xi = jnp.einsum('blgh,gph->blgp', uf, bbi)
    h0r, h0i = h0_re.astype(f32), h0_im.astype(f32)
    xr = xr.at[:, 0].add(lr * h0r - li * h0i)
    xi = xi.at[:, 0].add(lr * h0i + li * h0r)
    ar = jnp.broadcast_to(lr, (1, l) + lr.shape)
    ai = jnp.broadcast_to(li, (1, l) + li.shape)
    _, _, hr, hi = lax.associative_scan(_ssm_combine, (ar, ai, xr, xi), axis=1)
    y = (jnp.einsum('blgp,ghp->blgh', hr, c_re.astype(f32))
         - jnp.einsum('blgp,ghp->blgh', hi, c_im.astype(f32))
         + d_skip.astype(f32) * uf)
    g = jax.nn.gelu(y)
    gate = jnp.einsum('blgh,ghk->blgk', g, w_glu.astype(f32)) + b_glu.astype(f32)
    out = (g * jax.nn.sigmoid(gate)).reshape(b, l, SSM_WIDTH).astype(u.dtype)
    return out, hr[:, -1], hi[:, -1]


def mixer_back(x, o_att, o_ssm, g_att_out, g_ssm_out, w_out, g_ffn, w_gate, w_up, w_down):
    mix = jnp.concatenate([rms_norm(o_att, g_att_out), rms_norm(o_ssm.astype(o_att.dtype), g_ssm_out)], axis=-1)
    h = x + mix @ w_out
    f = rms_norm(h, g_ffn)
    return h + (jax.nn.silu(f @ w_gate) * (f @ w_up)) @ w_down


def setup_inputs(seed: int = 0) -> dict:
    key = jax.random.key(seed)
    ks = jax.random.split(key, 32)
    f32 = jnp.float32
    G, P, H = N_SSM_GROUPS, SSM_STATE, SSM_GROUP

    def nrm(k, shape, scale=1.0):
        return scale * jax.random.normal(k, shape, f32)

    def gain(k, shape):
        return 1.0 + 0.02 * jax.random.normal(k, shape, f32)

    n_idx = jnp.arange(P, dtype=f32)
    return {
        "x_prompt": nrm(ks[0], (BATCH, SEQ, D_MODEL)),
        "x_sample": nrm(ks[1], (DEC_BATCH, DEC_SEQ, D_MODEL)),
        "cache_k_win": nrm(ks[2], (DEPTH, DEC_BATCH, WINDOW, N_KV_HEADS, HEAD_DIM)),
        "cache_v_win": nrm(ks[3], (DEPTH, DEC_BATCH, WINDOW, N_KV_HEADS, HEAD_DIM)),
        "state_ssm_re": nrm(ks[4], (DEPTH, DEC_BATCH, G, P), 0.1),
        "state_ssm_im": nrm(ks[5], (DEPTH, DEC_BATCH, G, P), 0.1),
        "meta_tokens": nrm(ks[6], (N_META, D_MODEL)),
        "g_mix": gain(ks[7], (DEPTH, D_MODEL)),
        "w_in": nrm(ks[8], (DEPTH, D_MODEL, IN_COLS), D_MODEL ** -0.5),
        "g_q": gain(ks[9], (DEPTH, HEAD_DIM)),
        "g_k": gain(ks[10], (DEPTH, HEAD_DIM)),
        "sinks": nrm(ks[11], (DEPTH, N_Q_HEADS), 0.5),
        "ssm_a_re": -0.5 + nrm(ks[12], (DEPTH, G, P), 0.01),
        "ssm_a_im": math.pi * n_idx + nrm(ks[13], (DEPTH, G, P), 0.01),
        "ssm_log_dt": jax.random.uniform(ks[14], (DEPTH, G), f32, math.log(1e-3), math.log(1e-1)),
        "ssm_b_re": nrm(ks[15], (DEPTH, G, P, H), (2 * H) ** -0.5),
        "ssm_b_im": nrm(ks[16], (DEPTH, G, P, H), (2 * H) ** -0.5),
        "ssm_c_re": nrm(ks[17], (DEPTH, G, H, P), P ** -0.5),
        "ssm_c_im": nrm(ks[18], (DEPTH, G, H, P), P ** -0.5),
        "ssm_d": nrm(ks[19], (DEPTH, G, H)),
        "ssm_w_glu": nrm(ks[20], (DEPTH, G, H, H), H ** -0.5),
        "ssm_b_glu": nrm(ks[21], (DEPTH, G, H), 0.01),
        "g_att_out": gain(ks[22], (DEPTH, ATTN_WIDTH)),
        "g_ssm_out": gain(ks[23], (DEPTH, SSM_WIDTH)),
        "w_out": nrm(ks[24], (DEPTH, D_MODEL, D_MODEL), D_MODEL ** -0.5),
        "g_ffn": gain(ks[25], (DEPTH, D_MODEL)),
        "w_gate": nrm(ks[26], (DEPTH, D_MODEL, D_FF), D_MODEL ** -0.5),
        "w_up": nrm(ks[27], (DEPTH, D_MODEL, D_FF), D_MODEL ** -0.5),
        "w_down": nrm(ks[28], (DEPTH, D_FF, D_MODEL), D_FF ** -0.5),
    }


def reference(x_prompt, x_sample, cache_k_win, cache_v_win, state_ssm_re, state_ssm_im,
              meta_tokens, g_mix, w_in, g_q, g_k, sinks,
              ssm_a_re, ssm_a_im, ssm_log_dt, ssm_b_re, ssm_b_im, ssm_c_re, ssm_c_im,
              ssm_d, ssm_w_glu, ssm_b_glu, g_att_out, g_ssm_out, w_out,
              g_ffn, w_gate, w_up, w_down):
    bp = x_prompt.shape[0]
    meta = jnp.broadcast_to(meta_tokens.astype(x_prompt.dtype), (bp, N_META, D_MODEL))
    xp = jnp.concatenate([meta, x_prompt], axis=1)
    xs = x_sample
    kp_l, vp_l, rp_l, ip_l = [], [], [], []
    ks_l, vs_l, rs_l, is_l = [], [], [], []
    for li in range(DEPTH):
        ssm_p = (ssm_a_re[li], ssm_a_im[li], ssm_log_dt[li], ssm_b_re[li], ssm_b_im[li],
                 ssm_c_re[li], ssm_c_im[li], ssm_d[li], ssm_w_glu[li], ssm_b_glu[li])
        back_p = (g_att_out[li], g_ssm_out[li], w_out[li], g_ffn[li], w_gate[li], w_up[li], w_down[li])

        q, k, v, u = mixer_front(xp, g_mix[li], w_in[li], g_q[li], g_k[li])
        o_att = prompt_window_attention(q, k, v, sinks[li])
        h0 = jnp.zeros((bp, N_SSM_GROUPS, SSM_STATE), jnp.float32)
        o_ssm, hr, hi = ssm_mixer(u, h0, h0, *ssm_p)
        xp = mixer_back(xp, o_att, o_ssm, *back_p)
        kp_l.append(k[:, -WINDOW:])
        vp_l.append(v[:, -WINDOW:])
        rp_l.append(hr)
        ip_l.append(hi)

        q, k, v, u = mixer_front(xs, g_mix[li], w_in[li], g_q[li], g_k[li])
        o_att, nk, nv = sample_window_attention(q, k, v, cache_k_win[li], cache_v_win[li], sinks[li])
        o_ssm, hr, hi = ssm_mixer(u, state_ssm_re[li], state_ssm_im[li], *ssm_p)
        xs = mixer_back(xs, o_att, o_ssm, *back_p)
        ks_l.append(nk)
        vs_l.append(nv)
        rs_l.append(hr)
        is_l.append(hi)

    y_prompt = xp[:, N_META:]
    y_sample = xs
    return (y_prompt, y_sample,
            jnp.stack(kp_l), jnp.stack(vp_l), jnp.stack(rp_l), jnp.stack(ip_l),
            jnp.stack(ks_l), jnp.stack(vs_l), jnp.stack(rs_l), jnp.stack(is_l))
```

```python
import functools
import math

import jax
import jax.numpy as jnp
from jax import lax
from jax.experimental import pallas as pl
from jax.experimental.pallas import tpu as pltpu

D_MODEL = 1024
N_META = 16
HEAD_DIM = 64
ATTN_WIDTH = 512
N_Q_HEADS = 8
Q_PER_KV = 4
N_KV_HEADS = 2
KV_WIDTH = 128
WINDOW = 128
BLOCK = 128
SSM_WIDTH = 512
SSM_GROUP = 16
N_SSM_GROUPS = 32
SSM_STATE = 64
IN_COLS = ATTN_WIDTH + 2 * KV_WIDTH + SSM_WIDTH
D_FF = 2816
EPS = 1e-6
ATTN_SCALE = HEAD_DIM ** -0.5
STATE_COLS = N_SSM_GROUPS * SSM_STATE
HALF_STATE = STATE_COLS // 2
LANES = 128
SUBLANES = 8
VMEM_LIMIT = 56 * 1024 * 1024

F32 = jnp.float32
BF16 = jnp.bfloat16


def _const_spec(shape):
    return pl.BlockSpec(shape, lambda *_: (0,) * len(shape), pipeline_mode=pl.Buffered(1))


def _rms(x, g):
    return x * lax.rsqrt(jnp.mean(x * x, axis=-1, keepdims=True) + EPS) * g


def _front_kernel(x_ref, gmix_ref, win_ref, gq_ref, gk_ref, q_ref, k_ref, v_ref, u_ref):
    xn = _rms(x_ref[...], gmix_ref[...]).astype(BF16)
    z = jnp.dot(xn, win_ref[...], preferred_element_type=F32)
    left = lax.broadcasted_iota(jnp.int32, (1, LANES), 1) < HEAD_DIM

    def pair_norm(zz, g2):
        sq = zz * zz
        sl = jnp.sum(jnp.where(left, sq, 0.0), axis=-1, keepdims=True)
        sr = jnp.sum(jnp.where(left, 0.0, sq), axis=-1, keepdims=True)
        inv = jnp.where(left, lax.rsqrt(sl / HEAD_DIM + EPS), lax.rsqrt(sr / HEAD_DIM + EPS))
        return zz * inv * g2

    for p in range(ATTN_WIDTH // LANES):
        q_ref[:, p * LANES:(p + 1) * LANES] = pair_norm(z[:, p * LANES:(p + 1) * LANES], gq_ref[...])
    k_ref[...] = pair_norm(z[:, ATTN_WIDTH:ATTN_WIDTH + KV_WIDTH], gk_ref[...])
    v_ref[...] = z[:, ATTN_WIDTH + KV_WIDTH:ATTN_WIDTH + 2 * KV_WIDTH]
    u_ref[...] = z[:, ATTN_WIDTH + 2 * KV_WIDTH:]


def _front(x_rows, gmix, win_bf, gq2, gk2, tm):
    n = x_rows.shape[0]
    assert n % tm == 0
    row = lambda w: pl.BlockSpec((tm, w), lambda i: (i, 0))
    return pl.pallas_call(
        _front_kernel,
        grid=(n // tm,),
        in_specs=[row(D_MODEL), _const_spec((1, D_MODEL)), _const_spec((D_MODEL, IN_COLS)),
                  _const_spec((1, LANES)), _const_spec((1, LANES))],
        out_specs=[row(ATTN_WIDTH), row(KV_WIDTH), row(KV_WIDTH), row(SSM_WIDTH)],
        out_shape=[jax.ShapeDtypeStruct((n, ATTN_WIDTH), F32), jax.ShapeDtypeStruct((n, KV_WIDTH), F32),
                   jax.ShapeDtypeStruct((n, KV_WIDTH), F32), jax.ShapeDtypeStruct((n, SSM_WIDTH), F32)],
        compiler_params=pltpu.CompilerParams(dimension_semantics=("arbitrary",), vmem_limit_bytes=VMEM_LIMIT),
        name="front",
    )(x_rows, gmix, win_bf, gq2, gk2)


def _sink_softmax_pv(s, mask, sink, vg):
    s = jnp.where(mask, s * ATTN_SCALE, -jnp.inf)
    m = jnp.maximum(jnp.max(s, axis=-1, keepdims=True), sink)
    p = jnp.exp(s - m)
    denom = jnp.sum(p, axis=-1, keepdims=True) + jnp.exp(sink - m)
    o = jnp.dot(p.astype(BF16), vg, preferred_element_type=F32)
    return o / denom


def _attn_prompt_kernel(sink_ref, q_ref, kp_ref, kc_ref, vp_ref, vc_ref, o_ref):
    j = pl.program_id(1)
    q = q_ref[0]
    kk = jnp.concatenate([kp_ref[0], kc_ref[0]], axis=0).astype(BF16)
    vv = jnp.concatenate([vp_ref[0], vc_ref[0]], axis=0).astype(BF16)
    r = lax.broadcasted_iota(jnp.int32, (BLOCK, 2 * BLOCK), 0)
    c = lax.broadcasted_iota(jnp.int32, (BLOCK, 2 * BLOCK), 1)
    mask = (c >= r) & (c <= r + WINDOW) & ((j > 0) | (c >= BLOCK - N_META))
    for h in range(N_Q_HEADS):
        g = h // Q_PER_KV
        qh = q[:, h * HEAD_DIM:(h + 1) * HEAD_DIM].astype(BF16)
        kg = kk[:, g * HEAD_DIM:(g + 1) * HEAD_DIM]
        vg = vv[:, g * HEAD_DIM:(g + 1) * HEAD_DIM]
        s = lax.dot_general(qh, kg, (((1,), (1,)), ((), ())), preferred_element_type=F32)
        o_ref[0, :, h * HEAD_DIM:(h + 1) * HEAD_DIM] = _sink_softmax_pv(s, mask, sink_ref[h], vg)


def _attn_prompt(q, k_ext, v_ext, sinks):
    b, l, _ = q.shape
    nb = l // BLOCK
    kv = lambda off: pl.BlockSpec((1, BLOCK, KV_WIDTH), lambda bi, j: (bi, j + off, 0))
    return pl.pallas_call(
        _attn_prompt_kernel,
        grid=(b, nb),
        in_specs=[pl.BlockSpec(memory_space=pltpu.SMEM),
                  pl.BlockSpec((1, BLOCK, ATTN_WIDTH), lambda bi, j: (bi, j, 0)),
                  kv(0), kv(1), kv(0), kv(1)],
        out_specs=pl.BlockSpec((1, BLOCK, ATTN_WIDTH), lambda bi, j: (bi, j, 0)),
        out_shape=jax.ShapeDtypeStruct((b, l, ATTN_WIDTH), F32),
        compiler_params=pltpu.CompilerParams(dimension_semantics=("arbitrary", "arbitrary")),
        name="attn_prompt",
    )(sinks, q, k_ext, k_ext, v_ext, v_ext)


def _attn_sample_kernel(sink_ref, q_ref, kn_ref, vn_ref, ck_ref, cv_ref, o_ref, kw_ref, vw_ref, *, bb, t):
    rows = Q_PER_KV * t
    tk = WINDOW + t
    r = lax.broadcasted_iota(jnp.int32, (rows, tk), 0) % t
    c = lax.broadcasted_iota(jnp.int32, (rows, tk), 1)
    mask = (c >= r) & (c <= r + WINDOW)
    hrow = lax.broadcasted_iota(jnp.int32, (rows, 1), 0) // t
    for bi in range(bb):
        kk = jnp.concatenate([ck_ref[bi], kn_ref[bi]], axis=0)
        vv = jnp.concatenate([cv_ref[bi], vn_ref[bi]], axis=0)
        kw_ref[bi] = kk[t:]
        vw_ref[bi] = vv[t:]
        kkb = kk.astype(BF16)
        vvb = vv.astype(BF16)
        q = q_ref[bi]
        for g in range(N_KV_HEADS):
            heads = range(g * Q_PER_KV, (g + 1) * Q_PER_KV)
            q4 = jnp.concatenate([q[:, h * HEAD_DIM:(h + 1) * HEAD_DIM] for h in heads], axis=0).astype(BF16)
            sink = jnp.zeros((rows, 1), F32)
            for i, h in enumerate(heads):
                sink = jnp.where(hrow == i, sink_ref[h], sink)
            kg = kkb[:, g * HEAD_DIM:(g + 1) * HEAD_DIM]
            vg = vvb[:, g * HEAD_DIM:(g + 1) * HEAD_DIM]
            s = lax.dot_general(q4, kg, (((1,), (1,)), ((), ())), preferred_element_type=F32)
            o4 = _sink_softmax_pv(s, mask, sink, vg)
            for i, h in enumerate(heads):
                o_ref[bi, :, h * HEAD_DIM:(h + 1) * HEAD_DIM] = o4[i * t:(i + 1) * t]


def _attn_sample(q, kn, vn, ck, cv, sinks, bb=8):
    db, t, _ = q.shape
    blk = lambda r, w: pl.BlockSpec((bb, r, w), lambda i: (i, 0, 0))
    return pl.pallas_call(
        functools.partial(_attn_sample_kernel, bb=bb, t=t),
        grid=(db // bb,),
        in_specs=[pl.BlockSpec(memory_space=pltpu.SMEM), blk(t, ATTN_WIDTH), blk(t, KV_WIDTH), blk(t, KV_WIDTH),
                  blk(WINDOW, KV_WIDTH), blk(WINDOW, KV_WIDTH)],
        out_specs=[blk(t, ATTN_WIDTH), blk(WINDOW, KV_WIDTH), blk(WINDOW, KV_WIDTH)],
        out_shape=[jax.ShapeDtypeStruct((db, t, ATTN_WIDTH), F32),
                   jax.ShapeDtypeStruct((db, WINDOW, KV_WIDTH), F32),
                   jax.ShapeDtypeStruct((db, WINDOW, KV_WIDTH), F32)],
        compiler_params=pltpu.CompilerParams(dimension_semantics=("arbitrary",)),
        name="attn_sample",
    )(sinks, q, kn, vn, ck, cv)


SCAN_COLS = 512


def _ssm_kernel(u_ref, h0r_ref, h0i_ref, wlo_ref, whi_ref, lr_ref, li_ref, clo_ref, chi_ref, d_ref, wglu_ref,
                bglu_ref, o_ref, hr_ref, hi_ref, xs_ref, st_ref, *, tt, nb):
    rows = tt * nb

    @pl.when(pl.program_id(0) == 0)
    def _():
        for half in range(2):
            st_ref[:, (2 * half) * HALF_STATE:(2 * half + 1) * HALF_STATE] = h0r_ref[:, half * HALF_STATE:(half + 1) * HALF_STATE]
            st_ref[:, (2 * half + 1) * HALF_STATE:(2 * half + 2) * HALF_STATE] = h0i_ref[:, half * HALF_STATE:(half + 1) * HALF_STATE]

    u = u_ref[...].reshape(rows, SSM_WIDTH)
    ub = u.astype(BF16)
    half_in = SSM_WIDTH // 2
    xs_ref[:, 0:2 * HALF_STATE] = jnp.dot(ub[:, :half_in], wlo_ref[...], preferred_element_type=F32)
    xs_ref[:, 2 * HALF_STATE:] = jnp.dot(ub[:, half_in:], whi_ref[...], preferred_element_type=F32)

    for rg in range(nb // SUBLANES):
        for half in range(2):
            for cc in range(0, HALF_STATE, SCAN_COLS):
                re0 = 2 * half * HALF_STATE + cc
                im0 = re0 + HALF_STATE
                lam0 = half * HALF_STATE + cc
                lr = jnp.broadcast_to(lr_ref[:, lam0:lam0 + SCAN_COLS], (SUBLANES, SCAN_COLS))
                li = jnp.broadcast_to(li_ref[:, lam0:lam0 + SCAN_COLS], (SUBLANES, SCAN_COLS))
                r0 = rg * SUBLANES

                def body(t, carry, re0=re0, im0=im0, lr=lr, li=li, r0=r0):
                    hr, hi = carry
                    row = pl.multiple_of(t * nb + r0, SUBLANES)
                    xr = xs_ref[pl.ds(row, SUBLANES), re0:re0 + SCAN_COLS]
                    xi = xs_ref[pl.ds(row, SUBLANES), im0:im0 + SCAN_COLS]
                    nhr = lr * hr - li * hi + xr
                    nhi = lr * hi + li * hr + xi
                    xs_ref[pl.ds(row, SUBLANES), re0:re0 + SCAN_COLS] = nhr
                    xs_ref[pl.ds(row, SUBLANES), im0:im0 + SCAN_COLS] = nhi
                    return nhr, nhi

                hr, hi = lax.fori_loop(
                    0, tt, body,
                    (st_ref[r0:r0 + SUBLANES, re0:re0 + SCAN_COLS], st_ref[r0:r0 + SUBLANES, im0:im0 + SCAN_COLS]),
                    unroll=min(tt, 8))
                st_ref[r0:r0 + SUBLANES, re0:re0 + SCAN_COLS] = hr
                st_ref[r0:r0 + SUBLANES, im0:im0 + SCAN_COLS] = hi

    y_lo = jnp.dot(xs_ref[:, 0:2 * HALF_STATE].astype(BF16), clo_ref[...], preferred_element_type=F32)
    y_hi = jnp.dot(xs_ref[:, 2 * HALF_STATE:].astype(BF16), chi_ref[...], preferred_element_type=F32)
    y = jnp.concatenate([y_lo, y_hi], axis=1) + d_ref[...] * u
    g = jax.nn.gelu(y)
    gate = jnp.dot(g.astype(BF16), wglu_ref[...], preferred_element_type=F32) + bglu_ref[...]
    o_ref[...] = (g * jax.nn.sigmoid(gate)).reshape(tt, nb, SSM_WIDTH)
    hr_ref[...] = jnp.concatenate([st_ref[:, 0:HALF_STATE], st_ref[:, 2 * HALF_STATE:3 * HALF_STATE]], axis=1)
    hi_ref[...] = jnp.concatenate([st_ref[:, HALF_STATE:2 * HALF_STATE], st_ref[:, 3 * HALF_STATE:]], axis=1)


def _ssm(u_tb, h0r, h0i, sp, tt):
    t_total, nb, _ = u_tb.shape
    assert t_total % tt == 0 and nb % SUBLANES == 0
    rows = tt * nb
    st = pl.BlockSpec((nb, STATE_COLS), lambda i: (0, 0))
    return pl.pallas_call(
        functools.partial(_ssm_kernel, tt=tt, nb=nb),
        grid=(t_total // tt,),
        in_specs=[pl.BlockSpec((tt, nb, SSM_WIDTH), lambda i: (i, 0, 0)), st, st,
                  _const_spec((SSM_WIDTH // 2, 2 * HALF_STATE)), _const_spec((SSM_WIDTH // 2, 2 * HALF_STATE)),
                  _const_spec((1, STATE_COLS)), _const_spec((1, STATE_COLS)),
                  _const_spec((2 * HALF_STATE, SSM_WIDTH // 2)), _const_spec((2 * HALF_STATE, SSM_WIDTH // 2)),
                  _const_spec((1, SSM_WIDTH)), _const_spec((SSM_WIDTH, SSM_WIDTH)), _const_spec((1, SSM_WIDTH))],
        out_specs=[pl.BlockSpec((tt, nb, SSM_WIDTH), lambda i: (i, 0, 0)), st, st],
        out_shape=[jax.ShapeDtypeStruct((t_total, nb, SSM_WIDTH), F32),
                   jax.ShapeDtypeStruct((nb, STATE_COLS), F32), jax.ShapeDtypeStruct((nb, STATE_COLS), F32)],
        scratch_shapes=[pltpu.VMEM((rows, 2 * STATE_COLS), F32), pltpu.VMEM((nb, 2 * STATE_COLS), F32)],
        compiler_params=pltpu.CompilerParams(dimension_semantics=("arbitrary",), vmem_limit_bytes=VMEM_LIMIT),
        name="ssm",
    )(u_tb, h0r, h0i, sp["wlo"], sp["whi"], sp["lr"], sp["li"], sp["clo"], sp["chi"], sp["d"], sp["wglu"], sp["bglu"])


def _block_diag(blocks):
    n, a, b = blocks.shape
    eye = jnp.eye(n, dtype=blocks.dtype)
    return jnp.einsum('nab,nm->namb', blocks, eye).reshape(n * a, n * b)


def _ssm_params(a_re, a_im, log_dt, b_re, b_im, c_re, c_im, d_skip, w_glu, b_glu):
    dt = jnp.exp(log_dt)[:, None]
    mag = jnp.exp(a_re * dt)
    lr, li = mag * jnp.cos(a_im * dt), mag * jnp.sin(a_im * dt)
    nr, ni = lr - 1.0, li
    den = a_re * a_re + a_im * a_im
    fr, fi = (nr * a_re + ni * a_im) / den, (ni * a_re - nr * a_im) / den
    bbr = fr[..., None] * b_re - fi[..., None] * b_im
    bbi = fr[..., None] * b_im + fi[..., None] * b_re
    hg = N_SSM_GROUPS // 2

    def win(lo):
        sl = slice(lo, lo + hg)
        return jnp.concatenate([_block_diag(jnp.swapaxes(bbr[sl], 1, 2)), _block_diag(jnp.swapaxes(bbi[sl], 1, 2))],
                               axis=1).astype(BF16)

    def cmat(lo):
        sl = slice(lo, lo + hg)
        return jnp.concatenate([_block_diag(jnp.swapaxes(c_re[sl], 1, 2)), -_block_diag(jnp.swapaxes(c_im[sl], 1, 2))],
                               axis=0).astype(BF16)

    return dict(wlo=win(0), whi=win(hg), lr=lr.reshape(1, STATE_COLS), li=li.reshape(1, STATE_COLS),
                clo=cmat(0), chi=cmat(hg), d=d_skip.reshape(1, SSM_WIDTH), wglu=_block_diag(w_glu).astype(BF16),
                bglu=b_glu.reshape(1, SSM_WIDTH))


def _back_kernel(x_ref, oa_ref, os_ref, ga_ref, gs_ref, wout_ref, gf_ref, wg_ref, wu_ref, wd_ref, y_ref):
    mix = jnp.concatenate([_rms(oa_ref[...], ga_ref[...]), _rms(os_ref[...], gs_ref[...])], axis=-1).astype(BF16)
    h = x_ref[...] + jnp.dot(mix, wout_ref[...], preferred_element_type=F32)
    f = _rms(h, gf_ref[...]).astype(BF16)
    gate = jnp.dot(f, wg_ref[...], preferred_element_type=F32)
    up = jnp.dot(f, wu_ref[...], preferred_element_type=F32)
    a = (jax.nn.silu(gate) * up).astype(BF16)
    y_ref[...] = h + jnp.dot(a, wd_ref[...], preferred_element_type=F32)


def _back(x_rows, oa, os_, ga, gs, wout, gf, wg, wu, wd, tm):
    n = x_rows.shape[0]
    assert n % tm == 0
    row = lambda w: pl.BlockSpec((tm, w), lambda i: (i, 0))
    return pl.pallas_call(
        _back_kernel,
        grid=(n // tm,),
        in_specs=[row(D_MODEL), row(ATTN_WIDTH), row(SSM_WIDTH), _const_spec((1, ATTN_WIDTH)),
                  _const_spec((1, SSM_WIDTH)), _const_spec((D_MODEL, D_MODEL)), _const_spec((1, D_MODEL)),
                  _const_spec((D_MODEL, D_FF)), _const_spec((D_MODEL, D_FF)), _const_spec((D_FF, D_MODEL))],
        out_specs=row(D_MODEL),
        out_shape=jax.ShapeDtypeStruct((n, D_MODEL), F32),
        compiler_params=pltpu.CompilerParams(dimension_semantics=("arbitrary",), vmem_limit_bytes=VMEM_LIMIT),
        name="back",
    )(x_rows, oa, os_, ga, gs, wout, gf, wg, wu, wd)


def kernel(x_prompt, x_sample, cache_k_win, cache_v_win, state_ssm_re, state_ssm_im, meta_tokens, g_mix, w_in, g_q,
           g_k, sinks, ssm_a_re, ssm_a_im, ssm_log_dt, ssm_b_re, ssm_b_im, ssm_c_re, ssm_c_im, ssm_d, ssm_w_glu,
           ssm_b_glu, g_att_out, g_ssm_out, w_out, g_ffn, w_gate, w_up, w_down):
    bp, seq, _ = x_prompt.shape
    db, dseq, _ = x_sample.shape
    li = 0
    gmix = g_mix[li].reshape(1, D_MODEL)
    win = w_in[li].astype(BF16)
    gq2 = jnp.tile(g_q[li], 2).reshape(1, LANES)
    gk2 = jnp.tile(g_k[li], 2).reshape(1, LANES)
    sk = sinks[li]
    sp = _ssm_params(ssm_a_re[li], ssm_a_im[li], ssm_log_dt[li], ssm_b_re[li], ssm_b_im[li], ssm_c_re[li],
                     ssm_c_im[li], ssm_d[li], ssm_w_glu[li], ssm_b_glu[li])
    back_w = (g_att_out[li].reshape(1, ATTN_WIDTH), g_ssm_out[li].reshape(1, SSM_WIDTH), w_out[li].astype(BF16),
              g_ffn[li].reshape(1, D_MODEL), w_gate[li].astype(BF16), w_up[li].astype(BF16), w_down[li].astype(BF16))

    xs_rows = x_sample.reshape(db * dseq, D_MODEL)
    meta_rows = jnp.tile(meta_tokens, (SUBLANES, 1))
    small = jnp.concatenate([xs_rows, meta_rows], axis=0)
    q_s, k_s, v_s, u_s = _front(small, gmix, win, gq2, gk2, tm=128)
    n_s = db * dseq
    k_meta, v_meta = k_s[n_s:n_s + N_META], v_s[n_s:n_s + N_META]
    u_meta = u_s[n_s:].reshape(SUBLANES, N_META, SSM_WIDTH).transpose(1, 0, 2)
    zero_state = jnp.zeros((SUBLANES, STATE_COLS), F32)
    _, hm_r, hm_i = _ssm(u_meta, zero_state, zero_state, sp, tt=N_META)

    xp_rows = x_prompt.reshape(bp * seq, D_MODEL)
    q_p, k_p, v_p, u_p = _front(xp_rows, gmix, win, gq2, gk2, tm=512)
    k_p3, v_p3 = k_p.reshape(bp, seq, KV_WIDTH), v_p.reshape(bp, seq, KV_WIDTH)

    def ext(new, meta):
        lead = jnp.concatenate([jnp.zeros((BLOCK - N_META, KV_WIDTH), F32), meta], axis=0)
        return jnp.concatenate([jnp.broadcast_to(lead, (bp, BLOCK, KV_WIDTH)), new], axis=1)

    oa_p = _attn_prompt(q_p.reshape(bp, seq, ATTN_WIDTH), ext(k_p3, k_meta), ext(v_p3, v_meta), sk)
    u_tb = u_p.reshape(bp, seq, SSM_WIDTH).transpose(1, 0, 2)
    os_tb, hp_r, hp_i = _ssm(u_tb, hm_r, hm_i, sp, tt=BLOCK)
    os_p = os_tb.transpose(1, 0, 2).reshape(bp * seq, SSM_WIDTH)
    y_prompt = _back(xp_rows, oa_p.reshape(bp * seq, ATTN_WIDTH), os_p, *back_w, tm=512).reshape(bp, seq, D_MODEL)

    ck = cache_k_win[li].reshape(db, WINDOW, KV_WIDTH)
    cv = cache_v_win[li].reshape(db, WINDOW, KV_WIDTH)
    oa_s, kw_s, vw_s = _attn_sample(q_s[:n_s].reshape(db, dseq, ATTN_WIDTH), k_s[:n_s].reshape(db, dseq, KV_WIDTH),
                                    v_s[:n_s].reshape(db, dseq, KV_WIDTH), ck, cv, sk)
    us_tb = u_s[:n_s].reshape(db, dseq, SSM_WIDTH).transpose(1, 0, 2)
    os_s_tb, hs_r, hs_i = _ssm(us_tb, state_ssm_re[li].reshape(db, STATE_COLS),
                               state_ssm_im[li].reshape(db, STATE_COLS), sp, tt=dseq)
    os_s = os_s_tb.transpose(1, 0, 2).reshape(n_s, SSM_WIDTH)
    y_sample = _back(xs_rows, oa_s.reshape(n_s, ATTN_WIDTH), os_s, *back_w, tm=512).reshape(db, dseq, D_MODEL)

    kv5 = lambda a, n: a.reshape(1, n, WINDOW, N_KV_HEADS, HEAD_DIM)
    st4 = lambda a, n: a.reshape(1, n, N_SSM_GROUPS, SSM_STATE)
    return (y_prompt, y_sample,
            kv5(k_p3[:, seq - WINDOW:], bp), kv5(v_p3[:, seq - WINDOW:], bp), st4(hp_r, bp), st4(hp_i, bp),
            kv5(kw_s, db), kv5(vw_s, db), st4(hs_r, db), st4(hs_i, db))
```

```python
import functools

import jax
import jax.numpy as jnp
from jax import lax
from jax.experimental import pallas as pl
from jax.experimental.pallas import tpu as pltpu

D_MODEL = 1024
N_META = 16
HEAD_DIM = 64
ATTN_WIDTH = 512
N_Q_HEADS = 8
Q_PER_KV = 4
N_KV_HEADS = 2
KV_WIDTH = 128
WINDOW = 128
BLOCK = 128
SSM_WIDTH = 512
SSM_GROUP = 16
N_SSM_GROUPS = 32
SSM_STATE = 64
IN_COLS = ATTN_WIDTH + 2 * KV_WIDTH + SSM_WIDTH
D_FF = 2816
EPS = 1e-6
ATTN_SCALE = HEAD_DIM ** -0.5
STATE_COLS = N_SSM_GROUPS * SSM_STATE
HALF_STATE = STATE_COLS // 2
LANES = 128
SUBLANES = 8
VMEM_LIMIT = 56 * 1024 * 1024

F32 = jnp.float32
BF16 = jnp.bfloat16


def _const_spec(shape):
    return pl.BlockSpec(shape, lambda *_: (0,) * len(shape), pipeline_mode=pl.Buffered(1))


def _rms(x, g):
    return x * lax.rsqrt(jnp.mean(x * x, axis=-1, keepdims=True) + EPS) * g


def _left_half():
    return lax.broadcasted_iota(jnp.int32, (1, LANES), 1) < HEAD_DIM


def _pair_norm(zz, g2, left):
    sq = zz * zz
    sl = jnp.sum(jnp.where(left, sq, 0.0), axis=-1, keepdims=True)
    sr = jnp.sum(jnp.where(left, 0.0, sq), axis=-1, keepdims=True)
    inv = jnp.where(left, lax.rsqrt(sl / HEAD_DIM + EPS), lax.rsqrt(sr / HEAD_DIM + EPS))
    return zz * inv * g2


def _front_kernel(x_ref, gmix_ref, win_ref, gq_ref, gk_ref, q_ref, k_ref, v_ref, u_ref):
    xn = _rms(x_ref[...], gmix_ref[...]).astype(BF16)
    z = jnp.dot(xn, win_ref[...], preferred_element_type=F32)
    left = _left_half()
    for p in range(ATTN_WIDTH // LANES):
        q_ref[:, p * LANES:(p + 1) * LANES] = _pair_norm(z[:, p * LANES:(p + 1) * LANES], gq_ref[...], left)
    k_ref[...] = _pair_norm(z[:, ATTN_WIDTH:ATTN_WIDTH + KV_WIDTH], gk_ref[...], left)
    v_ref[...] = z[:, ATTN_WIDTH + KV_WIDTH:ATTN_WIDTH + 2 * KV_WIDTH]
    u_ref[...] = z[:, ATTN_WIDTH + 2 * KV_WIDTH:]


def _front(x_rows, gmix, win_bf, gq2, gk2, tm):
    n = x_rows.shape[0]
    assert n % tm == 0
    row = lambda w: pl.BlockSpec((tm, w), lambda i: (i, 0))
    return pl.pallas_call(
        _front_kernel,
        grid=(n // tm,),
        in_specs=[row(D_MODEL), _const_spec((1, D_MODEL)), _const_spec((D_MODEL, IN_COLS)),
                  _const_spec((1, LANES)), _const_spec((1, LANES))],
        out_specs=[row(ATTN_WIDTH), row(KV_WIDTH), row(KV_WIDTH), row(SSM_WIDTH)],
        out_shape=[jax.ShapeDtypeStruct((n, ATTN_WIDTH), F32), jax.ShapeDtypeStruct((n, KV_WIDTH), F32),
                   jax.ShapeDtypeStruct((n, KV_WIDTH), F32), jax.ShapeDtypeStruct((n, SSM_WIDTH), F32)],
        compiler_params=pltpu.CompilerParams(dimension_semantics=("arbitrary",), vmem_limit_bytes=VMEM_LIMIT),
        name="front",
    )(x_rows, gmix, win_bf, gq2, gk2)


def _sink_softmax_pv(s, mask, sink, vg):
    s = jnp.where(mask, s * ATTN_SCALE, -jnp.inf)
    m = jnp.maximum(jnp.max(s, axis=-1, keepdims=True), sink)
    p = jnp.exp(s - m)
    denom = jnp.sum(p, axis=-1, keepdims=True) + jnp.exp(sink - m)
    o = jnp.dot(p.astype(BF16), vg, preferred_element_type=F32)
    return o / denom


def _attn_sample_kernel(sink_ref, q_ref, kn_ref, vn_ref, ck_ref, cv_ref, o_ref, kw_ref, vw_ref, *, bb, t):
    rows = Q_PER_KV * t
    tk = WINDOW + t
    r = lax.broadcasted_iota(jnp.int32, (rows, tk), 0) % t
    c = lax.broadcasted_iota(jnp.int32, (rows, tk), 1)
    mask = (c >= r) & (c <= r + WINDOW)
    hrow = lax.broadcasted_iota(jnp.int32, (rows, 1), 0) // t
    for bi in range(bb):
        kk = jnp.concatenate([ck_ref[bi], kn_ref[bi]], axis=0)
        vv = jnp.concatenate([cv_ref[bi], vn_ref[bi]], axis=0)
        kw_ref[bi] = kk[t:]
        vw_ref[bi] = vv[t:]
        kkb = kk.astype(BF16)
        vvb = vv.astype(BF16)
        q = q_ref[bi]
        for g in range(N_KV_HEADS):
            heads = range(g * Q_PER_KV, (g + 1) * Q_PER_KV)
            q4 = jnp.concatenate([q[:, h * HEAD_DIM:(h + 1) * HEAD_DIM] for h in heads], axis=0).astype(BF16)
            sink = jnp.zeros((rows, 1), F32)
            for i, h in enumerate(heads):
                sink = jnp.where(hrow == i, sink_ref[h], sink)
            kg = kkb[:, g * HEAD_DIM:(g + 1) * HEAD_DIM]
            vg = vvb[:, g * HEAD_DIM:(g + 1) * HEAD_DIM]
            s = lax.dot_general(q4, kg, (((1,), (1,)), ((), ())), preferred_element_type=F32)
            o4 = _sink_softmax_pv(s, mask, sink, vg)
            for i, h in enumerate(heads):
                o_ref[bi, :, h * HEAD_DIM:(h + 1) * HEAD_DIM] = o4[i * t:(i + 1) * t]


def _attn_sample(q, kn, vn, ck, cv, sinks, bb=8):
    db, t, _ = q.shape
    blk = lambda r, w: pl.BlockSpec((bb, r, w), lambda i: (i, 0, 0))
    return pl.pallas_call(
        functools.partial(_attn_sample_kernel, bb=bb, t=t),
        grid=(db // bb,),
        in_specs=[pl.BlockSpec(memory_space=pltpu.SMEM), blk(t, ATTN_WIDTH), blk(t, KV_WIDTH), blk(t, KV_WIDTH),
                  blk(WINDOW, KV_WIDTH), blk(WINDOW, KV_WIDTH)],
        out_specs=[blk(t, ATTN_WIDTH), blk(WINDOW, KV_WIDTH), blk(WINDOW, KV_WIDTH)],
        out_shape=[jax.ShapeDtypeStruct((db, t, ATTN_WIDTH), F32),
                   jax.ShapeDtypeStruct((db, WINDOW, KV_WIDTH), F32),
                   jax.ShapeDtypeStruct((db, WINDOW, KV_WIDTH), F32)],
        compiler_params=pltpu.CompilerParams(dimension_semantics=("arbitrary",)),
        name="attn_sample",
    )(sinks, q, kn, vn, ck, cv)


SCAN_COLS = 512


def _scan_rows(xs_ref, st_ref, lr, li, *, tt, nb, r0, re0, im0, st_re0, st_im0):
    def body(t, carry):
        hr, hi = carry
        row = pl.multiple_of(t * nb + r0, SUBLANES)
        xr = xs_ref[pl.ds(row, SUBLANES), re0:re0 + SCAN_COLS]
        xi = xs_ref[pl.ds(row, SUBLANES), im0:im0 + SCAN_COLS]
        nhr = lr * hr - li * hi + xr
        nhi = lr * hi + li * hr + xi
        xs_ref[pl.ds(row, SUBLANES), re0:re0 + SCAN_COLS] = nhr
        xs_ref[pl.ds(row, SUBLANES), im0:im0 + SCAN_COLS] = nhi
        return nhr, nhi

    hr, hi = lax.fori_loop(
        0, tt, body,
        (st_ref[r0:r0 + SUBLANES, st_re0:st_re0 + SCAN_COLS], st_ref[r0:r0 + SUBLANES, st_im0:st_im0 + SCAN_COLS]),
        unroll=min(tt, 8))
    st_ref[r0:r0 + SUBLANES, st_re0:st_re0 + SCAN_COLS] = hr
    st_ref[r0:r0 + SUBLANES, st_im0:st_im0 + SCAN_COLS] = hi


def _ssm_kernel(u_ref, h0r_ref, h0i_ref, wlo_ref, whi_ref, lr_ref, li_ref, clo_ref, chi_ref, d_ref, wglu_ref,
                bglu_ref, o_ref, hr_ref, hi_ref, xs_ref, st_ref, *, tt, nb):
    rows = tt * nb

    @pl.when(pl.program_id(0) == 0)
    def _():
        for half in range(2):
            st_ref[:, (2 * half) * HALF_STATE:(2 * half + 1) * HALF_STATE] = h0r_ref[:, half * HALF_STATE:(half + 1) * HALF_STATE]
            st_ref[:, (2 * half + 1) * HALF_STATE:(2 * half + 2) * HALF_STATE] = h0i_ref[:, half * HALF_STATE:(half + 1) * HALF_STATE]

    u = u_ref[...].reshape(rows, SSM_WIDTH)
    ub = u.astype(BF16)
    half_in = SSM_WIDTH // 2
    xs_ref[:, 0:2 * HALF_STATE] = jnp.dot(ub[:, :half_in], wlo_ref[...], preferred_element_type=F32)
    xs_ref[:, 2 * HALF_STATE:] = jnp.dot(ub[:, half_in:], whi_ref[...], preferred_element_type=F32)

    for rg in range(nb // SUBLANES):
        for half in range(2):
            for cc in range(0, HALF_STATE, SCAN_COLS):
                re0 = 2 * half * HALF_STATE + cc
                lam0 = half * HALF_STATE + cc
                lr = jnp.broadcast_to(lr_ref[:, lam0:lam0 + SCAN_COLS], (SUBLANES, SCAN_COLS))
                li = jnp.broadcast_to(li_ref[:, lam0:lam0 + SCAN_COLS], (SUBLANES, SCAN_COLS))
                _scan_rows(xs_ref, st_ref, lr, li, tt=tt, nb=nb, r0=rg * SUBLANES, re0=re0, im0=re0 + HALF_STATE,
                           st_re0=re0, st_im0=re0 + HALF_STATE)

    y_lo = jnp.dot(xs_ref[:, 0:2 * HALF_STATE].astype(BF16), clo_ref[...], preferred_element_type=F32)
    y_hi = jnp.dot(xs_ref[:, 2 * HALF_STATE:].astype(BF16), chi_ref[...], preferred_element_type=F32)
    y = jnp.concatenate([y_lo, y_hi], axis=1) + d_ref[...] * u
    g = jax.nn.gelu(y)
    gate = jnp.dot(g.astype(BF16), wglu_ref[...], preferred_element_type=F32) + bglu_ref[...]
    o_ref[...] = (g * jax.nn.sigmoid(gate)).reshape(tt, nb, SSM_WIDTH)
    hr_ref[...] = jnp.concatenate([st_ref[:, 0:HALF_STATE], st_ref[:, 2 * HALF_STATE:3 * HALF_STATE]], axis=1)
    hi_ref[...] = jnp.concatenate([st_ref[:, HALF_STATE:2 * HALF_STATE], st_ref[:, 3 * HALF_STATE:]], axis=1)


def _ssm(u_tb, h0r, h0i, sp, tt):
    t_total, nb, _ = u_tb.shape
    assert t_total % tt == 0 and nb % SUBLANES == 0
    rows = tt * nb
    st = pl.BlockSpec((nb, STATE_COLS), lambda i: (0, 0))
    return pl.pallas_call(
        functools.partial(_ssm_kernel, tt=tt, nb=nb),
        grid=(t_total // tt,),
        in_specs=[pl.BlockSpec((tt, nb, SSM_WIDTH), lambda i: (i, 0, 0)), st, st,
                  _const_spec((SSM_WIDTH // 2, 2 * HALF_STATE)), _const_spec((SSM_WIDTH // 2, 2 * HALF_STATE)),
                  _const_spec((1, STATE_COLS)), _const_spec((1, STATE_COLS)),
                  _const_spec((2 * HALF_STATE, SSM_WIDTH // 2)), _const_spec((2 * HALF_STATE, SSM_WIDTH // 2)),
                  _const_spec((1, SSM_WIDTH)), _const_spec((SSM_WIDTH, SSM_WIDTH)), _const_spec((1, SSM_WIDTH))],
        out_specs=[pl.BlockSpec((tt, nb, SSM_WIDTH), lambda i: (i, 0, 0)), st, st],
        out_shape=[jax.ShapeDtypeStruct((t_total, nb, SSM_WIDTH), F32),
                   jax.ShapeDtypeStruct((nb, STATE_COLS), F32), jax.ShapeDtypeStruct((nb, STATE_COLS), F32)],
        scratch_shapes=[pltpu.VMEM((rows, 2 * STATE_COLS), F32), pltpu.VMEM((nb, 2 * STATE_COLS), F32)],
        compiler_params=pltpu.CompilerParams(dimension_semantics=("arbitrary",), vmem_limit_bytes=VMEM_LIMIT),
        name="ssm",
    )(u_tb, h0r, h0i, sp["wlo"], sp["whi"], sp["lr"], sp["li"], sp["clo"], sp["chi"], sp["d"], sp["wglu"], sp["bglu"])


def _block_diag(blocks):
    n, a, b = blocks.shape
    eye = jnp.eye(n, dtype=blocks.dtype)
    return jnp.einsum('nab,nm->namb', blocks, eye).reshape(n * a, n * b)


N_QUARTERS = 4
Q_GROUPS = N_SSM_GROUPS // N_QUARTERS
Q_STATE = Q_GROUPS * SSM_STATE


def _ssm_params(a_re, a_im, log_dt, b_re, b_im, c_re, c_im, d_skip, w_glu, b_glu):
    dt = jnp.exp(log_dt)[:, None]
    mag = jnp.exp(a_re * dt)
    lr, li = mag * jnp.cos(a_im * dt), mag * jnp.sin(a_im * dt)
    nr, ni = lr - 1.0, li
    den = a_re * a_re + a_im * a_im
    fr, fi = (nr * a_re + ni * a_im) / den, (ni * a_re - nr * a_im) / den
    bbr = fr[..., None] * b_re - fi[..., None] * b_im
    bbi = fr[..., None] * b_im + fi[..., None] * b_re

    def win(lo, n):
        sl = slice(lo, lo + n)
        return jnp.concatenate([_block_diag(jnp.swapaxes(bbr[sl], 1, 2)), _block_diag(jnp.swapaxes(bbi[sl], 1, 2))],
                               axis=1).astype(BF16)

    def cmat(lo, n):
        sl = slice(lo, lo + n)
        return jnp.concatenate([_block_diag(jnp.swapaxes(c_re[sl], 1, 2)), -_block_diag(jnp.swapaxes(c_im[sl], 1, 2))],
                               axis=0).astype(BF16)

    hg = N_SSM_GROUPS // 2
    quarters = [q * Q_GROUPS for q in range(N_QUARTERS)]
    return dict(wlo=win(0, hg), whi=win(hg, hg), lr=lr.reshape(1, STATE_COLS), li=li.reshape(1, STATE_COLS),
                clo=cmat(0, hg), chi=cmat(hg, hg), d=d_skip.reshape(1, SSM_WIDTH),
                wglu=_block_diag(w_glu).astype(BF16), bglu=b_glu.reshape(1, SSM_WIDTH),
                wq=jnp.stack([win(q, Q_GROUPS) for q in quarters]),
                cq=jnp.stack([cmat(q, Q_GROUPS) for q in quarters]),
                wgluq=jnp.stack([_block_diag(w_glu[q:q + Q_GROUPS]) for q in quarters]).astype(BF16))


def _ffn(h, gf_ref, wg_ref, wu_ref, wd_ref):
    f = _rms(h, gf_ref[...]).astype(BF16)
    gate = jnp.dot(f, wg_ref[...], preferred_element_type=F32)
    up = jnp.dot(f, wu_ref[...], preferred_element_type=F32)
    a = (jax.nn.silu(gate) * up).astype(BF16)
    return h + jnp.dot(a, wd_ref[...], preferred_element_type=F32)


def _back_kernel(x_ref, oa_ref, os_ref, ga_ref, gs_ref, wout_ref, gf_ref, wg_ref, wu_ref, wd_ref, y_ref):
    mix = jnp.concatenate([_rms(oa_ref[...], ga_ref[...]), _rms(os_ref[...], gs_ref[...])], axis=-1).astype(BF16)
    h = x_ref[...] + jnp.dot(mix, wout_ref[...], preferred_element_type=F32)
    y_ref[...] = _ffn(h, gf_ref, wg_ref, wu_ref, wd_ref)


def _ffn_kernel(x_ref, mix_ref, wout_ref, gf_ref, wg_ref, wu_ref, wd_ref, y_ref):
    h = x_ref[...] + jnp.dot(mix_ref[...], wout_ref[...], preferred_element_type=F32)
    y_ref[...] = _ffn(h, gf_ref, wg_ref, wu_ref, wd_ref)


def _back(x_rows, oa, os_, ga, gs, wout, gf, wg, wu, wd, tm):
    n = x_rows.shape[0]
    assert n % tm == 0
    row = lambda w: pl.BlockSpec((tm, w), lambda i: (i, 0))
    return pl.pallas_call(
        _back_kernel,
        grid=(n // tm,),
        in_specs=[row(D_MODEL), row(ATTN_WIDTH), row(SSM_WIDTH), _const_spec((1, ATTN_WIDTH)),
                  _const_spec((1, SSM_WIDTH)), _const_spec((D_MODEL, D_MODEL)), _const_spec((1, D_MODEL)),
                  _const_spec((D_MODEL, D_FF)), _const_spec((D_MODEL, D_FF)), _const_spec((D_FF, D_MODEL))],
        out_specs=row(D_MODEL),
        out_shape=jax.ShapeDtypeStruct((n, D_MODEL), F32),
        compiler_params=pltpu.CompilerParams(dimension_semantics=("arbitrary",), vmem_limit_bytes=VMEM_LIMIT),
        name="back",
    )(x_rows, oa, os_, ga, gs, wout, gf, wg, wu, wd)


def _ffn_call(x_rows, mix, wout, gf, wg, wu, wd, tm):
    n = x_rows.shape[0]
    assert n % tm == 0
    row = lambda w: pl.BlockSpec((tm, w), lambda i: (i, 0))
    return pl.pallas_call(
        _ffn_kernel,
        grid=(n // tm,),
        in_specs=[row(D_MODEL), row(D_MODEL), _const_spec((D_MODEL, D_MODEL)), _const_spec((1, D_MODEL)),
                  _const_spec((D_MODEL, D_FF)), _const_spec((D_MODEL, D_FF)), _const_spec((D_FF, D_MODEL))],
        out_specs=row(D_MODEL),
        out_shape=jax.ShapeDtypeStruct((n, D_MODEL), F32),
        compiler_params=pltpu.CompilerParams(dimension_semantics=("arbitrary",), vmem_limit_bytes=VMEM_LIMIT),
        name="ffn",
    )(x_rows, mix, wout, gf, wg, wu, wd)


N_SEQ = 8
ROWS = N_SEQ * BLOCK
Q_TILES = ATTN_WIDTH // LANES
PITCH = BLOCK + SUBLANES


def _mixer_kernel(sink_ref, x_ref, gmix_ref, win_ref, gq_ref, gk_ref, kmeta_ref, vmeta_ref, h0r_ref, h0i_ref,
                  wq_ref, lr_ref, li_ref, cq_ref, d_ref, wglu_ref, bglu_ref, ga_ref, gs_ref,
                  mix_ref, kw_ref, vw_ref, hr_ref, hi_ref,
                  qs_ref, kk_ref, vv_ref, p_ref, oatt_ref, usl_ref, utb_ref, xs_ref, st_ref, otb_ref, osl_ref):
    i = pl.program_id(0)
    left = _left_half()

    @pl.when(i == 0)
    def _():
        for b in range(N_SEQ):
            kk_ref[b, 0:BLOCK, :] = kmeta_ref[...].astype(BF16)
            vv_ref[b, 0:BLOCK, :] = vmeta_ref[...].astype(BF16)
        for q in range(N_QUARTERS):
            st_ref[:, 2 * q * Q_STATE:(2 * q + 1) * Q_STATE] = h0r_ref[:, q * Q_STATE:(q + 1) * Q_STATE]
            st_ref[:, (2 * q + 1) * Q_STATE:(2 * q + 2) * Q_STATE] = h0i_ref[:, q * Q_STATE:(q + 1) * Q_STATE]

    xn = _rms(x_ref[...].reshape(ROWS, D_MODEL), gmix_ref[...]).astype(BF16)

    def proj(c0):
        return jnp.dot(xn, win_ref[:, c0:c0 + 2 * LANES], preferred_element_type=F32)

    for c in range(Q_TILES // 2):
        zz = proj(c * 2 * LANES)
        for tt in range(2):
            t = 2 * c + tt
            qn = _pair_norm(zz[:, tt * LANES:(tt + 1) * LANES], gq_ref[...], left) * ATTN_SCALE
            qa = jnp.where(left, qn, 0.0).astype(BF16)
            qb = jnp.where(left, 0.0, qn).astype(BF16)
            for b in range(N_SEQ):
                qs_ref[b, 2 * t * BLOCK:(2 * t + 1) * BLOCK, :] = qa[b * BLOCK:(b + 1) * BLOCK]
                qs_ref[b, (2 * t + 1) * BLOCK:(2 * t + 2) * BLOCK, :] = qb[b * BLOCK:(b + 1) * BLOCK]
    zz = proj(ATTN_WIDTH)
    kn = _pair_norm(zz[:, :LANES], gk_ref[...], left)
    vn = zz[:, LANES:]
    kw_ref[...] = kn.reshape(N_SEQ, BLOCK, KV_WIDTH)
    vw_ref[...] = vn.reshape(N_SEQ, BLOCK, KV_WIDTH)
    kk_ref[:, BLOCK:, :] = kn.astype(BF16).reshape(N_SEQ, BLOCK, KV_WIDTH)
    vv_ref[:, BLOCK:, :] = vn.astype(BF16).reshape(N_SEQ, BLOCK, KV_WIDTH)
    for c in range(2):
        zz = proj(ATTN_WIDTH + 2 * KV_WIDTH + c * 2 * LANES)
        for tt in range(2):
            for b in range(N_SEQ):
                usl_ref[2 * c + tt, b * PITCH:b * PITCH + BLOCK, :] = zz[b * BLOCK:(b + 1) * BLOCK, tt * LANES:(tt + 1) * LANES]

    r = lax.broadcasted_iota(jnp.int32, (BLOCK, 2 * BLOCK), 0)
    c = lax.broadcasted_iota(jnp.int32, (BLOCK, 2 * BLOCK), 1)
    valid = (c >= r) & (c <= r + WINDOW) & ((i > 0) | (c >= BLOCK - N_META))
    bias = jnp.where(valid, 0.0, -jnp.inf)

    def attn_body(b, carry):
        s_all = lax.dot_general(qs_ref[b], kk_ref[b], (((1,), (1,)), ((), ())), preferred_element_type=F32)
        inv = []
        for g in range(2 * Q_TILES):
            sink = sink_ref[g // 2 + Q_PER_KV * (g % 2)]
            s = s_all[g * BLOCK:(g + 1) * BLOCK] + bias
            m = jnp.maximum(jnp.max(s, axis=-1, keepdims=True), sink)
            p = jnp.exp(s - m)
            inv.append(1.0 / (jnp.sum(p, axis=-1, keepdims=True) + jnp.exp(sink - m)))
            p_ref[g * BLOCK:(g + 1) * BLOCK, :] = p.astype(BF16)
        o_all = jnp.dot(p_ref[...], vv_ref[b], preferred_element_type=F32)
        for t in range(Q_TILES):
            oa = o_all[2 * t * BLOCK:(2 * t + 1) * BLOCK] * inv[2 * t]
            ob = o_all[(2 * t + 1) * BLOCK:(2 * t + 2) * BLOCK] * inv[2 * t + 1]
            oatt_ref[b, :, t * LANES:(t + 1) * LANES] = jnp.where(left, oa, ob)
        return carry

    lax.fori_loop(0, N_SEQ, attn_body, 0)
    kk_ref[:, 0:BLOCK, :] = kk_ref[:, BLOCK:, :]
    vv_ref[:, 0:BLOCK, :] = vv_ref[:, BLOCK:, :]

    def to_tb(t, carry):
        row = pl.multiple_of(t * SUBLANES, SUBLANES)
        for j in range(N_QUARTERS):
            utb_ref[pl.ds(row, SUBLANES), j * LANES:(j + 1) * LANES] = usl_ref[j, pl.ds(t, SUBLANES, stride=PITCH), :]
        return carry

    lax.fori_loop(0, BLOCK, to_tb, 0, unroll=8)

    for q in range(N_QUARTERS):
        uq = utb_ref[:, q * LANES:(q + 1) * LANES]
        xs_ref[...] = jnp.dot(uq.astype(BF16), wq_ref[q], preferred_element_type=F32)
        lr = jnp.broadcast_to(lr_ref[:, q * Q_STATE:(q + 1) * Q_STATE], (SUBLANES, Q_STATE))
        li = jnp.broadcast_to(li_ref[:, q * Q_STATE:(q + 1) * Q_STATE], (SUBLANES, Q_STATE))
        _scan_rows(xs_ref, st_ref, lr, li, tt=BLOCK, nb=N_SEQ, r0=0, re0=0, im0=Q_STATE,
                   st_re0=2 * q * Q_STATE, st_im0=(2 * q + 1) * Q_STATE)
        y = jnp.dot(xs_ref[...].astype(BF16), cq_ref[q], preferred_element_type=F32)
        y = y + d_ref[:, q * LANES:(q + 1) * LANES] * uq
        g = jax.nn.gelu(y)
        gate = jnp.dot(g.astype(BF16), wglu_ref[q], preferred_element_type=F32) + bglu_ref[:, q * LANES:(q + 1) * LANES]
        otb_ref[...] = g * jax.nn.sigmoid(gate)

        def to_bt(t, carry, q=q):
            row = pl.multiple_of(t * SUBLANES, SUBLANES)
            osl_ref[q, pl.ds(t, SUBLANES, stride=PITCH), :] = otb_ref[pl.ds(row, SUBLANES), :]
            return carry

        lax.fori_loop(0, BLOCK, to_bt, 0, unroll=8)

    hr_ref[...] = jnp.concatenate([st_ref[:, 2 * q * Q_STATE:(2 * q + 1) * Q_STATE] for q in range(N_QUARTERS)], axis=1)
    hi_ref[...] = jnp.concatenate([st_ref[:, (2 * q + 1) * Q_STATE:(2 * q + 2) * Q_STATE] for q in range(N_QUARTERS)], axis=1)

    for b in range(N_SEQ):
        mix_ref[b, :, 0:ATTN_WIDTH] = _rms(oatt_ref[b], ga_ref[...]).astype(BF16)
        sl = [osl_ref[j, b * PITCH:b * PITCH + BLOCK, :] for j in range(N_QUARTERS)]
        ms = sum(jnp.sum(s * s, axis=-1, keepdims=True) for s in sl) / SSM_WIDTH
        inv = lax.rsqrt(ms + EPS)
        for j in range(N_QUARTERS):
            mix_ref[b, :, ATTN_WIDTH + j * LANES:ATTN_WIDTH + (j + 1) * LANES] = (
                sl[j] * inv * gs_ref[:, j * LANES:(j + 1) * LANES]).astype(BF16)


def _mixer(x_prompt, sinks_perm, gmix, win_perm, gq2, gk2, kmeta_blk, vmeta_blk, h0r, h0i, sp, ga_perm, gs):
    nseq, seq, _ = x_prompt.shape
    assert nseq == N_SEQ and seq % BLOCK == 0
    st = pl.BlockSpec((N_SEQ, STATE_COLS), lambda i: (0, 0))
    kvw = pl.BlockSpec((N_SEQ, BLOCK, KV_WIDTH), lambda i: (0, 0, 0))
    return pl.pallas_call(
        _mixer_kernel,
        grid=(seq // BLOCK,),
        in_specs=[pl.BlockSpec(memory_space=pltpu.SMEM),
                  pl.BlockSpec((N_SEQ, BLOCK, D_MODEL), lambda i: (0, i, 0)),
                  _const_spec((1, D_MODEL)), _const_spec((D_MODEL, IN_COLS)), _const_spec((1, LANES)),
                  _const_spec((1, LANES)), _const_spec((BLOCK, KV_WIDTH)), _const_spec((BLOCK, KV_WIDTH)), st, st,
                  _const_spec((N_QUARTERS, LANES, 2 * Q_STATE)), _const_spec((1, STATE_COLS)),
                  _const_spec((1, STATE_COLS)), _const_spec((N_QUARTERS, 2 * Q_STATE, LANES)),
                  _const_spec((1, SSM_WIDTH)), _const_spec((N_QUARTERS, LANES, LANES)), _const_spec((1, SSM_WIDTH)),
                  _const_spec((1, ATTN_WIDTH)), _const_spec((1, SSM_WIDTH))],
        out_specs=[pl.BlockSpec((N_SEQ, BLOCK, D_MODEL), lambda i: (0, i, 0)), kvw, kvw, st, st],
        out_shape=[jax.ShapeDtypeStruct((N_SEQ, seq, D_MODEL), BF16),
                   jax.ShapeDtypeStruct((N_SEQ, BLOCK, KV_WIDTH), F32), jax.ShapeDtypeStruct((N_SEQ, BLOCK, KV_WIDTH), F32),
                   jax.ShapeDtypeStruct((N_SEQ, STATE_COLS), F32), jax.ShapeDtypeStruct((N_SEQ, STATE_COLS), F32)],
        scratch_shapes=[pltpu.VMEM((N_SEQ, 2 * Q_TILES * BLOCK, LANES), BF16),
                        pltpu.VMEM((N_SEQ, 2 * BLOCK, KV_WIDTH), BF16),
                        pltpu.VMEM((N_SEQ, 2 * BLOCK, KV_WIDTH), BF16),
                        pltpu.VMEM((2 * Q_TILES * BLOCK, 2 * BLOCK), BF16),
                        pltpu.VMEM((N_SEQ, BLOCK, ATTN_WIDTH), F32),
                        pltpu.VMEM((N_QUARTERS, N_SEQ * PITCH, LANES), F32),
                        pltpu.VMEM((ROWS, SSM_WIDTH), F32),
                        pltpu.VMEM((ROWS, 2 * Q_STATE), F32),
                        pltpu.VMEM((N_SEQ, 2 * STATE_COLS), F32),
                        pltpu.VMEM((ROWS, LANES), F32),
                        pltpu.VMEM((N_QUARTERS, N_SEQ * PITCH, LANES), F32)],
        compiler_params=pltpu.CompilerParams(dimension_semantics=("arbitrary",), vmem_limit_bytes=VMEM_LIMIT),
        name="mixer",
    )(sinks_perm, x_prompt, gmix, win_perm, gq2, gk2, kmeta_blk, vmeta_blk, h0r, h0i, sp["wq"], sp["lr"], sp["li"],
      sp["cq"], sp["d"], sp["wgluq"], sp["bglu"], ga_perm, gs)


def kernel(x_prompt, x_sample, cache_k_win, cache_v_win, state_ssm_re, state_ssm_im, meta_tokens, g_mix, w_in, g_q,
           g_k, sinks, ssm_a_re, ssm_a_im, ssm_log_dt, ssm_b_re, ssm_b_im, ssm_c_re, ssm_c_im, ssm_d, ssm_w_glu,
           ssm_b_glu, g_att_out, g_ssm_out, w_out, g_ffn, w_gate, w_up, w_down):
    bp, seq, _ = x_prompt.shape
    db, dseq, _ = x_sample.shape
    li = 0
    gmix = g_mix[li].reshape(1, D_MODEL)
    win = w_in[li].astype(BF16)
    gq2 = jnp.tile(g_q[li], 2).reshape(1, LANES)
    gk2 = jnp.tile(g_k[li], 2).reshape(1, LANES)
    sk = sinks[li]
    sp = _ssm_params(ssm_a_re[li], ssm_a_im[li], ssm_log_dt[li], ssm_b_re[li], ssm_b_im[li], ssm_c_re[li],
                     ssm_c_im[li], ssm_d[li], ssm_w_glu[li], ssm_b_glu[li])
    ga = g_att_out[li].reshape(1, ATTN_WIDTH)
    gs = g_ssm_out[li].reshape(1, SSM_WIDTH)
    wout = w_out[li].astype(BF16)
    ffn_w = (g_ffn[li].reshape(1, D_MODEL), w_gate[li].astype(BF16), w_up[li].astype(BF16), w_down[li].astype(BF16))
    perm = jnp.concatenate([jnp.arange(HEAD_DIM) + HEAD_DIM * (t + Q_PER_KV * s) for t in range(Q_TILES) for s in range(2)])
    win_perm = jnp.concatenate([win[:, perm], win[:, ATTN_WIDTH:]], axis=1)
    wout_perm = jnp.concatenate([wout[perm], wout[ATTN_WIDTH:]], axis=0)
    ga_perm = ga[:, perm]

    xs_rows = x_sample.reshape(db * dseq, D_MODEL)
    meta_rows = jnp.tile(meta_tokens, (SUBLANES, 1))
    small = jnp.concatenate([xs_rows, meta_rows], axis=0)
    q_s, k_s, v_s, u_s = _front(small, gmix, win, gq2, gk2, tm=128)
    n_s = db * dseq
    lead = jnp.zeros((BLOCK - N_META, KV_WIDTH), F32)
    kmeta_blk = jnp.concatenate([lead, k_s[n_s:n_s + N_META]], axis=0)
    vmeta_blk = jnp.concatenate([lead, v_s[n_s:n_s + N_META]], axis=0)
    u_meta = u_s[n_s:].reshape(SUBLANES, N_META, SSM_WIDTH).transpose(1, 0, 2)
    zero_state = jnp.zeros((SUBLANES, STATE_COLS), F32)
    _, hm_r, hm_i = _ssm(u_meta, zero_state, zero_state, sp, tt=N_META)

    mix, kw_p, vw_p, hp_r, hp_i = _mixer(x_prompt, sk, gmix, win_perm, gq2, gk2, kmeta_blk, vmeta_blk, hm_r, hm_i, sp,
                                         ga_perm, gs)
    y_prompt = _ffn_call(x_prompt.reshape(bp * seq, D_MODEL), mix.reshape(bp * seq, D_MODEL), wout_perm, *ffn_w,
                         tm=512).reshape(bp, seq, D_MODEL)

    ck = cache_k_win[li].reshape(db, WINDOW, KV_WIDTH)
    cv = cache_v_win[li].reshape(db, WINDOW, KV_WIDTH)
    oa_s, kw_s, vw_s = _attn_sample(q_s[:n_s].reshape(db, dseq, ATTN_WIDTH), k_s[:n_s].reshape(db, dseq, KV_WIDTH),
                                    v_s[:n_s].reshape(db, dseq, KV_WIDTH), ck, cv, sk)
    us_tb = u_s[:n_s].reshape(db, dseq, SSM_WIDTH).transpose(1, 0, 2)
    os_s_tb, hs_r, hs_i = _ssm(us_tb, state_ssm_re[li].reshape(db, STATE_COLS),
                               state_ssm_im[li].reshape(db, STATE_COLS), sp, tt=dseq)
    os_s = os_s_tb.transpose(1, 0, 2).reshape(n_s, SSM_WIDTH)
    y_sample = _back(xs_rows, oa_s.reshape(n_s, ATTN_WIDTH), os_s, ga, gs, wout, *ffn_w, tm=512).reshape(db, dseq, D_MODEL)

    kv5 = lambda a, n: a.reshape(1, n, WINDOW, N_KV_HEADS, HEAD_DIM)
    st4 = lambda a, n: a.reshape(1, n, N_SSM_GROUPS, SSM_STATE)
    return (y_prompt, y_sample, kv5(kw_p, bp), kv5(vw_p, bp), st4(hp_r, bp), st4(hp_i, bp),
            kv5(kw_s, db), kv5(vw_s, db), st4(hs_r, db), st4(hs_i, db))
```

```python
import functools

import jax
import jax.numpy as jnp
from jax import lax
from jax.experimental import pallas as pl
from jax.experimental.pallas import tpu as pltpu

D_MODEL = 1024
N_META = 16
HEAD_DIM = 64
ATTN_WIDTH = 512
N_Q_HEADS = 8
Q_PER_KV = 4
N_KV_HEADS = 2
KV_WIDTH = 128
WINDOW = 128
BLOCK = 128
SSM_WIDTH = 512
SSM_GROUP = 16
N_SSM_GROUPS = 32
SSM_STATE = 64
IN_COLS = ATTN_WIDTH + 2 * KV_WIDTH + SSM_WIDTH
D_FF = 2816
EPS = 1e-6
ATTN_SCALE = HEAD_DIM ** -0.5
STATE_COLS = N_SSM_GROUPS * SSM_STATE
HALF_STATE = STATE_COLS // 2
LANES = 128
SUBLANES = 8
VMEM_LIMIT = 56 * 1024 * 1024

F32 = jnp.float32
BF16 = jnp.bfloat16


def _const_spec(shape):
    return pl.BlockSpec(shape, lambda *_: (0,) * len(shape), pipeline_mode=pl.Buffered(1))


def _rms(x, g):
    return x * lax.rsqrt(jnp.mean(x * x, axis=-1, keepdims=True) + EPS) * g


def _left_half():
    return lax.broadcasted_iota(jnp.int32, (1, LANES), 1) < HEAD_DIM


def _pair_norm(zz, g2, left):
    sq = zz * zz
    sl = jnp.sum(jnp.where(left, sq, 0.0), axis=-1, keepdims=True)
    sr = jnp.sum(jnp.where(left, 0.0, sq), axis=-1, keepdims=True)
    inv = jnp.where(left, lax.rsqrt(sl / HEAD_DIM + EPS), lax.rsqrt(sr / HEAD_DIM + EPS))
    return zz * inv * g2


def _front_kernel(x_ref, gmix_ref, win_ref, gq_ref, gk_ref, q_ref, k_ref, v_ref, u_ref):
    xn = _rms(x_ref[...], gmix_ref[...]).astype(BF16)
    z = jnp.dot(xn, win_ref[...], preferred_element_type=F32)
    left = _left_half()
    for p in range(ATTN_WIDTH // LANES):
        q_ref[:, p * LANES:(p + 1) * LANES] = _pair_norm(z[:, p * LANES:(p + 1) * LANES], gq_ref[...], left)
    k_ref[...] = _pair_norm(z[:, ATTN_WIDTH:ATTN_WIDTH + KV_WIDTH], gk_ref[...], left)
    v_ref[...] = z[:, ATTN_WIDTH + KV_WIDTH:ATTN_WIDTH + 2 * KV_WIDTH]
    u_ref[...] = z[:, ATTN_WIDTH + 2 * KV_WIDTH:]


def _front(x_rows, gmix, win_bf, gq2, gk2, tm):
    n = x_rows.shape[0]
    assert n % tm == 0
    row = lambda w: pl.BlockSpec((tm, w), lambda i: (i, 0))
    return pl.pallas_call(
        _front_kernel,
        grid=(n // tm,),
        in_specs=[row(D_MODEL), _const_spec((1, D_MODEL)), _const_spec((D_MODEL, IN_COLS)),
                  _const_spec((1, LANES)), _const_spec((1, LANES))],
        out_specs=[row(ATTN_WIDTH), row(KV_WIDTH), row(KV_WIDTH), row(SSM_WIDTH)],
        out_shape=[jax.ShapeDtypeStruct((n, ATTN_WIDTH), F32), jax.ShapeDtypeStruct((n, KV_WIDTH), F32),
                   jax.ShapeDtypeStruct((n, KV_WIDTH), F32), jax.ShapeDtypeStruct((n, SSM_WIDTH), F32)],
        compiler_params=pltpu.CompilerParams(dimension_semantics=("arbitrary",), vmem_limit_bytes=VMEM_LIMIT),
        name="front",
    )(x_rows, gmix, win_bf, gq2, gk2)


def _sink_softmax_pv(s, mask, sink, vg):
    s = jnp.where(mask, s * ATTN_SCALE, -jnp.inf)
    m = jnp.maximum(jnp.max(s, axis=-1, keepdims=True), sink)
    p = jnp.exp(s - m)
    denom = jnp.sum(p, axis=-1, keepdims=True) + jnp.exp(sink - m)
    o = jnp.dot(p.astype(BF16), vg, preferred_element_type=F32)
    return o / denom


def _attn_sample_kernel(sink_ref, q_ref, kn_ref, vn_ref, ck_ref, cv_ref, o_ref, kw_ref, vw_ref, *, bb, t):
    rows = Q_PER_KV * t
    tk = WINDOW + t
    r = lax.broadcasted_iota(jnp.int32, (rows, tk), 0) % t
    c = lax.broadcasted_iota(jnp.int32, (rows, tk), 1)
    mask = (c >= r) & (c <= r + WINDOW)
    hrow = lax.broadcasted_iota(jnp.int32, (rows, 1), 0) // t
    for bi in range(bb):
        kk = jnp.concatenate([ck_ref[bi], kn_ref[bi]], axis=0)
        vv = jnp.concatenate([cv_ref[bi], vn_ref[bi]], axis=0)
        kw_ref[bi] = kk[t:]
        vw_ref[bi] = vv[t:]
        kkb = kk.astype(BF16)
        vvb = vv.astype(BF16)
        q = q_ref[bi]
        for g in range(N_KV_HEADS):
            heads = range(g * Q_PER_KV, (g + 1) * Q_PER_KV)
            q4 = jnp.concatenate([q[:, h * HEAD_DIM:(h + 1) * HEAD_DIM] for h in heads], axis=0).astype(BF16)
            sink = jnp.zeros((rows, 1), F32)
            for i, h in enumerate(heads):
                sink = jnp.where(hrow == i, sink_ref[h], sink)
            kg = kkb[:, g * HEAD_DIM:(g + 1) * HEAD_DIM]
            vg = vvb[:, g * HEAD_DIM:(g + 1) * HEAD_DIM]
            s = lax.dot_general(q4, kg, (((1,), (1,)), ((), ())), preferred_element_type=F32)
            o4 = _sink_softmax_pv(s, mask, sink, vg)
            for i, h in enumerate(heads):
                o_ref[bi, :, h * HEAD_DIM:(h + 1) * HEAD_DIM] = o4[i * t:(i + 1) * t]


def _attn_sample(q, kn, vn, ck, cv, sinks, bb=8):
    db, t, _ = q.shape
    blk = lambda r, w: pl.BlockSpec((bb, r, w), lambda i: (i, 0, 0))
    return pl.pallas_call(
        functools.partial(_attn_sample_kernel, bb=bb, t=t),
        grid=(db // bb,),
        in_specs=[pl.BlockSpec(memory_space=pltpu.SMEM), blk(t, ATTN_WIDTH), blk(t, KV_WIDTH), blk(t, KV_WIDTH),
                  blk(WINDOW, KV_WIDTH), blk(WINDOW, KV_WIDTH)],
        out_specs=[blk(t, ATTN_WIDTH), blk(WINDOW, KV_WIDTH), blk(WINDOW, KV_WIDTH)],
        out_shape=[jax.ShapeDtypeStruct((db, t, ATTN_WIDTH), F32),
                   jax.ShapeDtypeStruct((db, WINDOW, KV_WIDTH), F32),
                   jax.ShapeDtypeStruct((db, WINDOW, KV_WIDTH), F32)],
        compiler_params=pltpu.CompilerParams(dimension_semantics=("arbitrary",)),
        name="attn_sample",
    )(sinks, q, kn, vn, ck, cv)


SCAN_COLS = 512


def _scan_rows(xs_ref, st_ref, lr, li, *, tt, nb, r0, re0, im0, st_re0, st_im0):
    def body(t, carry):
        hr, hi = carry
        row = pl.multiple_of(t * nb + r0, SUBLANES)
        xr = xs_ref[pl.ds(row, SUBLANES), re0:re0 + SCAN_COLS]
        xi = xs_ref[pl.ds(row, SUBLANES), im0:im0 + SCAN_COLS]
        nhr = lr * hr - li * hi + xr
        nhi = lr * hi + li * hr + xi
        xs_ref[pl.ds(row, SUBLANES), re0:re0 + SCAN_COLS] = nhr
        xs_ref[pl.ds(row, SUBLANES), im0:im0 + SCAN_COLS] = nhi
        return nhr, nhi

    hr, hi = lax.fori_loop(
        0, tt, body,
        (st_ref[r0:r0 + SUBLANES, st_re0:st_re0 + SCAN_COLS], st_ref[r0:r0 + SUBLANES, st_im0:st_im0 + SCAN_COLS]),
        unroll=min(tt, 8))
    st_ref[r0:r0 + SUBLANES, st_re0:st_re0 + SCAN_COLS] = hr
    st_ref[r0:r0 + SUBLANES, st_im0:st_im0 + SCAN_COLS] = hi


def _ssm_kernel(u_ref, h0r_ref, h0i_ref, wlo_ref, whi_ref, lr_ref, li_ref, clo_ref, chi_ref, d_ref, wglu_ref,
                bglu_ref, o_ref, hr_ref, hi_ref, xs_ref, st_ref, *, tt, nb):
    rows = tt * nb

    @pl.when(pl.program_id(0) == 0)
    def _():
        for half in range(2):
            st_ref[:, (2 * half) * HALF_STATE:(2 * half + 1) * HALF_STATE] = h0r_ref[:, half * HALF_STATE:(half + 1) * HALF_STATE]
            st_ref[:, (2 * half + 1) * HALF_STATE:(2 * half + 2) * HALF_STATE] = h0i_ref[:, half * HALF_STATE:(half + 1) * HALF_STATE]

    u = u_ref[...].reshape(rows, SSM_WIDTH)
    ub = u.astype(BF16)
    half_in = SSM_WIDTH // 2
    xs_ref[:, 0:2 * HALF_STATE] = jnp.dot(ub[:, :half_in], wlo_ref[...], preferred_element_type=F32)
    xs_ref[:, 2 * HALF_STATE:] = jnp.dot(ub[:, half_in:], whi_ref[...], preferred_element_type=F32)

    for rg in range(nb // SUBLANES):
        for half in range(2):
            for cc in range(0, HALF_STATE, SCAN_COLS):
                re0 = 2 * half * HALF_STATE + cc
                lam0 = half * HALF_STATE + cc
                lr = jnp.broadcast_to(lr_ref[:, lam0:lam0 + SCAN_COLS], (SUBLANES, SCAN_COLS))
                li = jnp.broadcast_to(li_ref[:, lam0:lam0 + SCAN_COLS], (SUBLANES, SCAN_COLS))
                _scan_rows(xs_ref, st_ref, lr, li, tt=tt, nb=nb, r0=rg * SUBLANES, re0=re0, im0=re0 + HALF_STATE,
                           st_re0=re0, st_im0=re0 + HALF_STATE)

    y_lo = jnp.dot(xs_ref[:, 0:2 * HALF_STATE].astype(BF16), clo_ref[...], preferred_element_type=F32)
    y_hi = jnp.dot(xs_ref[:, 2 * HALF_STATE:].astype(BF16), chi_ref[...], preferred_element_type=F32)
    y = jnp.concatenate([y_lo, y_hi], axis=1) + d_ref[...] * u
    g = jax.nn.gelu(y)
    gate = jnp.dot(g.astype(BF16), wglu_ref[...], preferred_element_type=F32) + bglu_ref[...]
    o_ref[...] = (g * jax.nn.sigmoid(gate)).reshape(tt, nb, SSM_WIDTH)
    hr_ref[...] = jnp.concatenate([st_ref[:, 0:HALF_STATE], st_ref[:, 2 * HALF_STATE:3 * HALF_STATE]], axis=1)
    hi_ref[...] = jnp.concatenate([st_ref[:, HALF_STATE:2 * HALF_STATE], st_ref[:, 3 * HALF_STATE:]], axis=1)


def _ssm(u_tb, h0r, h0i, sp, tt):
    t_total, nb, _ = u_tb.shape
    assert t_total % tt == 0 and nb % SUBLANES == 0
    rows = tt * nb
    st = pl.BlockSpec((nb, STATE_COLS), lambda i: (0, 0))
    return pl.pallas_call(
        functools.partial(_ssm_kernel, tt=tt, nb=nb),
        grid=(t_total // tt,),
        in_specs=[pl.BlockSpec((tt, nb, SSM_WIDTH), lambda i: (i, 0, 0)), st, st,
                  _const_spec((SSM_WIDTH // 2, 2 * HALF_STATE)), _const_spec((SSM_WIDTH // 2, 2 * HALF_STATE)),
                  _const_spec((1, STATE_COLS)), _const_spec((1, STATE_COLS)),
                  _const_spec((2 * HALF_STATE, SSM_WIDTH // 2)), _const_spec((2 * HALF_STATE, SSM_WIDTH // 2)),
                  _const_spec((1, SSM_WIDTH)), _const_spec((SSM_WIDTH, SSM_WIDTH)), _const_spec((1, SSM_WIDTH))],
        out_specs=[pl.BlockSpec((tt, nb, SSM_WIDTH), lambda i: (i, 0, 0)), st, st],
        out_shape=[jax.ShapeDtypeStruct((t_total, nb, SSM_WIDTH), F32),
                   jax.ShapeDtypeStruct((nb, STATE_COLS), F32), jax.ShapeDtypeStruct((nb, STATE_COLS), F32)],
        scratch_shapes=[pltpu.VMEM((rows, 2 * STATE_COLS), F32), pltpu.VMEM((nb, 2 * STATE_COLS), F32)],
        compiler_params=pltpu.CompilerParams(dimension_semantics=("arbitrary",), vmem_limit_bytes=VMEM_LIMIT),
        name="ssm",
    )(u_tb, h0r, h0i, sp["wlo"], sp["whi"], sp["lr"], sp["li"], sp["clo"], sp["chi"], sp["d"], sp["wglu"], sp["bglu"])


def _block_diag(blocks):
    n, a, b = blocks.shape
    eye = jnp.eye(n, dtype=blocks.dtype)
    return jnp.einsum('nab,nm->namb', blocks, eye).reshape(n * a, n * b)


N_QUARTERS = 4
Q_GROUPS = N_SSM_GROUPS // N_QUARTERS
Q_STATE = Q_GROUPS * SSM_STATE


def _ssm_params(a_re, a_im, log_dt, b_re, b_im, c_re, c_im, d_skip, w_glu, b_glu):
    dt = jnp.exp(log_dt)[:, None]
    mag = jnp.exp(a_re * dt)
    lr, li = mag * jnp.cos(a_im * dt), mag * jnp.sin(a_im * dt)
    nr, ni = lr - 1.0, li
    den = a_re * a_re + a_im * a_im
    fr, fi = (nr * a_re + ni * a_im) / den, (ni * a_re - nr * a_im) / den
    bbr = fr[..., None] * b_re - fi[..., None] * b_im
    bbi = fr[..., None] * b_im + fi[..., None] * b_re

    def win(lo, n):
        sl = slice(lo, lo + n)
        return jnp.concatenate([_block_diag(jnp.swapaxes(bbr[sl], 1, 2)), _block_diag(jnp.swapaxes(bbi[sl], 1, 2))],
                               axis=1).astype(BF16)

    def cmat(lo, n):
        sl = slice(lo, lo + n)
        return jnp.concatenate([_block_diag(jnp.swapaxes(c_re[sl], 1, 2)), -_block_diag(jnp.swapaxes(c_im[sl], 1, 2))],
                               axis=0).astype(BF16)

    hg = N_SSM_GROUPS // 2
    quarters = [q * Q_GROUPS for q in range(N_QUARTERS)]
    return dict(wlo=win(0, hg), whi=win(hg, hg), lr=lr.reshape(1, STATE_COLS), li=li.reshape(1, STATE_COLS),
                clo=cmat(0, hg), chi=cmat(hg, hg), d=d_skip.reshape(1, SSM_WIDTH),
                wglu=_block_diag(w_glu).astype(BF16), bglu=b_glu.reshape(1, SSM_WIDTH),
                wq=jnp.stack([win(q, Q_GROUPS) for q in quarters]),
                cq=jnp.stack([cmat(q, Q_GROUPS) for q in quarters]),
                wgluq=jnp.stack([_block_diag(w_glu[q:q + Q_GROUPS]) for q in quarters]).astype(BF16))


def _ffn(h, gf_ref, wg_ref, wu_ref, wd_ref):
    f = _rms(h, gf_ref[...]).astype(BF16)
    gate = jnp.dot(f, wg_ref[...], preferred_element_type=F32)
    up = jnp.dot(f, wu_ref[...], preferred_element_type=F32)
    a = (jax.nn.silu(gate) * up).astype(BF16)
    return h + jnp.dot(a, wd_ref[...], preferred_element_type=F32)


def _back_kernel(x_ref, oa_ref, os_ref, ga_ref, gs_ref, wout_ref, gf_ref, wg_ref, wu_ref, wd_ref, y_ref):
    mix = jnp.concatenate([_rms(oa_ref[...], ga_ref[...]), _rms(os_ref[...], gs_ref[...])], axis=-1).astype(BF16)
    h = x_ref[...] + jnp.dot(mix, wout_ref[...], preferred_element_type=F32)
    y_ref[...] = _ffn(h, gf_ref, wg_ref, wu_ref, wd_ref)


def _ffn_kernel(x_ref, mix_ref, wout_ref, gf_ref, wg_ref, wu_ref, wd_ref, y_ref):
    h = x_ref[...] + jnp.dot(mix_ref[...], wout_ref[...], preferred_element_type=F32)
    y_ref[...] = _ffn(h, gf_ref, wg_ref, wu_ref, wd_ref)


def _back(x_rows, oa, os_, ga, gs, wout, gf, wg, wu, wd, tm):
    n = x_rows.shape[0]
    assert n % tm == 0
    row = lambda w: pl.BlockSpec((tm, w), lambda i: (i, 0))
    return pl.pallas_call(
        _back_kernel,
        grid=(n // tm,),
        in_specs=[row(D_MODEL), row(ATTN_WIDTH), row(SSM_WIDTH), _const_spec((1, ATTN_WIDTH)),
                  _const_spec((1, SSM_WIDTH)), _const_spec((D_MODEL, D_MODEL)), _const_spec((1, D_MODEL)),
                  _const_spec((D_MODEL, D_FF)), _const_spec((D_MODEL, D_FF)), _const_spec((D_FF, D_MODEL))],
        out_specs=row(D_MODEL),
        out_shape=jax.ShapeDtypeStruct((n, D_MODEL), F32),
        compiler_params=pltpu.CompilerParams(dimension_semantics=("arbitrary",), vmem_limit_bytes=VMEM_LIMIT),
        name="back",
    )(x_rows, oa, os_, ga, gs, wout, gf, wg, wu, wd)


def _ffn_call(x_rows, mix, wout, gf, wg, wu, wd, tm):
    n = x_rows.shape[0]
    assert n % tm == 0
    row = lambda w: pl.BlockSpec((tm, w), lambda i: (i, 0))
    return pl.pallas_call(
        _ffn_kernel,
        grid=(n // tm,),
        in_specs=[row(D_MODEL), row(D_MODEL), _const_spec((D_MODEL, D_MODEL)), _const_spec((1, D_MODEL)),
                  _const_spec((D_MODEL, D_FF)), _const_spec((D_MODEL, D_FF)), _const_spec((D_FF, D_MODEL))],
        out_specs=row(D_MODEL),
        out_shape=jax.ShapeDtypeStruct((n, D_MODEL), F32),
        compiler_params=pltpu.CompilerParams(dimension_semantics=("arbitrary",), vmem_limit_bytes=VMEM_LIMIT),
        name="ffn",
    )(x_rows, mix, wout, gf, wg, wu, wd)


N_SEQ = 8
ROWS = N_SEQ * BLOCK
Q_TILES = ATTN_WIDTH // LANES
PITCH = BLOCK + SUBLANES
CHUNK_T = BLOCK // N_SEQ


def _mixer_kernel(sink_ref, x_ref, gmix_ref, win_ref, gq_ref, gk_ref, kmeta_ref, vmeta_ref, h0r_ref, h0i_ref,
                  wq_ref, lr_ref, li_ref, cq_ref, d_ref, wglu_ref, bglu_ref, ga_ref, gs_ref,
                  mix_ref, kw_ref, vw_ref, hr_ref, hi_ref,
                  qs_ref, kk_ref, vv_ref, p_ref, oatt_ref, usl_ref, st_ref, osl_ref):
    i = pl.program_id(0)
    left = _left_half()

    @pl.when(i == 0)
    def _():
        for b in range(N_SEQ):
            kk_ref[b, 0:BLOCK, :] = kmeta_ref[...].astype(BF16)
            vv_ref[b, 0:BLOCK, :] = vmeta_ref[...].astype(BF16)
        for q in range(N_QUARTERS):
            st_ref[:, 2 * q * Q_STATE:(2 * q + 1) * Q_STATE] = h0r_ref[:, q * Q_STATE:(q + 1) * Q_STATE]
            st_ref[:, (2 * q + 1) * Q_STATE:(2 * q + 2) * Q_STATE] = h0i_ref[:, q * Q_STATE:(q + 1) * Q_STATE]

    xn = _rms(x_ref[...].reshape(ROWS, D_MODEL), gmix_ref[...]).astype(BF16)

    def proj(c0):
        return jnp.dot(xn, win_ref[:, c0:c0 + 2 * LANES], preferred_element_type=F32)

    for c in range(Q_TILES // 2):
        zz = proj(c * 2 * LANES)
        for tt in range(2):
            t = 2 * c + tt
            qn = _pair_norm(zz[:, tt * LANES:(tt + 1) * LANES], gq_ref[...], left) * ATTN_SCALE
            qa = jnp.where(left, qn, 0.0).astype(BF16)
            qb = jnp.where(left, 0.0, qn).astype(BF16)
            for b in range(N_SEQ):
                qs_ref[b, 2 * t * BLOCK:(2 * t + 1) * BLOCK, :] = qa[b * BLOCK:(b + 1) * BLOCK]
                qs_ref[b, (2 * t + 1) * BLOCK:(2 * t + 2) * BLOCK, :] = qb[b * BLOCK:(b + 1) * BLOCK]
    zz = proj(ATTN_WIDTH)
    kn = _pair_norm(zz[:, :LANES], gk_ref[...], left)
    vn = zz[:, LANES:]
    kw_ref[...] = kn.reshape(N_SEQ, BLOCK, KV_WIDTH)
    vw_ref[...] = vn.reshape(N_SEQ, BLOCK, KV_WIDTH)
    kk_ref[:, BLOCK:, :] = kn.astype(BF16).reshape(N_SEQ, BLOCK, KV_WIDTH)
    vv_ref[:, BLOCK:, :] = vn.astype(BF16).reshape(N_SEQ, BLOCK, KV_WIDTH)
    for c in range(2):
        zz = proj(ATTN_WIDTH + 2 * KV_WIDTH + c * 2 * LANES)
        for tt in range(2):
            for b in range(N_SEQ):
                usl_ref[2 * c + tt, b * PITCH:b * PITCH + BLOCK, :] = zz[b * BLOCK:(b + 1) * BLOCK, tt * LANES:(tt + 1) * LANES]

    r = lax.broadcasted_iota(jnp.int32, (BLOCK, 2 * BLOCK), 0)
    c = lax.broadcasted_iota(jnp.int32, (BLOCK, 2 * BLOCK), 1)
    valid = (c >= r) & (c <= r + WINDOW) & ((i > 0) | (c >= BLOCK - N_META))
    bias = jnp.where(valid, 0.0, -jnp.inf)

    def attention(b):
        s_all = lax.dot_general(qs_ref[b], kk_ref[b], (((1,), (1,)), ((), ())), preferred_element_type=F32)
        inv = []
        for g in range(2 * Q_TILES):
            sink = sink_ref[g // 2 + Q_PER_KV * (g % 2)]
            s = s_all[g * BLOCK:(g + 1) * BLOCK] + bias
            m = jnp.maximum(jnp.max(s, axis=-1, keepdims=True), sink)
            p = jnp.exp(s - m)
            inv.append(1.0 / (jnp.sum(p, axis=-1, keepdims=True) + jnp.exp(sink - m)))
            p_ref[g * BLOCK:(g + 1) * BLOCK, :] = p.astype(BF16)
        o_all = jnp.dot(p_ref[...], vv_ref[b], preferred_element_type=F32)
        for t in range(Q_TILES):
            oa = o_all[2 * t * BLOCK:(2 * t + 1) * BLOCK] * inv[2 * t]
            ob = o_all[(2 * t + 1) * BLOCK:(2 * t + 2) * BLOCK] * inv[2 * t + 1]
            oatt_ref[b, :, t * LANES:(t + 1) * LANES] = jnp.where(left, oa, ob)

    def ssm_chunk(n):
        t0 = n * CHUNK_T
        u_c = jnp.concatenate(
            [jnp.concatenate([usl_ref[j, pl.ds(t0 + tl, SUBLANES, stride=PITCH), :] for j in range(N_QUARTERS)], axis=1)
             for tl in range(CHUNK_T)], axis=0)
        for q in range(N_QUARTERS):
            uq = u_c[:, q * LANES:(q + 1) * LANES]
            xs = jnp.dot(uq.astype(BF16), wq_ref[q], preferred_element_type=F32)
            lr = jnp.broadcast_to(lr_ref[:, q * Q_STATE:(q + 1) * Q_STATE], (SUBLANES, Q_STATE))
            li = jnp.broadcast_to(li_ref[:, q * Q_STATE:(q + 1) * Q_STATE], (SUBLANES, Q_STATE))
            hr = st_ref[:, 2 * q * Q_STATE:(2 * q + 1) * Q_STATE]
            hi = st_ref[:, (2 * q + 1) * Q_STATE:(2 * q + 2) * Q_STATE]
            hs_r, hs_i = [], []
            for tl in range(CHUNK_T):
                xr = xs[tl * SUBLANES:(tl + 1) * SUBLANES, 0:Q_STATE]
                xi = xs[tl * SUBLANES:(tl + 1) * SUBLANES, Q_STATE:]
                hr, hi = lr * hr - li * hi + xr, lr * hi + li * hr + xi
                hs_r.append(hr)
                hs_i.append(hi)
            st_ref[:, 2 * q * Q_STATE:(2 * q + 1) * Q_STATE] = hr
            st_ref[:, (2 * q + 1) * Q_STATE:(2 * q + 2) * Q_STATE] = hi
            h = jnp.concatenate([jnp.concatenate(hs_r, axis=0), jnp.concatenate(hs_i, axis=0)], axis=1)
            y = jnp.dot(h.astype(BF16), cq_ref[q], preferred_element_type=F32)
            y = y + d_ref[:, q * LANES:(q + 1) * LANES] * uq
            g = jax.nn.gelu(y)
            gate = jnp.dot(g.astype(BF16), wglu_ref[q], preferred_element_type=F32) + bglu_ref[:, q * LANES:(q + 1) * LANES]
            o = g * jax.nn.sigmoid(gate)
            for tl in range(CHUNK_T):
                osl_ref[q, pl.ds(t0 + tl, SUBLANES, stride=PITCH), :] = o[tl * SUBLANES:(tl + 1) * SUBLANES]

    def pair_body(n, carry):
        attention(n)
        ssm_chunk(n)
        return carry

    lax.fori_loop(0, N_SEQ, pair_body, 0)
    kk_ref[:, 0:BLOCK, :] = kk_ref[:, BLOCK:, :]
    vv_ref[:, 0:BLOCK, :] = vv_ref[:, BLOCK:, :]

    hr_ref[...] = jnp.concatenate([st_ref[:, 2 * q * Q_STATE:(2 * q + 1) * Q_STATE] for q in range(N_QUARTERS)], axis=1)
    hi_ref[...] = jnp.concatenate([st_ref[:, (2 * q + 1) * Q_STATE:(2 * q + 2) * Q_STATE] for q in range(N_QUARTERS)], axis=1)

    for b in range(N_SEQ):
        mix_ref[b, :, 0:ATTN_WIDTH] = _rms(oatt_ref[b], ga_ref[...]).astype(BF16)
        sl = [osl_ref[j, b * PITCH:b * PITCH + BLOCK, :] for j in range(N_QUARTERS)]
        ms = sum(jnp.sum(s * s, axis=-1, keepdims=True) for s in sl) / SSM_WIDTH
        inv = lax.rsqrt(ms + EPS)
        for j in range(N_QUARTERS):
            mix_ref[b, :, ATTN_WIDTH + j * LANES:ATTN_WIDTH + (j + 1) * LANES] = (
                sl[j] * inv * gs_ref[:, j * LANES:(j + 1) * LANES]).astype(BF16)


def _mixer(x_prompt, sinks_perm, gmix, win_perm, gq2, gk2, kmeta_blk, vmeta_blk, h0r, h0i, sp, ga_perm, gs):
    nseq, seq, _ = x_prompt.shape
    assert nseq == N_SEQ and seq % BLOCK == 0
    st = pl.BlockSpec((N_SEQ, STATE_COLS), lambda i: (0, 0))
    kvw = pl.BlockSpec((N_SEQ, BLOCK, KV_WIDTH), lambda i: (0, 0, 0))
    return pl.pallas_call(
        _mixer_kernel,
        grid=(seq // BLOCK,),
        in_specs=[pl.BlockSpec(memory_space=pltpu.SMEM),
                  pl.BlockSpec((N_SEQ, BLOCK, D_MODEL), lambda i: (0, i, 0)),
                  _const_spec((1, D_MODEL)), _const_spec((D_MODEL, IN_COLS)), _const_spec((1, LANES)),
                  _const_spec((1, LANES)), _const_spec((BLOCK, KV_WIDTH)), _const_spec((BLOCK, KV_WIDTH)), st, st,
                  _const_spec((N_QUARTERS, LANES, 2 * Q_STATE)), _const_spec((1, STATE_COLS)),
                  _const_spec((1, STATE_COLS)), _const_spec((N_QUARTERS, 2 * Q_STATE, LANES)),
                  _const_spec((1, SSM_WIDTH)), _const_spec((N_QUARTERS, LANES, LANES)), _const_spec((1, SSM_WIDTH)),
                  _const_spec((1, ATTN_WIDTH)), _const_spec((1, SSM_WIDTH))],
        out_specs=[pl.BlockSpec((N_SEQ, BLOCK, D_MODEL), lambda i: (0, i, 0)), kvw, kvw, st, st],
        out_shape=[jax.ShapeDtypeStruct((N_SEQ, seq, D_MODEL), BF16),
                   jax.ShapeDtypeStruct((N_SEQ, BLOCK, KV_WIDTH), F32), jax.ShapeDtypeStruct((N_SEQ, BLOCK, KV_WIDTH), F32),
                   jax.ShapeDtypeStruct((N_SEQ, STATE_COLS), F32), jax.ShapeDtypeStruct((N_SEQ, STATE_COLS), F32)],
        scratch_shapes=[pltpu.VMEM((N_SEQ, 2 * Q_TILES * BLOCK, LANES), BF16),
                        pltpu.VMEM((N_SEQ, 2 * BLOCK, KV_WIDTH), BF16),
                        pltpu.VMEM((N_SEQ, 2 * BLOCK, KV_WIDTH), BF16),
                        pltpu.VMEM((2 * Q_TILES * BLOCK, 2 * BLOCK), BF16),
                        pltpu.VMEM((N_SEQ, BLOCK, ATTN_WIDTH), F32),
                        pltpu.VMEM((N_QUARTERS, N_SEQ * PITCH, LANES), F32),
                        pltpu.VMEM((N_SEQ, 2 * STATE_COLS), F32),
                        pltpu.VMEM((N_QUARTERS, N_SEQ * PITCH, LANES), F32)],
        compiler_params=pltpu.CompilerParams(dimension_semantics=("arbitrary",), vmem_limit_bytes=VMEM_LIMIT),
        name="mixer",
    )(sinks_perm, x_prompt, gmix, win_perm, gq2, gk2, kmeta_blk, vmeta_blk, h0r, h0i, sp["wq"], sp["lr"], sp["li"],
      sp["cq"], sp["d"], sp["wgluq"], sp["bglu"], ga_perm, gs)


def kernel(x_prompt, x_sample, cache_k_win, cache_v_win, state_ssm_re, state_ssm_im, meta_tokens, g_mix, w_in, g_q,
           g_k, sinks, ssm_a_re, ssm_a_im, ssm_log_dt, ssm_b_re, ssm_b_im, ssm_c_re, ssm_c_im, ssm_d, ssm_w_glu,
           ssm_b_glu, g_att_out, g_ssm_out, w_out, g_ffn, w_gate, w_up, w_down):
    bp, seq, _ = x_prompt.shape
    db, dseq, _ = x_sample.shape
    li = 0
    gmix = g_mix[li].reshape(1, D_MODEL)
    win = w_in[li].astype(BF16)
    gq2 = jnp.tile(g_q[li], 2).reshape(1, LANES)
    gk2 = jnp.tile(g_k[li], 2).reshape(1, LANES)
    sk = sinks[li]
    sp = _ssm_params(ssm_a_re[li], ssm_a_im[li], ssm_log_dt[li], ssm_b_re[li], ssm_b_im[li], ssm_c_re[li],
                     ssm_c_im[li], ssm_d[li], ssm_w_glu[li], ssm_b_glu[li])
    ga = g_att_out[li].reshape(1, ATTN_WIDTH)
    gs = g_ssm_out[li].reshape(1, SSM_WIDTH)
    wout = w_out[li].astype(BF16)
    ffn_w = (g_ffn[li].reshape(1, D_MODEL), w_gate[li].astype(BF16), w_up[li].astype(BF16), w_down[li].astype(BF16))
    perm = jnp.concatenate([jnp.arange(HEAD_DIM) + HEAD_DIM * (t + Q_PER_KV * s) for t in range(Q_TILES) for s in range(2)])
    win_perm = jnp.concatenate([win[:, perm], win[:, ATTN_WIDTH:]], axis=1)
    wout_perm = jnp.concatenate([wout[perm], wout[ATTN_WIDTH:]], axis=0)
    ga_perm = ga[:, perm]

    xs_rows = x_sample.reshape(db * dseq, D_MODEL)
    meta_rows = jnp.tile(meta_tokens, (SUBLANES, 1))
    small = jnp.concatenate([xs_rows, meta_rows], axis=0)
    q_s, k_s, v_s, u_s = _front(small, gmix, win, gq2, gk2, tm=128)
    n_s = db * dseq
    lead = jnp.zeros((BLOCK - N_META, KV_WIDTH), F32)
    kmeta_blk = jnp.concatenate([lead, k_s[n_s:n_s + N_META]], axis=0)
    vmeta_blk = jnp.concatenate([lead, v_s[n_s:n_s + N_META]], axis=0)
    u_meta = u_s[n_s:].reshape(SUBLANES, N_META, SSM_WIDTH).transpose(1, 0, 2)
    zero_state = jnp.zeros((SUBLANES, STATE_COLS), F32)
    _, hm_r, hm_i = _ssm(u_meta, zero_state, zero_state, sp, tt=N_META)

    mix, kw_p, vw_p, hp_r, hp_i = _mixer(x_prompt, sk, gmix, win_perm, gq2, gk2, kmeta_blk, vmeta_blk, hm_r, hm_i, sp,
                                         ga_perm, gs)
    y_prompt = _ffn_call(x_prompt.reshape(bp * seq, D_MODEL), mix.reshape(bp * seq, D_MODEL), wout_perm, *ffn_w,
                         tm=512).reshape(bp, seq, D_MODEL)

    ck = cache_k_win[li].reshape(db, WINDOW, KV_WIDTH)
    cv = cache_v_win[li].reshape(db, WINDOW, KV_WIDTH)
    oa_s, kw_s, vw_s = _attn_sample(q_s[:n_s].reshape(db, dseq, ATTN_WIDTH), k_s[:n_s].reshape(db, dseq, KV_WIDTH),
                                    v_s[:n_s].reshape(db, dseq, KV_WIDTH), ck, cv, sk)
    us_tb = u_s[:n_s].reshape(db, dseq, SSM_WIDTH).transpose(1, 0, 2)
    os_s_tb, hs_r, hs_i = _ssm(us_tb, state_ssm_re[li].reshape(db, STATE_COLS),
                               state_ssm_im[li].reshape(db, STATE_COLS), sp, tt=dseq)
    os_s = os_s_tb.transpose(1, 0, 2).reshape(n_s, SSM_WIDTH)
    y_sample = _back(xs_rows, oa_s.reshape(n_s, ATTN_WIDTH), os_s, ga, gs, wout, *ffn_w, tm=512).reshape(db, dseq, D_MODEL)

    kv5 = lambda a, n: a.reshape(1, n, WINDOW, N_KV_HEADS, HEAD_DIM)
    st4 = lambda a, n: a.reshape(1, n, N_SSM_GROUPS, SSM_STATE)
    return (y_prompt, y_sample, kv5(kw_p, bp), kv5(vw_p, bp), st4(hp_r, bp), st4(hp_i, bp),
            kv5(kw_s, db), kv5(vw_s, db), st4(hs_r, db), st4(hs_i, db))
```

```python
import functools

import jax
import jax.numpy as jnp
from jax import lax
from jax.experimental import pallas as pl
from jax.experimental.pallas import tpu as pltpu

D_MODEL = 1024
N_META = 16
HEAD_DIM = 64
ATTN_WIDTH = 512
N_Q_HEADS = 8
Q_PER_KV = 4
N_KV_HEADS = 2
KV_WIDTH = 128
WINDOW = 128
BLOCK = 128
SSM_WIDTH = 512
SSM_GROUP = 16
N_SSM_GROUPS = 32
SSM_STATE = 64
IN_COLS = ATTN_WIDTH + 2 * KV_WIDTH + SSM_WIDTH
D_FF = 2816
EPS = 1e-6
ATTN_SCALE = HEAD_DIM ** -0.5
STATE_COLS = N_SSM_GROUPS * SSM_STATE
HALF_STATE = STATE_COLS // 2
LANES = 128
SUBLANES = 8
VMEM_LIMIT = 56 * 1024 * 1024

F32 = jnp.float32
BF16 = jnp.bfloat16


def _const_spec(shape):
    return pl.BlockSpec(shape, lambda *_: (0,) * len(shape), pipeline_mode=pl.Buffered(1))


def _rms(x, g):
    return x * lax.rsqrt(jnp.mean(x * x, axis=-1, keepdims=True) + EPS) * g


def _left_half():
    return lax.broadcasted_iota(jnp.int32, (1, LANES), 1) < HEAD_DIM


def _pair_norm(zz, g2, left):
    sq = zz * zz
    sl = jnp.sum(jnp.where(left, sq, 0.0), axis=-1, keepdims=True)
    sr = jnp.sum(jnp.where(left, 0.0, sq), axis=-1, keepdims=True)
    inv = jnp.where(left, lax.rsqrt(sl / HEAD_DIM + EPS), lax.rsqrt(sr / HEAD_DIM + EPS))
    return zz * inv * g2


def _front_kernel(x_ref, gmix_ref, win_ref, gq_ref, gk_ref, q_ref, k_ref, v_ref, u_ref):
    xn = _rms(x_ref[...], gmix_ref[...]).astype(BF16)
    z = jnp.dot(xn, win_ref[...], preferred_element_type=F32)
    left = _left_half()
    for p in range(ATTN_WIDTH // LANES):
        q_ref[:, p * LANES:(p + 1) * LANES] = _pair_norm(z[:, p * LANES:(p + 1) * LANES], gq_ref[...], left)
    k_ref[...] = _pair_norm(z[:, ATTN_WIDTH:ATTN_WIDTH + KV_WIDTH], gk_ref[...], left)
    v_ref[...] = z[:, ATTN_WIDTH + KV_WIDTH:ATTN_WIDTH + 2 * KV_WIDTH]
    u_ref[...] = z[:, ATTN_WIDTH + 2 * KV_WIDTH:]


def _front(x_rows, gmix, win_bf, gq2, gk2, tm):
    n = x_rows.shape[0]
    assert n % tm == 0
    row = lambda w: pl.BlockSpec((tm, w), lambda i: (i, 0))
    return pl.pallas_call(
        _front_kernel,
        grid=(n // tm,),
        in_specs=[row(D_MODEL), _const_spec((1, D_MODEL)), _const_spec((D_MODEL, IN_COLS)),
                  _const_spec((1, LANES)), _const_spec((1, LANES))],
        out_specs=[row(ATTN_WIDTH), row(KV_WIDTH), row(KV_WIDTH), row(SSM_WIDTH)],
        out_shape=[jax.ShapeDtypeStruct((n, ATTN_WIDTH), F32), jax.ShapeDtypeStruct((n, KV_WIDTH), F32),
                   jax.ShapeDtypeStruct((n, KV_WIDTH), F32), jax.ShapeDtypeStruct((n, SSM_WIDTH), F32)],
        compiler_params=pltpu.CompilerParams(dimension_semantics=("arbitrary",), vmem_limit_bytes=VMEM_LIMIT),
        name="front",
    )(x_rows, gmix, win_bf, gq2, gk2)


def _sink_softmax_pv(s, mask, sink, vg):
    s = jnp.where(mask, s * ATTN_SCALE, -jnp.inf)
    m = jnp.maximum(jnp.max(s, axis=-1, keepdims=True), sink)
    p = jnp.exp(s - m)
    denom = jnp.sum(p, axis=-1, keepdims=True) + jnp.exp(sink - m)
    o = jnp.dot(p.astype(BF16), vg, preferred_element_type=F32)
    return o / denom


def _attn_sample_kernel(sink_ref, q_ref, kn_ref, vn_ref, ck_ref, cv_ref, o_ref, kw_ref, vw_ref, *, bb, t):
    rows = Q_PER_KV * t
    tk = WINDOW + t
    r = lax.broadcasted_iota(jnp.int32, (rows, tk), 0) % t
    c = lax.broadcasted_iota(jnp.int32, (rows, tk), 1)
    mask = (c >= r) & (c <= r + WINDOW)
    hrow = lax.broadcasted_iota(jnp.int32, (rows, 1), 0) // t
    for bi in range(bb):
        kk = jnp.concatenate([ck_ref[bi], kn_ref[bi]], axis=0)
        vv = jnp.concatenate([cv_ref[bi], vn_ref[bi]], axis=0)
        kw_ref[bi] = kk[t:]
        vw_ref[bi] = vv[t:]
        kkb = kk.astype(BF16)
        vvb = vv.astype(BF16)
        q = q_ref[bi]
        for g in range(N_KV_HEADS):
            heads = range(g * Q_PER_KV, (g + 1) * Q_PER_KV)
            q4 = jnp.concatenate([q[:, h * HEAD_DIM:(h + 1) * HEAD_DIM] for h in heads], axis=0).astype(BF16)
            sink = jnp.zeros((rows, 1), F32)
            for i, h in enumerate(heads):
                sink = jnp.where(hrow == i, sink_ref[h], sink)
            kg = kkb[:, g * HEAD_DIM:(g + 1) * HEAD_DIM]
            vg = vvb[:, g * HEAD_DIM:(g + 1) * HEAD_DIM]
            s = lax.dot_general(q4, kg, (((1,), (1,)), ((), ())), preferred_element_type=F32)
            o4 = _sink_softmax_pv(s, mask, sink, vg)
            for i, h in enumerate(heads):
                o_ref[bi, :, h * HEAD_DIM:(h + 1) * HEAD_DIM] = o4[i * t:(i + 1) * t]


def _attn_sample(q, kn, vn, ck, cv, sinks, bb=8):
    db, t, _ = q.shape
    blk = lambda r, w: pl.BlockSpec((bb, r, w), lambda i: (i, 0, 0))
    return pl.pallas_call(
        functools.partial(_attn_sample_kernel, bb=bb, t=t),
        grid=(db // bb,),
        in_specs=[pl.BlockSpec(memory_space=pltpu.SMEM), blk(t, ATTN_WIDTH), blk(t, KV_WIDTH), blk(t, KV_WIDTH),
                  blk(WINDOW, KV_WIDTH), blk(WINDOW, KV_WIDTH)],
        out_specs=[blk(t, ATTN_WIDTH), blk(WINDOW, KV_WIDTH), blk(WINDOW, KV_WIDTH)],
        out_shape=[jax.ShapeDtypeStruct((db, t, ATTN_WIDTH), F32),
                   jax.ShapeDtypeStruct((db, WINDOW, KV_WIDTH), F32),
                   jax.ShapeDtypeStruct((db, WINDOW, KV_WIDTH), F32)],
        compiler_params=pltpu.CompilerParams(dimension_semantics=("arbitrary",)),
        name="attn_sample",
    )(sinks, q, kn, vn, ck, cv)


SCAN_COLS = 512


def _scan_rows(xs_ref, st_ref, lr, li, *, tt, nb, r0, re0, im0, st_re0, st_im0):
    def body(t, carry):
        hr, hi = carry
        row = pl.multiple_of(t * nb + r0, SUBLANES)
        xr = xs_ref[pl.ds(row, SUBLANES), re0:re0 + SCAN_COLS]
        xi = xs_ref[pl.ds(row, SUBLANES), im0:im0 + SCAN_COLS]
        nhr = lr * hr - li * hi + xr
        nhi = lr * hi + li * hr + xi
        xs_ref[pl.ds(row, SUBLANES), re0:re0 + SCAN_COLS] = nhr
        xs_ref[pl.ds(row, SUBLANES), im0:im0 + SCAN_COLS] = nhi
        return nhr, nhi

    hr, hi = lax.fori_loop(
        0, tt, body,
        (st_ref[r0:r0 + SUBLANES, st_re0:st_re0 + SCAN_COLS], st_ref[r0:r0 + SUBLANES, st_im0:st_im0 + SCAN_COLS]),
        unroll=min(tt, 8))
    st_ref[r0:r0 + SUBLANES, st_re0:st_re0 + SCAN_COLS] = hr
    st_ref[r0:r0 + SUBLANES, st_im0:st_im0 + SCAN_COLS] = hi


def _ssm_kernel(u_ref, h0r_ref, h0i_ref, wlo_ref, whi_ref, lr_ref, li_ref, clo_ref, chi_ref, d_ref, wglu_ref,
                bglu_ref, o_ref, hr_ref, hi_ref, xs_ref, st_ref, *, tt, nb):
    rows = tt * nb

    @pl.when(pl.program_id(0) == 0)
    def _():
        for half in range(2):
            st_ref[:, (2 * half) * HALF_STATE:(2 * half + 1) * HALF_STATE] = h0r_ref[:, half * HALF_STATE:(half + 1) * HALF_STATE]
            st_ref[:, (2 * half + 1) * HALF_STATE:(2 * half + 2) * HALF_STATE] = h0i_ref[:, half * HALF_STATE:(half + 1) * HALF_STATE]

    u = u_ref[...].reshape(rows, SSM_WIDTH)
    ub = u.astype(BF16)
    half_in = SSM_WIDTH // 2
    xs_ref[:, 0:2 * HALF_STATE] = jnp.dot(ub[:, :half_in], wlo_ref[...], preferred_element_type=F32)
    xs_ref[:, 2 * HALF_STATE:] = jnp.dot(ub[:, half_in:], whi_ref[...], preferred_element_type=F32)

    for rg in range(nb // SUBLANES):
        for half in range(2):
            for cc in range(0, HALF_STATE, SCAN_COLS):
                re0 = 2 * half * HALF_STATE + cc
                lam0 = half * HALF_STATE + cc
                lr = jnp.broadcast_to(lr_ref[:, lam0:lam0 + SCAN_COLS], (SUBLANES, SCAN_COLS))
                li = jnp.broadcast_to(li_ref[:, lam0:lam0 + SCAN_COLS], (SUBLANES, SCAN_COLS))
                _scan_rows(xs_ref, st_ref, lr, li, tt=tt, nb=nb, r0=rg * SUBLANES, re0=re0, im0=re0 + HALF_STATE,
                           st_re0=re0, st_im0=re0 + HALF_STATE)

    y_lo = jnp.dot(xs_ref[:, 0:2 * HALF_STATE].astype(BF16), clo_ref[...], preferred_element_type=F32)
    y_hi = jnp.dot(xs_ref[:, 2 * HALF_STATE:].astype(BF16), chi_ref[...], preferred_element_type=F32)
    y = jnp.concatenate([y_lo, y_hi], axis=1) + d_ref[...] * u
    g = jax.nn.gelu(y)
    gate = jnp.dot(g.astype(BF16), wglu_ref[...], preferred_element_type=F32) + bglu_ref[...]
    o_ref[...] = (g * jax.nn.sigmoid(gate)).reshape(tt, nb, SSM_WIDTH)
    hr_ref[...] = jnp.concatenate([st_ref[:, 0:HALF_STATE], st_ref[:, 2 * HALF_STATE:3 * HALF_STATE]], axis=1)
    hi_ref[...] = jnp.concatenate([st_ref[:, HALF_STATE:2 * HALF_STATE], st_ref[:, 3 * HALF_STATE:]], axis=1)


def _ssm(u_tb, h0r, h0i, sp, tt):
    t_total, nb, _ = u_tb.shape
    assert t_total % tt == 0 and nb % SUBLANES == 0
    rows = tt * nb
    st = pl.BlockSpec((nb, STATE_COLS), lambda i: (0, 0))
    return pl.pallas_call(
        functools.partial(_ssm_kernel, tt=tt, nb=nb),
        grid=(t_total // tt,),
        in_specs=[pl.BlockSpec((tt, nb, SSM_WIDTH), lambda i: (i, 0, 0)), st, st,
                  _const_spec((SSM_WIDTH // 2, 2 * HALF_STATE)), _const_spec((SSM_WIDTH // 2, 2 * HALF_STATE)),
                  _const_spec((1, STATE_COLS)), _const_spec((1, STATE_COLS)),
                  _const_spec((2 * HALF_STATE, SSM_WIDTH // 2)), _const_spec((2 * HALF_STATE, SSM_WIDTH // 2)),
                  _const_spec((1, SSM_WIDTH)), _const_spec((SSM_WIDTH, SSM_WIDTH)), _const_spec((1, SSM_WIDTH))],
        out_specs=[pl.BlockSpec((tt, nb, SSM_WIDTH), lambda i: (i, 0, 0)), st, st],
        out_shape=[jax.ShapeDtypeStruct((t_total, nb, SSM_WIDTH), F32),
                   jax.ShapeDtypeStruct((nb, STATE_COLS), F32), jax.ShapeDtypeStruct((nb, STATE_COLS), F32)],
        scratch_shapes=[pltpu.VMEM((rows, 2 * STATE_COLS), F32), pltpu.VMEM((nb, 2 * STATE_COLS), F32)],
        compiler_params=pltpu.CompilerParams(dimension_semantics=("arbitrary",), vmem_limit_bytes=VMEM_LIMIT),
        name="ssm",
    )(u_tb, h0r, h0i, sp["wlo"], sp["whi"], sp["lr"], sp["li"], sp["clo"], sp["chi"], sp["d"], sp["wglu"], sp["bglu"])


def _block_diag(blocks):
    n, a, b = blocks.shape
    eye = jnp.eye(n, dtype=blocks.dtype)
    return jnp.einsum('nab,nm->namb', blocks, eye).reshape(n * a, n * b)


N_QUARTERS = 4
Q_GROUPS = N_SSM_GROUPS // N_QUARTERS
Q_STATE = Q_GROUPS * SSM_STATE


def _ssm_params(a_re, a_im, log_dt, b_re, b_im, c_re, c_im, d_skip, w_glu, b_glu):
    dt = jnp.exp(log_dt)[:, None]
    mag = jnp.exp(a_re * dt)
    lr, li = mag * jnp.cos(a_im * dt), mag * jnp.sin(a_im * dt)
    nr, ni = lr - 1.0, li
    den = a_re * a_re + a_im * a_im
    fr, fi = (nr * a_re + ni * a_im) / den, (ni * a_re - nr * a_im) / den
    bbr = fr[..., None] * b_re - fi[..., None] * b_im
    bbi = fr[..., None] * b_im + fi[..., None] * b_re

    def win(lo, n):
        sl = slice(lo, lo + n)
        return jnp.concatenate([_block_diag(jnp.swapaxes(bbr[sl], 1, 2)), _block_diag(jnp.swapaxes(bbi[sl], 1, 2))],
                               axis=1).astype(BF16)

    def cmat(lo, n):
        sl = slice(lo, lo + n)
        return jnp.concatenate([_block_diag(jnp.swapaxes(c_re[sl], 1, 2)), -_block_diag(jnp.swapaxes(c_im[sl], 1, 2))],
                               axis=0).astype(BF16)

    hg = N_SSM_GROUPS // 2
    quarters = [q * Q_GROUPS for q in range(N_QUARTERS)]
    return dict(wlo=win(0, hg), whi=win(hg, hg), lr=lr.reshape(1, STATE_COLS), li=li.reshape(1, STATE_COLS),
                clo=cmat(0, hg), chi=cmat(hg, hg), d=d_skip.reshape(1, SSM_WIDTH),
                wglu=_block_diag(w_glu).astype(BF16), bglu=b_glu.reshape(1, SSM_WIDTH),
                wq=jnp.stack([win(q, Q_GROUPS) for q in quarters]),
                cq=jnp.stack([cmat(q, Q_GROUPS) for q in quarters]),
                wgluq=jnp.stack([_block_diag(w_glu[q:q + Q_GROUPS]) for q in quarters]).astype(BF16))


def _ffn(h, gf_ref, wg_ref, wu_ref, wd_ref):
    f = _rms(h, gf_ref[...]).astype(BF16)
    gate = jnp.dot(f, wg_ref[...], preferred_element_type=F32)
    up = jnp.dot(f, wu_ref[...], preferred_element_type=F32)
    a = (jax.nn.silu(gate) * up).astype(BF16)
    return h + jnp.dot(a, wd_ref[...], preferred_element_type=F32)


def _back_kernel(x_ref, oa_ref, os_ref, ga_ref, gs_ref, wout_ref, gf_ref, wg_ref, wu_ref, wd_ref, y_ref):
    mix = jnp.concatenate([_rms(oa_ref[...], ga_ref[...]), _rms(os_ref[...], gs_ref[...])], axis=-1).astype(BF16)
    h = x_ref[...] + jnp.dot(mix, wout_ref[...], preferred_element_type=F32)
    y_ref[...] = _ffn(h, gf_ref, wg_ref, wu_ref, wd_ref)


def _ffn_kernel(x_ref, mix_ref, wout_ref, gf_ref, wg_ref, wu_ref, wd_ref, y_ref):
    h = x_ref[...] + jnp.dot(mix_ref[...], wout_ref[...], preferred_element_type=F32)
    y_ref[...] = _ffn(h, gf_ref, wg_ref, wu_ref, wd_ref)


def _back(x_rows, oa, os_, ga, gs, wout, gf, wg, wu, wd, tm):
    n = x_rows.shape[0]
    assert n % tm == 0
    row = lambda w: pl.BlockSpec((tm, w), lambda i: (i, 0))
    return pl.pallas_call(
        _back_kernel,
        grid=(n // tm,),
        in_specs=[row(D_MODEL), row(ATTN_WIDTH), row(SSM_WIDTH), _const_spec((1, ATTN_WIDTH)),
                  _const_spec((1, SSM_WIDTH)), _const_spec((D_MODEL, D_MODEL)), _const_spec((1, D_MODEL)),
                  _const_spec((D_MODEL, D_FF)), _const_spec((D_MODEL, D_FF)), _const_spec((D_FF, D_MODEL))],
        out_specs=row(D_MODEL),
        out_shape=jax.ShapeDtypeStruct((n, D_MODEL), F32),
        compiler_params=pltpu.CompilerParams(dimension_semantics=("arbitrary",), vmem_limit_bytes=VMEM_LIMIT),
        name="back",
    )(x_rows, oa, os_, ga, gs, wout, gf, wg, wu, wd)


def _ffn_call(x_rows, mix, wout, gf, wg, wu, wd, tm):
    n = x_rows.shape[0]
    assert n % tm == 0
    row = lambda w: pl.BlockSpec((tm, w), lambda i: (i, 0))
    return pl.pallas_call(
        _ffn_kernel,
        grid=(n // tm,),
        in_specs=[row(D_MODEL), row(D_MODEL), _const_spec((D_MODEL, D_MODEL)), _const_spec((1, D_MODEL)),
                  _const_spec((D_MODEL, D_FF)), _const_spec((D_MODEL, D_FF)), _const_spec((D_FF, D_MODEL))],
        out_specs=row(D_MODEL),
        out_shape=jax.ShapeDtypeStruct((n, D_MODEL), F32),
        compiler_params=pltpu.CompilerParams(dimension_semantics=("arbitrary",), vmem_limit_bytes=VMEM_LIMIT),
        name="ffn",
    )(x_rows, mix, wout, gf, wg, wu, wd)


N_SEQ = 8
ROWS = N_SEQ * BLOCK
Q_TILES = ATTN_WIDTH // LANES
PITCH = BLOCK + SUBLANES
CHUNK_T = BLOCK // N_SEQ


def _mixer_kernel(sink_ref, x_ref, gmix_ref, win_ref, gq_ref, gk_ref, kmeta_ref, vmeta_ref, h0r_ref, h0i_ref,
                  wq_ref, lr_ref, li_ref, cq_ref, d_ref, wglu_ref, bglu_ref, ga_ref, gs_ref,
                  mix_ref, kw_ref, vw_ref, hr_ref, hi_ref,
                  qs_ref, kk_ref, vv_ref, s_ref, p_ref, oatt_ref, usl_ref, xs_ref, hb_ref, st_ref, osl_ref):
    i = pl.program_id(0)
    left = _left_half()

    @pl.when(i == 0)
    def _():
        for b in range(N_SEQ):
            kk_ref[b, 0:BLOCK, :] = kmeta_ref[...].astype(BF16)
            vv_ref[b, 0:BLOCK, :] = vmeta_ref[...].astype(BF16)
        for q in range(N_QUARTERS):
            st_ref[:, 2 * q * Q_STATE:(2 * q + 1) * Q_STATE] = h0r_ref[:, q * Q_STATE:(q + 1) * Q_STATE]
            st_ref[:, (2 * q + 1) * Q_STATE:(2 * q + 2) * Q_STATE] = h0i_ref[:, q * Q_STATE:(q + 1) * Q_STATE]

    xn = _rms(x_ref[...].reshape(ROWS, D_MODEL), gmix_ref[...]).astype(BF16)

    def proj(c0):
        return jnp.dot(xn, win_ref[:, c0:c0 + 2 * LANES], preferred_element_type=F32)

    for c in range(Q_TILES // 2):
        zz = proj(c * 2 * LANES)
        for tt in range(2):
            t = 2 * c + tt
            qn = _pair_norm(zz[:, tt * LANES:(tt + 1) * LANES], gq_ref[...], left) * ATTN_SCALE
            qa = jnp.where(left, qn, 0.0).astype(BF16)
            qb = jnp.where(left, 0.0, qn).astype(BF16)
            for b in range(N_SEQ):
                qs_ref[b, 2 * t * BLOCK:(2 * t + 1) * BLOCK, :] = qa[b * BLOCK:(b + 1) * BLOCK]
                qs_ref[b, (2 * t + 1) * BLOCK:(2 * t + 2) * BLOCK, :] = qb[b * BLOCK:(b + 1) * BLOCK]
    zz = proj(ATTN_WIDTH)
    kn = _pair_norm(zz[:, :LANES], gk_ref[...], left)
    vn = zz[:, LANES:]
    kw_ref[...] = kn.reshape(N_SEQ, BLOCK, KV_WIDTH)
    vw_ref[...] = vn.reshape(N_SEQ, BLOCK, KV_WIDTH)
    kk_ref[:, BLOCK:, :] = kn.astype(BF16).reshape(N_SEQ, BLOCK, KV_WIDTH)
    vv_ref[:, BLOCK:, :] = vn.astype(BF16).reshape(N_SEQ, BLOCK, KV_WIDTH)
    for c in range(2):
        zz = proj(ATTN_WIDTH + 2 * KV_WIDTH + c * 2 * LANES)
        for tt in range(2):
            for b in range(N_SEQ):
                usl_ref[2 * c + tt, b * PITCH:b * PITCH + BLOCK, :] = zz[b * BLOCK:(b + 1) * BLOCK, tt * LANES:(tt + 1) * LANES]

    r = lax.broadcasted_iota(jnp.int32, (BLOCK, 2 * BLOCK), 0)
    c = lax.broadcasted_iota(jnp.int32, (BLOCK, 2 * BLOCK), 1)
    valid = (c >= r) & (c <= r + WINDOW) & ((i > 0) | (c >= BLOCK - N_META))
    bias = jnp.where(valid, 0.0, -jnp.inf)

    def gather_u(n):
        t0 = n * CHUNK_T
        return jnp.concatenate(
            [jnp.concatenate([usl_ref[j, pl.ds(t0 + tl, SUBLANES, stride=PITCH), :] for j in range(N_QUARTERS)], axis=1)
             for tl in range(CHUNK_T)], axis=0)

    def feed(n, par):
        s_ref[par] = lax.dot_general(qs_ref[n], kk_ref[n], (((1,), (1,)), ((), ())), preferred_element_type=F32)
        u_c = gather_u(n)
        for q in range(N_QUARTERS):
            xs_ref[par, :, 2 * q * Q_STATE:(2 * q + 2) * Q_STATE] = jnp.dot(
                u_c[:, q * LANES:(q + 1) * LANES].astype(BF16), wq_ref[q], preferred_element_type=F32)

    def mid(n, par):
        for g in range(2 * Q_TILES):
            sink = sink_ref[g // 2 + Q_PER_KV * (g % 2)]
            s = s_ref[par, g * BLOCK:(g + 1) * BLOCK, :] + bias
            m = jnp.maximum(jnp.max(s, axis=-1, keepdims=True), sink)
            p = jnp.exp(s - m)
            inv = 1.0 / (jnp.sum(p, axis=-1, keepdims=True) + jnp.exp(sink - m))
            p_ref[par, g * BLOCK:(g + 1) * BLOCK, :] = (p * inv).astype(BF16)
        for q in range(N_QUARTERS):
            lr = jnp.broadcast_to(lr_ref[:, q * Q_STATE:(q + 1) * Q_STATE], (SUBLANES, Q_STATE))
            li = jnp.broadcast_to(li_ref[:, q * Q_STATE:(q + 1) * Q_STATE], (SUBLANES, Q_STATE))
            re0, im0 = 2 * q * Q_STATE, (2 * q + 1) * Q_STATE
            hr = st_ref[:, re0:re0 + Q_STATE]
            hi = st_ref[:, im0:im0 + Q_STATE]
            for tp in range(CHUNK_T // 2):
                pair_r, pair_i = [], []
                for tl in (2 * tp, 2 * tp + 1):
                    xr = xs_ref[par, tl * SUBLANES:(tl + 1) * SUBLANES, re0:re0 + Q_STATE]
                    xi = xs_ref[par, tl * SUBLANES:(tl + 1) * SUBLANES, im0:im0 + Q_STATE]
                    hr, hi = lr * hr - li * hi + xr, lr * hi + li * hr + xi
                    pair_r.append(hr)
                    pair_i.append(hi)
                rows = slice(2 * tp * SUBLANES, (2 * tp + 2) * SUBLANES)
                hb_ref[par, rows, re0:re0 + Q_STATE] = jnp.concatenate(pair_r, axis=0).astype(BF16)
                hb_ref[par, rows, im0:im0 + Q_STATE] = jnp.concatenate(pair_i, axis=0).astype(BF16)
            st_ref[:, re0:re0 + Q_STATE] = hr
            st_ref[:, im0:im0 + Q_STATE] = hi

    def tail(n, par):
        o_all = jnp.dot(p_ref[par], vv_ref[n], preferred_element_type=F32)
        for t in range(Q_TILES):
            oatt_ref[n, :, t * LANES:(t + 1) * LANES] = jnp.where(
                left, o_all[2 * t * BLOCK:(2 * t + 1) * BLOCK], o_all[(2 * t + 1) * BLOCK:(2 * t + 2) * BLOCK])
        u_c = gather_u(n)
        t0 = n * CHUNK_T
        for q in range(N_QUARTERS):
            cols = slice(q * LANES, (q + 1) * LANES)
            y = jnp.dot(hb_ref[par, :, 2 * q * Q_STATE:(2 * q + 2) * Q_STATE], cq_ref[q], preferred_element_type=F32)
            y = y + d_ref[:, cols] * u_c[:, cols]
            g = jax.nn.gelu(y)
            gate = jnp.dot(g.astype(BF16), wglu_ref[q], preferred_element_type=F32) + bglu_ref[:, cols]
            o = g * jax.nn.sigmoid(gate)
            for tl in range(CHUNK_T):
                osl_ref[q, pl.ds(t0 + tl, SUBLANES, stride=PITCH), :] = o[tl * SUBLANES:(tl + 1) * SUBLANES]

    def steady(k, carry):
        n = 2 * k + 1
        feed(n + 1, 0)
        mid(n, 1)
        tail(n - 1, 0)
        feed(n + 2, 1)
        mid(n + 1, 0)
        tail(n, 1)
        return carry

    feed(0, 0)
    feed(1, 1)
    mid(0, 0)
    lax.fori_loop(0, (N_SEQ - 2) // 2, steady, 0)
    mid(N_SEQ - 1, 1)
    tail(N_SEQ - 2, 0)
    tail(N_SEQ - 1, 1)
    kk_ref[:, 0:BLOCK, :] = kk_ref[:, BLOCK:, :]
    vv_ref[:, 0:BLOCK, :] = vv_ref[:, BLOCK:, :]

    hr_ref[...] = jnp.concatenate([st_ref[:, 2 * q * Q_STATE:(2 * q + 1) * Q_STATE] for q in range(N_QUARTERS)], axis=1)
    hi_ref[...] = jnp.concatenate([st_ref[:, (2 * q + 1) * Q_STATE:(2 * q + 2) * Q_STATE] for q in range(N_QUARTERS)], axis=1)

    for b in range(N_SEQ):
        mix_ref[b, :, 0:ATTN_WIDTH] = _rms(oatt_ref[b], ga_ref[...]).astype(BF16)
        sl = [osl_ref[j, b * PITCH:b * PITCH + BLOCK, :] for j in range(N_QUARTERS)]
        ms = sum(jnp.sum(s * s, axis=-1, keepdims=True) for s in sl) / SSM_WIDTH
        inv = lax.rsqrt(ms + EPS)
        for j in range(N_QUARTERS):
            mix_ref[b, :, ATTN_WIDTH + j * LANES:ATTN_WIDTH + (j + 1) * LANES] = (
                sl[j] * inv * gs_ref[:, j * LANES:(j + 1) * LANES]).astype(BF16)


def _mixer(x_prompt, sinks_perm, gmix, win_perm, gq2, gk2, kmeta_blk, vmeta_blk, h0r, h0i, sp, ga_perm, gs):
    nseq, seq, _ = x_prompt.shape
    assert nseq == N_SEQ and seq % BLOCK == 0
    st = pl.BlockSpec((N_SEQ, STATE_COLS), lambda i: (0, 0))
    kvw = pl.BlockSpec((N_SEQ, BLOCK, KV_WIDTH), lambda i: (0, 0, 0))
    return pl.pallas_call(
        _mixer_kernel,
        grid=(seq // BLOCK,),
        in_specs=[pl.BlockSpec(memory_space=pltpu.SMEM),
                  pl.BlockSpec((N_SEQ, BLOCK, D_MODEL), lambda i: (0, i, 0)),
                  _const_spec((1, D_MODEL)), _const_spec((D_MODEL, IN_COLS)), _const_spec((1, LANES)),
                  _const_spec((1, LANES)), _const_spec((BLOCK, KV_WIDTH)), _const_spec((BLOCK, KV_WIDTH)), st, st,
                  _const_spec((N_QUARTERS, LANES, 2 * Q_STATE)), _const_spec((1, STATE_COLS)),
                  _const_spec((1, STATE_COLS)), _const_spec((N_QUARTERS, 2 * Q_STATE, LANES)),
                  _const_spec((1, SSM_WIDTH)), _const_spec((N_QUARTERS, LANES, LANES)), _const_spec((1, SSM_WIDTH)),
                  _const_spec((1, ATTN_WIDTH)), _const_spec((1, SSM_WIDTH))],
        out_specs=[pl.BlockSpec((N_SEQ, BLOCK, D_MODEL), lambda i: (0, i, 0)), kvw, kvw, st, st],
        out_shape=[jax.ShapeDtypeStruct((N_SEQ, seq, D_MODEL), BF16),
                   jax.ShapeDtypeStruct((N_SEQ, BLOCK, KV_WIDTH), F32), jax.ShapeDtypeStruct((N_SEQ, BLOCK, KV_WIDTH), F32),
                   jax.ShapeDtypeStruct((N_SEQ, STATE_COLS), F32), jax.ShapeDtypeStruct((N_SEQ, STATE_COLS), F32)],
        scratch_shapes=[pltpu.VMEM((N_SEQ, 2 * Q_TILES * BLOCK, LANES), BF16),
                        pltpu.VMEM((N_SEQ, 2 * BLOCK, KV_WIDTH), BF16),
                        pltpu.VMEM((N_SEQ, 2 * BLOCK, KV_WIDTH), BF16),
                        pltpu.VMEM((2, 2 * Q_TILES * BLOCK, 2 * BLOCK), F32),
                        pltpu.VMEM((2, 2 * Q_TILES * BLOCK, 2 * BLOCK), BF16),
                        pltpu.VMEM((N_SEQ, BLOCK, ATTN_WIDTH), F32),
                        pltpu.VMEM((N_QUARTERS, N_SEQ * PITCH, LANES), F32),
                        pltpu.VMEM((2, CHUNK_T * N_SEQ, 2 * STATE_COLS), F32),
                        pltpu.VMEM((2, CHUNK_T * N_SEQ, 2 * STATE_COLS), BF16),
                        pltpu.VMEM((N_SEQ, 2 * STATE_COLS), F32),
                        pltpu.VMEM((N_QUARTERS, N_SEQ * PITCH, LANES), F32)],
        compiler_params=pltpu.CompilerParams(dimension_semantics=("arbitrary",), vmem_limit_bytes=VMEM_LIMIT),
        name="mixer",
    )(sinks_perm, x_prompt, gmix, win_perm, gq2, gk2, kmeta_blk, vmeta_blk, h0r, h0i, sp["wq"], sp["lr"], sp["li"],
      sp["cq"], sp["d"], sp["wgluq"], sp["bglu"], ga_perm, gs)


def kernel(x_prompt, x_sample, cache_k_win, cache_v_win, state_ssm_re, state_ssm_im, meta_tokens, g_mix, w_in, g_q,
           g_k, sinks, ssm_a_re, ssm_a_im, ssm_log_dt, ssm_b_re, ssm_b_im, ssm_c_re, ssm_c_im, ssm_d, ssm_w_glu,
           ssm_b_glu, g_att_out, g_ssm_out, w_out, g_ffn, w_gate, w_up, w_down):
    bp, seq, _ = x_prompt.shape
    db, dseq, _ = x_sample.shape
    li = 0
    gmix = g_mix[li].reshape(1, D_MODEL)
    win = w_in[li].astype(BF16)
    gq2 = jnp.tile(g_q[li], 2).reshape(1, LANES)
    gk2 = jnp.tile(g_k[li], 2).reshape(1, LANES)
    sk = sinks[li]
    sp = _ssm_params(ssm_a_re[li], ssm_a_im[li], ssm_log_dt[li], ssm_b_re[li], ssm_b_im[li], ssm_c_re[li],
                     ssm_c_im[li], ssm_d[li], ssm_w_glu[li], ssm_b_glu[li])
    ga = g_att_out[li].reshape(1, ATTN_WIDTH)
    gs = g_ssm_out[li].reshape(1, SSM_WIDTH)
    wout = w_out[li].astype(BF16)
    ffn_w = (g_ffn[li].reshape(1, D_MODEL), w_gate[li].astype(BF16), w_up[li].astype(BF16), w_down[li].astype(BF16))
    perm = jnp.concatenate([jnp.arange(HEAD_DIM) + HEAD_DIM * (t + Q_PER_KV * s) for t in range(Q_TILES) for s in range(2)])
    win_perm = jnp.concatenate([win[:, perm], win[:, ATTN_WIDTH:]], axis=1)
    wout_perm = jnp.concatenate([wout[perm], wout[ATTN_WIDTH:]], axis=0)
    ga_perm = ga[:, perm]

    xs_rows = x_sample.reshape(db * dseq, D_MODEL)
    meta_rows = jnp.tile(meta_tokens, (SUBLANES, 1))
    small = jnp.concatenate([xs_rows, meta_rows], axis=0)
    q_s, k_s, v_s, u_s = _front(small, gmix, win, gq2, gk2, tm=128)
    n_s = db * dseq
    lead = jnp.zeros((BLOCK - N_META, KV_WIDTH), F32)
    kmeta_blk = jnp.concatenate([lead, k_s[n_s:n_s + N_META]], axis=0)
    vmeta_blk = jnp.concatenate([lead, v_s[n_s:n_s + N_META]], axis=0)
    u_meta = u_s[n_s:].reshape(SUBLANES, N_META, SSM_WIDTH).transpose(1, 0, 2)
    zero_state = jnp.zeros((SUBLANES, STATE_COLS), F32)
    _, hm_r, hm_i = _ssm(u_meta, zero_state, zero_state, sp, tt=N_META)

    mix, kw_p, vw_p, hp_r, hp_i = _mixer(x_prompt, sk, gmix, win_perm, gq2, gk2, kmeta_blk, vmeta_blk, hm_r, hm_i, sp,
                                         ga_perm, gs)
    y_prompt = _ffn_call(x_prompt.reshape(bp * seq, D_MODEL), mix.reshape(bp * seq, D_MODEL), wout_perm, *ffn_w,
                         tm=512).reshape(bp, seq, D_MODEL)

    ck = cache_k_win[li].reshape(db, WINDOW, KV_WIDTH)
    cv = cache_v_win[li].reshape(db, WINDOW, KV_WIDTH)
    oa_s, kw_s, vw_s = _attn_sample(q_s[:n_s].reshape(db, dseq, ATTN_WIDTH), k_s[:n_s].reshape(db, dseq, KV_WIDTH),
                                    v_s[:n_s].reshape(db, dseq, KV_WIDTH), ck, cv, sk)
    us_tb = u_s[:n_s].reshape(db, dseq, SSM_WIDTH).transpose(1, 0, 2)
    os_s_tb, hs_r, hs_i = _ssm(us_tb, state_ssm_re[li].reshape(db, STATE_COLS),
                               state_ssm_im[li].reshape(db, STATE_COLS), sp, tt=dseq)
    os_s = os_s_tb.transpose(1, 0, 2).reshape(n_s, SSM_WIDTH)
    y_sample = _back(xs_rows, oa_s.reshape(n_s, ATTN_WIDTH), os_s, ga, gs, wout, *ffn_w, tm=512).reshape(db, dseq, D_MODEL)

    kv5 = lambda a, n: a.reshape(1, n, WINDOW, N_KV_HEADS, HEAD_DIM)
    st4 = lambda a, n: a.reshape(1, n, N_SSM_GROUPS, SSM_STATE)
    return (y_prompt, y_sample, kv5(kw_p, bp), kv5(vw_p, bp), st4(hp_r, bp), st4(hp_i, bp),
            kv5(kw_s, db), kv5(vw_s, db), st4(hs_r, db), st4(hs_i, db))
```

```python
import functools

import jax
import jax.numpy as jnp
from jax import lax
from jax.experimental import pallas as pl
from jax.experimental.pallas import tpu as pltpu

D_MODEL = 1024
N_META = 16
HEAD_DIM = 64
ATTN_WIDTH = 512
N_Q_HEADS = 8
Q_PER_KV = 4
N_KV_HEADS = 2
KV_WIDTH = 128
WINDOW = 128
BLOCK = 128
SSM_WIDTH = 512
SSM_GROUP = 16
N_SSM_GROUPS = 32
SSM_STATE = 64
IN_COLS = ATTN_WIDTH + 2 * KV_WIDTH + SSM_WIDTH
D_FF = 2816
EPS = 1e-6
ATTN_SCALE = HEAD_DIM ** -0.5
STATE_COLS = N_SSM_GROUPS * SSM_STATE
LANES = 128
SUBLANES = 8
VMEM_LIMIT = 56 * 1024 * 1024

F32 = jnp.float32
BF16 = jnp.bfloat16


def _const_spec(shape):
    return pl.BlockSpec(shape, lambda *_: (0,) * len(shape), pipeline_mode=pl.Buffered(1))


def _rms(x, g):
    return x * lax.rsqrt(jnp.mean(x * x, axis=-1, keepdims=True) + EPS) * g


def _left_half():
    return lax.broadcasted_iota(jnp.int32, (1, LANES), 1) < HEAD_DIM


def _pair_norm(zz, g2, left):
    sq = zz * zz
    sl = jnp.sum(jnp.where(left, sq, 0.0), axis=-1, keepdims=True)
    sr = jnp.sum(jnp.where(left, 0.0, sq), axis=-1, keepdims=True)
    inv = jnp.where(left, lax.rsqrt(sl / HEAD_DIM + EPS), lax.rsqrt(sr / HEAD_DIM + EPS))
    return zz * inv * g2


def _front_kernel(x_ref, gmix_ref, win_ref, gq_ref, gk_ref, q_ref, k_ref, v_ref, u_ref):
    xn = _rms(x_ref[...], gmix_ref[...]).astype(BF16)
    z = jnp.dot(xn, win_ref[...], preferred_element_type=F32)
    left = _left_half()
    for p in range(ATTN_WIDTH // LANES):
        q_ref[:, p * LANES:(p + 1) * LANES] = _pair_norm(z[:, p * LANES:(p + 1) * LANES], gq_ref[...], left)
    k_ref[...] = _pair_norm(z[:, ATTN_WIDTH:ATTN_WIDTH + KV_WIDTH], gk_ref[...], left)
    v_ref[...] = z[:, ATTN_WIDTH + KV_WIDTH:ATTN_WIDTH + 2 * KV_WIDTH]
    u_ref[...] = z[:, ATTN_WIDTH + 2 * KV_WIDTH:]


def _front(x_rows, gmix, win_bf, gq2, gk2, tm):
    n = x_rows.shape[0]
    assert n % tm == 0
    row = lambda w: pl.BlockSpec((tm, w), lambda i: (i, 0))
    return pl.pallas_call(
        _front_kernel,
        grid=(n // tm,),
        in_specs=[row(D_MODEL), _const_spec((1, D_MODEL)), _const_spec((D_MODEL, IN_COLS)),
                  _const_spec((1, LANES)), _const_spec((1, LANES))],
        out_specs=[row(ATTN_WIDTH), row(KV_WIDTH), row(KV_WIDTH), row(SSM_WIDTH)],
        out_shape=[jax.ShapeDtypeStruct((n, ATTN_WIDTH), F32), jax.ShapeDtypeStruct((n, KV_WIDTH), F32),
                   jax.ShapeDtypeStruct((n, KV_WIDTH), F32), jax.ShapeDtypeStruct((n, SSM_WIDTH), F32)],
        compiler_params=pltpu.CompilerParams(dimension_semantics=("arbitrary",), vmem_limit_bytes=VMEM_LIMIT),
        name="front",
    )(x_rows, gmix, win_bf, gq2, gk2)


def _sink_softmax_pv(s, mask, sink, vg):
    s = jnp.where(mask, s * ATTN_SCALE, -jnp.inf)
    m = jnp.maximum(jnp.max(s, axis=-1, keepdims=True), sink)
    p = jnp.exp(s - m)
    denom = jnp.sum(p, axis=-1, keepdims=True) + jnp.exp(sink - m)
    o = jnp.dot(p.astype(BF16), vg, preferred_element_type=F32)
    return o / denom


def _attn_sample_kernel(sink_ref, q_ref, kn_ref, vn_ref, ck_ref, cv_ref, o_ref, kw_ref, vw_ref, *, bb, t):
    rows = Q_PER_KV * t
    tk = WINDOW + t
    r = lax.broadcasted_iota(jnp.int32, (rows, tk), 0) % t
    c = lax.broadcasted_iota(jnp.int32, (rows, tk), 1)
    mask = (c >= r) & (c <= r + WINDOW)
    hrow = lax.broadcasted_iota(jnp.int32, (rows, 1), 0) // t
    for bi in range(bb):
        kk = jnp.concatenate([ck_ref[bi], kn_ref[bi]], axis=0)
        vv = jnp.concatenate([cv_ref[bi], vn_ref[bi]], axis=0)
        kw_ref[bi] = kk[t:]
        vw_ref[bi] = vv[t:]
        kkb = kk.astype(BF16)
        vvb = vv.astype(BF16)
        q = q_ref[bi]
        for g in range(N_KV_HEADS):
            heads = range(g * Q_PER_KV, (g + 1) * Q_PER_KV)
            q4 = jnp.concatenate([q[:, h * HEAD_DIM:(h + 1) * HEAD_DIM] for h in heads], axis=0).astype(BF16)
            sink = jnp.zeros((rows, 1), F32)
            for i, h in enumerate(heads):
                sink = jnp.where(hrow == i, sink_ref[h], sink)
            kg = kkb[:, g * HEAD_DIM:(g + 1) * HEAD_DIM]
            vg = vvb[:, g * HEAD_DIM:(g + 1) * HEAD_DIM]
            s = lax.dot_general(q4, kg, (((1,), (1,)), ((), ())), preferred_element_type=F32)
            o4 = _sink_softmax_pv(s, mask, sink, vg)
            for i, h in enumerate(heads):
                o_ref[bi, :, h * HEAD_DIM:(h + 1) * HEAD_DIM] = o4[i * t:(i + 1) * t]


def _attn_sample(q, kn, vn, ck, cv, sinks, bb=8):
    db, t, _ = q.shape
    blk = lambda r, w: pl.BlockSpec((bb, r, w), lambda i: (i, 0, 0))
    return pl.pallas_call(
        functools.partial(_attn_sample_kernel, bb=bb, t=t),
        grid=(db // bb,),
        in_specs=[pl.BlockSpec(memory_space=pltpu.SMEM), blk(t, ATTN_WIDTH), blk(t, KV_WIDTH), blk(t, KV_WIDTH),
                  blk(WINDOW, KV_WIDTH), blk(WINDOW, KV_WIDTH)],
        out_specs=[blk(t, ATTN_WIDTH), blk(WINDOW, KV_WIDTH), blk(WINDOW, KV_WIDTH)],
        out_shape=[jax.ShapeDtypeStruct((db, t, ATTN_WIDTH), F32),
                   jax.ShapeDtypeStruct((db, WINDOW, KV_WIDTH), F32),
                   jax.ShapeDtypeStruct((db, WINDOW, KV_WIDTH), F32)],
        compiler_params=pltpu.CompilerParams(dimension_semantics=("arbitrary",)),
        name="attn_sample",
    )(sinks, q, kn, vn, ck, cv)


N_QUARTERS = 4
Q_GROUPS = N_SSM_GROUPS // N_QUARTERS
Q_STATE = Q_GROUPS * SSM_STATE


def _ssm_params(a_re, a_im, log_dt, b_re, b_im, c_re, c_im, d_skip, w_glu, b_glu):
    dt = jnp.exp(log_dt)[:, None]
    mag = jnp.exp(a_re * dt)
    lr, li = mag * jnp.cos(a_im * dt), mag * jnp.sin(a_im * dt)
    nr, ni = lr - 1.0, li
    den = a_re * a_re + a_im * a_im
    fr, fi = (nr * a_re + ni * a_im) / den, (ni * a_re - nr * a_im) / den
    bbr = fr[..., None] * b_re - fi[..., None] * b_im
    bbi = fr[..., None] * b_im + fi[..., None] * b_re
    eye = jnp.eye(Q_GROUPS, dtype=F32)
    quartered = lambda a: a.reshape((a.shape[0], N_QUARTERS, Q_GROUPS) + a.shape[2:])
    wq = jnp.einsum('rqgph,gm->qghrmp', quartered(jnp.stack([bbr, bbi])), eye).reshape(N_QUARTERS, LANES, 2 * Q_STATE)
    cq = jnp.einsum('rqghp,gm->qrgpmh', quartered(jnp.stack([c_re, -c_im])), eye).reshape(N_QUARTERS, 2 * Q_STATE, LANES)
    wglu = jnp.einsum('qghk,gm->qghmk', w_glu.reshape(N_QUARTERS, Q_GROUPS, SSM_GROUP, SSM_GROUP), eye)
    return dict(wq=wq.astype(BF16), cq=cq.astype(BF16), wgluq=wglu.reshape(N_QUARTERS, LANES, LANES).astype(BF16),
                lr=lr.reshape(1, STATE_COLS), li=li.reshape(1, STATE_COLS), d=d_skip.reshape(1, SSM_WIDTH),
                bglu=b_glu.reshape(1, SSM_WIDTH))


def _scan_step(lr, li, hr, hi, xr, xi):
    return lr * hr - li * hi + xr, lr * hi + li * hr + xi


def _glu_readout(y, u, q, d_ref, wglu_ref, bglu_ref):
    cols = slice(q * LANES, (q + 1) * LANES)
    g = jax.nn.gelu(y + d_ref[:, cols] * u)
    gate = jnp.dot(g.astype(BF16), wglu_ref[q], preferred_element_type=F32) + bglu_ref[:, cols]
    return g * jax.nn.sigmoid(gate)


def _ssm_sample_kernel(u0_ref, u1_ref, u2_ref, u3_ref, h0r_ref, h0i_ref, wq_ref, lr_ref, li_ref, cq_ref, d_ref,
                       wglu_ref, bglu_ref, o_ref, hr_ref, hi_ref, utb_ref, xs_ref, otb_ref, *, nseq, t_steps):
    u_refs = (u0_ref, u1_ref, u2_ref, u3_ref)
    n_rg = nseq // SUBLANES
    rg_rows = SUBLANES * t_steps

    for t in range(t_steps):
        for rg in range(n_rg):
            for j in range(N_QUARTERS):
                utb_ref[t * nseq + rg * SUBLANES:t * nseq + (rg + 1) * SUBLANES, j * LANES:(j + 1) * LANES] = (
                    u_refs[j][pl.ds(rg * rg_rows + t, SUBLANES, stride=t_steps), :])

    for q in range(N_QUARTERS):
        uq = utb_ref[:, q * LANES:(q + 1) * LANES]
        xs_ref[...] = jnp.dot(uq.astype(BF16), wq_ref[q], preferred_element_type=F32)
        lr = jnp.broadcast_to(lr_ref[:, q * Q_STATE:(q + 1) * Q_STATE], (SUBLANES, Q_STATE))
        li = jnp.broadcast_to(li_ref[:, q * Q_STATE:(q + 1) * Q_STATE], (SUBLANES, Q_STATE))

        def scan_group(rg, carry, q=q, lr=lr, li=li):
            r0 = pl.multiple_of(rg * SUBLANES, SUBLANES)
            hr = h0r_ref[pl.ds(r0, SUBLANES), q * Q_STATE:(q + 1) * Q_STATE]
            hi = h0i_ref[pl.ds(r0, SUBLANES), q * Q_STATE:(q + 1) * Q_STATE]
            for t in range(t_steps):
                rows = pl.ds(pl.multiple_of(t * nseq + r0, SUBLANES), SUBLANES)
                hr, hi = _scan_step(lr, li, hr, hi, xs_ref[rows, 0:Q_STATE], xs_ref[rows, Q_STATE:])
                xs_ref[rows, 0:Q_STATE] = hr
                xs_ref[rows, Q_STATE:] = hi
            hr_ref[pl.ds(r0, SUBLANES), q * Q_STATE:(q + 1) * Q_STATE] = hr
            hi_ref[pl.ds(r0, SUBLANES), q * Q_STATE:(q + 1) * Q_STATE] = hi
            return carry

        lax.fori_loop(0, n_rg, scan_group, 0)
        y = jnp.dot(xs_ref[...].astype(BF16), cq_ref[q], preferred_element_type=F32)
        otb_ref[...] = _glu_readout(y, uq, q, d_ref, wglu_ref, bglu_ref)
        for t in range(t_steps):
            for rg in range(n_rg):
                o_ref[q, pl.ds(rg * rg_rows + t, SUBLANES, stride=t_steps), :] = (
                    otb_ref[t * nseq + rg * SUBLANES:t * nseq + (rg + 1) * SUBLANES, :])


def _ssm_sample(u_rows, h0r, h0i, sp, nseq, t_steps):
    n = nseq * t_steps
    slab = lambda j: pl.BlockSpec((n, LANES), lambda i, j=j: (0, j))
    st = pl.BlockSpec((nseq, STATE_COLS), lambda i: (0, 0))
    return pl.pallas_call(
        functools.partial(_ssm_sample_kernel, nseq=nseq, t_steps=t_steps),
        grid=(1,),
        in_specs=[slab(0), slab(1), slab(2), slab(3), st, st,
                  _const_spec((N_QUARTERS, LANES, 2 * Q_STATE)), _const_spec((1, STATE_COLS)),
                  _const_spec((1, STATE_COLS)), _const_spec((N_QUARTERS, 2 * Q_STATE, LANES)),
                  _const_spec((1, SSM_WIDTH)), _const_spec((N_QUARTERS, LANES, LANES)), _const_spec((1, SSM_WIDTH))],
        out_specs=[pl.BlockSpec((N_QUARTERS, n, LANES), lambda i: (0, 0, 0)), st, st],
        out_shape=[jax.ShapeDtypeStruct((N_QUARTERS, n, LANES), F32),
                   jax.ShapeDtypeStruct((nseq, STATE_COLS), F32), jax.ShapeDtypeStruct((nseq, STATE_COLS), F32)],
        scratch_shapes=[pltpu.VMEM((n, SSM_WIDTH), F32), pltpu.VMEM((n, 2 * Q_STATE), F32), pltpu.VMEM((n, LANES), F32)],
        compiler_params=pltpu.CompilerParams(dimension_semantics=("arbitrary",), vmem_limit_bytes=VMEM_LIMIT),
        name="ssm_sample",
    )(u_rows, u_rows, u_rows, u_rows, h0r, h0i, sp["wq"], sp["lr"], sp["li"], sp["cq"], sp["d"], sp["wgluq"], sp["bglu"])


def _ffn(h, gf_ref, wg_ref, wu_ref, wd_ref):
    f = _rms(h, gf_ref[...]).astype(BF16)
    gate = jnp.dot(f, wg_ref[...], preferred_element_type=F32)
    up = jnp.dot(f, wu_ref[...], preferred_element_type=F32)
    a = (jax.nn.silu(gate) * up).astype(BF16)
    return h + jnp.dot(a, wd_ref[...], preferred_element_type=F32)


def _back_kernel(x_ref, oa_ref, os_ref, ga_ref, gs_ref, wout_ref, gf_ref, wg_ref, wu_ref, wd_ref, y_ref):
    o_ssm = jnp.concatenate([os_ref[j] for j in range(N_QUARTERS)], axis=-1)
    mix = jnp.concatenate([_rms(oa_ref[...], ga_ref[...]), _rms(o_ssm, gs_ref[...])], axis=-1).astype(BF16)
    h = x_ref[...] + jnp.dot(mix, wout_ref[...], preferred_element_type=F32)
    y_ref[...] = _ffn(h, gf_ref, wg_ref, wu_ref, wd_ref)


def _ffn_kernel(x_ref, mix_ref, wout_ref, gf_ref, wg_ref, wu_ref, wd_ref, y_ref):
    h = x_ref[...] + jnp.dot(mix_ref[...], wout_ref[...], preferred_element_type=F32)
    y_ref[...] = _ffn(h, gf_ref, wg_ref, wu_ref, wd_ref)


def _back(x_rows, oa, os_, ga, gs, wout, gf, wg, wu, wd, tm):
    n = x_rows.shape[0]
    assert n % tm == 0
    row = lambda w: pl.BlockSpec((tm, w), lambda i: (i, 0))
    return pl.pallas_call(
        _back_kernel,
        grid=(n // tm,),
        in_specs=[row(D_MODEL), row(ATTN_WIDTH), pl.BlockSpec((N_QUARTERS, tm, LANES), lambda i: (0, i, 0)),
                  _const_spec((1, ATTN_WIDTH)), _const_spec((1, SSM_WIDTH)), _const_spec((D_MODEL, D_MODEL)),
                  _const_spec((1, D_MODEL)), _const_spec((D_MODEL, D_FF)), _const_spec((D_MODEL, D_FF)),
                  _const_spec((D_FF, D_MODEL))],
        out_specs=row(D_MODEL),
        out_shape=jax.ShapeDtypeStruct((n, D_MODEL), F32),
        compiler_params=pltpu.CompilerParams(dimension_semantics=("arbitrary",), vmem_limit_bytes=VMEM_LIMIT),
        name="back",
    )(x_rows, oa, os_, ga, gs, wout, gf, wg, wu, wd)


def _ffn_call(x_rows, mix, wout, gf, wg, wu, wd, tm):
    n = x_rows.shape[0]
    assert n % tm == 0
    row = lambda w: pl.BlockSpec((tm, w), lambda i: (i, 0))
    return pl.pallas_call(
        _ffn_kernel,
        grid=(n // tm,),
        in_specs=[row(D_MODEL), row(D_MODEL), _const_spec((D_MODEL, D_MODEL)), _const_spec((1, D_MODEL)),
                  _const_spec((D_MODEL, D_FF)), _const_spec((D_MODEL, D_FF)), _const_spec((D_FF, D_MODEL))],
        out_specs=row(D_MODEL),
        out_shape=jax.ShapeDtypeStruct((n, D_MODEL), F32),
        compiler_params=pltpu.CompilerParams(dimension_semantics=("arbitrary",), vmem_limit_bytes=VMEM_LIMIT),
        name="ffn",
    )(x_rows, mix, wout, gf, wg, wu, wd)


N_SEQ = 8
ROWS = N_SEQ * BLOCK
Q_TILES = ATTN_WIDTH // LANES
PITCH = BLOCK + SUBLANES
CHUNK_T = BLOCK // N_SEQ


def _mixer_kernel(sink_ref, x_ref, meta_ref, gmix_ref, win_ref, gq_ref, gk_ref,
                  wq_ref, lr_ref, li_ref, cq_ref, d_ref, wglu_ref, bglu_ref, ga_ref, gs_ref,
                  mix_ref, kw_ref, vw_ref, hr_ref, hi_ref,
                  qs_ref, kk_ref, vv_ref, s_ref, p_ref, oatt_ref, usl_ref, xs_ref, hb_ref, st_ref, osl_ref):
    i = pl.program_id(0)
    left = _left_half()

    @pl.when(i == 0)
    def _():
        xm = _rms(meta_ref[...], gmix_ref[...]).astype(BF16)
        zm = jnp.dot(xm, win_ref[:, ATTN_WIDTH:], preferred_element_type=F32)
        km = _pair_norm(zm[:, :KV_WIDTH], gk_ref[...], left).astype(BF16)
        vm = zm[:, KV_WIDTH:2 * KV_WIDTH].astype(BF16)
        lead = jnp.zeros((BLOCK - N_META, KV_WIDTH), BF16)
        for b in range(N_SEQ):
            kk_ref[b, 0:BLOCK - N_META, :] = lead
            vv_ref[b, 0:BLOCK - N_META, :] = lead
            kk_ref[b, BLOCK - N_META:BLOCK, :] = km
            vv_ref[b, BLOCK - N_META:BLOCK, :] = vm
        um = zm[:, 2 * KV_WIDTH:]
        for q in range(N_QUARTERS):
            xm_q = jnp.dot(um[:, q * LANES:(q + 1) * LANES].astype(BF16), wq_ref[q], preferred_element_type=F32)
            lr = jnp.broadcast_to(lr_ref[:, q * Q_STATE:(q + 1) * Q_STATE], (SUBLANES, Q_STATE))
            li = jnp.broadcast_to(li_ref[:, q * Q_STATE:(q + 1) * Q_STATE], (SUBLANES, Q_STATE))
            hr = jnp.zeros((SUBLANES, Q_STATE), F32)
            hi = jnp.zeros((SUBLANES, Q_STATE), F32)
            for t in range(N_META):
                xr = jnp.broadcast_to(xm_q[t:t + 1, 0:Q_STATE], (SUBLANES, Q_STATE))
                xi = jnp.broadcast_to(xm_q[t:t + 1, Q_STATE:], (SUBLANES, Q_STATE))
                hr, hi = _scan_step(lr, li, hr, hi, xr, xi)
            st_ref[:, 2 * q * Q_STATE:(2 * q + 1) * Q_STATE] = hr
            st_ref[:, (2 * q + 1) * Q_STATE:(2 * q + 2) * Q_STATE] = hi

    xn = _rms(x_ref[...].reshape(ROWS, D_MODEL), gmix_ref[...]).astype(BF16)

    def proj(c0):
        return jnp.dot(xn, win_ref[:, c0:c0 + 2 * LANES], preferred_element_type=F32)

    for c in range(Q_TILES // 2):
        zz = proj(c * 2 * LANES)
        for tt in range(2):
            t = 2 * c + tt
            qn = _pair_norm(zz[:, tt * LANES:(tt + 1) * LANES], gq_ref[...], left) * ATTN_SCALE
            qa = jnp.where(left, qn, 0.0).astype(BF16)
            qb = jnp.where(left, 0.0, qn).astype(BF16)
            for b in range(N_SEQ):
                qs_ref[b, 2 * t * BLOCK:(2 * t + 1) * BLOCK, :] = qa[b * BLOCK:(b + 1) * BLOCK]
                qs_ref[b, (2 * t + 1) * BLOCK:(2 * t + 2) * BLOCK, :] = qb[b * BLOCK:(b + 1) * BLOCK]
    zz = proj(ATTN_WIDTH)
    kn = _pair_norm(zz[:, :LANES], gk_ref[...], left)
    vn = zz[:, LANES:]
    kw_ref[...] = kn.reshape(N_SEQ, BLOCK, KV_WIDTH)
    vw_ref[...] = vn.reshape(N_SEQ, BLOCK, KV_WIDTH)
    kk_ref[:, BLOCK:, :] = kn.astype(BF16).reshape(N_SEQ, BLOCK, KV_WIDTH)
    vv_ref[:, BLOCK:, :] = vn.astype(BF16).reshape(N_SEQ, BLOCK, KV_WIDTH)
    for c in range(2):
        zz = proj(ATTN_WIDTH + 2 * KV_WIDTH + c * 2 * LANES)
        for tt in range(2):
            for b in range(N_SEQ):
                usl_ref[2 * c + tt, b * PITCH:b * PITCH + BLOCK, :] = zz[b * BLOCK:(b + 1) * BLOCK, tt * LANES:(tt + 1) * LANES]

    r = lax.broadcasted_iota(jnp.int32, (BLOCK, 2 * BLOCK), 0)
    c = lax.broadcasted_iota(jnp.int32, (BLOCK, 2 * BLOCK), 1)
    valid = (c >= r) & (c <= r + WINDOW) & ((i > 0) | (c >= BLOCK - N_META))
    bias = jnp.where(valid, 0.0, -jnp.inf)

    def gather_u(n):
        t0 = n * CHUNK_T
        return jnp.concatenate(
            [jnp.concatenate([usl_ref[j, pl.ds(t0 + tl, SUBLANES, stride=PITCH), :] for j in range(N_QUARTERS)], axis=1)
             for tl in range(CHUNK_T)], axis=0)

    def feed(n, par):
        s_ref[par] = lax.dot_general(qs_ref[n], kk_ref[n], (((1,), (1,)), ((), ())), preferred_element_type=F32)
        u_c = gather_u(n)
        for q in range(N_QUARTERS):
            xs_ref[par, :, 2 * q * Q_STATE:(2 * q + 2) * Q_STATE] = jnp.dot(
                u_c[:, q * LANES:(q + 1) * LANES].astype(BF16), wq_ref[q], preferred_element_type=F32)

    def mid(n, par):
        for g in range(2 * Q_TILES):
            sink = sink_ref[g // 2 + Q_PER_KV * (g % 2)]
            s = s_ref[par, g * BLOCK:(g + 1) * BLOCK, :] + bias
            m = jnp.maximum(jnp.max(s, axis=-1, keepdims=True), sink)
            p = jnp.exp(s - m)
            inv = 1.0 / (jnp.sum(p, axis=-1, keepdims=True) + jnp.exp(sink - m))
            p_ref[par, g * BLOCK:(g + 1) * BLOCK, :] = (p * inv).astype(BF16)
        for q in range(N_QUARTERS):
            lr = jnp.broadcast_to(lr_ref[:, q * Q_STATE:(q + 1) * Q_STATE], (SUBLANES, Q_STATE))
            li = jnp.broadcast_to(li_ref[:, q * Q_STATE:(q + 1) * Q_STATE], (SUBLANES, Q_STATE))
            re0, im0 = 2 * q * Q_STATE, (2 * q + 1) * Q_STATE
            hr = st_ref[:, re0:re0 + Q_STATE]
            hi = st_ref[:, im0:im0 + Q_STATE]
            for tp in range(CHUNK_T // 2):
                pair_r, pair_i = [], []
                for tl in (2 * tp, 2 * tp + 1):
                    xr = xs_ref[par, tl * SUBLANES:(tl + 1) * SUBLANES, re0:re0 + Q_STATE]
                    xi = xs_ref[par, tl * SUBLANES:(tl + 1) * SUBLANES, im0:im0 + Q_STATE]
                    hr, hi = _scan_step(lr, li, hr, hi, xr, xi)
                    pair_r.append(hr)
                    pair_i.append(hi)
                rows = slice(2 * tp * SUBLANES, (2 * tp + 2) * SUBLANES)
                hb_ref[par, rows, re0:re0 + Q_STATE] = jnp.concatenate(pair_r, axis=0).astype(BF16)
                hb_ref[par, rows, im0:im0 + Q_STATE] = jnp.concatenate(pair_i, axis=0).astype(BF16)
            st_ref[:, re0:re0 + Q_STATE] = hr
            st_ref[:, im0:im0 + Q_STATE] = hi

    def tail(n, par):
        o_all = jnp.dot(p_ref[par], vv_ref[n], preferred_element_type=F32)
        for t in range(Q_TILES):
            oatt_ref[n, :, t * LANES:(t + 1) * LANES] = jnp.where(
                left, o_all[2 * t * BLOCK:(2 * t + 1) * BLOCK], o_all[(2 * t + 1) * BLOCK:(2 * t + 2) * BLOCK])
        u_c = gather_u(n)
        t0 = n * CHUNK_T
        for q in range(N_QUARTERS):
            y = jnp.dot(hb_ref[par, :, 2 * q * Q_STATE:(2 * q + 2) * Q_STATE], cq_ref[q], preferred_element_type=F32)
            o = _glu_readout(y, u_c[:, q * LANES:(q + 1) * LANES], q, d_ref, wglu_ref, bglu_ref)
            for tl in range(CHUNK_T):
                osl_ref[q, pl.ds(t0 + tl, SUBLANES, stride=PITCH), :] = o[tl * SUBLANES:(tl + 1) * SUBLANES]

    def steady(k, carry):
        n = 2 * k + 1
        feed(n + 1, 0)
        mid(n, 1)
        tail(n - 1, 0)
        feed(n + 2, 1)
        mid(n + 1, 0)
        tail(n, 1)
        return carry

    feed(0, 0)
    feed(1, 1)
    mid(0, 0)
    lax.fori_loop(0, (N_SEQ - 2) // 2, steady, 0)
    mid(N_SEQ - 1, 1)
    tail(N_SEQ - 2, 0)
    tail(N_SEQ - 1, 1)
    kk_ref[:, 0:BLOCK, :] = kk_ref[:, BLOCK:, :]
    vv_ref[:, 0:BLOCK, :] = vv_ref[:, BLOCK:, :]

    hr_ref[...] = jnp.concatenate([st_ref[:, 2 * q * Q_STATE:(2 * q + 1) * Q_STATE] for q in range(N_QUARTERS)], axis=1)
    hi_ref[...] = jnp.concatenate([st_ref[:, (2 * q + 1) * Q_STATE:(2 * q + 2) * Q_STATE] for q in range(N_QUARTERS)], axis=1)

    for b in range(N_SEQ):
        mix_ref[b, :, 0:ATTN_WIDTH] = _rms(oatt_ref[b], ga_ref[...]).astype(BF16)
        sl = [osl_ref[j, b * PITCH:b * PITCH + BLOCK, :] for j in range(N_QUARTERS)]
        ms = sum(jnp.sum(s * s, axis=-1, keepdims=True) for s in sl) / SSM_WIDTH
        inv = lax.rsqrt(ms + EPS)
        for j in range(N_QUARTERS):
            mix_ref[b, :, ATTN_WIDTH + j * LANES:ATTN_WIDTH + (j + 1) * LANES] = (
                sl[j] * inv * gs_ref[:, j * LANES:(j + 1) * LANES]).astype(BF16)


def _mixer(x_prompt, meta_tokens, sinks, gmix, win_perm, gq2, gk2, sp, ga_perm, gs):
    nseq, seq, _ = x_prompt.shape
    assert nseq == N_SEQ and seq % BLOCK == 0
    st = pl.BlockSpec((N_SEQ, STATE_COLS), lambda i: (0, 0))
    kvw = pl.BlockSpec((N_SEQ, BLOCK, KV_WIDTH), lambda i: (0, 0, 0))
    return pl.pallas_call(
        _mixer_kernel,
        grid=(seq // BLOCK,),
        in_specs=[pl.BlockSpec(memory_space=pltpu.SMEM),
                  pl.BlockSpec((N_SEQ, BLOCK, D_MODEL), lambda i: (0, i, 0)), _const_spec((N_META, D_MODEL)),
                  _const_spec((1, D_MODEL)), _const_spec((D_MODEL, IN_COLS)), _const_spec((1, LANES)),
                  _const_spec((1, LANES)),
                  _const_spec((N_QUARTERS, LANES, 2 * Q_STATE)), _const_spec((1, STATE_COLS)),
                  _const_spec((1, STATE_COLS)), _const_spec((N_QUARTERS, 2 * Q_STATE, LANES)),
                  _const_spec((1, SSM_WIDTH)), _const_spec((N_QUARTERS, LANES, LANES)), _const_spec((1, SSM_WIDTH)),
                  _const_spec((1, ATTN_WIDTH)), _const_spec((1, SSM_WIDTH))],
        out_specs=[pl.BlockSpec((N_SEQ, BLOCK, D_MODEL), lambda i: (0, i, 0)), kvw, kvw, st, st],
        out_shape=[jax.ShapeDtypeStruct((N_SEQ, seq, D_MODEL), BF16),
                   jax.ShapeDtypeStruct((N_SEQ, BLOCK, KV_WIDTH), F32), jax.ShapeDtypeStruct((N_SEQ, BLOCK, KV_WIDTH), F32),
                   jax.ShapeDtypeStruct((N_SEQ, STATE_COLS), F32), jax.ShapeDtypeStruct((N_SEQ, STATE_COLS), F32)],
        scratch_shapes=[pltpu.VMEM((N_SEQ, 2 * Q_TILES * BLOCK, LANES), BF16),
                        pltpu.VMEM((N_SEQ, 2 * BLOCK, KV_WIDTH), BF16),
                        pltpu.VMEM((N_SEQ, 2 * BLOCK, KV_WIDTH), BF16),
                        pltpu.VMEM((2, 2 * Q_TILES * BLOCK, 2 * BLOCK), F32),
                        pltpu.VMEM((2, 2 * Q_TILES * BLOCK, 2 * BLOCK), BF16),
                        pltpu.VMEM((N_SEQ, BLOCK, ATTN_WIDTH), F32),
                        pltpu.VMEM((N_QUARTERS, N_SEQ * PITCH, LANES), F32),
                        pltpu.VMEM((2, CHUNK_T * N_SEQ, 2 * STATE_COLS), F32),
                        pltpu.VMEM((2, CHUNK_T * N_SEQ, 2 * STATE_COLS), BF16),
                        pltpu.VMEM((N_SEQ, 2 * STATE_COLS), F32),
                        pltpu.VMEM((N_QUARTERS, N_SEQ * PITCH, LANES), F32)],
        compiler_params=pltpu.CompilerParams(dimension_semantics=("arbitrary",), vmem_limit_bytes=VMEM_LIMIT),
        name="mixer",
    )(sinks, x_prompt, meta_tokens, gmix, win_perm, gq2, gk2, sp["wq"], sp["lr"], sp["li"], sp["cq"], sp["d"],
      sp["wgluq"], sp["bglu"], ga_perm, gs)


def kernel(x_prompt, x_sample, cache_k_win, cache_v_win, state_ssm_re, state_ssm_im, meta_tokens, g_mix, w_in, g_q,
           g_k, sinks, ssm_a_re, ssm_a_im, ssm_log_dt, ssm_b_re, ssm_b_im, ssm_c_re, ssm_c_im, ssm_d, ssm_w_glu,
           ssm_b_glu, g_att_out, g_ssm_out, w_out, g_ffn, w_gate, w_up, w_down):
    bp, seq, _ = x_prompt.shape
    db, dseq, _ = x_sample.shape
    li = 0
    gmix = g_mix[li].reshape(1, D_MODEL)
    win = w_in[li].astype(BF16)
    gq2 = jnp.tile(g_q[li], 2).reshape(1, LANES)
    gk2 = jnp.tile(g_k[li], 2).reshape(1, LANES)
    sk = sinks[li]
    sp = _ssm_params(ssm_a_re[li], ssm_a_im[li], ssm_log_dt[li], ssm_b_re[li], ssm_b_im[li], ssm_c_re[li],
                     ssm_c_im[li], ssm_d[li], ssm_w_glu[li], ssm_b_glu[li])
    ga = g_att_out[li].reshape(1, ATTN_WIDTH)
    gs = g_ssm_out[li].reshape(1, SSM_WIDTH)
    wout = w_out[li].astype(BF16)
    ffn_w = (g_ffn[li].reshape(1, D_MODEL), w_gate[li].astype(BF16), w_up[li].astype(BF16), w_down[li].astype(BF16))
    regroup = lambda a, axis: jnp.swapaxes(
        a.reshape(a.shape[:axis] + (2, Q_PER_KV, HEAD_DIM) + a.shape[axis + 1:]), axis, axis + 1).reshape(a.shape)
    win_perm = jnp.concatenate([regroup(win[:, :ATTN_WIDTH], 1), win[:, ATTN_WIDTH:]], axis=1)
    wout_perm = jnp.concatenate([regroup(wout[:ATTN_WIDTH], 0), wout[ATTN_WIDTH:]], axis=0)
    ga_perm = regroup(ga, 1)

    mix, kw_p, vw_p, hp_r, hp_i = _mixer(x_prompt, meta_tokens, sk, gmix, win_perm, gq2, gk2, sp, ga_perm, gs)
    y_prompt = _ffn_call(x_prompt.reshape(bp * seq, D_MODEL), mix.reshape(bp * seq, D_MODEL), wout_perm, *ffn_w,
                         tm=512).reshape(bp, seq, D_MODEL)

    n_s = db * dseq
    xs_rows = x_sample.reshape(n_s, D_MODEL)
    q_s, k_s, v_s, u_s = _front(xs_rows, gmix, win, gq2, gk2, tm=512)
    ck = cache_k_win[li].reshape(db, WINDOW, KV_WIDTH)
    cv = cache_v_win[li].reshape(db, WINDOW, KV_WIDTH)
    oa_s, kw_s, vw_s = _attn_sample(q_s.reshape(db, dseq, ATTN_WIDTH), k_s.reshape(db, dseq, KV_WIDTH),
                                    v_s.reshape(db, dseq, KV_WIDTH), ck, cv, sk)
    os_s, hs_r, hs_i = _ssm_sample(u_s, state_ssm_re[li].reshape(db, STATE_COLS),
                                   state_ssm_im[li].reshape(db, STATE_COLS), sp, nseq=db, t_steps=dseq)
    y_sample = _back(xs_rows, oa_s.reshape(n_s, ATTN_WIDTH), os_s, ga, gs, wout, *ffn_w, tm=512).reshape(db, dseq, D_MODEL)

    kv5 = lambda a, n: a.reshape(1, n, WINDOW, N_KV_HEADS, HEAD_DIM)
    st4 = lambda a, n: a.reshape(1, n, N_SSM_GROUPS, SSM_STATE)
    return (y_prompt, y_sample, kv5(kw_p, bp), kv5(vw_p, bp), st4(hp_r, bp), st4(hp_i, bp),
            kv5(kw_s, db), kv5(vw_s, db), st4(hs_r, db), st4(hs_i, db))
```

```python
import functools

import jax
import jax.numpy as jnp
from jax import lax
from jax.experimental import pallas as pl
from jax.experimental.pallas import tpu as pltpu

D_MODEL = 1024
N_META = 16
HEAD_DIM = 64
ATTN_WIDTH = 512
N_Q_HEADS = 8
Q_PER_KV = 4
N_KV_HEADS = 2
KV_WIDTH = 128
WINDOW = 128
BLOCK = 128
SSM_WIDTH = 512
SSM_GROUP = 16
N_SSM_GROUPS = 32
SSM_STATE = 64
IN_COLS = ATTN_WIDTH + 2 * KV_WIDTH + SSM_WIDTH
D_FF = 2816
EPS = 1e-6
ATTN_SCALE = HEAD_DIM ** -0.5
STATE_COLS = N_SSM_GROUPS * SSM_STATE
LANES = 128
SUBLANES = 8
VMEM_LIMIT = 56 * 1024 * 1024

F32 = jnp.float32
BF16 = jnp.bfloat16


def _const_spec(shape):
    return pl.BlockSpec(shape, lambda *_: (0,) * len(shape), pipeline_mode=pl.Buffered(1))


def _rms(x, g):
    return x * lax.rsqrt(jnp.mean(x * x, axis=-1, keepdims=True) + EPS) * g


def _left_half():
    return lax.broadcasted_iota(jnp.int32, (1, LANES), 1) < HEAD_DIM


def _pair_norm(zz, g2, left):
    sq = zz * zz
    sl = jnp.sum(jnp.where(left, sq, 0.0), axis=-1, keepdims=True)
    sr = jnp.sum(jnp.where(left, 0.0, sq), axis=-1, keepdims=True)
    inv = jnp.where(left, lax.rsqrt(sl / HEAD_DIM + EPS), lax.rsqrt(sr / HEAD_DIM + EPS))
    return zz * inv * g2


def _front_kernel(x_ref, gmix_ref, win_ref, gq_ref, gk_ref, q_ref, k_ref, v_ref, u_ref):
    xn = _rms(x_ref[...], gmix_ref[...]).astype(BF16)
    z = jnp.dot(xn, win_ref[...], preferred_element_type=F32)
    left = _left_half()
    for p in range(ATTN_WIDTH // LANES):
        q_ref[:, p * LANES:(p + 1) * LANES] = _pair_norm(z[:, p * LANES:(p + 1) * LANES], gq_ref[...], left)
    k_ref[...] = _pair_norm(z[:, ATTN_WIDTH:ATTN_WIDTH + KV_WIDTH], gk_ref[...], left)
    v_ref[...] = z[:, ATTN_WIDTH + KV_WIDTH:ATTN_WIDTH + 2 * KV_WIDTH]
    u_ref[...] = z[:, ATTN_WIDTH + 2 * KV_WIDTH:]


def _front(x_rows, gmix, win_bf, gq2, gk2, tm):
    n = x_rows.shape[0]
    assert n % tm == 0
    row = lambda w: pl.BlockSpec((tm, w), lambda i: (i, 0))
    return pl.pallas_call(
        _front_kernel,
        grid=(n // tm,),
        in_specs=[row(D_MODEL), _const_spec((1, D_MODEL)), _const_spec((D_MODEL, IN_COLS)),
                  _const_spec((1, LANES)), _const_spec((1, LANES))],
        out_specs=[row(ATTN_WIDTH), row(KV_WIDTH), row(KV_WIDTH), row(SSM_WIDTH)],
        out_shape=[jax.ShapeDtypeStruct((n, ATTN_WIDTH), F32), jax.ShapeDtypeStruct((n, KV_WIDTH), F32),
                   jax.ShapeDtypeStruct((n, KV_WIDTH), F32), jax.ShapeDtypeStruct((n, SSM_WIDTH), F32)],
        compiler_params=pltpu.CompilerParams(dimension_semantics=("arbitrary",), vmem_limit_bytes=VMEM_LIMIT),
        name="front",
    )(x_rows, gmix, win_bf, gq2, gk2)


Q_TILES = ATTN_WIDTH // LANES


def _attn_sample_kernel(sink_ref, q_ref, kn_ref, vn_ref, ck_ref, cv_ref, o_ref, kw_ref, vw_ref, *, bb, t):
    left = _left_half()
    n_heads = 2 * Q_TILES
    rows = n_heads * t
    tk = WINDOW + t
    r = lax.broadcasted_iota(jnp.int32, (rows, tk), 0) % t
    c = lax.broadcasted_iota(jnp.int32, (rows, tk), 1)
    bias = jnp.where((c >= r) & (c <= r + WINDOW), 0.0, -jnp.inf)
    hrow = lax.broadcasted_iota(jnp.int32, (rows, 1), 0) // t
    sink = jnp.zeros((rows, 1), F32)
    for g in range(n_heads):
        sink = jnp.where(hrow == g, sink_ref[g // 2 + Q_PER_KV * (g % 2)], sink)
    scores, values = [], []
    for bi in range(bb):
        kk = jnp.concatenate([ck_ref[bi], kn_ref[bi]], axis=0)
        vv = jnp.concatenate([cv_ref[bi], vn_ref[bi]], axis=0)
        kw_ref[bi] = kk[t:]
        vw_ref[bi] = vv[t:]
        q = q_ref[bi] * ATTN_SCALE
        pieces = []
        for tile in range(Q_TILES):
            qt = q[:, tile * LANES:(tile + 1) * LANES]
            pieces += [jnp.where(left, qt, 0.0), jnp.where(left, 0.0, qt)]
        qs = jnp.concatenate(pieces, axis=0).astype(BF16)
        scores.append(lax.dot_general(qs, kk.astype(BF16), (((1,), (1,)), ((), ())), preferred_element_type=F32))
        values.append(vv.astype(BF16))
    probs = []
    for bi in range(bb):
        s = scores[bi] + bias
        m = jnp.maximum(jnp.max(s, axis=-1, keepdims=True), sink)
        p = jnp.exp(s - m)
        inv = 1.0 / (jnp.sum(p, axis=-1, keepdims=True) + jnp.exp(sink - m))
        probs.append((p * inv).astype(BF16))
    for bi in range(bb):
        o = jnp.dot(probs[bi], values[bi], preferred_element_type=F32)
        for tile in range(Q_TILES):
            o_ref[bi, :, tile * LANES:(tile + 1) * LANES] = jnp.where(
                left, o[2 * tile * t:(2 * tile + 1) * t], o[(2 * tile + 1) * t:(2 * tile + 2) * t])


def _attn_sample(q, kn, vn, ck, cv, sinks, bb=8):
    db, t, _ = q.shape
    blk = lambda r, w: pl.BlockSpec((bb, r, w), lambda i: (i, 0, 0))
    return pl.pallas_call(
        functools.partial(_attn_sample_kernel, bb=bb, t=t),
        grid=(db // bb,),
        in_specs=[pl.BlockSpec(memory_space=pltpu.SMEM), blk(t, ATTN_WIDTH), blk(t, KV_WIDTH), blk(t, KV_WIDTH),
                  blk(WINDOW, KV_WIDTH), blk(WINDOW, KV_WIDTH)],
        out_specs=[blk(t, ATTN_WIDTH), blk(WINDOW, KV_WIDTH), blk(WINDOW, KV_WIDTH)],
        out_shape=[jax.ShapeDtypeStruct((db, t, ATTN_WIDTH), F32),
                   jax.ShapeDtypeStruct((db, WINDOW, KV_WIDTH), F32),
                   jax.ShapeDtypeStruct((db, WINDOW, KV_WIDTH), F32)],
        compiler_params=pltpu.CompilerParams(dimension_semantics=("arbitrary",)),
        name="attn_sample",
    )(sinks, q, kn, vn, ck, cv)


N_QUARTERS = 4
Q_GROUPS = N_SSM_GROUPS // N_QUARTERS
Q_STATE = Q_GROUPS * SSM_STATE


def _ssm_params(a_re, a_im, log_dt, b_re, b_im, c_re, c_im, d_skip, w_glu, b_glu):
    dt = jnp.exp(log_dt)[:, None]
    mag = jnp.exp(a_re * dt)
    lr, li = mag * jnp.cos(a_im * dt), mag * jnp.sin(a_im * dt)
    nr, ni = lr - 1.0, li
    den = a_re * a_re + a_im * a_im
    fr, fi = (nr * a_re + ni * a_im) / den, (ni * a_re - nr * a_im) / den
    bbr = fr[..., None] * b_re - fi[..., None] * b_im
    bbi = fr[..., None] * b_im + fi[..., None] * b_re
    eye = jnp.eye(Q_GROUPS, dtype=F32)
    quartered = lambda a: a.reshape((a.shape[0], N_QUARTERS, Q_GROUPS) + a.shape[2:])
    wq = jnp.einsum('rqgph,gm->qghrmp', quartered(jnp.stack([bbr, bbi])), eye).reshape(N_QUARTERS, LANES, 2 * Q_STATE)
    cq = jnp.einsum('rqghp,gm->qrgpmh', quartered(jnp.stack([c_re, -c_im])), eye).reshape(N_QUARTERS, 2 * Q_STATE, LANES)
    wglu = jnp.einsum('qghk,gm->qghmk', w_glu.reshape(N_QUARTERS, Q_GROUPS, SSM_GROUP, SSM_GROUP), eye)
    return dict(wq=wq.astype(BF16), cq=cq.astype(BF16), wgluq=wglu.reshape(N_QUARTERS, LANES, LANES).astype(BF16),
                lr=lr.reshape(1, STATE_COLS), li=li.reshape(1, STATE_COLS), d=d_skip.reshape(1, SSM_WIDTH),
                bglu=b_glu.reshape(1, SSM_WIDTH))


def _scan_step(lr, li, hr, hi, xr, xi):
    return lr * hr - li * hi + xr, lr * hi + li * hr + xi


def _glu_readout(y, u, q, d_ref, wglu_ref, bglu_ref):
    cols = slice(q * LANES, (q + 1) * LANES)
    g = jax.nn.gelu(y + d_ref[:, cols] * u)
    gate = jnp.dot(g.astype(BF16), wglu_ref[q], preferred_element_type=F32) + bglu_ref[:, cols]
    return g * jax.nn.sigmoid(gate)


def _ssm_sample_kernel(u0_ref, u1_ref, u2_ref, u3_ref, h0r_ref, h0i_ref, wq_ref, lr_ref, li_ref, cq_ref, d_ref,
                       wglu_ref, bglu_ref, o_ref, hr_ref, hi_ref, utb_ref, xs_ref, otb_ref, *, nseq, t_steps):
    u_refs = (u0_ref, u1_ref, u2_ref, u3_ref)
    n_rg = nseq // SUBLANES
    rg_rows = SUBLANES * t_steps

    for t in range(t_steps):
        for rg in range(n_rg):
            for j in range(N_QUARTERS):
                utb_ref[t * nseq + rg * SUBLANES:t * nseq + (rg + 1) * SUBLANES, j * LANES:(j + 1) * LANES] = (
                    u_refs[j][pl.ds(rg * rg_rows + t, SUBLANES, stride=t_steps), :])

    for q in range(N_QUARTERS):
        uq = utb_ref[:, q * LANES:(q + 1) * LANES]
        xs_ref[...] = jnp.dot(uq.astype(BF16), wq_ref[q], preferred_element_type=F32)
        lr = jnp.broadcast_to(lr_ref[:, q * Q_STATE:(q + 1) * Q_STATE], (SUBLANES, Q_STATE))
        li = jnp.broadcast_to(li_ref[:, q * Q_STATE:(q + 1) * Q_STATE], (SUBLANES, Q_STATE))

        def scan_group(rg, carry, q=q, lr=lr, li=li):
            r0 = pl.multiple_of(rg * SUBLANES, SUBLANES)
            hr = h0r_ref[pl.ds(r0, SUBLANES), q * Q_STATE:(q + 1) * Q_STATE]
            hi = h0i_ref[pl.ds(r0, SUBLANES), q * Q_STATE:(q + 1) * Q_STATE]
            for t in range(t_steps):
                rows = pl.ds(pl.multiple_of(t * nseq + r0, SUBLANES), SUBLANES)
                hr, hi = _scan_step(lr, li, hr, hi, xs_ref[rows, 0:Q_STATE], xs_ref[rows, Q_STATE:])
                xs_ref[rows, 0:Q_STATE] = hr
                xs_ref[rows, Q_STATE:] = hi
            hr_ref[pl.ds(r0, SUBLANES), q * Q_STATE:(q + 1) * Q_STATE] = hr
            hi_ref[pl.ds(r0, SUBLANES), q * Q_STATE:(q + 1) * Q_STATE] = hi
            return carry

        lax.fori_loop(0, n_rg, scan_group, 0)
        y = jnp.dot(xs_ref[...].astype(BF16), cq_ref[q], preferred_element_type=F32)
        otb_ref[...] = _glu_readout(y, uq, q, d_ref, wglu_ref, bglu_ref)
        for t in range(t_steps):
            for rg in range(n_rg):
                o_ref[q, pl.ds(rg * rg_rows + t, SUBLANES, stride=t_steps), :] = (
                    otb_ref[t * nseq + rg * SUBLANES:t * nseq + (rg + 1) * SUBLANES, :])


def _ssm_sample(u_rows, h0r, h0i, sp, nseq, t_steps):
    n = nseq * t_steps
    slab = lambda j: pl.BlockSpec((n, LANES), lambda i, j=j: (0, j))
    st = pl.BlockSpec((nseq, STATE_COLS), lambda i: (0, 0))
    return pl.pallas_call(
        functools.partial(_ssm_sample_kernel, nseq=nseq, t_steps=t_steps),
        grid=(1,),
        in_specs=[slab(0), slab(1), slab(2), slab(3), st, st,
                  _const_spec((N_QUARTERS, LANES, 2 * Q_STATE)), _const_spec((1, STATE_COLS)),
                  _const_spec((1, STATE_COLS)), _const_spec((N_QUARTERS, 2 * Q_STATE, LANES)),
                  _const_spec((1, SSM_WIDTH)), _const_spec((N_QUARTERS, LANES, LANES)), _const_spec((1, SSM_WIDTH))],
        out_specs=[pl.BlockSpec((N_QUARTERS, n, LANES), lambda i: (0, 0, 0)), st, st],
        out_shape=[jax.ShapeDtypeStruct((N_QUARTERS, n, LANES), F32),
                   jax.ShapeDtypeStruct((nseq, STATE_COLS), F32), jax.ShapeDtypeStruct((nseq, STATE_COLS), F32)],
        scratch_shapes=[pltpu.VMEM((n, SSM_WIDTH), F32), pltpu.VMEM((n, 2 * Q_STATE), F32), pltpu.VMEM((n, LANES), F32)],
        compiler_params=pltpu.CompilerParams(dimension_semantics=("arbitrary",), vmem_limit_bytes=VMEM_LIMIT),
        name="ssm_sample",
    )(u_rows, u_rows, u_rows, u_rows, h0r, h0i, sp["wq"], sp["lr"], sp["li"], sp["cq"], sp["d"], sp["wgluq"], sp["bglu"])


def _ffn(h, gf_ref, wg_ref, wu_ref, wd_ref):
    f = _rms(h, gf_ref[...]).astype(BF16)
    gate = jnp.dot(f, wg_ref[...], preferred_element_type=F32)
    up = jnp.dot(f, wu_ref[...], preferred_element_type=F32)
    a = (jax.nn.silu(gate) * up).astype(BF16)
    return h + jnp.dot(a, wd_ref[...], preferred_element_type=F32)


def _back_kernel(x_ref, oa_ref, os_ref, ga_ref, gs_ref, wout_ref, gf_ref, wg_ref, wu_ref, wd_ref, y_ref):
    o_ssm = jnp.concatenate([os_ref[j] for j in range(N_QUARTERS)], axis=-1)
    mix = jnp.concatenate([_rms(oa_ref[...], ga_ref[...]), _rms(o_ssm, gs_ref[...])], axis=-1).astype(BF16)
    h = x_ref[...] + jnp.dot(mix, wout_ref[...], preferred_element_type=F32)
    y_ref[...] = _ffn(h, gf_ref, wg_ref, wu_ref, wd_ref)


def _ffn_kernel(x_ref, mix_ref, wout_ref, gf_ref, wg_ref, wu_ref, wd_ref, y_ref):
    h = x_ref[...] + jnp.dot(mix_ref[...], wout_ref[...], preferred_element_type=F32)
    y_ref[...] = _ffn(h, gf_ref, wg_ref, wu_ref, wd_ref)


def _back(x_rows, oa, os_, ga, gs, wout, gf, wg, wu, wd, tm):
    n = x_rows.shape[0]
    assert n % tm == 0
    row = lambda w: pl.BlockSpec((tm, w), lambda i: (i, 0))
    return pl.pallas_call(
        _back_kernel,
        grid=(n // tm,),
        in_specs=[row(D_MODEL), row(ATTN_WIDTH), pl.BlockSpec((N_QUARTERS, tm, LANES), lambda i: (0, i, 0)),
                  _const_spec((1, ATTN_WIDTH)), _const_spec((1, SSM_WIDTH)), _const_spec((D_MODEL, D_MODEL)),
                  _const_spec((1, D_MODEL)), _const_spec((D_MODEL, D_FF)), _const_spec((D_MODEL, D_FF)),
                  _const_spec((D_FF, D_MODEL))],
        out_specs=row(D_MODEL),
        out_shape=jax.ShapeDtypeStruct((n, D_MODEL), F32),
        compiler_params=pltpu.CompilerParams(dimension_semantics=("arbitrary",), vmem_limit_bytes=VMEM_LIMIT),
        name="back",
    )(x_rows, oa, os_, ga, gs, wout, gf, wg, wu, wd)


def _ffn_call(x_rows, mix, wout, gf, wg, wu, wd, tm):
    n = x_rows.shape[0]
    assert n % tm == 0
    row = lambda w: pl.BlockSpec((tm, w), lambda i: (i, 0))
    return pl.pallas_call(
        _ffn_kernel,
        grid=(n // tm,),
        in_specs=[row(D_MODEL), row(D_MODEL), _const_spec((D_MODEL, D_MODEL)), _const_spec((1, D_MODEL)),
                  _const_spec((D_MODEL, D_FF)), _const_spec((D_MODEL, D_FF)), _const_spec((D_FF, D_MODEL))],
        out_specs=row(D_MODEL),
        out_shape=jax.ShapeDtypeStruct((n, D_MODEL), F32),
        compiler_params=pltpu.CompilerParams(dimension_semantics=("arbitrary",), vmem_limit_bytes=VMEM_LIMIT),
        name="ffn",
    )(x_rows, mix, wout, gf, wg, wu, wd)


N_SEQ = 8
ROWS = N_SEQ * BLOCK
PITCH = BLOCK + SUBLANES
CHUNK_T = BLOCK // N_SEQ


def _interleave(*stages):
    keyed = [((i + 0.5) / len(steps), k, i, step) for k, steps in enumerate(stages) for i, step in enumerate(steps)]
    return [step for _, _, _, step in sorted(keyed, key=lambda e: e[:3])]


def _mixer_kernel(sink_ref, x_ref, meta_ref, gmix_ref, win_ref, gq_ref, gk_ref,
                  wq_ref, lr_ref, li_ref, cq_ref, d_ref, wglu_ref, bglu_ref, ga_ref, gs_ref,
                  mix_ref, kw_ref, vw_ref, hr_ref, hi_ref,
                  qs_ref, kk_ref, vv_ref, s_ref, p_ref, oatt_ref, usl_ref, xs_ref, hb_ref, st_ref, osl_ref):
    i = pl.program_id(0)
    left = _left_half()

    @pl.when(i == 0)
    def _():
        xm = _rms(meta_ref[...], gmix_ref[...]).astype(BF16)
        zm = jnp.dot(xm, win_ref[:, ATTN_WIDTH:], preferred_element_type=F32)
        km = _pair_norm(zm[:, :KV_WIDTH], gk_ref[...], left).astype(BF16)
        vm = zm[:, KV_WIDTH:2 * KV_WIDTH].astype(BF16)
        lead = jnp.zeros((BLOCK - N_META, KV_WIDTH), BF16)
        for b in range(N_SEQ):
            kk_ref[b, 0:BLOCK - N_META, :] = lead
            vv_ref[b, 0:BLOCK - N_META, :] = lead
            kk_ref[b, BLOCK - N_META:BLOCK, :] = km
            vv_ref[b, BLOCK - N_META:BLOCK, :] = vm
        um = zm[:, 2 * KV_WIDTH:]
        for q in range(N_QUARTERS):
            xm_q = jnp.dot(um[:, q * LANES:(q + 1) * LANES].astype(BF16), wq_ref[q], preferred_element_type=F32)
            lr = jnp.broadcast_to(lr_ref[:, q * Q_STATE:(q + 1) * Q_STATE], (SUBLANES, Q_STATE))
            li = jnp.broadcast_to(li_ref[:, q * Q_STATE:(q + 1) * Q_STATE], (SUBLANES, Q_STATE))
            hr = jnp.zeros((SUBLANES, Q_STATE), F32)
            hi = jnp.zeros((SUBLANES, Q_STATE), F32)
            for t in range(N_META):
                xr = jnp.broadcast_to(xm_q[t:t + 1, 0:Q_STATE], (SUBLANES, Q_STATE))
                xi = jnp.broadcast_to(xm_q[t:t + 1, Q_STATE:], (SUBLANES, Q_STATE))
                hr, hi = _scan_step(lr, li, hr, hi, xr, xi)
            st_ref[:, 2 * q * Q_STATE:(2 * q + 1) * Q_STATE] = hr
            st_ref[:, (2 * q + 1) * Q_STATE:(2 * q + 2) * Q_STATE] = hi

    xn = _rms(x_ref[...].reshape(ROWS, D_MODEL), gmix_ref[...]).astype(BF16)

    def proj(c0):
        return jnp.dot(xn, win_ref[:, c0:c0 + 2 * LANES], preferred_element_type=F32)

    for c in range(Q_TILES // 2):
        zz = proj(c * 2 * LANES)
        for tt in range(2):
            t = 2 * c + tt
            qn = _pair_norm(zz[:, tt * LANES:(tt + 1) * LANES], gq_ref[...], left) * ATTN_SCALE
            qa = jnp.where(left, qn, 0.0).astype(BF16)
            qb = jnp.where(left, 0.0, qn).astype(BF16)
            for b in range(N_SEQ):
                qs_ref[b, 2 * t * BLOCK:(2 * t + 1) * BLOCK, :] = qa[b * BLOCK:(b + 1) * BLOCK]
                qs_ref[b, (2 * t + 1) * BLOCK:(2 * t + 2) * BLOCK, :] = qb[b * BLOCK:(b + 1) * BLOCK]
    zz = proj(ATTN_WIDTH)
    kn = _pair_norm(zz[:, :LANES], gk_ref[...], left)
    vn = zz[:, LANES:]
    kw_ref[...] = kn.reshape(N_SEQ, BLOCK, KV_WIDTH)
    vw_ref[...] = vn.reshape(N_SEQ, BLOCK, KV_WIDTH)
    kk_ref[:, BLOCK:, :] = kn.astype(BF16).reshape(N_SEQ, BLOCK, KV_WIDTH)
    vv_ref[:, BLOCK:, :] = vn.astype(BF16).reshape(N_SEQ, BLOCK, KV_WIDTH)
    for c in range(2):
        zz = proj(ATTN_WIDTH + 2 * KV_WIDTH + c * 2 * LANES)
        for tt in range(2):
            for b in range(N_SEQ):
                usl_ref[2 * c + tt, b * PITCH:b * PITCH + BLOCK, :] = zz[b * BLOCK:(b + 1) * BLOCK, tt * LANES:(tt + 1) * LANES]

    r = lax.broadcasted_iota(jnp.int32, (BLOCK, 2 * BLOCK), 0)
    c = lax.broadcasted_iota(jnp.int32, (BLOCK, 2 * BLOCK), 1)
    valid = (c >= r) & (c <= r + WINDOW) & ((i > 0) | (c >= BLOCK - N_META))
    bias = jnp.where(valid, 0.0, -jnp.inf)

    def gather_u(n):
        t0 = n * CHUNK_T
        return jnp.concatenate(
            [jnp.concatenate([usl_ref[j, pl.ds(t0 + tl, SUBLANES, stride=PITCH), :] for j in range(N_QUARTERS)], axis=1)
             for tl in range(CHUNK_T)], axis=0)

    def feed(n, par):
        env = {}

        def scores():
            s_ref[par] = lax.dot_general(qs_ref[n], kk_ref[n], (((1,), (1,)), ((), ())), preferred_element_type=F32)

        def gather():
            env["u"] = gather_u(n)

        def scan_inputs(q):
            xs_ref[par, :, 2 * q * Q_STATE:(2 * q + 2) * Q_STATE] = jnp.dot(
                env["u"][:, q * LANES:(q + 1) * LANES].astype(BF16), wq_ref[q], preferred_element_type=F32)

        return [scores, gather] + [functools.partial(scan_inputs, q) for q in range(N_QUARTERS)]

    def mid(n, par):
        env = {}

        def row_max(g):
            s = s_ref[par, g * BLOCK:(g + 1) * BLOCK, :] + bias
            env[g] = (s, jnp.maximum(jnp.max(s, axis=-1, keepdims=True), sink_ref[g // 2 + Q_PER_KV * (g % 2)]))

        def exponent(g):
            s, m = env[g]
            p = jnp.exp(s - m)
            sink = sink_ref[g // 2 + Q_PER_KV * (g % 2)]
            env[g] = (p, 1.0 / (jnp.sum(p, axis=-1, keepdims=True) + jnp.exp(sink - m)))

        def normalise(g):
            p, inv = env.pop(g)
            p_ref[par, g * BLOCK:(g + 1) * BLOCK, :] = (p * inv).astype(BF16)

        softmax = [functools.partial(f, g) for g in range(2 * Q_TILES) for f in (row_max, exponent, normalise)]

        def load_state():
            for q in range(N_QUARTERS):
                env["lam", q] = (jnp.broadcast_to(lr_ref[:, q * Q_STATE:(q + 1) * Q_STATE], (SUBLANES, Q_STATE)),
                                 jnp.broadcast_to(li_ref[:, q * Q_STATE:(q + 1) * Q_STATE], (SUBLANES, Q_STATE)))
                env["h", q] = (st_ref[:, 2 * q * Q_STATE:(2 * q + 1) * Q_STATE],
                               st_ref[:, (2 * q + 1) * Q_STATE:(2 * q + 2) * Q_STATE])

        def scan_pair(tp, q):
            lr, li = env["lam", q]
            hr, hi = env["h", q]
            re0, im0 = 2 * q * Q_STATE, (2 * q + 1) * Q_STATE
            pair_r, pair_i = [], []
            for tl in (2 * tp, 2 * tp + 1):
                xr = xs_ref[par, tl * SUBLANES:(tl + 1) * SUBLANES, re0:re0 + Q_STATE]
                xi = xs_ref[par, tl * SUBLANES:(tl + 1) * SUBLANES, im0:im0 + Q_STATE]
                hr, hi = _scan_step(lr, li, hr, hi, xr, xi)
                pair_r.append(hr)
                pair_i.append(hi)
            env["h", q] = (hr, hi)
            rows = slice(2 * tp * SUBLANES, (2 * tp + 2) * SUBLANES)
            hb_ref[par, rows, re0:re0 + Q_STATE] = jnp.concatenate(pair_r, axis=0).astype(BF16)
            hb_ref[par, rows, im0:im0 + Q_STATE] = jnp.concatenate(pair_i, axis=0).astype(BF16)

        def store_state():
            for q in range(N_QUARTERS):
                hr, hi = env["h", q]
                st_ref[:, 2 * q * Q_STATE:(2 * q + 1) * Q_STATE] = hr
                st_ref[:, (2 * q + 1) * Q_STATE:(2 * q + 2) * Q_STATE] = hi

        scan = ([load_state] + [functools.partial(scan_pair, tp, q) for tp in range(CHUNK_T // 2) for q in range(N_QUARTERS)]
                + [store_state])
        return _interleave(softmax, scan)

    def tail(n, par):
        env = {}

        def attn_out():
            o_all = jnp.dot(p_ref[par], vv_ref[n], preferred_element_type=F32)
            for t in range(Q_TILES):
                oatt_ref[n, :, t * LANES:(t + 1) * LANES] = jnp.where(
                    left, o_all[2 * t * BLOCK:(2 * t + 1) * BLOCK], o_all[(2 * t + 1) * BLOCK:(2 * t + 2) * BLOCK])

        def gather():
            env["u"] = gather_u(n)

        def readout(q):
            env["y", q] = jnp.dot(hb_ref[par, :, 2 * q * Q_STATE:(2 * q + 2) * Q_STATE], cq_ref[q],
                                  preferred_element_type=F32)

        def activate(q):
            cols = slice(q * LANES, (q + 1) * LANES)
            env["g", q] = jax.nn.gelu(env.pop(("y", q)) + d_ref[:, cols] * env["u"][:, cols])

        def gate(q):
            cols = slice(q * LANES, (q + 1) * LANES)
            env["gate", q] = jnp.dot(env["g", q].astype(BF16), wglu_ref[q], preferred_element_type=F32) + bglu_ref[:, cols]

        def emit(q):
            o = env.pop(("g", q)) * jax.nn.sigmoid(env.pop(("gate", q)))
            t0 = n * CHUNK_T
            for tl in range(CHUNK_T):
                osl_ref[q, pl.ds(t0 + tl, SUBLANES, stride=PITCH), :] = o[tl * SUBLANES:(tl + 1) * SUBLANES]

        per_quarter = lambda f: [functools.partial(f, q) for q in range(N_QUARTERS)]
        return [attn_out, gather] + per_quarter(readout) + per_quarter(activate) + per_quarter(gate) + per_quarter(emit)

    def run(*stages):
        for step in _interleave(*stages):
            step()

    def steady(k, carry):
        n = 2 * k + 1
        run(feed(n + 1, 0), mid(n, 1), tail(n - 1, 0))
        run(feed(n + 2, 1), mid(n + 1, 0), tail(n, 1))
        return carry

    run(feed(0, 0))
    run(feed(1, 1), mid(0, 0))
    lax.fori_loop(0, (N_SEQ - 2) // 2, steady, 0)
    run(mid(N_SEQ - 1, 1), tail(N_SEQ - 2, 0))
    run(tail(N_SEQ - 1, 1))
    kk_ref[:, 0:BLOCK, :] = kk_ref[:, BLOCK:, :]
    vv_ref[:, 0:BLOCK, :] = vv_ref[:, BLOCK:, :]

    hr_ref[...] = jnp.concatenate([st_ref[:, 2 * q * Q_STATE:(2 * q + 1) * Q_STATE] for q in range(N_QUARTERS)], axis=1)
    hi_ref[...] = jnp.concatenate([st_ref[:, (2 * q + 1) * Q_STATE:(2 * q + 2) * Q_STATE] for q in range(N_QUARTERS)], axis=1)

    for b in range(N_SEQ):
        mix_ref[b, :, 0:ATTN_WIDTH] = _rms(oatt_ref[b], ga_ref[...]).astype(BF16)
        sl = [osl_ref[j, b * PITCH:b * PITCH + BLOCK, :] for j in range(N_QUARTERS)]
        ms = sum(jnp.sum(s * s, axis=-1, keepdims=True) for s in sl) / SSM_WIDTH
        inv = lax.rsqrt(ms + EPS)
        for j in range(N_QUARTERS):
            mix_ref[b, :, ATTN_WIDTH + j * LANES:ATTN_WIDTH + (j + 1) * LANES] = (
                sl[j] * inv * gs_ref[:, j * LANES:(j + 1) * LANES]).astype(BF16)


def _mixer(x_prompt, meta_tokens, sinks, gmix, win, gq2, gk2, sp, ga, gs):
    nseq, seq, _ = x_prompt.shape
    assert nseq == N_SEQ and seq % BLOCK == 0
    st = pl.BlockSpec((N_SEQ, STATE_COLS), lambda i: (0, 0))
    kvw = pl.BlockSpec((N_SEQ, BLOCK, KV_WIDTH), lambda i: (0, 0, 0))
    return pl.pallas_call(
        _mixer_kernel,
        grid=(seq // BLOCK,),
        in_specs=[pl.BlockSpec(memory_space=pltpu.SMEM),
                  pl.BlockSpec((N_SEQ, BLOCK, D_MODEL), lambda i: (0, i, 0)), _const_spec((N_META, D_MODEL)),
                  _const_spec((1, D_MODEL)), _const_spec((D_MODEL, IN_COLS)), _const_spec((1, LANES)),
                  _const_spec((1, LANES)),
                  _const_spec((N_QUARTERS, LANES, 2 * Q_STATE)), _const_spec((1, STATE_COLS)),
                  _const_spec((1, STATE_COLS)), _const_spec((N_QUARTERS, 2 * Q_STATE, LANES)),
                  _const_spec((1, SSM_WIDTH)), _const_spec((N_QUARTERS, LANES, LANES)), _const_spec((1, SSM_WIDTH)),
                  _const_spec((1, ATTN_WIDTH)), _const_spec((1, SSM_WIDTH))],
        out_specs=[pl.BlockSpec((N_SEQ, BLOCK, D_MODEL), lambda i: (0, i, 0)), kvw, kvw, st, st],
        out_shape=[jax.ShapeDtypeStruct((N_SEQ, seq, D_MODEL), BF16),
                   jax.ShapeDtypeStruct((N_SEQ, BLOCK, KV_WIDTH), F32), jax.ShapeDtypeStruct((N_SEQ, BLOCK, KV_WIDTH), F32),
                   jax.ShapeDtypeStruct((N_SEQ, STATE_COLS), F32), jax.ShapeDtypeStruct((N_SEQ, STATE_COLS), F32)],
        scratch_shapes=[pltpu.VMEM((N_SEQ, 2 * Q_TILES * BLOCK, LANES), BF16),
                        pltpu.VMEM((N_SEQ, 2 * BLOCK, KV_WIDTH), BF16),
                        pltpu.VMEM((N_SEQ, 2 * BLOCK, KV_WIDTH), BF16),
                        pltpu.VMEM((2, 2 * Q_TILES * BLOCK, 2 * BLOCK), F32),
                        pltpu.VMEM((2, 2 * Q_TILES * BLOCK, 2 * BLOCK), BF16),
                        pltpu.VMEM((N_SEQ, BLOCK, ATTN_WIDTH), F32),
                        pltpu.VMEM((N_QUARTERS, N_SEQ * PITCH, LANES), F32),
                        pltpu.VMEM((2, CHUNK_T * N_SEQ, 2 * STATE_COLS), F32),
                        pltpu.VMEM((2, CHUNK_T * N_SEQ, 2 * STATE_COLS), BF16),
                        pltpu.VMEM((N_SEQ, 2 * STATE_COLS), F32),
                        pltpu.VMEM((N_QUARTERS, N_SEQ * PITCH, LANES), F32)],
        compiler_params=pltpu.CompilerParams(dimension_semantics=("arbitrary",), vmem_limit_bytes=VMEM_LIMIT),
        name="mixer",
    )(sinks, x_prompt, meta_tokens, gmix, win, gq2, gk2, sp["wq"], sp["lr"], sp["li"], sp["cq"], sp["d"],
      sp["wgluq"], sp["bglu"], ga, gs)


def kernel(x_prompt, x_sample, cache_k_win, cache_v_win, state_ssm_re, state_ssm_im, meta_tokens, g_mix, w_in, g_q,
           g_k, sinks, ssm_a_re, ssm_a_im, ssm_log_dt, ssm_b_re, ssm_b_im, ssm_c_re, ssm_c_im, ssm_d, ssm_w_glu,
           ssm_b_glu, g_att_out, g_ssm_out, w_out, g_ffn, w_gate, w_up, w_down):
    bp, seq, _ = x_prompt.shape
    db, dseq, _ = x_sample.shape
    li = 0
    gmix = g_mix[li].reshape(1, D_MODEL)
    win = w_in[li].astype(BF16)
    gq2 = jnp.tile(g_q[li], 2).reshape(1, LANES)
    gk2 = jnp.tile(g_k[li], 2).reshape(1, LANES)
    sk = sinks[li]
    sp = _ssm_params(ssm_a_re[li], ssm_a_im[li], ssm_log_dt[li], ssm_b_re[li], ssm_b_im[li], ssm_c_re[li],
                     ssm_c_im[li], ssm_d[li], ssm_w_glu[li], ssm_b_glu[li])
    ga = g_att_out[li].reshape(1, ATTN_WIDTH)
    gs = g_ssm_out[li].reshape(1, SSM_WIDTH)
    wout = w_out[li].astype(BF16)
    ffn_w = (g_ffn[li].reshape(1, D_MODEL), w_gate[li].astype(BF16), w_up[li].astype(BF16), w_down[li].astype(BF16))
    regroup = lambda a, axis: jnp.swapaxes(
        a.reshape(a.shape[:axis] + (2, Q_PER_KV, HEAD_DIM) + a.shape[axis + 1:]), axis, axis + 1).reshape(a.shape)
    win = jnp.concatenate([regroup(win[:, :ATTN_WIDTH], 1), win[:, ATTN_WIDTH:]], axis=1)
    wout = jnp.concatenate([regroup(wout[:ATTN_WIDTH], 0), wout[ATTN_WIDTH:]], axis=0)
    ga = regroup(ga, 1)

    mix, kw_p, vw_p, hp_r, hp_i = _mixer(x_prompt, meta_tokens, sk, gmix, win, gq2, gk2, sp, ga, gs)
    y_prompt = _ffn_call(x_prompt.reshape(bp * seq, D_MODEL), mix.reshape(bp * seq, D_MODEL), wout, *ffn_w,
                         tm=512).reshape(bp, seq, D_MODEL)

    n_s = db * dseq
    xs_rows = x_sample.reshape(n_s, D_MODEL)
    q_s, k_s, v_s, u_s = _front(xs_rows, gmix, win, gq2, gk2, tm=512)
    ck = cache_k_win[li].reshape(db, WINDOW, KV_WIDTH)
    cv = cache_v_win[li].reshape(db, WINDOW, KV_WIDTH)
    oa_s, kw_s, vw_s = _attn_sample(q_s.reshape(db, dseq, ATTN_WIDTH), k_s.reshape(db, dseq, KV_WIDTH),
                                    v_s.reshape(db, dseq, KV_WIDTH), ck, cv, sk)
    os_s, hs_r, hs_i = _ssm_sample(u_s, state_ssm_re[li].reshape(db, STATE_COLS),
                                   state_ssm_im[li].reshape(db, STATE_COLS), sp, nseq=db, t_steps=dseq)
    y_sample = _back(xs_rows, oa_s.reshape(n_s, ATTN_WIDTH), os_s, ga, gs, wout, *ffn_w, tm=512).reshape(db, dseq, D_MODEL)

    kv5 = lambda a, n: a.reshape(1, n, WINDOW, N_KV_HEADS, HEAD_DIM)
    st4 = lambda a, n: a.reshape(1, n, N_SSM_GROUPS, SSM_STATE)
    return (y_prompt, y_sample, kv5(kw_p, bp), kv5(vw_p, bp), st4(hp_r, bp), st4(hp_i, bp),
            kv5(kw_s, db), kv5(vw_s, db), st4(hs_r, db), st4(hs_i, db))
```

```python
import functools

import jax
import jax.numpy as jnp
from jax import lax
from jax.experimental import pallas as pl
from jax.experimental.pallas import tpu as pltpu

D_MODEL = 1024
N_META = 16
HEAD_DIM = 64
ATTN_WIDTH = 512
N_Q_HEADS = 8
Q_PER_KV = 4
N_KV_HEADS = 2
KV_WIDTH = 128
WINDOW = 128
BLOCK = 128
SSM_WIDTH = 512
SSM_GROUP = 16
N_SSM_GROUPS = 32
SSM_STATE = 64
IN_COLS = ATTN_WIDTH + 2 * KV_WIDTH + SSM_WIDTH
D_FF = 2816
EPS = 1e-6
ATTN_SCALE = HEAD_DIM ** -0.5
STATE_COLS = N_SSM_GROUPS * SSM_STATE
LANES = 128
SUBLANES = 8
VMEM_LIMIT = 56 * 1024 * 1024

F32 = jnp.float32
BF16 = jnp.bfloat16


def _const_spec(shape):
    return pl.BlockSpec(shape, lambda *_: (0,) * len(shape), pipeline_mode=pl.Buffered(1))


def _rms(x, g):
    return x * lax.rsqrt(jnp.mean(x * x, axis=-1, keepdims=True) + EPS) * g


def _left_half():
    return lax.broadcasted_iota(jnp.int32, (1, LANES), 1) < HEAD_DIM


def _pair_norm(zz, g2, left):
    sq = zz * zz
    sl = jnp.sum(jnp.where(left, sq, 0.0), axis=-1, keepdims=True)
    sr = jnp.sum(jnp.where(left, 0.0, sq), axis=-1, keepdims=True)
    inv = jnp.where(left, lax.rsqrt(sl / HEAD_DIM + EPS), lax.rsqrt(sr / HEAD_DIM + EPS))
    return zz * inv * g2


def _front_kernel(x_ref, gmix_ref, win_ref, gq_ref, gk_ref, q_ref, k_ref, v_ref, u_ref):
    xn = _rms(x_ref[...], gmix_ref[...]).astype(BF16)
    z = jnp.dot(xn, win_ref[...], preferred_element_type=F32)
    left = _left_half()
    for p in range(ATTN_WIDTH // LANES):
        q_ref[:, p * LANES:(p + 1) * LANES] = _pair_norm(z[:, p * LANES:(p + 1) * LANES], gq_ref[...], left)
    k_ref[...] = _pair_norm(z[:, ATTN_WIDTH:ATTN_WIDTH + KV_WIDTH], gk_ref[...], left)
    v_ref[...] = z[:, ATTN_WIDTH + KV_WIDTH:ATTN_WIDTH + 2 * KV_WIDTH]
    u_ref[...] = z[:, ATTN_WIDTH + 2 * KV_WIDTH:]


def _front(x_rows, gmix, win_bf, gq2, gk2, tm):
    n = x_rows.shape[0]
    assert n % tm == 0
    row = lambda w: pl.BlockSpec((tm, w), lambda i: (i, 0))
    return pl.pallas_call(
        _front_kernel,
        grid=(n // tm,),
        in_specs=[row(D_MODEL), _const_spec((1, D_MODEL)), _const_spec((D_MODEL, IN_COLS)),
                  _const_spec((1, LANES)), _const_spec((1, LANES))],
        out_specs=[row(ATTN_WIDTH), row(KV_WIDTH), row(KV_WIDTH), row(SSM_WIDTH)],
        out_shape=[jax.ShapeDtypeStruct((n, ATTN_WIDTH), F32), jax.ShapeDtypeStruct((n, KV_WIDTH), F32),
                   jax.ShapeDtypeStruct((n, KV_WIDTH), F32), jax.ShapeDtypeStruct((n, SSM_WIDTH), F32)],
        compiler_params=pltpu.CompilerParams(dimension_semantics=("arbitrary",), vmem_limit_bytes=VMEM_LIMIT),
        name="front",
    )(x_rows, gmix, win_bf, gq2, gk2)


Q_TILES = ATTN_WIDTH // LANES


def _attn_sample_kernel(sink_ref, q_ref, kn_ref, vn_ref, ck_ref, cv_ref, o_ref, kw_ref, vw_ref, *, bb, t):
    left = _left_half()
    n_heads = 2 * Q_TILES
    rows = n_heads * t
    tk = WINDOW + t
    r = lax.broadcasted_iota(jnp.int32, (rows, tk), 0) % t
    c = lax.broadcasted_iota(jnp.int32, (rows, tk), 1)
    bias = jnp.where((c >= r) & (c <= r + WINDOW), 0.0, -jnp.inf)
    hrow = lax.broadcasted_iota(jnp.int32, (rows, 1), 0) // t
    sink = jnp.zeros((rows, 1), F32)
    for g in range(n_heads):
        sink = jnp.where(hrow == g, sink_ref[g // 2 + Q_PER_KV * (g % 2)], sink)
    scores, values = [], []
    for bi in range(bb):
        kk = jnp.concatenate([ck_ref[bi], kn_ref[bi]], axis=0)
        vv = jnp.concatenate([cv_ref[bi], vn_ref[bi]], axis=0)
        kw_ref[bi] = kk[t:]
        vw_ref[bi] = vv[t:]
        q = q_ref[bi] * ATTN_SCALE
        pieces = []
        for tile in range(Q_TILES):
            qt = q[:, tile * LANES:(tile + 1) * LANES]
            pieces += [jnp.where(left, qt, 0.0), jnp.where(left, 0.0, qt)]
        qs = jnp.concatenate(pieces, axis=0).astype(BF16)
        scores.append(lax.dot_general(qs, kk.astype(BF16), (((1,), (1,)), ((), ())), preferred_element_type=F32))
        values.append(vv.astype(BF16))
    probs = []
    for bi in range(bb):
        s = scores[bi] + bias
        m = jnp.maximum(jnp.max(s, axis=-1, keepdims=True), sink)
        p = jnp.exp(s - m)
        inv = 1.0 / (jnp.sum(p, axis=-1, keepdims=True) + jnp.exp(sink - m))
        probs.append((p * inv).astype(BF16))
    for bi in range(bb):
        o = jnp.dot(probs[bi], values[bi], preferred_element_type=F32)
        for tile in range(Q_TILES):
            o_ref[bi, :, tile * LANES:(tile + 1) * LANES] = jnp.where(
                left, o[2 * tile * t:(2 * tile + 1) * t], o[(2 * tile + 1) * t:(2 * tile + 2) * t])


def _attn_sample(q, kn, vn, ck, cv, sinks, bb=8):
    db, t, _ = q.shape
    blk = lambda r, w: pl.BlockSpec((bb, r, w), lambda i: (i, 0, 0))
    return pl.pallas_call(
        functools.partial(_attn_sample_kernel, bb=bb, t=t),
        grid=(db // bb,),
        in_specs=[pl.BlockSpec(memory_space=pltpu.SMEM), blk(t, ATTN_WIDTH), blk(t, KV_WIDTH), blk(t, KV_WIDTH),
                  blk(WINDOW, KV_WIDTH), blk(WINDOW, KV_WIDTH)],
        out_specs=[blk(t, ATTN_WIDTH), blk(WINDOW, KV_WIDTH), blk(WINDOW, KV_WIDTH)],
        out_shape=[jax.ShapeDtypeStruct((db, t, ATTN_WIDTH), F32),
                   jax.ShapeDtypeStruct((db, WINDOW, KV_WIDTH), F32),
                   jax.ShapeDtypeStruct((db, WINDOW, KV_WIDTH), F32)],
        compiler_params=pltpu.CompilerParams(dimension_semantics=("arbitrary",)),
        name="attn_sample",
    )(sinks, q, kn, vn, ck, cv)


N_QUARTERS = 4
Q_GROUPS = N_SSM_GROUPS // N_QUARTERS
Q_STATE = Q_GROUPS * SSM_STATE


def _ssm_params(a_re, a_im, log_dt, b_re, b_im, c_re, c_im, d_skip, w_glu, b_glu):
    dt = jnp.exp(log_dt)[:, None]
    mag = jnp.exp(a_re * dt)
    lr, li = mag * jnp.cos(a_im * dt), mag * jnp.sin(a_im * dt)
    nr, ni = lr - 1.0, li
    den = a_re * a_re + a_im * a_im
    fr, fi = (nr * a_re + ni * a_im) / den, (ni * a_re - nr * a_im) / den
    bbr = fr[..., None] * b_re - fi[..., None] * b_im
    bbi = fr[..., None] * b_im + fi[..., None] * b_re
    eye = jnp.eye(Q_GROUPS, dtype=F32)
    quartered = lambda a: a.reshape((a.shape[0], N_QUARTERS, Q_GROUPS) + a.shape[2:])
    wq = jnp.einsum('rqgph,gm->qghrmp', quartered(jnp.stack([bbr, bbi])), eye).reshape(N_QUARTERS, LANES, 2 * Q_STATE)
    cq = jnp.einsum('rqghp,gm->qrgpmh', quartered(jnp.stack([c_re, -c_im])), eye).reshape(N_QUARTERS, 2 * Q_STATE, LANES)
    wglu = jnp.einsum('qghk,gm->qghmk', w_glu.reshape(N_QUARTERS, Q_GROUPS, SSM_GROUP, SSM_GROUP), eye)
    return dict(wq=wq.astype(BF16), cq=cq.astype(BF16), wgluq=wglu.reshape(N_QUARTERS, LANES, LANES).astype(BF16),
                lr=lr.reshape(1, STATE_COLS), li=li.reshape(1, STATE_COLS), d=d_skip.reshape(1, SSM_WIDTH),
                bglu=b_glu.reshape(1, SSM_WIDTH))


def _scan_step(lr, li, hr, hi, xr, xi):
    return lr * hr - li * hi + xr, lr * hi + li * hr + xi


def _glu_readout(y, u, q, d_ref, wglu_ref, bglu_ref):
    cols = slice(q * LANES, (q + 1) * LANES)
    g = jax.nn.gelu(y + d_ref[:, cols] * u)
    gate = jnp.dot(g.astype(BF16), wglu_ref[q], preferred_element_type=F32) + bglu_ref[:, cols]
    return g * jax.nn.sigmoid(gate)


def _ssm_sample_kernel(u0_ref, u1_ref, u2_ref, u3_ref, h0r_ref, h0i_ref, wq_ref, lr_ref, li_ref, cq_ref, d_ref,
                       wglu_ref, bglu_ref, o_ref, hr_ref, hi_ref, utb_ref, xs_ref, otb_ref, *, nseq, t_steps):
    u_refs = (u0_ref, u1_ref, u2_ref, u3_ref)
    n_rg = nseq // SUBLANES
    rg_rows = SUBLANES * t_steps

    for t in range(t_steps):
        for rg in range(n_rg):
            for j in range(N_QUARTERS):
                utb_ref[t * nseq + rg * SUBLANES:t * nseq + (rg + 1) * SUBLANES, j * LANES:(j + 1) * LANES] = (
                    u_refs[j][pl.ds(rg * rg_rows + t, SUBLANES, stride=t_steps), :])

    for q in range(N_QUARTERS):
        uq = utb_ref[:, q * LANES:(q + 1) * LANES]
        xs_ref[...] = jnp.dot(uq.astype(BF16), wq_ref[q], preferred_element_type=F32)
        lr = jnp.broadcast_to(lr_ref[:, q * Q_STATE:(q + 1) * Q_STATE], (SUBLANES, Q_STATE))
        li = jnp.broadcast_to(li_ref[:, q * Q_STATE:(q + 1) * Q_STATE], (SUBLANES, Q_STATE))

        def scan_group(rg, carry, q=q, lr=lr, li=li):
            r0 = pl.multiple_of(rg * SUBLANES, SUBLANES)
            hr = h0r_ref[pl.ds(r0, SUBLANES), q * Q_STATE:(q + 1) * Q_STATE]
            hi = h0i_ref[pl.ds(r0, SUBLANES), q * Q_STATE:(q + 1) * Q_STATE]
            for t in range(t_steps):
                rows = pl.ds(pl.multiple_of(t * nseq + r0, SUBLANES), SUBLANES)
                hr, hi = _scan_step(lr, li, hr, hi, xs_ref[rows, 0:Q_STATE], xs_ref[rows, Q_STATE:])
                xs_ref[rows, 0:Q_STATE] = hr
                xs_ref[rows, Q_STATE:] = hi
            hr_ref[pl.ds(r0, SUBLANES), q * Q_STATE:(q + 1) * Q_STATE] = hr
            hi_ref[pl.ds(r0, SUBLANES), q * Q_STATE:(q + 1) * Q_STATE] = hi
            return carry

        lax.fori_loop(0, n_rg, scan_group, 0)
        y = jnp.dot(xs_ref[...].astype(BF16), cq_ref[q], preferred_element_type=F32)
        otb_ref[...] = _glu_readout(y, uq, q, d_ref, wglu_ref, bglu_ref)
        for t in range(t_steps):
            for rg in range(n_rg):
                o_ref[q, pl.ds(rg * rg_rows + t, SUBLANES, stride=t_steps), :] = (
                    otb_ref[t * nseq + rg * SUBLANES:t * nseq + (rg + 1) * SUBLANES, :])


def _ssm_sample(u_rows, h0r, h0i, sp, nseq, t_steps):
    n = nseq * t_steps
    slab = lambda j: pl.BlockSpec((n, LANES), lambda i, j=j: (0, j))
    st = pl.BlockSpec((nseq, STATE_COLS), lambda i: (0, 0))
    return pl.pallas_call(
        functools.partial(_ssm_sample_kernel, nseq=nseq, t_steps=t_steps),
        grid=(1,),
        in_specs=[slab(0), slab(1), slab(2), slab(3), st, st,
                  _const_spec((N_QUARTERS, LANES, 2 * Q_STATE)), _const_spec((1, STATE_COLS)),
                  _const_spec((1, STATE_COLS)), _const_spec((N_QUARTERS, 2 * Q_STATE, LANES)),
                  _const_spec((1, SSM_WIDTH)), _const_spec((N_QUARTERS, LANES, LANES)), _const_spec((1, SSM_WIDTH))],
        out_specs=[pl.BlockSpec((N_QUARTERS, n, LANES), lambda i: (0, 0, 0)), st, st],
        out_shape=[jax.ShapeDtypeStruct((N_QUARTERS, n, LANES), F32),
                   jax.ShapeDtypeStruct((nseq, STATE_COLS), F32), jax.ShapeDtypeStruct((nseq, STATE_COLS), F32)],
        scratch_shapes=[pltpu.VMEM((n, SSM_WIDTH), F32), pltpu.VMEM((n, 2 * Q_STATE), F32), pltpu.VMEM((n, LANES), F32)],
        compiler_params=pltpu.CompilerParams(dimension_semantics=("arbitrary",), vmem_limit_bytes=VMEM_LIMIT),
        name="ssm_sample",
    )(u_rows, u_rows, u_rows, u_rows, h0r, h0i, sp["wq"], sp["lr"], sp["li"], sp["cq"], sp["d"], sp["wgluq"], sp["bglu"])


def _ffn(h, gf_ref, wg_ref, wu_ref, wd_ref):
    f = _rms(h, gf_ref[...]).astype(BF16)
    gate = jnp.dot(f, wg_ref[...], preferred_element_type=F32)
    up = jnp.dot(f, wu_ref[...], preferred_element_type=F32)
    a = (jax.nn.silu(gate) * up).astype(BF16)
    return h + jnp.dot(a, wd_ref[...], preferred_element_type=F32)


def _back_kernel(x_ref, oa_ref, os_ref, ga_ref, gs_ref, wout_ref, gf_ref, wg_ref, wu_ref, wd_ref, y_ref):
    o_ssm = jnp.concatenate([os_ref[j] for j in range(N_QUARTERS)], axis=-1)
    mix = jnp.concatenate([_rms(oa_ref[...], ga_ref[...]), _rms(o_ssm, gs_ref[...])], axis=-1).astype(BF16)
    h = x_ref[...] + jnp.dot(mix, wout_ref[...], preferred_element_type=F32)
    y_ref[...] = _ffn(h, gf_ref, wg_ref, wu_ref, wd_ref)


def _ffn_kernel(x_ref, hp_ref, gf_ref, wg_ref, wu_ref, wd_ref, y_ref):
    hp = jnp.concatenate([hp_ref[c] for c in range(hp_ref.shape[0])], axis=-1)
    y_ref[...] = _ffn(x_ref[...] + hp, gf_ref, wg_ref, wu_ref, wd_ref)


def _back(x_rows, oa, os_, ga, gs, wout, gf, wg, wu, wd, tm):
    n = x_rows.shape[0]
    assert n % tm == 0
    row = lambda w: pl.BlockSpec((tm, w), lambda i: (i, 0))
    return pl.pallas_call(
        _back_kernel,
        grid=(n // tm,),
        in_specs=[row(D_MODEL), row(ATTN_WIDTH), pl.BlockSpec((N_QUARTERS, tm, LANES), lambda i: (0, i, 0)),
                  _const_spec((1, ATTN_WIDTH)), _const_spec((1, SSM_WIDTH)), _const_spec((D_MODEL, D_MODEL)),
                  _const_spec((1, D_MODEL)), _const_spec((D_MODEL, D_FF)), _const_spec((D_MODEL, D_FF)),
                  _const_spec((D_FF, D_MODEL))],
        out_specs=row(D_MODEL),
        out_shape=jax.ShapeDtypeStruct((n, D_MODEL), F32),
        compiler_params=pltpu.CompilerParams(dimension_semantics=("arbitrary",), vmem_limit_bytes=VMEM_LIMIT),
        name="back",
    )(x_rows, oa, os_, ga, gs, wout, gf, wg, wu, wd)


def _ffn_call(x_rows, hp_rows, gf, wg, wu, wd, tm):
    n = x_rows.shape[0]
    assert n % tm == 0
    row = lambda w: pl.BlockSpec((tm, w), lambda i: (i, 0))
    return pl.pallas_call(
        _ffn_kernel,
        grid=(n // tm,),
        in_specs=[row(D_MODEL), pl.BlockSpec((hp_rows.shape[0], tm, hp_rows.shape[2]), lambda i: (0, i, 0)),
                  _const_spec((1, D_MODEL)),
                  _const_spec((D_MODEL, D_FF)), _const_spec((D_MODEL, D_FF)), _const_spec((D_FF, D_MODEL))],
        out_specs=row(D_MODEL),
        out_shape=jax.ShapeDtypeStruct((n, D_MODEL), F32),
        compiler_params=pltpu.CompilerParams(dimension_semantics=("arbitrary",), vmem_limit_bytes=VMEM_LIMIT),
        name="ffn",
    )(x_rows, hp_rows, gf, wg, wu, wd)


N_SEQ = 8
ROWS = N_SEQ * BLOCK
PITCH = BLOCK + SUBLANES
CHUNK_T = BLOCK // N_SEQ


def _interleave(*stages):
    keyed = [((i + 0.5) / len(steps), k, i, step) for k, steps in enumerate(stages) for i, step in enumerate(steps)]
    return [step for _, _, _, step in sorted(keyed, key=lambda e: e[:3])]


def _mixer_kernel(sink_ref, x_ref, meta_ref, gmix_ref, win_ref, gq_ref, gk_ref,
                  wq_ref, lr_ref, li_ref, cq_ref, d_ref, wglu_ref, bglu_ref, ga_ref, gs_ref, wout_ref,
                  hp_ref, kw_ref, vw_ref, hr_ref, hi_ref,
                  qs_ref, kk_ref, vv_ref, s_ref, p_ref, oatt_ref, usl_ref, xs_ref, hb_ref, st_ref, osl_ref, mix_ref):
    i = pl.program_id(0)
    n_blocks = pl.num_programs(0) - 1

    @pl.when(i < n_blocks)
    def _():
        _mixer_block(i, sink_ref, x_ref, meta_ref, gmix_ref, win_ref, gq_ref, gk_ref,
                     wq_ref, lr_ref, li_ref, cq_ref, d_ref, wglu_ref, bglu_ref, ga_ref, gs_ref, wout_ref,
                     hp_ref, kw_ref, vw_ref, hr_ref, hi_ref,
                     qs_ref, kk_ref, vv_ref, s_ref, p_ref, oatt_ref, usl_ref, xs_ref, hb_ref, st_ref, osl_ref, mix_ref)

    @pl.when(i == n_blocks)
    def _():
        for c in range(OUT_CHUNKS):
            _project_chunk(mix_ref, wout_ref, hp_ref, c)


OUT_CHUNKS = 4
OUT_CHUNK = D_MODEL // OUT_CHUNKS


def _project_chunk(mix_ref, wout_ref, hp_ref, c):
    hp_ref[c] = jnp.dot(mix_ref[...], wout_ref[c], preferred_element_type=F32).reshape(N_SEQ, BLOCK, OUT_CHUNK)


def _mixer_block(i, sink_ref, x_ref, meta_ref, gmix_ref, win_ref, gq_ref, gk_ref,
                 wq_ref, lr_ref, li_ref, cq_ref, d_ref, wglu_ref, bglu_ref, ga_ref, gs_ref, wout_ref,
                 hp_ref, kw_ref, vw_ref, hr_ref, hi_ref,
                 qs_ref, kk_ref, vv_ref, s_ref, p_ref, oatt_ref, usl_ref, xs_ref, hb_ref, st_ref, osl_ref, mix_ref):
    left = _left_half()
    project = lambda c: [functools.partial(_project_chunk, mix_ref, wout_ref, hp_ref, c)]

    @pl.when(i == 0)
    def _():
        mix_ref[...] = jnp.zeros((ROWS, D_MODEL), BF16)
        xm = _rms(meta_ref[...], gmix_ref[...]).astype(BF16)
        zm = jnp.dot(xm, win_ref[:, ATTN_WIDTH:], preferred_element_type=F32)
        km = _pair_norm(zm[:, :KV_WIDTH], gk_ref[...], left).astype(BF16)
        vm = zm[:, KV_WIDTH:2 * KV_WIDTH].astype(BF16)
        lead = jnp.zeros((BLOCK - N_META, KV_WIDTH), BF16)
        for b in range(N_SEQ):
            kk_ref[b, 0:BLOCK - N_META, :] = lead
            vv_ref[b, 0:BLOCK - N_META, :] = lead
            kk_ref[b, BLOCK - N_META:BLOCK, :] = km
            vv_ref[b, BLOCK - N_META:BLOCK, :] = vm
        um = zm[:, 2 * KV_WIDTH:]
        for q in range(N_QUARTERS):
            xm_q = jnp.dot(um[:, q * LANES:(q + 1) * LANES].astype(BF16), wq_ref[q], preferred_element_type=F32)
            lr = jnp.broadcast_to(lr_ref[:, q * Q_STATE:(q + 1) * Q_STATE], (SUBLANES, Q_STATE))
            li = jnp.broadcast_to(li_ref[:, q * Q_STATE:(q + 1) * Q_STATE], (SUBLANES, Q_STATE))
            hr = jnp.zeros((SUBLANES, Q_STATE), F32)
            hi = jnp.zeros((SUBLANES, Q_STATE), F32)
            for t in range(N_META):
                xr = jnp.broadcast_to(xm_q[t:t + 1, 0:Q_STATE], (SUBLANES, Q_STATE))
                xi = jnp.broadcast_to(xm_q[t:t + 1, Q_STATE:], (SUBLANES, Q_STATE))
                hr, hi = _scan_step(lr, li, hr, hi, xr, xi)
            st_ref[:, 2 * q * Q_STATE:(2 * q + 1) * Q_STATE] = hr
            st_ref[:, (2 * q + 1) * Q_STATE:(2 * q + 2) * Q_STATE] = hi

    xn = _rms(x_ref[...].reshape(ROWS, D_MODEL), gmix_ref[...]).astype(BF16)

    def proj(c0):
        return jnp.dot(xn, win_ref[:, c0:c0 + 2 * LANES], preferred_element_type=F32)

    for c in range(Q_TILES // 2):
        zz = proj(c * 2 * LANES)
        for tt in range(2):
            t = 2 * c + tt
            qn = _pair_norm(zz[:, tt * LANES:(tt + 1) * LANES], gq_ref[...], left) * ATTN_SCALE
            qa = jnp.where(left, qn, 0.0).astype(BF16)
            qb = jnp.where(left, 0.0, qn).astype(BF16)
            for b in range(N_SEQ):
                qs_ref[b, 2 * t * BLOCK:(2 * t + 1) * BLOCK, :] = qa[b * BLOCK:(b + 1) * BLOCK]
                qs_ref[b, (2 * t + 1) * BLOCK:(2 * t + 2) * BLOCK, :] = qb[b * BLOCK:(b + 1) * BLOCK]
    zz = proj(ATTN_WIDTH)
    kn = _pair_norm(zz[:, :LANES], gk_ref[...], left)
    vn = zz[:, LANES:]
    kw_ref[...] = kn.reshape(N_SEQ, BLOCK, KV_WIDTH)
    vw_ref[...] = vn.reshape(N_SEQ, BLOCK, KV_WIDTH)
    kk_ref[:, BLOCK:, :] = kn.astype(BF16).reshape(N_SEQ, BLOCK, KV_WIDTH)
    vv_ref[:, BLOCK:, :] = vn.astype(BF16).reshape(N_SEQ, BLOCK, KV_WIDTH)
    for c in range(2):
        zz = proj(ATTN_WIDTH + 2 * KV_WIDTH + c * 2 * LANES)
        for tt in range(2):
            for b in range(N_SEQ):
                usl_ref[2 * c + tt, b * PITCH:b * PITCH + BLOCK, :] = zz[b * BLOCK:(b + 1) * BLOCK, tt * LANES:(tt + 1) * LANES]

    r = lax.broadcasted_iota(jnp.int32, (BLOCK, 2 * BLOCK), 0)
    c = lax.broadcasted_iota(jnp.int32, (BLOCK, 2 * BLOCK), 1)
    valid = (c >= r) & (c <= r + WINDOW) & ((i > 0) | (c >= BLOCK - N_META))
    bias = jnp.where(valid, 0.0, -jnp.inf)

    def gather_u(n):
        t0 = n * CHUNK_T
        return jnp.concatenate(
            [jnp.concatenate([usl_ref[j, pl.ds(t0 + tl, SUBLANES, stride=PITCH), :] for j in range(N_QUARTERS)], axis=1)
             for tl in range(CHUNK_T)], axis=0)

    def feed(n, par):
        env = {}

        def scores():
            s_ref[par] = lax.dot_general(qs_ref[n], kk_ref[n], (((1,), (1,)), ((), ())), preferred_element_type=F32)

        def gather():
            env["u"] = gather_u(n)

        def scan_inputs(q):
            xs_ref[par, :, 2 * q * Q_STATE:(2 * q + 2) * Q_STATE] = jnp.dot(
                env["u"][:, q * LANES:(q + 1) * LANES].astype(BF16), wq_ref[q], preferred_element_type=F32)

        return [scores, gather] + [functools.partial(scan_inputs, q) for q in range(N_QUARTERS)]

    def mid(n, par):
        env = {}

        def row_max(g):
            s = s_ref[par, g * BLOCK:(g + 1) * BLOCK, :] + bias
            env[g] = (s, jnp.maximum(jnp.max(s, axis=-1, keepdims=True), sink_ref[g // 2 + Q_PER_KV * (g % 2)]))

        def exponent(g):
            s, m = env[g]
            p = jnp.exp(s - m)
            sink = sink_ref[g // 2 + Q_PER_KV * (g % 2)]
            env[g] = (p, 1.0 / (jnp.sum(p, axis=-1, keepdims=True) + jnp.exp(sink - m)))

        def normalise(g):
            p, inv = env.pop(g)
            p_ref[par, g * BLOCK:(g + 1) * BLOCK, :] = (p * inv).astype(BF16)

        softmax = [functools.partial(f, g) for g in range(2 * Q_TILES) for f in (row_max, exponent, normalise)]

        def load_state():
            for q in range(N_QUARTERS):
                env["lam", q] = (jnp.broadcast_to(lr_ref[:, q * Q_STATE:(q + 1) * Q_STATE], (SUBLANES, Q_STATE)),
                                 jnp.broadcast_to(li_ref[:, q * Q_STATE:(q + 1) * Q_STATE], (SUBLANES, Q_STATE)))
                env["h", q] = (st_ref[:, 2 * q * Q_STATE:(2 * q + 1) * Q_STATE],
                               st_ref[:, (2 * q + 1) * Q_STATE:(2 * q + 2) * Q_STATE])

        def scan_pair(tp, q):
            lr, li = env["lam", q]
            hr, hi = env["h", q]
            re0, im0 = 2 * q * Q_STATE, (2 * q + 1) * Q_STATE
            pair_r, pair_i = [], []
            for tl in (2 * tp, 2 * tp + 1):
                xr = xs_ref[par, tl * SUBLANES:(tl + 1) * SUBLANES, re0:re0 + Q_STATE]
                xi = xs_ref[par, tl * SUBLANES:(tl + 1) * SUBLANES, im0:im0 + Q_STATE]
                hr, hi = _scan_step(lr, li, hr, hi, xr, xi)
                pair_r.append(hr)
                pair_i.append(hi)
            env["h", q] = (hr, hi)
            rows = slice(2 * tp * SUBLANES, (2 * tp + 2) * SUBLANES)
            hb_ref[par, rows, re0:re0 + Q_STATE] = jnp.concatenate(pair_r, axis=0).astype(BF16)
            hb_ref[par, rows, im0:im0 + Q_STATE] = jnp.concatenate(pair_i, axis=0).astype(BF16)

        def store_state():
            for q in range(N_QUARTERS):
                hr, hi = env["h", q]
                st_ref[:, 2 * q * Q_STATE:(2 * q + 1) * Q_STATE] = hr
                st_ref[:, (2 * q + 1) * Q_STATE:(2 * q + 2) * Q_STATE] = hi

        scan = ([load_state] + [functools.partial(scan_pair, tp, q) for tp in range(CHUNK_T // 2) for q in range(N_QUARTERS)]
                + [store_state])
        return _interleave(softmax, scan)

    def tail(n, par):
        env = {}

        def attn_out():
            o_all = jnp.dot(p_ref[par], vv_ref[n], preferred_element_type=F32)
            for t in range(Q_TILES):
                oatt_ref[n, :, t * LANES:(t + 1) * LANES] = jnp.where(
                    left, o_all[2 * t * BLOCK:(2 * t + 1) * BLOCK], o_all[(2 * t + 1) * BLOCK:(2 * t + 2) * BLOCK])

        def gather():
            env["u"] = gather_u(n)

        def readout(q):
            env["y", q] = jnp.dot(hb_ref[par, :, 2 * q * Q_STATE:(2 * q + 2) * Q_STATE], cq_ref[q],
                                  preferred_element_type=F32)

        def activate(q):
            cols = slice(q * LANES, (q + 1) * LANES)
            env["g", q] = jax.nn.gelu(env.pop(("y", q)) + d_ref[:, cols] * env["u"][:, cols])

        def gate(q):
            cols = slice(q * LANES, (q + 1) * LANES)
            env["gate", q] = jnp.dot(env["g", q].astype(BF16), wglu_ref[q], preferred_element_type=F32) + bglu_ref[:, cols]

        def emit(q):
            o = env.pop(("g", q)) * jax.nn.sigmoid(env.pop(("gate", q)))
            t0 = n * CHUNK_T
            for tl in range(CHUNK_T):
                osl_ref[q, pl.ds(t0 + tl, SUBLANES, stride=PITCH), :] = o[tl * SUBLANES:(tl + 1) * SUBLANES]

        per_quarter = lambda f: [functools.partial(f, q) for q in range(N_QUARTERS)]
        return [attn_out, gather] + per_quarter(readout) + per_quarter(activate) + per_quarter(gate) + per_quarter(emit)

    def run(*stages):
        for step in _interleave(*stages):
            step()

    def steady(k, carry):
        n = 2 * k + 1
        run(feed(n + 1, 0), mid(n, 1), tail(n - 1, 0), project(k))
        run(feed(n + 2, 1), mid(n + 1, 0), tail(n, 1))
        return carry

    run(feed(0, 0))
    run(feed(1, 1), mid(0, 0))
    n_steady = (N_SEQ - 2) // 2
    assert n_steady == OUT_CHUNKS - 1
    lax.fori_loop(0, n_steady, steady, 0)
    run(mid(N_SEQ - 1, 1), tail(N_SEQ - 2, 0), project(OUT_CHUNKS - 1))
    run(tail(N_SEQ - 1, 1))
    kk_ref[:, 0:BLOCK, :] = kk_ref[:, BLOCK:, :]
    vv_ref[:, 0:BLOCK, :] = vv_ref[:, BLOCK:, :]

    hr_ref[...] = jnp.concatenate([st_ref[:, 2 * q * Q_STATE:(2 * q + 1) * Q_STATE] for q in range(N_QUARTERS)], axis=1)
    hi_ref[...] = jnp.concatenate([st_ref[:, (2 * q + 1) * Q_STATE:(2 * q + 2) * Q_STATE] for q in range(N_QUARTERS)], axis=1)

    for b in range(N_SEQ):
        rows = slice(b * BLOCK, (b + 1) * BLOCK)
        mix_ref[rows, 0:ATTN_WIDTH] = _rms(oatt_ref[b], ga_ref[...]).astype(BF16)
        sl = [osl_ref[j, b * PITCH:b * PITCH + BLOCK, :] for j in range(N_QUARTERS)]
        ms = sum(jnp.sum(s * s, axis=-1, keepdims=True) for s in sl) / SSM_WIDTH
        inv = lax.rsqrt(ms + EPS)
        for j in range(N_QUARTERS):
            mix_ref[rows, ATTN_WIDTH + j * LANES:ATTN_WIDTH + (j + 1) * LANES] = (
                sl[j] * inv * gs_ref[:, j * LANES:(j + 1) * LANES]).astype(BF16)


def _mixer(x_prompt, meta_tokens, sinks, gmix, win, gq2, gk2, sp, ga, gs, wout):
    nseq, seq, _ = x_prompt.shape
    assert nseq == N_SEQ and seq % BLOCK == 0
    n_blocks = seq // BLOCK
    st = pl.BlockSpec((N_SEQ, STATE_COLS), lambda i: (0, 0))
    kvw = pl.BlockSpec((N_SEQ, BLOCK, KV_WIDTH), lambda i: (0, 0, 0))
    return pl.pallas_call(
        _mixer_kernel,
        grid=(n_blocks + 1,),
        in_specs=[pl.BlockSpec(memory_space=pltpu.SMEM),
                  pl.BlockSpec((N_SEQ, BLOCK, D_MODEL), lambda i: (0, jnp.minimum(i, n_blocks - 1), 0)),
                  _const_spec((N_META, D_MODEL)),
                  _const_spec((1, D_MODEL)), _const_spec((D_MODEL, IN_COLS)), _const_spec((1, LANES)),
                  _const_spec((1, LANES)),
                  _const_spec((N_QUARTERS, LANES, 2 * Q_STATE)), _const_spec((1, STATE_COLS)),
                  _const_spec((1, STATE_COLS)), _const_spec((N_QUARTERS, 2 * Q_STATE, LANES)),
                  _const_spec((1, SSM_WIDTH)), _const_spec((N_QUARTERS, LANES, LANES)), _const_spec((1, SSM_WIDTH)),
                  _const_spec((1, ATTN_WIDTH)), _const_spec((1, SSM_WIDTH)),
                  _const_spec((OUT_CHUNKS, D_MODEL, OUT_CHUNK))],
        out_specs=[pl.BlockSpec((OUT_CHUNKS, N_SEQ, BLOCK, OUT_CHUNK), lambda i: (0, 0, jnp.maximum(i - 1, 0), 0)),
                   kvw, kvw, st, st],
        out_shape=[jax.ShapeDtypeStruct((OUT_CHUNKS, N_SEQ, seq, OUT_CHUNK), F32),
                   jax.ShapeDtypeStruct((N_SEQ, BLOCK, KV_WIDTH), F32), jax.ShapeDtypeStruct((N_SEQ, BLOCK, KV_WIDTH), F32),
                   jax.ShapeDtypeStruct((N_SEQ, STATE_COLS), F32), jax.ShapeDtypeStruct((N_SEQ, STATE_COLS), F32)],
        scratch_shapes=[pltpu.VMEM((N_SEQ, 2 * Q_TILES * BLOCK, LANES), BF16),
                        pltpu.VMEM((N_SEQ, 2 * BLOCK, KV_WIDTH), BF16),
                        pltpu.VMEM((N_SEQ, 2 * BLOCK, KV_WIDTH), BF16),
                        pltpu.VMEM((2, 2 * Q_TILES * BLOCK, 2 * BLOCK), F32),
                        pltpu.VMEM((2, 2 * Q_TILES * BLOCK, 2 * BLOCK), BF16),
                        pltpu.VMEM((N_SEQ, BLOCK, ATTN_WIDTH), F32),
                        pltpu.VMEM((N_QUARTERS, N_SEQ * PITCH, LANES), F32),
                        pltpu.VMEM((2, CHUNK_T * N_SEQ, 2 * STATE_COLS), F32),
                        pltpu.VMEM((2, CHUNK_T * N_SEQ, 2 * STATE_COLS), BF16),
                        pltpu.VMEM((N_SEQ, 2 * STATE_COLS), F32),
                        pltpu.VMEM((N_QUARTERS, N_SEQ * PITCH, LANES), F32),
                        pltpu.VMEM((ROWS, D_MODEL), BF16)],
        compiler_params=pltpu.CompilerParams(dimension_semantics=("arbitrary",), vmem_limit_bytes=VMEM_LIMIT),
        name="mixer",
    )(sinks, x_prompt, meta_tokens, gmix, win, gq2, gk2, sp["wq"], sp["lr"], sp["li"], sp["cq"], sp["d"],
      sp["wgluq"], sp["bglu"], ga, gs, wout)


def kernel(x_prompt, x_sample, cache_k_win, cache_v_win, state_ssm_re, state_ssm_im, meta_tokens, g_mix, w_in, g_q,
           g_k, sinks, ssm_a_re, ssm_a_im, ssm_log_dt, ssm_b_re, ssm_b_im, ssm_c_re, ssm_c_im, ssm_d, ssm_w_glu,
           ssm_b_glu, g_att_out, g_ssm_out, w_out, g_ffn, w_gate, w_up, w_down):
    bp, seq, _ = x_prompt.shape
    db, dseq, _ = x_sample.shape
    li = 0
    gmix = g_mix[li].reshape(1, D_MODEL)
    win = w_in[li].astype(BF16)
    gq2 = jnp.tile(g_q[li], 2).reshape(1, LANES)
    gk2 = jnp.tile(g_k[li], 2).reshape(1, LANES)
    sk = sinks[li]
    sp = _ssm_params(ssm_a_re[li], ssm_a_im[li], ssm_log_dt[li], ssm_b_re[li], ssm_b_im[li], ssm_c_re[li],
                     ssm_c_im[li], ssm_d[li], ssm_w_glu[li], ssm_b_glu[li])
    ga = g_att_out[li].reshape(1, ATTN_WIDTH)
    gs = g_ssm_out[li].reshape(1, SSM_WIDTH)
    wout = w_out[li].astype(BF16)
    ffn_w = (g_ffn[li].reshape(1, D_MODEL), w_gate[li].astype(BF16), w_up[li].astype(BF16), w_down[li].astype(BF16))
    regroup = lambda a, axis: jnp.swapaxes(
        a.reshape(a.shape[:axis] + (2, Q_PER_KV, HEAD_DIM) + a.shape[axis + 1:]), axis, axis + 1).reshape(a.shape)
    win = jnp.concatenate([regroup(win[:, :ATTN_WIDTH], 1), win[:, ATTN_WIDTH:]], axis=1)
    wout = jnp.concatenate([regroup(wout[:ATTN_WIDTH], 0), wout[ATTN_WIDTH:]], axis=0)
    ga = regroup(ga, 1)

    wout_chunks = wout.reshape(D_MODEL, OUT_CHUNKS, OUT_CHUNK).transpose(1, 0, 2)
    hproj, kw_p, vw_p, hp_r, hp_i = _mixer(x_prompt, meta_tokens, sk, gmix, win, gq2, gk2, sp, ga, gs, wout_chunks)
    y_prompt = _ffn_call(x_prompt.reshape(bp * seq, D_MODEL), hproj.reshape(OUT_CHUNKS, bp * seq, OUT_CHUNK), *ffn_w,
                         tm=512).reshape(bp, seq, D_MODEL)

    n_s = db * dseq
    xs_rows = x_sample.reshape(n_s, D_MODEL)
    q_s, k_s, v_s, u_s = _front(xs_rows, gmix, win, gq2, gk2, tm=512)
    ck = cache_k_win[li].reshape(db, WINDOW, KV_WIDTH)
    cv = cache_v_win[li].reshape(db, WINDOW, KV_WIDTH)
    oa_s, kw_s, vw_s = _attn_sample(q_s.reshape(db, dseq, ATTN_WIDTH), k_s.reshape(db, dseq, KV_WIDTH),
                                    v_s.reshape(db, dseq, KV_WIDTH), ck, cv, sk)
    os_s, hs_r, hs_i = _ssm_sample(u_s, state_ssm_re[li].reshape(db, STATE_COLS),
                                   state_ssm_im[li].reshape(db, STATE_COLS), sp, nseq=db, t_steps=dseq)
    y_sample = _back(xs_rows, oa_s.reshape(n_s, ATTN_WIDTH), os_s, ga, gs, wout, *ffn_w, tm=512).reshape(db, dseq, D_MODEL)

    kv5 = lambda a, n: a.reshape(1, n, WINDOW, N_KV_HEADS, HEAD_DIM)
    st4 = lambda a, n: a.reshape(1, n, N_SSM_GROUPS, SSM_STATE)
    return (y_prompt, y_sample, kv5(kw_p, bp), kv5(vw_p, bp), st4(hp_r, bp), st4(hp_i, bp),
            kv5(kw_s, db), kv5(vw_s, db), st4(hs_r, db), st4(hs_i, db))
```

```python
import functools

import jax
import jax.numpy as jnp
from jax import lax
from jax.experimental import pallas as pl
from jax.experimental.pallas import tpu as pltpu

D_MODEL = 1024
N_META = 16
HEAD_DIM = 64
ATTN_WIDTH = 512
N_Q_HEADS = 8
Q_PER_KV = 4
N_KV_HEADS = 2
KV_WIDTH = 128
WINDOW = 128
BLOCK = 128
SSM_WIDTH = 512
SSM_GROUP = 16
N_SSM_GROUPS = 32
SSM_STATE = 64
IN_COLS = ATTN_WIDTH + 2 * KV_WIDTH + SSM_WIDTH
D_FF = 2816
EPS = 1e-6
ATTN_SCALE = HEAD_DIM ** -0.5
STATE_COLS = N_SSM_GROUPS * SSM_STATE
LANES = 128
SUBLANES = 8
VMEM_LIMIT = 56 * 1024 * 1024

F32 = jnp.float32
BF16 = jnp.bfloat16


def _const_spec(shape):
    return pl.BlockSpec(shape, lambda *_: (0,) * len(shape), pipeline_mode=pl.Buffered(1))


def _rms(x, g):
    return x * lax.rsqrt(jnp.mean(x * x, axis=-1, keepdims=True) + EPS) * g


def _left_half():
    return lax.broadcasted_iota(jnp.int32, (1, LANES), 1) < HEAD_DIM


def _pair_norm(zz, g2, left):
    sq = zz * zz
    sl = jnp.sum(jnp.where(left, sq, 0.0), axis=-1, keepdims=True)
    sr = jnp.sum(jnp.where(left, 0.0, sq), axis=-1, keepdims=True)
    inv = jnp.where(left, lax.rsqrt(sl / HEAD_DIM + EPS), lax.rsqrt(sr / HEAD_DIM + EPS))
    return zz * inv * g2


def _front_kernel(x_ref, gmix_ref, win_ref, gq_ref, gk_ref, q_ref, k_ref, v_ref, u_ref):
    xn = _rms(x_ref[...], gmix_ref[...]).astype(BF16)
    z = jnp.dot(xn, win_ref[...], preferred_element_type=F32)
    left = _left_half()
    for p in range(ATTN_WIDTH // LANES):
        q_ref[:, p * LANES:(p + 1) * LANES] = _pair_norm(z[:, p * LANES:(p + 1) * LANES], gq_ref[...], left)
    k_ref[...] = _pair_norm(z[:, ATTN_WIDTH:ATTN_WIDTH + KV_WIDTH], gk_ref[...], left)
    v_ref[...] = z[:, ATTN_WIDTH + KV_WIDTH:ATTN_WIDTH + 2 * KV_WIDTH]
    u_ref[...] = z[:, ATTN_WIDTH + 2 * KV_WIDTH:]


def _front(x_rows, gmix, win_bf, gq2, gk2, tm):
    n = x_rows.shape[0]
    assert n % tm == 0
    row = lambda w: pl.BlockSpec((tm, w), lambda i: (i, 0))
    return pl.pallas_call(
        _front_kernel,
        grid=(n // tm,),
        in_specs=[row(D_MODEL), _const_spec((1, D_MODEL)), _const_spec((D_MODEL, IN_COLS)),
                  _const_spec((1, LANES)), _const_spec((1, LANES))],
        out_specs=[row(ATTN_WIDTH), row(KV_WIDTH), row(KV_WIDTH), row(SSM_WIDTH)],
        out_shape=[jax.ShapeDtypeStruct((n, ATTN_WIDTH), F32), jax.ShapeDtypeStruct((n, KV_WIDTH), F32),
                   jax.ShapeDtypeStruct((n, KV_WIDTH), F32), jax.ShapeDtypeStruct((n, SSM_WIDTH), F32)],
        compiler_params=pltpu.CompilerParams(dimension_semantics=("arbitrary",), vmem_limit_bytes=VMEM_LIMIT),
        name="front",
    )(x_rows, gmix, win_bf, gq2, gk2)


Q_TILES = ATTN_WIDTH // LANES


def _attn_sample_kernel(sink_ref, q_ref, kn_ref, vn_ref, ck_ref, cv_ref, o_ref, kw_ref, vw_ref, *, bb, t):
    left = _left_half()
    n_heads = 2 * Q_TILES
    rows = n_heads * t
    tk = WINDOW + t
    r = lax.broadcasted_iota(jnp.int32, (rows, tk), 0) % t
    c = lax.broadcasted_iota(jnp.int32, (rows, tk), 1)
    bias = jnp.where((c >= r) & (c <= r + WINDOW), 0.0, -jnp.inf)
    hrow = lax.broadcasted_iota(jnp.int32, (rows, 1), 0) // t
    sink = jnp.zeros((rows, 1), F32)
    for g in range(n_heads):
        sink = jnp.where(hrow == g, sink_ref[g // 2 + Q_PER_KV * (g % 2)], sink)
    scores, values = [], []
    for bi in range(bb):
        kk = jnp.concatenate([ck_ref[bi], kn_ref[bi]], axis=0)
        vv = jnp.concatenate([cv_ref[bi], vn_ref[bi]], axis=0)
        kw_ref[bi] = kk[t:]
        vw_ref[bi] = vv[t:]
        q = q_ref[bi] * ATTN_SCALE
        pieces = []
        for tile in range(Q_TILES):
            qt = q[:, tile * LANES:(tile + 1) * LANES]
            pieces += [jnp.where(left, qt, 0.0), jnp.where(left, 0.0, qt)]
        qs = jnp.concatenate(pieces, axis=0).astype(BF16)
        scores.append(lax.dot_general(qs, kk.astype(BF16), (((1,), (1,)), ((), ())), preferred_element_type=F32))
        values.append(vv.astype(BF16))
    probs = []
    for bi in range(bb):
        s = scores[bi] + bias
        m = jnp.maximum(jnp.max(s, axis=-1, keepdims=True), sink)
        p = jnp.exp(s - m)
        inv = 1.0 / (jnp.sum(p, axis=-1, keepdims=True) + jnp.exp(sink - m))
        probs.append((p * inv).astype(BF16))
    for bi in range(bb):
        o = jnp.dot(probs[bi], values[bi], preferred_element_type=F32)
        for tile in range(Q_TILES):
            o_ref[bi, :, tile * LANES:(tile + 1) * LANES] = jnp.where(
                left, o[2 * tile * t:(2 * tile + 1) * t], o[(2 * tile + 1) * t:(2 * tile + 2) * t])


def _attn_sample(q, kn, vn, ck, cv, sinks, bb=8):
    db, t, _ = q.shape
    blk = lambda r, w: pl.BlockSpec((bb, r, w), lambda i: (i, 0, 0))
    return pl.pallas_call(
        functools.partial(_attn_sample_kernel, bb=bb, t=t),
        grid=(db // bb,),
        in_specs=[pl.BlockSpec(memory_space=pltpu.SMEM), blk(t, ATTN_WIDTH), blk(t, KV_WIDTH), blk(t, KV_WIDTH),
                  blk(WINDOW, KV_WIDTH), blk(WINDOW, KV_WIDTH)],
        out_specs=[blk(t, ATTN_WIDTH), blk(WINDOW, KV_WIDTH), blk(WINDOW, KV_WIDTH)],
        out_shape=[jax.ShapeDtypeStruct((db, t, ATTN_WIDTH), F32),
                   jax.ShapeDtypeStruct((db, WINDOW, KV_WIDTH), F32),
                   jax.ShapeDtypeStruct((db, WINDOW, KV_WIDTH), F32)],
        compiler_params=pltpu.CompilerParams(dimension_semantics=("arbitrary",)),
        name="attn_sample",
    )(sinks, q, kn, vn, ck, cv)


N_QUARTERS = 4
Q_GROUPS = N_SSM_GROUPS // N_QUARTERS
Q_STATE = Q_GROUPS * SSM_STATE


def _ssm_params(a_re, a_im, log_dt, b_re, b_im, c_re, c_im, d_skip, w_glu, b_glu):
    dt = jnp.exp(log_dt)[:, None]
    mag = jnp.exp(a_re * dt)
    lr, li = mag * jnp.cos(a_im * dt), mag * jnp.sin(a_im * dt)
    nr, ni = lr - 1.0, li
    den = a_re * a_re + a_im * a_im
    fr, fi = (nr * a_re + ni * a_im) / den, (ni * a_re - nr * a_im) / den
    bbr = fr[..., None] * b_re - fi[..., None] * b_im
    bbi = fr[..., None] * b_im + fi[..., None] * b_re
    eye = jnp.eye(Q_GROUPS, dtype=F32)
    quartered = lambda a: a.reshape((a.shape[0], N_QUARTERS, Q_GROUPS) + a.shape[2:])
    wq = jnp.einsum('rqgph,gm->qghrmp', quartered(jnp.stack([bbr, bbi])), eye).reshape(N_QUARTERS, LANES, 2 * Q_STATE)
    cq = jnp.einsum('rqghp,gm->qrgpmh', quartered(jnp.stack([c_re, -c_im])), eye).reshape(N_QUARTERS, 2 * Q_STATE, LANES)
    wglu = jnp.einsum('qghk,gm->qghmk', w_glu.reshape(N_QUARTERS, Q_GROUPS, SSM_GROUP, SSM_GROUP), eye)
    return dict(wq=wq.astype(BF16), cq=cq.astype(BF16), wgluq=wglu.reshape(N_QUARTERS, LANES, LANES).astype(BF16),
                lr=lr.reshape(1, STATE_COLS), li=li.reshape(1, STATE_COLS), d=d_skip.reshape(1, SSM_WIDTH),
                bglu=b_glu.reshape(1, SSM_WIDTH))


def _scan_step(lr, li, hr, hi, xr, xi):
    return lr * hr - li * hi + xr, lr * hi + li * hr + xi


def _ssm_sample_kernel(u0_ref, u1_ref, u2_ref, u3_ref, h0r_ref, h0i_ref, wq_ref, lr_ref, li_ref, cq_ref, d_ref,
                       wglu_ref, bglu_ref, o_ref, hr_ref, hi_ref, utb_ref, xs_ref, *, nseq, t_steps):
    u_refs = (u0_ref, u1_ref, u2_ref, u3_ref)
    quarters = range(N_QUARTERS)
    n_rg = nseq // SUBLANES
    rg_rows = SUBLANES * t_steps

    for t in range(t_steps):
        for rg in range(n_rg):
            for j in quarters:
                utb_ref[t * nseq + rg * SUBLANES:t * nseq + (rg + 1) * SUBLANES, j * LANES:(j + 1) * LANES] = (
                    u_refs[j][pl.ds(rg * rg_rows + t, SUBLANES, stride=t_steps), :])

    u = [utb_ref[:, q * LANES:(q + 1) * LANES] for q in quarters]
    for q in quarters:
        xs_ref[q] = jnp.dot(u[q].astype(BF16), wq_ref[q], preferred_element_type=F32)
    lam = [(jnp.broadcast_to(lr_ref[:, q * Q_STATE:(q + 1) * Q_STATE], (SUBLANES, Q_STATE)),
            jnp.broadcast_to(li_ref[:, q * Q_STATE:(q + 1) * Q_STATE], (SUBLANES, Q_STATE))) for q in quarters]

    def scan_group(rg, carry):
        r0 = pl.multiple_of(rg * SUBLANES, SUBLANES)
        h = [(h0r_ref[pl.ds(r0, SUBLANES), q * Q_STATE:(q + 1) * Q_STATE],
              h0i_ref[pl.ds(r0, SUBLANES), q * Q_STATE:(q + 1) * Q_STATE]) for q in quarters]
        for t in range(t_steps):
            rows = pl.ds(pl.multiple_of(t * nseq + r0, SUBLANES), SUBLANES)
            for q in quarters:
                h[q] = _scan_step(*lam[q], *h[q], xs_ref[q, rows, 0:Q_STATE], xs_ref[q, rows, Q_STATE:])
                xs_ref[q, rows, 0:Q_STATE] = h[q][0]
                xs_ref[q, rows, Q_STATE:] = h[q][1]
        for q in quarters:
            hr_ref[pl.ds(r0, SUBLANES), q * Q_STATE:(q + 1) * Q_STATE] = h[q][0]
            hi_ref[pl.ds(r0, SUBLANES), q * Q_STATE:(q + 1) * Q_STATE] = h[q][1]
        return carry

    lax.fori_loop(0, n_rg, scan_group, 0)
    y = [jnp.dot(xs_ref[q].astype(BF16), cq_ref[q], preferred_element_type=F32) for q in quarters]
    g = [jax.nn.gelu(y[q] + d_ref[:, q * LANES:(q + 1) * LANES] * u[q]) for q in quarters]
    gate = [jnp.dot(g[q].astype(BF16), wglu_ref[q], preferred_element_type=F32) + bglu_ref[:, q * LANES:(q + 1) * LANES]
            for q in quarters]
    for q in quarters:
        o = g[q] * jax.nn.sigmoid(gate[q])
        for t in range(t_steps):
            for rg in range(n_rg):
                o_ref[q, pl.ds(rg * rg_rows + t, SUBLANES, stride=t_steps), :] = (
                    o[t * nseq + rg * SUBLANES:t * nseq + (rg + 1) * SUBLANES, :])


def _ssm_sample(u_rows, h0r, h0i, sp, nseq, t_steps):
    n = nseq * t_steps
    slab = lambda j: pl.BlockSpec((n, LANES), lambda i, j=j: (0, j))
    st = pl.BlockSpec((nseq, STATE_COLS), lambda i: (0, 0))
    return pl.pallas_call(
        functools.partial(_ssm_sample_kernel, nseq=nseq, t_steps=t_steps),
        grid=(1,),
        in_specs=[slab(0), slab(1), slab(2), slab(3), st, st,
                  _const_spec((N_QUARTERS, LANES, 2 * Q_STATE)), _const_spec((1, STATE_COLS)),
                  _const_spec((1, STATE_COLS)), _const_spec((N_QUARTERS, 2 * Q_STATE, LANES)),
                  _const_spec((1, SSM_WIDTH)), _const_spec((N_QUARTERS, LANES, LANES)), _const_spec((1, SSM_WIDTH))],
        out_specs=[pl.BlockSpec((N_QUARTERS, n, LANES), lambda i: (0, 0, 0)), st, st],
        out_shape=[jax.ShapeDtypeStruct((N_QUARTERS, n, LANES), F32),
                   jax.ShapeDtypeStruct((nseq, STATE_COLS), F32), jax.ShapeDtypeStruct((nseq, STATE_COLS), F32)],
        scratch_shapes=[pltpu.VMEM((n, SSM_WIDTH), F32), pltpu.VMEM((N_QUARTERS, n, 2 * Q_STATE), F32)],
        compiler_params=pltpu.CompilerParams(dimension_semantics=("arbitrary",), vmem_limit_bytes=VMEM_LIMIT),
        name="ssm_sample",
    )(u_rows, u_rows, u_rows, u_rows, h0r, h0i, sp["wq"], sp["lr"], sp["li"], sp["cq"], sp["d"], sp["wgluq"], sp["bglu"])


def _ffn(h, gf_ref, wg_ref, wu_ref, wd_ref):
    f = _rms(h, gf_ref[...]).astype(BF16)
    gate = jnp.dot(f, wg_ref[...], preferred_element_type=F32)
    up = jnp.dot(f, wu_ref[...], preferred_element_type=F32)
    a = (jax.nn.silu(gate) * up).astype(BF16)
    return h + jnp.dot(a, wd_ref[...], preferred_element_type=F32)


def _back_kernel(x_ref, oa_ref, os_ref, ga_ref, gs_ref, wout_ref, gf_ref, wg_ref, wu_ref, wd_ref, y_ref):
    o_ssm = jnp.concatenate([os_ref[j] for j in range(N_QUARTERS)], axis=-1)
    mix = jnp.concatenate([_rms(oa_ref[...], ga_ref[...]), _rms(o_ssm, gs_ref[...])], axis=-1).astype(BF16)
    h = x_ref[...] + jnp.dot(mix, wout_ref[...], preferred_element_type=F32)
    y_ref[...] = _ffn(h, gf_ref, wg_ref, wu_ref, wd_ref)


def _ffn_kernel(x_ref, hp_ref, gf_ref, wg_ref, wu_ref, wd_ref, y_ref):
    hp = jnp.concatenate([hp_ref[c] for c in range(hp_ref.shape[0])], axis=-1)
    y_ref[...] = _ffn(x_ref[...] + hp, gf_ref, wg_ref, wu_ref, wd_ref)


def _back(x_rows, oa, os_, ga, gs, wout, gf, wg, wu, wd, tm):
    n = x_rows.shape[0]
    assert n % tm == 0
    row = lambda w: pl.BlockSpec((tm, w), lambda i: (i, 0))
    return pl.pallas_call(
        _back_kernel,
        grid=(n // tm,),
        in_specs=[row(D_MODEL), row(ATTN_WIDTH), pl.BlockSpec((N_QUARTERS, tm, LANES), lambda i: (0, i, 0)),
                  _const_spec((1, ATTN_WIDTH)), _const_spec((1, SSM_WIDTH)), _const_spec((D_MODEL, D_MODEL)),
                  _const_spec((1, D_MODEL)), _const_spec((D_MODEL, D_FF)), _const_spec((D_MODEL, D_FF)),
                  _const_spec((D_FF, D_MODEL))],
        out_specs=row(D_MODEL),
        out_shape=jax.ShapeDtypeStruct((n, D_MODEL), F32),
        compiler_params=pltpu.CompilerParams(dimension_semantics=("arbitrary",), vmem_limit_bytes=VMEM_LIMIT),
        name="back",
    )(x_rows, oa, os_, ga, gs, wout, gf, wg, wu, wd)


def _ffn_call(x_rows, hp_rows, gf, wg, wu, wd, tm):
    n = x_rows.shape[0]
    assert n % tm == 0
    row = lambda w: pl.BlockSpec((tm, w), lambda i: (i, 0))
    return pl.pallas_call(
        _ffn_kernel,
        grid=(n // tm,),
        in_specs=[row(D_MODEL), pl.BlockSpec((hp_rows.shape[0], tm, hp_rows.shape[2]), lambda i: (0, i, 0)),
                  _const_spec((1, D_MODEL)),
                  _const_spec((D_MODEL, D_FF)), _const_spec((D_MODEL, D_FF)), _const_spec((D_FF, D_MODEL))],
        out_specs=row(D_MODEL),
        out_shape=jax.ShapeDtypeStruct((n, D_MODEL), F32),
        compiler_params=pltpu.CompilerParams(dimension_semantics=("arbitrary",), vmem_limit_bytes=VMEM_LIMIT),
        name="ffn",
    )(x_rows, hp_rows, gf, wg, wu, wd)


N_SEQ = 8
ROWS = N_SEQ * BLOCK
PITCH = BLOCK + SUBLANES
CHUNK_T = BLOCK // N_SEQ


def _interleave(*stages):
    keyed = [((i + 0.5) / len(steps), k, i, step) for k, steps in enumerate(stages) for i, step in enumerate(steps)]
    return [step for _, _, _, step in sorted(keyed, key=lambda e: e[:3])]


def _mixer_kernel(sink_ref, x_ref, meta_ref, gmix_ref, win_ref, gq_ref, gk_ref,
                  wq_ref, lr_ref, li_ref, cq_ref, d_ref, wglu_ref, bglu_ref, ga_ref, gs_ref, wout_ref,
                  hp_ref, kw_ref, vw_ref, hr_ref, hi_ref,
                  qs_ref, kk_ref, vv_ref, s_ref, p_ref, oatt_ref, usl_ref, xs_ref, hb_ref, st_ref, osl_ref, mix_ref):
    i = pl.program_id(0)
    n_blocks = pl.num_programs(0) - 1

    @pl.when(i < n_blocks)
    def _():
        _mixer_block(i, sink_ref, x_ref, meta_ref, gmix_ref, win_ref, gq_ref, gk_ref,
                     wq_ref, lr_ref, li_ref, cq_ref, d_ref, wglu_ref, bglu_ref, ga_ref, gs_ref, wout_ref,
                     hp_ref, kw_ref, vw_ref, hr_ref, hi_ref,
                     qs_ref, kk_ref, vv_ref, s_ref, p_ref, oatt_ref, usl_ref, xs_ref, hb_ref, st_ref, osl_ref, mix_ref)

    @pl.when(i == n_blocks)
    def _():
        for c in range(OUT_CHUNKS):
            _project_chunk(mix_ref, wout_ref, hp_ref, c)


OUT_CHUNKS = 4
OUT_CHUNK = D_MODEL // OUT_CHUNKS


def _project_chunk(mix_ref, wout_ref, hp_ref, c):
    hp_ref[c] = jnp.dot(mix_ref[...], wout_ref[c], preferred_element_type=F32).reshape(N_SEQ, BLOCK, OUT_CHUNK)


def _mixer_block(i, sink_ref, x_ref, meta_ref, gmix_ref, win_ref, gq_ref, gk_ref,
                 wq_ref, lr_ref, li_ref, cq_ref, d_ref, wglu_ref, bglu_ref, ga_ref, gs_ref, wout_ref,
                 hp_ref, kw_ref, vw_ref, hr_ref, hi_ref,
                 qs_ref, kk_ref, vv_ref, s_ref, p_ref, oatt_ref, usl_ref, xs_ref, hb_ref, st_ref, osl_ref, mix_ref):
    left = _left_half()
    project = lambda c: [functools.partial(_project_chunk, mix_ref, wout_ref, hp_ref, c)]

    @pl.when(i == 0)
    def _():
        mix_ref[...] = jnp.zeros((ROWS, D_MODEL), BF16)
        xm = _rms(meta_ref[...], gmix_ref[...]).astype(BF16)
        zm = jnp.dot(xm, win_ref[:, ATTN_WIDTH:], preferred_element_type=F32)
        km = _pair_norm(zm[:, :KV_WIDTH], gk_ref[...], left).astype(BF16)
        vm = zm[:, KV_WIDTH:2 * KV_WIDTH].astype(BF16)
        lead = jnp.zeros((BLOCK - N_META, KV_WIDTH), BF16)
        for b in range(N_SEQ):
            kk_ref[b, 0:BLOCK - N_META, :] = lead
            vv_ref[b, 0:BLOCK - N_META, :] = lead
            kk_ref[b, BLOCK - N_META:BLOCK, :] = km
            vv_ref[b, BLOCK - N_META:BLOCK, :] = vm
        um = zm[:, 2 * KV_WIDTH:]
        for q in range(N_QUARTERS):
            xm_q = jnp.dot(um[:, q * LANES:(q + 1) * LANES].astype(BF16), wq_ref[q], preferred_element_type=F32)
            lr = jnp.broadcast_to(lr_ref[:, q * Q_STATE:(q + 1) * Q_STATE], (SUBLANES, Q_STATE))
            li = jnp.broadcast_to(li_ref[:, q * Q_STATE:(q + 1) * Q_STATE], (SUBLANES, Q_STATE))
            hr = jnp.zeros((SUBLANES, Q_STATE), F32)
            hi = jnp.zeros((SUBLANES, Q_STATE), F32)
            for t in range(N_META):
                xr = jnp.broadcast_to(xm_q[t:t + 1, 0:Q_STATE], (SUBLANES, Q_STATE))
                xi = jnp.broadcast_to(xm_q[t:t + 1, Q_STATE:], (SUBLANES, Q_STATE))
                hr, hi = _scan_step(lr, li, hr, hi, xr, xi)
            st_ref[:, 2 * q * Q_STATE:(2 * q + 1) * Q_STATE] = hr
            st_ref[:, (2 * q + 1) * Q_STATE:(2 * q + 2) * Q_STATE] = hi

    xn = _rms(x_ref[...].reshape(ROWS, D_MODEL), gmix_ref[...]).astype(BF16)

    def proj(c0):
        return jnp.dot(xn, win_ref[:, c0:c0 + 2 * LANES], preferred_element_type=F32)

    for c in range(Q_TILES // 2):
        zz = proj(c * 2 * LANES)
        for tt in range(2):
            t = 2 * c + tt
            qn = _pair_norm(zz[:, tt * LANES:(tt + 1) * LANES], gq_ref[...], left) * ATTN_SCALE
            qa = jnp.where(left, qn, 0.0).astype(BF16)
            qb = jnp.where(left, 0.0, qn).astype(BF16)
            for b in range(N_SEQ):
                qs_ref[b, 2 * t * BLOCK:(2 * t + 1) * BLOCK, :] = qa[b * BLOCK:(b + 1) * BLOCK]
                qs_ref[b, (2 * t + 1) * BLOCK:(2 * t + 2) * BLOCK, :] = qb[b * BLOCK:(b + 1) * BLOCK]
    zz = proj(ATTN_WIDTH)
    kn = _pair_norm(zz[:, :LANES], gk_ref[...], left)
    vn = zz[:, LANES:]
    kw_ref[...] = kn.reshape(N_SEQ, BLOCK, KV_WIDTH)
    vw_ref[...] = vn.reshape(N_SEQ, BLOCK, KV_WIDTH)
    kk_ref[:, BLOCK:, :] = kn.astype(BF16).reshape(N_SEQ, BLOCK, KV_WIDTH)
    vv_ref[:, BLOCK:, :] = vn.astype(BF16).reshape(N_SEQ, BLOCK, KV_WIDTH)
    for c in range(2):
        zz = proj(ATTN_WIDTH + 2 * KV_WIDTH + c * 2 * LANES)
        for tt in range(2):
            for b in range(N_SEQ):
                usl_ref[2 * c + tt, b * PITCH:b * PITCH + BLOCK, :] = zz[b * BLOCK:(b + 1) * BLOCK, tt * LANES:(tt + 1) * LANES]

    r = lax.broadcasted_iota(jnp.int32, (BLOCK, 2 * BLOCK), 0)
    c = lax.broadcasted_iota(jnp.int32, (BLOCK, 2 * BLOCK), 1)
    valid = (c >= r) & (c <= r + WINDOW) & ((i > 0) | (c >= BLOCK - N_META))
    bias = jnp.where(valid, 0.0, -jnp.inf)

    def gather_u(n):
        t0 = n * CHUNK_T
        return jnp.concatenate(
            [jnp.concatenate([usl_ref[j, pl.ds(t0 + tl, SUBLANES, stride=PITCH), :] for j in range(N_QUARTERS)], axis=1)
             for tl in range(CHUNK_T)], axis=0)

    def feed(n, par):
        env = {}

        def scores():
            s_ref[par] = lax.dot_general(qs_ref[n], kk_ref[n], (((1,), (1,)), ((), ())), preferred_element_type=F32)

        def gather():
            env["u"] = gather_u(n)

        def scan_inputs(q):
            xs_ref[par, :, 2 * q * Q_STATE:(2 * q + 2) * Q_STATE] = jnp.dot(
                env["u"][:, q * LANES:(q + 1) * LANES].astype(BF16), wq_ref[q], preferred_element_type=F32)

        return [scores, gather] + [functools.partial(scan_inputs, q) for q in range(N_QUARTERS)]

    def mid(n, par):
        env = {}

        def row_max(g):
            s = s_ref[par, g * BLOCK:(g + 1) * BLOCK, :] + bias
            env[g] = (s, jnp.maximum(jnp.max(s, axis=-1, keepdims=True), sink_ref[g // 2 + Q_PER_KV * (g % 2)]))

        def exponent(g):
            s, m = env[g]
            p = jnp.exp(s - m)
            sink = sink_ref[g // 2 + Q_PER_KV * (g % 2)]
            env[g] = (p, 1.0 / (jnp.sum(p, axis=-1, keepdims=True) + jnp.exp(sink - m)))

        def normalise(g):
            p, inv = env.pop(g)
            p_ref[par, g * BLOCK:(g + 1) * BLOCK, :] = (p * inv).astype(BF16)

        softmax = [functools.partial(f, g) for g in range(2 * Q_TILES) for f in (row_max, exponent, normalise)]

        def load_state():
            for q in range(N_QUARTERS):
                env["lam", q] = (jnp.broadcast_to(lr_ref[:, q * Q_STATE:(q + 1) * Q_STATE], (SUBLANES, Q_STATE)),
                                 jnp.broadcast_to(li_ref[:, q * Q_STATE:(q + 1) * Q_STATE], (SUBLANES, Q_STATE)))
                env["h", q] = (st_ref[:, 2 * q * Q_STATE:(2 * q + 1) * Q_STATE],
                               st_ref[:, (2 * q + 1) * Q_STATE:(2 * q + 2) * Q_STATE])

        def scan_pair(tp, q):
            lr, li = env["lam", q]
            hr, hi = env["h", q]
            re0, im0 = 2 * q * Q_STATE, (2 * q + 1) * Q_STATE
            pair_r, pair_i = [], []
            for tl in (2 * tp, 2 * tp + 1):
                xr = xs_ref[par, tl * SUBLANES:(tl + 1) * SUBLANES, re0:re0 + Q_STATE]
                xi = xs_ref[par, tl * SUBLANES:(tl + 1) * SUBLANES, im0:im0 + Q_STATE]
                hr, hi = _scan_step(lr, li, hr, hi, xr, xi)
                pair_r.append(hr)
                pair_i.append(hi)
            env["h", q] = (hr, hi)
            rows = slice(2 * tp * SUBLANES, (2 * tp + 2) * SUBLANES)
            hb_ref[par, rows, re0:re0 + Q_STATE] = jnp.concatenate(pair_r, axis=0).astype(BF16)
            hb_ref[par, rows, im0:im0 + Q_STATE] = jnp.concatenate(pair_i, axis=0).astype(BF16)

        def store_state():
            for q in range(N_QUARTERS):
                hr, hi = env["h", q]
                st_ref[:, 2 * q * Q_STATE:(2 * q + 1) * Q_STATE] = hr
                st_ref[:, (2 * q + 1) * Q_STATE:(2 * q + 2) * Q_STATE] = hi

        scan = ([load_state] + [functools.partial(scan_pair, tp, q) for tp in range(CHUNK_T // 2) for q in range(N_QUARTERS)]
                + [store_state])
        return _interleave(softmax, scan)

    def tail(n, par):
        env = {}

        def attn_out():
            o_all = jnp.dot(p_ref[par], vv_ref[n], preferred_element_type=F32)
            for t in range(Q_TILES):
                oatt_ref[n, :, t * LANES:(t + 1) * LANES] = jnp.where(
                    left, o_all[2 * t * BLOCK:(2 * t + 1) * BLOCK], o_all[(2 * t + 1) * BLOCK:(2 * t + 2) * BLOCK])

        def gather():
            env["u"] = gather_u(n)

        def readout(q):
            env["y", q] = jnp.dot(hb_ref[par, :, 2 * q * Q_STATE:(2 * q + 2) * Q_STATE], cq_ref[q],
                                  preferred_element_type=F32)

        def activate(q):
            cols = slice(q * LANES, (q + 1) * LANES)
            env["g", q] = jax.nn.gelu(env.pop(("y", q)) + d_ref[:, cols] * env["u"][:, cols])

        def gate(q):
            cols = slice(q * LANES, (q + 1) * LANES)
            env["gate", q] = jnp.dot(env["g", q].astype(BF16), wglu_ref[q], preferred_element_type=F32) + bglu_ref[:, cols]

        def emit(q):
            o = env.pop(("g", q)) * jax.nn.sigmoid(env.pop(("gate", q)))
            t0 = n * CHUNK_T
            for tl in range(CHUNK_T):
                osl_ref[q, pl.ds(t0 + tl, SUBLANES, stride=PITCH), :] = o[tl * SUBLANES:(tl + 1) * SUBLANES]

        per_quarter = lambda f: [functools.partial(f, q) for q in range(N_QUARTERS)]
        return [attn_out, gather] + per_quarter(readout) + per_quarter(activate) + per_quarter(gate) + per_quarter(emit)

    def run(*stages):
        for step in _interleave(*stages):
            step()

    def steady(k, carry):
        n = 2 * k + 1
        run(feed(n + 1, 0), mid(n, 1), tail(n - 1, 0), project(k))
        run(feed(n + 2, 1), mid(n + 1, 0), tail(n, 1))
        return carry

    run(feed(0, 0))
    run(feed(1, 1), mid(0, 0))
    n_steady = (N_SEQ - 2) // 2
    assert n_steady == OUT_CHUNKS - 1
    lax.fori_loop(0, n_steady, steady, 0)
    run(mid(N_SEQ - 1, 1), tail(N_SEQ - 2, 0), project(OUT_CHUNKS - 1))
    run(tail(N_SEQ - 1, 1))
    kk_ref[:, 0:BLOCK, :] = kk_ref[:, BLOCK:, :]
    vv_ref[:, 0:BLOCK, :] = vv_ref[:, BLOCK:, :]

    hr_ref[...] = jnp.concatenate([st_ref[:, 2 * q * Q_STATE:(2 * q + 1) * Q_STATE] for q in range(N_QUARTERS)], axis=1)
    hi_ref[...] = jnp.concatenate([st_ref[:, (2 * q + 1) * Q_STATE:(2 * q + 2) * Q_STATE] for q in range(N_QUARTERS)], axis=1)

    for b in range(N_SEQ):
        rows = slice(b * BLOCK, (b + 1) * BLOCK)
        mix_ref[rows, 0:ATTN_WIDTH] = _rms(oatt_ref[b], ga_ref[...]).astype(BF16)
        sl = [osl_ref[j, b * PITCH:b * PITCH + BLOCK, :] for j in range(N_QUARTERS)]
        ms = sum(jnp.sum(s * s, axis=-1, keepdims=True) for s in sl) / SSM_WIDTH
        inv = lax.rsqrt(ms + EPS)
        for j in range(N_QUARTERS):
            mix_ref[rows, ATTN_WIDTH + j * LANES:ATTN_WIDTH + (j + 1) * LANES] = (
                sl[j] * inv * gs_ref[:, j * LANES:(j + 1) * LANES]).astype(BF16)


def _mixer(x_prompt, meta_tokens, sinks, gmix, win, gq2, gk2, sp, ga, gs, wout):
    nseq, seq, _ = x_prompt.shape
    assert nseq == N_SEQ and seq % BLOCK == 0
    n_blocks = seq // BLOCK
    st = pl.BlockSpec((N_SEQ, STATE_COLS), lambda i: (0, 0))
    kvw = pl.BlockSpec((N_SEQ, BLOCK, KV_WIDTH), lambda i: (0, 0, 0))
    return pl.pallas_call(
        _mixer_kernel,
        grid=(n_blocks + 1,),
        in_specs=[pl.BlockSpec(memory_space=pltpu.SMEM),
                  pl.BlockSpec((N_SEQ, BLOCK, D_MODEL), lambda i: (0, jnp.minimum(i, n_blocks - 1), 0)),
                  _const_spec((N_META, D_MODEL)),
                  _const_spec((1, D_MODEL)), _const_spec((D_MODEL, IN_COLS)), _const_spec((1, LANES)),
                  _const_spec((1, LANES)),
                  _const_spec((N_QUARTERS, LANES, 2 * Q_STATE)), _const_spec((1, STATE_COLS)),
                  _const_spec((1, STATE_COLS)), _const_spec((N_QUARTERS, 2 * Q_STATE, LANES)),
                  _const_spec((1, SSM_WIDTH)), _const_spec((N_QUARTERS, LANES, LANES)), _const_spec((1, SSM_WIDTH)),
                  _const_spec((1, ATTN_WIDTH)), _const_spec((1, SSM_WIDTH)),
                  _const_spec((OUT_CHUNKS, D_MODEL, OUT_CHUNK))],
        out_specs=[pl.BlockSpec((OUT_CHUNKS, N_SEQ, BLOCK, OUT_CHUNK), lambda i: (0, 0, jnp.maximum(i - 1, 0), 0)),
                   kvw, kvw, st, st],
        out_shape=[jax.ShapeDtypeStruct((OUT_CHUNKS, N_SEQ, seq, OUT_CHUNK), F32),
                   jax.ShapeDtypeStruct((N_SEQ, BLOCK, KV_WIDTH), F32), jax.ShapeDtypeStruct((N_SEQ, BLOCK, KV_WIDTH), F32),
                   jax.ShapeDtypeStruct((N_SEQ, STATE_COLS), F32), jax.ShapeDtypeStruct((N_SEQ, STATE_COLS), F32)],
        scratch_shapes=[pltpu.VMEM((N_SEQ, 2 * Q_TILES * BLOCK, LANES), BF16),
                        pltpu.VMEM((N_SEQ, 2 * BLOCK, KV_WIDTH), BF16),
                        pltpu.VMEM((N_SEQ, 2 * BLOCK, KV_WIDTH), BF16),
                        pltpu.VMEM((2, 2 * Q_TILES * BLOCK, 2 * BLOCK), F32),
                        pltpu.VMEM((2, 2 * Q_TILES * BLOCK, 2 * BLOCK), BF16),
                        pltpu.VMEM((N_SEQ, BLOCK, ATTN_WIDTH), F32),
                        pltpu.VMEM((N_QUARTERS, N_SEQ * PITCH, LANES), F32),
                        pltpu.VMEM((2, CHUNK_T * N_SEQ, 2 * STATE_COLS), F32),
                        pltpu.VMEM((2, CHUNK_T * N_SEQ, 2 * STATE_COLS), BF16),
                        pltpu.VMEM((N_SEQ, 2 * STATE_COLS), F32),
                        pltpu.VMEM((N_QUARTERS, N_SEQ * PITCH, LANES), F32),
                        pltpu.VMEM((ROWS, D_MODEL), BF16)],
        compiler_params=pltpu.CompilerParams(dimension_semantics=("arbitrary",), vmem_limit_bytes=VMEM_LIMIT),
        name="mixer",
    )(sinks, x_prompt, meta_tokens, gmix, win, gq2, gk2, sp["wq"], sp["lr"], sp["li"], sp["cq"], sp["d"],
      sp["wgluq"], sp["bglu"], ga, gs, wout)


def kernel(x_prompt, x_sample, cache_k_win, cache_v_win, state_ssm_re, state_ssm_im, meta_tokens, g_mix, w_in, g_q,
           g_k, sinks, ssm_a_re, ssm_a_im, ssm_log_dt, ssm_b_re, ssm_b_im, ssm_c_re, ssm_c_im, ssm_d, ssm_w_glu,
           ssm_b_glu, g_att_out, g_ssm_out, w_out, g_ffn, w_gate, w_up, w_down):
    bp, seq, _ = x_prompt.shape
    db, dseq, _ = x_sample.shape
    li = 0
    gmix = g_mix[li].reshape(1, D_MODEL)
    win = w_in[li].astype(BF16)
    gq2 = jnp.tile(g_q[li], 2).reshape(1, LANES)
    gk2 = jnp.tile(g_k[li], 2).reshape(1, LANES)
    sk = sinks[li]
    sp = _ssm_params(ssm_a_re[li], ssm_a_im[li], ssm_log_dt[li], ssm_b_re[li], ssm_b_im[li], ssm_c_re[li],
                     ssm_c_im[li], ssm_d[li], ssm_w_glu[li], ssm_b_glu[li])
    ga = g_att_out[li].reshape(1, ATTN_WIDTH)
    gs = g_ssm_out[li].reshape(1, SSM_WIDTH)
    wout = w_out[li].astype(BF16)
    ffn_w = (g_ffn[li].reshape(1, D_MODEL), w_gate[li].astype(BF16), w_up[li].astype(BF16), w_down[li].astype(BF16))
    regroup = lambda a, axis: jnp.swapaxes(
        a.reshape(a.shape[:axis] + (2, Q_PER_KV, HEAD_DIM) + a.shape[axis + 1:]), axis, axis + 1).reshape(a.shape)
    win = jnp.concatenate([regroup(win[:, :ATTN_WIDTH], 1), win[:, ATTN_WIDTH:]], axis=1)
    wout = jnp.concatenate([regroup(wout[:ATTN_WIDTH], 0), wout[ATTN_WIDTH:]], axis=0)
    ga = regroup(ga, 1)

    wout_chunks = wout.reshape(D_MODEL, OUT_CHUNKS, OUT_CHUNK).transpose(1, 0, 2)
    hproj, kw_p, vw_p, hp_r, hp_i = _mixer(x_prompt, meta_tokens, sk, gmix, win, gq2, gk2, sp, ga, gs, wout_chunks)
    y_prompt = _ffn_call(x_prompt.reshape(bp * seq, D_MODEL), hproj.reshape(OUT_CHUNKS, bp * seq, OUT_CHUNK), *ffn_w,
                         tm=512).reshape(bp, seq, D_MODEL)

    n_s = db * dseq
    xs_rows = x_sample.reshape(n_s, D_MODEL)
    q_s, k_s, v_s, u_s = _front(xs_rows, gmix, win, gq2, gk2, tm=512)
    ck = cache_k_win[li].reshape(db, WINDOW, KV_WIDTH)
    cv = cache_v_win[li].reshape(db, WINDOW, KV_WIDTH)
    oa_s, kw_s, vw_s = _attn_sample(q_s.reshape(db, dseq, ATTN_WIDTH), k_s.reshape(db, dseq, KV_WIDTH),
                                    v_s.reshape(db, dseq, KV_WIDTH), ck, cv, sk)
    os_s, hs_r, hs_i = _ssm_sample(u_s, state_ssm_re[li].reshape(db, STATE_COLS),
                                   state_ssm_im[li].reshape(db, STATE_COLS), sp, nseq=db, t_steps=dseq)
    y_sample = _back(xs_rows, oa_s.reshape(n_s, ATTN_WIDTH), os_s, ga, gs, wout, *ffn_w, tm=512).reshape(db, dseq, D_MODEL)

    kv5 = lambda a, n: a.reshape(1, n, WINDOW, N_KV_HEADS, HEAD_DIM)
    st4 = lambda a, n: a.reshape(1, n, N_SSM_GROUPS, SSM_STATE)
    return (y_prompt, y_sample, kv5(kw_p, bp), kv5(vw_p, bp), st4(hp_r, bp), st4(hp_i, bp),
            kv5(kw_s, db), kv5(vw_s, db), st4(hs_r, db), st4(hs_i, db))
```

```python
import functools

import jax
import jax.numpy as jnp
from jax import lax
from jax.experimental import pallas as pl
from jax.experimental.pallas import tpu as pltpu

D_MODEL = 1024
N_META = 16
HEAD_DIM = 64
ATTN_WIDTH = 512
N_Q_HEADS = 8
Q_PER_KV = 4
N_KV_HEADS = 2
KV_WIDTH = 128
WINDOW = 128
BLOCK = 128
SSM_WIDTH = 512
SSM_GROUP = 16
N_SSM_GROUPS = 32
SSM_STATE = 64
IN_COLS = ATTN_WIDTH + 2 * KV_WIDTH + SSM_WIDTH
D_FF = 2816
EPS = 1e-6
ATTN_SCALE = HEAD_DIM ** -0.5
STATE_COLS = N_SSM_GROUPS * SSM_STATE
LANES = 128
SUBLANES = 8
VMEM_LIMIT = 56 * 1024 * 1024

F32 = jnp.float32
BF16 = jnp.bfloat16


def _const_spec(shape):
    return pl.BlockSpec(shape, lambda *_: (0,) * len(shape), pipeline_mode=pl.Buffered(1))


def _rms(x, g):
    return x * lax.rsqrt(jnp.mean(x * x, axis=-1, keepdims=True) + EPS) * g


def _left_half():
    return lax.broadcasted_iota(jnp.int32, (1, LANES), 1) < HEAD_DIM


def _pair_norm(zz, g2, left):
    sq = zz * zz
    sl = jnp.sum(jnp.where(left, sq, 0.0), axis=-1, keepdims=True)
    sr = jnp.sum(jnp.where(left, 0.0, sq), axis=-1, keepdims=True)
    inv = jnp.where(left, lax.rsqrt(sl / HEAD_DIM + EPS), lax.rsqrt(sr / HEAD_DIM + EPS))
    return zz * inv * g2


def _front_kernel(x_ref, gmix_ref, win_ref, gq_ref, gk_ref, q_ref, k_ref, v_ref, u_ref):
    xn = _rms(x_ref[...], gmix_ref[...]).astype(BF16)
    z = jnp.dot(xn, win_ref[...], preferred_element_type=F32)
    left = _left_half()
    for p in range(ATTN_WIDTH // LANES):
        q_ref[:, p * LANES:(p + 1) * LANES] = _pair_norm(z[:, p * LANES:(p + 1) * LANES], gq_ref[...], left)
    k_ref[...] = _pair_norm(z[:, ATTN_WIDTH:ATTN_WIDTH + KV_WIDTH], gk_ref[...], left)
    v_ref[...] = z[:, ATTN_WIDTH + KV_WIDTH:ATTN_WIDTH + 2 * KV_WIDTH]
    u_ref[...] = z[:, ATTN_WIDTH + 2 * KV_WIDTH:]


def _front(x_rows, gmix, win_bf, gq2, gk2, tm):
    n = x_rows.shape[0]
    assert n % tm == 0
    row = lambda w: pl.BlockSpec((tm, w), lambda i: (i, 0))
    return pl.pallas_call(
        _front_kernel,
        grid=(n // tm,),
        in_specs=[row(D_MODEL), _const_spec((1, D_MODEL)), _const_spec((D_MODEL, IN_COLS)),
                  _const_spec((1, LANES)), _const_spec((1, LANES))],
        out_specs=[row(ATTN_WIDTH), row(KV_WIDTH), row(KV_WIDTH), row(SSM_WIDTH)],
        out_shape=[jax.ShapeDtypeStruct((n, ATTN_WIDTH), F32), jax.ShapeDtypeStruct((n, KV_WIDTH), F32),
                   jax.ShapeDtypeStruct((n, KV_WIDTH), F32), jax.ShapeDtypeStruct((n, SSM_WIDTH), F32)],
        compiler_params=pltpu.CompilerParams(dimension_semantics=("arbitrary",), vmem_limit_bytes=VMEM_LIMIT),
        name="front",
    )(x_rows, gmix, win_bf, gq2, gk2)


Q_TILES = ATTN_WIDTH // LANES


def _attn_sample_kernel(sink_ref, q_ref, kn_ref, vn_ref, ck_ref, cv_ref, o_ref, kw_ref, vw_ref, *, bb, t):
    left = _left_half()
    n_heads = 2 * Q_TILES
    rows = n_heads * t
    tk = WINDOW + t
    r = lax.broadcasted_iota(jnp.int32, (rows, tk), 0) % t
    c = lax.broadcasted_iota(jnp.int32, (rows, tk), 1)
    bias = jnp.where((c >= r) & (c <= r + WINDOW), 0.0, -jnp.inf)
    hrow = lax.broadcasted_iota(jnp.int32, (rows, 1), 0) // t
    sink = jnp.zeros((rows, 1), F32)
    for g in range(n_heads):
        sink = jnp.where(hrow == g, sink_ref[g // 2 + Q_PER_KV * (g % 2)], sink)
    scores, values = [], []
    for bi in range(bb):
        kk = jnp.concatenate([ck_ref[bi], kn_ref[bi]], axis=0)
        vv = jnp.concatenate([cv_ref[bi], vn_ref[bi]], axis=0)
        kw_ref[bi] = kk[t:]
        vw_ref[bi] = vv[t:]
        q = q_ref[bi] * ATTN_SCALE
        pieces = []
        for tile in range(Q_TILES):
            qt = q[:, tile * LANES:(tile + 1) * LANES]
            pieces += [jnp.where(left, qt, 0.0), jnp.where(left, 0.0, qt)]
        qs = jnp.concatenate(pieces, axis=0).astype(BF16)
        scores.append(lax.dot_general(qs, kk.astype(BF16), (((1,), (1,)), ((), ())), preferred_element_type=F32))
        values.append(vv.astype(BF16))
    probs = []
    for bi in range(bb):
        s = scores[bi] + bias
        m = jnp.maximum(jnp.max(s, axis=-1, keepdims=True), sink)
        p = jnp.exp(s - m)
        inv = 1.0 / (jnp.sum(p, axis=-1, keepdims=True) + jnp.exp(sink - m))
        probs.append((p * inv).astype(BF16))
    for bi in range(bb):
        o = jnp.dot(probs[bi], values[bi], preferred_element_type=F32)
        for tile in range(Q_TILES):
            o_ref[bi, :, tile * LANES:(tile + 1) * LANES] = jnp.where(
                left, o[2 * tile * t:(2 * tile + 1) * t], o[(2 * tile + 1) * t:(2 * tile + 2) * t])


def _attn_sample(q, kn, vn, ck, cv, sinks, bb=8):
    db, t, _ = q.shape
    blk = lambda r, w: pl.BlockSpec((bb, r, w), lambda i: (i, 0, 0))
    return pl.pallas_call(
        functools.partial(_attn_sample_kernel, bb=bb, t=t),
        grid=(db // bb,),
        in_specs=[pl.BlockSpec(memory_space=pltpu.SMEM), blk(t, ATTN_WIDTH), blk(t, KV_WIDTH), blk(t, KV_WIDTH),
                  blk(WINDOW, KV_WIDTH), blk(WINDOW, KV_WIDTH)],
        out_specs=[blk(t, ATTN_WIDTH), blk(WINDOW, KV_WIDTH), blk(WINDOW, KV_WIDTH)],
        out_shape=[jax.ShapeDtypeStruct((db, t, ATTN_WIDTH), F32),
                   jax.ShapeDtypeStruct((db, WINDOW, KV_WIDTH), F32),
                   jax.ShapeDtypeStruct((db, WINDOW, KV_WIDTH), F32)],
        compiler_params=pltpu.CompilerParams(dimension_semantics=("arbitrary",)),
        name="attn_sample",
    )(sinks, q, kn, vn, ck, cv)


N_QUARTERS = 4
Q_GROUPS = N_SSM_GROUPS // N_QUARTERS
Q_STATE = Q_GROUPS * SSM_STATE


def _ssm_params(a_re, a_im, log_dt, b_re, b_im, c_re, c_im, d_skip, w_glu, b_glu):
    dt = jnp.exp(log_dt)[:, None]
    mag = jnp.exp(a_re * dt)
    lr, li = mag * jnp.cos(a_im * dt), mag * jnp.sin(a_im * dt)
    nr, ni = lr - 1.0, li
    den = a_re * a_re + a_im * a_im
    fr, fi = (nr * a_re + ni * a_im) / den, (ni * a_re - nr * a_im) / den
    bbr = fr[..., None] * b_re - fi[..., None] * b_im
    bbi = fr[..., None] * b_im + fi[..., None] * b_re
    eye = jnp.eye(Q_GROUPS, dtype=F32)
    quartered = lambda a: a.reshape((a.shape[0], N_QUARTERS, Q_GROUPS) + a.shape[2:])
    wq = jnp.einsum('rqgph,gm->qghrmp', quartered(jnp.stack([bbr, bbi])), eye).reshape(N_QUARTERS, LANES, 2 * Q_STATE)
    cq = jnp.einsum('rqghp,gm->qrgpmh', quartered(jnp.stack([c_re, -c_im])), eye).reshape(N_QUARTERS, 2 * Q_STATE, LANES)
    wglu = jnp.einsum('qghk,gm->qghmk', w_glu.reshape(N_QUARTERS, Q_GROUPS, SSM_GROUP, SSM_GROUP), eye)
    return dict(wq=wq.astype(BF16), cq=cq.astype(BF16), wgluq=wglu.reshape(N_QUARTERS, LANES, LANES).astype(BF16),
                lr=lr.reshape(1, STATE_COLS), li=li.reshape(1, STATE_COLS), d=d_skip.reshape(1, SSM_WIDTH),
                bglu=b_glu.reshape(1, SSM_WIDTH))


def _scan_step(lr, li, hr, hi, xr, xi):
    return lr * hr - li * hi + xr, lr * hi + li * hr + xi


def _ssm_sample_kernel(u0_ref, u1_ref, u2_ref, u3_ref, h0r_ref, h0i_ref, wq_ref, lr_ref, li_ref, cq_ref, d_ref,
                       wglu_ref, bglu_ref, o_ref, hr_ref, hi_ref, utb_ref, xs_ref, *, nseq, t_steps):
    u_refs = (u0_ref, u1_ref, u2_ref, u3_ref)
    quarters = range(N_QUARTERS)
    n_rg = nseq // SUBLANES
    rg_rows = SUBLANES * t_steps

    for t in range(t_steps):
        for rg in range(n_rg):
            for j in quarters:
                utb_ref[t * nseq + rg * SUBLANES:t * nseq + (rg + 1) * SUBLANES, j * LANES:(j + 1) * LANES] = (
                    u_refs[j][pl.ds(rg * rg_rows + t, SUBLANES, stride=t_steps), :])

    u = [utb_ref[:, q * LANES:(q + 1) * LANES] for q in quarters]
    for q in quarters:
        xs_ref[q] = jnp.dot(u[q].astype(BF16), wq_ref[q], preferred_element_type=F32)
    lam = [(jnp.broadcast_to(lr_ref[:, q * Q_STATE:(q + 1) * Q_STATE], (SUBLANES, Q_STATE)),
            jnp.broadcast_to(li_ref[:, q * Q_STATE:(q + 1) * Q_STATE], (SUBLANES, Q_STATE))) for q in quarters]

    def scan_group(rg, carry):
        r0 = pl.multiple_of(rg * SUBLANES, SUBLANES)
        h = [(h0r_ref[pl.ds(r0, SUBLANES), q * Q_STATE:(q + 1) * Q_STATE],
              h0i_ref[pl.ds(r0, SUBLANES), q * Q_STATE:(q + 1) * Q_STATE]) for q in quarters]
        for t in range(t_steps):
            rows = pl.ds(pl.multiple_of(t * nseq + r0, SUBLANES), SUBLANES)
            for q in quarters:
                h[q] = _scan_step(*lam[q], *h[q], xs_ref[q, rows, 0:Q_STATE], xs_ref[q, rows, Q_STATE:])
                xs_ref[q, rows, 0:Q_STATE] = h[q][0]
                xs_ref[q, rows, Q_STATE:] = h[q][1]
        for q in quarters:
            hr_ref[pl.ds(r0, SUBLANES), q * Q_STATE:(q + 1) * Q_STATE] = h[q][0]
            hi_ref[pl.ds(r0, SUBLANES), q * Q_STATE:(q + 1) * Q_STATE] = h[q][1]
        return carry

    lax.fori_loop(0, n_rg, scan_group, 0)
    y = [jnp.dot(xs_ref[q].astype(BF16), cq_ref[q], preferred_element_type=F32) for q in quarters]
    g = [jax.nn.gelu(y[q] + d_ref[:, q * LANES:(q + 1) * LANES] * u[q]) for q in quarters]
    gate = [jnp.dot(g[q].astype(BF16), wglu_ref[q], preferred_element_type=F32) + bglu_ref[:, q * LANES:(q + 1) * LANES]
            for q in quarters]
    for q in quarters:
        o = g[q] * jax.nn.sigmoid(gate[q])
        for t in range(t_steps):
            for rg in range(n_rg):
                o_ref[q, pl.ds(rg * rg_rows + t, SUBLANES, stride=t_steps), :] = (
                    o[t * nseq + rg * SUBLANES:t * nseq + (rg + 1) * SUBLANES, :])


def _ssm_sample(u_rows, h0r, h0i, sp, nseq, t_steps):
    n = nseq * t_steps
    slab = lambda j: pl.BlockSpec((n, LANES), lambda i, j=j: (0, j))
    st = pl.BlockSpec((nseq, STATE_COLS), lambda i: (0, 0))
    return pl.pallas_call(
        functools.partial(_ssm_sample_kernel, nseq=nseq, t_steps=t_steps),
        grid=(1,),
        in_specs=[slab(0), slab(1), slab(2), slab(3), st, st,
                  _const_spec((N_QUARTERS, LANES, 2 * Q_STATE)), _const_spec((1, STATE_COLS)),
                  _const_spec((1, STATE_COLS)), _const_spec((N_QUARTERS, 2 * Q_STATE, LANES)),
                  _const_spec((1, SSM_WIDTH)), _const_spec((N_QUARTERS, LANES, LANES)), _const_spec((1, SSM_WIDTH))],
        out_specs=[pl.BlockSpec((N_QUARTERS, n, LANES), lambda i: (0, 0, 0)), st, st],
        out_shape=[jax.ShapeDtypeStruct((N_QUARTERS, n, LANES), F32),
                   jax.ShapeDtypeStruct((nseq, STATE_COLS), F32), jax.ShapeDtypeStruct((nseq, STATE_COLS), F32)],
        scratch_shapes=[pltpu.VMEM((n, SSM_WIDTH), F32), pltpu.VMEM((N_QUARTERS, n, 2 * Q_STATE), F32)],
        compiler_params=pltpu.CompilerParams(dimension_semantics=("arbitrary",), vmem_limit_bytes=VMEM_LIMIT),
        name="ssm_sample",
    )(u_rows, u_rows, u_rows, u_rows, h0r, h0i, sp["wq"], sp["lr"], sp["li"], sp["cq"], sp["d"], sp["wgluq"], sp["bglu"])


def _ffn(h, gf_ref, wg_ref, wu_ref, wd_ref):
    f = _rms(h, gf_ref[...]).astype(BF16)
    gate = jnp.dot(f, wg_ref[...], preferred_element_type=F32)
    up = jnp.dot(f, wu_ref[...], preferred_element_type=F32)
    a = (jax.nn.silu(gate) * up).astype(BF16)
    return h + jnp.dot(a, wd_ref[...], preferred_element_type=F32)


def _back_kernel(x_ref, oa_ref, os_ref, ga_ref, gs_ref, wout_ref, gf_ref, wg_ref, wu_ref, wd_ref, y_ref):
    o_ssm = jnp.concatenate([os_ref[j] for j in range(N_QUARTERS)], axis=-1)
    mix = jnp.concatenate([_rms(oa_ref[...], ga_ref[...]), _rms(o_ssm, gs_ref[...])], axis=-1).astype(BF16)
    h = x_ref[...] + jnp.dot(mix, wout_ref[...], preferred_element_type=F32)
    y_ref[...] = _ffn(h, gf_ref, wg_ref, wu_ref, wd_ref)


def _ffn_kernel(x_ref, hp_ref, gf_ref, wg_ref, wu_ref, wd_ref, y_ref):
    hp = jnp.concatenate([hp_ref[c] for c in range(hp_ref.shape[0])], axis=-1)
    y_ref[...] = _ffn(x_ref[...] + hp, gf_ref, wg_ref, wu_ref, wd_ref)


def _back(x_rows, oa, os_, ga, gs, wout, gf, wg, wu, wd, tm):
    n = x_rows.shape[0]
    assert n % tm == 0
    row = lambda w: pl.BlockSpec((tm, w), lambda i: (i, 0))
    return pl.pallas_call(
        _back_kernel,
        grid=(n // tm,),
        in_specs=[row(D_MODEL), row(ATTN_WIDTH), pl.BlockSpec((N_QUARTERS, tm, LANES), lambda i: (0, i, 0)),
                  _const_spec((1, ATTN_WIDTH)), _const_spec((1, SSM_WIDTH)), _const_spec((D_MODEL, D_MODEL)),
                  _const_spec((1, D_MODEL)), _const_spec((D_MODEL, D_FF)), _const_spec((D_MODEL, D_FF)),
                  _const_spec((D_FF, D_MODEL))],
        out_specs=row(D_MODEL),
        out_shape=jax.ShapeDtypeStruct((n, D_MODEL), F32),
        compiler_params=pltpu.CompilerParams(dimension_semantics=("arbitrary",), vmem_limit_bytes=VMEM_LIMIT),
        name="back",
    )(x_rows, oa, os_, ga, gs, wout, gf, wg, wu, wd)


def _ffn_call(x_rows, hp_rows, gf, wg, wu, wd, tm):
    n = x_rows.shape[0]
    assert n % tm == 0
    row = lambda w: pl.BlockSpec((tm, w), lambda i: (i, 0))
    return pl.pallas_call(
        _ffn_kernel,
        grid=(n // tm,),
        in_specs=[row(D_MODEL), pl.BlockSpec((hp_rows.shape[0], tm, hp_rows.shape[2]), lambda i: (0, i, 0)),
                  _const_spec((1, D_MODEL)),
                  _const_spec((D_MODEL, D_FF)), _const_spec((D_MODEL, D_FF)), _const_spec((D_FF, D_MODEL))],
        out_specs=row(D_MODEL),
        out_shape=jax.ShapeDtypeStruct((n, D_MODEL), F32),
        compiler_params=pltpu.CompilerParams(dimension_semantics=("arbitrary",), vmem_limit_bytes=VMEM_LIMIT),
        name="ffn",
    )(x_rows, hp_rows, gf, wg, wu, wd)


N_SEQ = 8
ROWS = N_SEQ * BLOCK
PITCH = BLOCK + SUBLANES
CHUNK_T = BLOCK // N_SEQ


def _interleave(*stages):
    keyed = [((i + 0.5) / len(steps), k, i, step) for k, steps in enumerate(stages) for i, step in enumerate(steps)]
    return [step for _, _, _, step in sorted(keyed, key=lambda e: e[:3])]


def _mixer_kernel(sink_ref, x_ref, meta_ref, gmix_ref, win_ref, gq_ref, gk_ref,
                  wq_ref, lr_ref, li_ref, cq_ref, d_ref, wglu_ref, bglu_ref, ga_ref, gs_ref, wout_ref,
                  hp_ref, kw_ref, vw_ref, hr_ref, hi_ref,
                  qs_ref, kk_ref, vv_ref, s_ref, p_ref, oatt_ref, usl_ref, xs_ref, hb_ref, st_ref, osl_ref, mix_ref):
    i = pl.program_id(0)
    n_blocks = pl.num_programs(0) - 1

    @pl.when(i < n_blocks)
    def _():
        _mixer_block(i, sink_ref, x_ref, meta_ref, gmix_ref, win_ref, gq_ref, gk_ref,
                     wq_ref, lr_ref, li_ref, cq_ref, d_ref, wglu_ref, bglu_ref, ga_ref, gs_ref, wout_ref,
                     hp_ref, kw_ref, vw_ref, hr_ref, hi_ref,
                     qs_ref, kk_ref, vv_ref, s_ref, p_ref, oatt_ref, usl_ref, xs_ref, hb_ref, st_ref, osl_ref, mix_ref)

    @pl.when(i == n_blocks)
    def _():
        for step in _output_norm_steps(oatt_ref, osl_ref, ga_ref, gs_ref, mix_ref):
            step()
        for c in range(OUT_CHUNKS):
            _project_chunk(mix_ref, wout_ref, hp_ref, c)


OUT_CHUNKS = 4
OUT_CHUNK = D_MODEL // OUT_CHUNKS


def _project_chunk(mix_ref, wout_ref, hp_ref, c):
    hp_ref[c] = jnp.dot(mix_ref[...], wout_ref[c], preferred_element_type=F32).reshape(N_SEQ, BLOCK, OUT_CHUNK)


def _output_norm_steps(oatt_ref, osl_ref, ga_ref, gs_ref, mix_ref):
    def one(b):
        rows = slice(b * BLOCK, (b + 1) * BLOCK)
        mix_ref[rows, 0:ATTN_WIDTH] = _rms(oatt_ref[b], ga_ref[...]).astype(BF16)
        sl = [osl_ref[j, b * PITCH:b * PITCH + BLOCK, :] for j in range(N_QUARTERS)]
        ms = sum(jnp.sum(s * s, axis=-1, keepdims=True) for s in sl) / SSM_WIDTH
        inv = lax.rsqrt(ms + EPS)
        for j in range(N_QUARTERS):
            mix_ref[rows, ATTN_WIDTH + j * LANES:ATTN_WIDTH + (j + 1) * LANES] = (
                sl[j] * inv * gs_ref[:, j * LANES:(j + 1) * LANES]).astype(BF16)
    return [functools.partial(one, b) for b in range(N_SEQ)]


def _mixer_block(i, sink_ref, x_ref, meta_ref, gmix_ref, win_ref, gq_ref, gk_ref,
                 wq_ref, lr_ref, li_ref, cq_ref, d_ref, wglu_ref, bglu_ref, ga_ref, gs_ref, wout_ref,
                 hp_ref, kw_ref, vw_ref, hr_ref, hi_ref,
                 qs_ref, kk_ref, vv_ref, s_ref, p_ref, oatt_ref, usl_ref, xs_ref, hb_ref, st_ref, osl_ref, mix_ref):
    left = _left_half()
    project = lambda c: [functools.partial(_project_chunk, mix_ref, wout_ref, hp_ref, c)]

    @pl.when(i == 0)
    def _():
        oatt_ref[...] = jnp.zeros((N_SEQ, BLOCK, ATTN_WIDTH), F32)
        osl_ref[...] = jnp.zeros((N_QUARTERS, N_SEQ * PITCH, LANES), F32)
        xm = _rms(meta_ref[...], gmix_ref[...]).astype(BF16)
        zm = jnp.dot(xm, win_ref[:, ATTN_WIDTH:], preferred_element_type=F32)
        km = _pair_norm(zm[:, :KV_WIDTH], gk_ref[...], left).astype(BF16)
        vm = zm[:, KV_WIDTH:2 * KV_WIDTH].astype(BF16)
        lead = jnp.zeros((BLOCK - N_META, KV_WIDTH), BF16)
        for b in range(N_SEQ):
            kk_ref[b, 0:BLOCK - N_META, :] = lead
            vv_ref[b, 0:BLOCK - N_META, :] = lead
            kk_ref[b, BLOCK - N_META:BLOCK, :] = km
            vv_ref[b, BLOCK - N_META:BLOCK, :] = vm
        um = zm[:, 2 * KV_WIDTH:]
        for q in range(N_QUARTERS):
            xm_q = jnp.dot(um[:, q * LANES:(q + 1) * LANES].astype(BF16), wq_ref[q], preferred_element_type=F32)
            lr = jnp.broadcast_to(lr_ref[:, q * Q_STATE:(q + 1) * Q_STATE], (SUBLANES, Q_STATE))
            li = jnp.broadcast_to(li_ref[:, q * Q_STATE:(q + 1) * Q_STATE], (SUBLANES, Q_STATE))
            hr = jnp.zeros((SUBLANES, Q_STATE), F32)
            hi = jnp.zeros((SUBLANES, Q_STATE), F32)
            for t in range(N_META):
                xr = jnp.broadcast_to(xm_q[t:t + 1, 0:Q_STATE], (SUBLANES, Q_STATE))
                xi = jnp.broadcast_to(xm_q[t:t + 1, Q_STATE:], (SUBLANES, Q_STATE))
                hr, hi = _scan_step(lr, li, hr, hi, xr, xi)
            st_ref[:, 2 * q * Q_STATE:(2 * q + 1) * Q_STATE] = hr
            st_ref[:, (2 * q + 1) * Q_STATE:(2 * q + 2) * Q_STATE] = hi

    p1 = {}

    def pre_norm():
        p1["xn"] = _rms(x_ref[...].reshape(ROWS, D_MODEL), gmix_ref[...]).astype(BF16)

    def project_in(key, c0):
        p1[key] = jnp.dot(p1["xn"], win_ref[:, c0:c0 + 2 * LANES], preferred_element_type=F32)

    def finish_q(c):
        zz = p1.pop(("q", c))
        for tt in range(2):
            t = 2 * c + tt
            qn = _pair_norm(zz[:, tt * LANES:(tt + 1) * LANES], gq_ref[...], left) * ATTN_SCALE
            qa = jnp.where(left, qn, 0.0).astype(BF16)
            qb = jnp.where(left, 0.0, qn).astype(BF16)
            for b in range(N_SEQ):
                qs_ref[b, 2 * t * BLOCK:(2 * t + 1) * BLOCK, :] = qa[b * BLOCK:(b + 1) * BLOCK]
                qs_ref[b, (2 * t + 1) * BLOCK:(2 * t + 2) * BLOCK, :] = qb[b * BLOCK:(b + 1) * BLOCK]

    def finish_kv():
        zz = p1.pop("kv")
        kn = _pair_norm(zz[:, :LANES], gk_ref[...], left)
        vn = zz[:, LANES:]
        kw_ref[...] = kn.reshape(N_SEQ, BLOCK, KV_WIDTH)
        vw_ref[...] = vn.reshape(N_SEQ, BLOCK, KV_WIDTH)
        kk_ref[:, BLOCK:, :] = kn.astype(BF16).reshape(N_SEQ, BLOCK, KV_WIDTH)
        vv_ref[:, BLOCK:, :] = vn.astype(BF16).reshape(N_SEQ, BLOCK, KV_WIDTH)

    def finish_u(c):
        zz = p1.pop(("u", c))
        for tt in range(2):
            for b in range(N_SEQ):
                usl_ref[2 * c + tt, b * PITCH:b * PITCH + BLOCK, :] = zz[b * BLOCK:(b + 1) * BLOCK, tt * LANES:(tt + 1) * LANES]

    dot_kv = functools.partial(project_in, "kv", ATTN_WIDTH)
    dot_q = [functools.partial(project_in, ("q", c), c * 2 * LANES) for c in range(2)]
    dot_u = [functools.partial(project_in, ("u", c), ATTN_WIDTH + 2 * KV_WIDTH + c * 2 * LANES) for c in range(2)]
    post_q = [functools.partial(finish_q, c) for c in range(2)]
    post_u = [functools.partial(finish_u, c) for c in range(2)]

    r = lax.broadcasted_iota(jnp.int32, (BLOCK, 2 * BLOCK), 0)
    c = lax.broadcasted_iota(jnp.int32, (BLOCK, 2 * BLOCK), 1)
    valid = (c >= r) & (c <= r + WINDOW) & ((i > 0) | (c >= BLOCK - N_META))
    bias = jnp.where(valid, 0.0, -jnp.inf)

    def gather_u(n):
        t0 = n * CHUNK_T
        return jnp.concatenate(
            [jnp.concatenate([usl_ref[j, pl.ds(t0 + tl, SUBLANES, stride=PITCH), :] for j in range(N_QUARTERS)], axis=1)
             for tl in range(CHUNK_T)], axis=0)

    def feed(n, par):
        env = {}

        def scores():
            s_ref[par] = lax.dot_general(qs_ref[n], kk_ref[n], (((1,), (1,)), ((), ())), preferred_element_type=F32)

        def gather():
            env["u"] = gather_u(n)

        def scan_inputs(q):
            xs_ref[par, :, 2 * q * Q_STATE:(2 * q + 2) * Q_STATE] = jnp.dot(
                env["u"][:, q * LANES:(q + 1) * LANES].astype(BF16), wq_ref[q], preferred_element_type=F32)

        return [scores, gather] + [functools.partial(scan_inputs, q) for q in range(N_QUARTERS)]

    def mid(n, par):
        env = {}

        def row_max(g):
            s = s_ref[par, g * BLOCK:(g + 1) * BLOCK, :] + bias
            env[g] = (s, jnp.maximum(jnp.max(s, axis=-1, keepdims=True), sink_ref[g // 2 + Q_PER_KV * (g % 2)]))

        def exponent(g):
            s, m = env[g]
            p = jnp.exp(s - m)
            sink = sink_ref[g // 2 + Q_PER_KV * (g % 2)]
            env[g] = (p, 1.0 / (jnp.sum(p, axis=-1, keepdims=True) + jnp.exp(sink - m)))

        def normalise(g):
            p, inv = env.pop(g)
            p_ref[par, g * BLOCK:(g + 1) * BLOCK, :] = (p * inv).astype(BF16)

        softmax = [functools.partial(f, g) for g in range(2 * Q_TILES) for f in (row_max, exponent, normalise)]

        def load_state():
            for q in range(N_QUARTERS):
                env["lam", q] = (jnp.broadcast_to(lr_ref[:, q * Q_STATE:(q + 1) * Q_STATE], (SUBLANES, Q_STATE)),
                                 jnp.broadcast_to(li_ref[:, q * Q_STATE:(q + 1) * Q_STATE], (SUBLANES, Q_STATE)))
                env["h", q] = (st_ref[:, 2 * q * Q_STATE:(2 * q + 1) * Q_STATE],
                               st_ref[:, (2 * q + 1) * Q_STATE:(2 * q + 2) * Q_STATE])

        def scan_pair(tp, q):
            lr, li = env["lam", q]
            hr, hi = env["h", q]
            re0, im0 = 2 * q * Q_STATE, (2 * q + 1) * Q_STATE
            pair_r, pair_i = [], []
            for tl in (2 * tp, 2 * tp + 1):
                xr = xs_ref[par, tl * SUBLANES:(tl + 1) * SUBLANES, re0:re0 + Q_STATE]
                xi = xs_ref[par, tl * SUBLANES:(tl + 1) * SUBLANES, im0:im0 + Q_STATE]
                hr, hi = _scan_step(lr, li, hr, hi, xr, xi)
                pair_r.append(hr)
                pair_i.append(hi)
            env["h", q] = (hr, hi)
            rows = slice(2 * tp * SUBLANES, (2 * tp + 2) * SUBLANES)
            hb_ref[par, rows, re0:re0 + Q_STATE] = jnp.concatenate(pair_r, axis=0).astype(BF16)
            hb_ref[par, rows, im0:im0 + Q_STATE] = jnp.concatenate(pair_i, axis=0).astype(BF16)

        def store_state():
            for q in range(N_QUARTERS):
                hr, hi = env["h", q]
                st_ref[:, 2 * q * Q_STATE:(2 * q + 1) * Q_STATE] = hr
                st_ref[:, (2 * q + 1) * Q_STATE:(2 * q + 2) * Q_STATE] = hi

        scan = ([load_state] + [functools.partial(scan_pair, tp, q) for tp in range(CHUNK_T // 2) for q in range(N_QUARTERS)]
                + [store_state])
        return softmax, scan

    def tail(n, par):
        env = {}

        def attn_out():
            o_all = jnp.dot(p_ref[par], vv_ref[n], preferred_element_type=F32)
            for t in range(Q_TILES):
                oatt_ref[n, :, t * LANES:(t + 1) * LANES] = jnp.where(
                    left, o_all[2 * t * BLOCK:(2 * t + 1) * BLOCK], o_all[(2 * t + 1) * BLOCK:(2 * t + 2) * BLOCK])

        def gather():
            env["u"] = gather_u(n)

        def readout(q):
            env["y", q] = jnp.dot(hb_ref[par, :, 2 * q * Q_STATE:(2 * q + 2) * Q_STATE], cq_ref[q],
                                  preferred_element_type=F32)

        def activate(q):
            cols = slice(q * LANES, (q + 1) * LANES)
            env["g", q] = jax.nn.gelu(env.pop(("y", q)) + d_ref[:, cols] * env["u"][:, cols])

        def gate(q):
            cols = slice(q * LANES, (q + 1) * LANES)
            env["gate", q] = jnp.dot(env["g", q].astype(BF16), wglu_ref[q], preferred_element_type=F32) + bglu_ref[:, cols]

        def emit(q):
            o = env.pop(("g", q)) * jax.nn.sigmoid(env.pop(("gate", q)))
            t0 = n * CHUNK_T
            for tl in range(CHUNK_T):
                osl_ref[q, pl.ds(t0 + tl, SUBLANES, stride=PITCH), :] = o[tl * SUBLANES:(tl + 1) * SUBLANES]

        per_quarter = lambda f: [functools.partial(f, q) for q in range(N_QUARTERS)]
        return [attn_out, gather] + per_quarter(readout) + per_quarter(activate) + per_quarter(gate) + per_quarter(emit)

    def run(*stages):
        for step in _interleave(*stages):
            step()

    def steady(k, carry):
        n = 2 * k + 1
        run(feed(n + 1, 0), *mid(n, 1), tail(n - 1, 0), project(k))
        run(feed(n + 2, 1), *mid(n + 1, 0), tail(n, 1))
        return carry

    norms = _output_norm_steps(oatt_ref, osl_ref, ga_ref, gs_ref, mix_ref)
    feed0, feed1 = feed(0, 0), feed(1, 1)
    softmax0, scan0 = mid(0, 0)
    half = len(softmax0) // 2
    run([pre_norm, dot_kv, dot_q[0], finish_kv, dot_q[1], post_q[0], dot_u[0], post_q[1]], norms)
    run(feed0[:1])
    run([dot_u[1]] + feed1[:1] + post_u, softmax0[:half])
    run(feed0[1:], softmax0[half:])
    run(feed1[1:], scan0)
    n_steady = (N_SEQ - 2) // 2
    assert n_steady == OUT_CHUNKS - 1
    lax.fori_loop(0, n_steady, steady, 0)
    run(*mid(N_SEQ - 1, 1), tail(N_SEQ - 2, 0), project(OUT_CHUNKS - 1))
    run(tail(N_SEQ - 1, 1))
    kk_ref[:, 0:BLOCK, :] = kk_ref[:, BLOCK:, :]
    vv_ref[:, 0:BLOCK, :] = vv_ref[:, BLOCK:, :]

    hr_ref[...] = jnp.concatenate([st_ref[:, 2 * q * Q_STATE:(2 * q + 1) * Q_STATE] for q in range(N_QUARTERS)], axis=1)
    hi_ref[...] = jnp.concatenate([st_ref[:, (2 * q + 1) * Q_STATE:(2 * q + 2) * Q_STATE] for q in range(N_QUARTERS)], axis=1)


def _mixer(x_prompt, meta_tokens, sinks, gmix, win, gq2, gk2, sp, ga, gs, wout):
    nseq, seq, _ = x_prompt.shape
    assert nseq == N_SEQ and seq % BLOCK == 0
    n_blocks = seq // BLOCK
    st = pl.BlockSpec((N_SEQ, STATE_COLS), lambda i: (0, 0))
    kvw = pl.BlockSpec((N_SEQ, BLOCK, KV_WIDTH), lambda i: (0, 0, 0))
    return pl.pallas_call(
        _mixer_kernel,
        grid=(n_blocks + 1,),
        in_specs=[pl.BlockSpec(memory_space=pltpu.SMEM),
                  pl.BlockSpec((N_SEQ, BLOCK, D_MODEL), lambda i: (0, jnp.minimum(i, n_blocks - 1), 0)),
                  _const_spec((N_META, D_MODEL)),
                  _const_spec((1, D_MODEL)), _const_spec((D_MODEL, IN_COLS)), _const_spec((1, LANES)),
                  _const_spec((1, LANES)),
                  _const_spec((N_QUARTERS, LANES, 2 * Q_STATE)), _const_spec((1, STATE_COLS)),
                  _const_spec((1, STATE_COLS)), _const_spec((N_QUARTERS, 2 * Q_STATE, LANES)),
                  _const_spec((1, SSM_WIDTH)), _const_spec((N_QUARTERS, LANES, LANES)), _const_spec((1, SSM_WIDTH)),
                  _const_spec((1, ATTN_WIDTH)), _const_spec((1, SSM_WIDTH)),
                  _const_spec((OUT_CHUNKS, D_MODEL, OUT_CHUNK))],
        out_specs=[pl.BlockSpec((OUT_CHUNKS, N_SEQ, BLOCK, OUT_CHUNK), lambda i: (0, 0, jnp.maximum(i - 1, 0), 0)),
                   kvw, kvw, st, st],
        out_shape=[jax.ShapeDtypeStruct((OUT_CHUNKS, N_SEQ, seq, OUT_CHUNK), F32),
                   jax.ShapeDtypeStruct((N_SEQ, BLOCK, KV_WIDTH), F32), jax.ShapeDtypeStruct((N_SEQ, BLOCK, KV_WIDTH), F32),
                   jax.ShapeDtypeStruct((N_SEQ, STATE_COLS), F32), jax.ShapeDtypeStruct((N_SEQ, STATE_COLS), F32)],
        scratch_shapes=[pltpu.VMEM((N_SEQ, 2 * Q_TILES * BLOCK, LANES), BF16),
                        pltpu.VMEM((N_SEQ, 2 * BLOCK, KV_WIDTH), BF16),
                        pltpu.VMEM((N_SEQ, 2 * BLOCK, KV_WIDTH), BF16),
                        pltpu.VMEM((2, 2 * Q_TILES * BLOCK, 2 * BLOCK), F32),
                        pltpu.VMEM((2, 2 * Q_TILES * BLOCK, 2 * BLOCK), BF16),
                        pltpu.VMEM((N_SEQ, BLOCK, ATTN_WIDTH), F32),
                        pltpu.VMEM((N_QUARTERS, N_SEQ * PITCH, LANES), F32),
                        pltpu.VMEM((2, CHUNK_T * N_SEQ, 2 * STATE_COLS), F32),
                        pltpu.VMEM((2, CHUNK_T * N_SEQ, 2 * STATE_COLS), BF16),
                        pltpu.VMEM((N_SEQ, 2 * STATE_COLS), F32),
                        pltpu.VMEM((N_QUARTERS, N_SEQ * PITCH, LANES), F32),
                        pltpu.VMEM((ROWS, D_MODEL), BF16)],
        compiler_params=pltpu.CompilerParams(dimension_semantics=("arbitrary",), vmem_limit_bytes=VMEM_LIMIT),
        name="mixer",
    )(sinks, x_prompt, meta_tokens, gmix, win, gq2, gk2, sp["wq"], sp["lr"], sp["li"], sp["cq"], sp["d"],
      sp["wgluq"], sp["bglu"], ga, gs, wout)


def kernel(x_prompt, x_sample, cache_k_win, cache_v_win, state_ssm_re, state_ssm_im, meta_tokens, g_mix, w_in, g_q,
           g_k, sinks, ssm_a_re, ssm_a_im, ssm_log_dt, ssm_b_re, ssm_b_im, ssm_c_re, ssm_c_im, ssm_d, ssm_w_glu,
           ssm_b_glu, g_att_out, g_ssm_out, w_out, g_ffn, w_gate, w_up, w_down):
    bp, seq, _ = x_prompt.shape
    db, dseq, _ = x_sample.shape
    li = 0
    gmix = g_mix[li].reshape(1, D_MODEL)
    win = w_in[li].astype(BF16)
    gq2 = jnp.tile(g_q[li], 2).reshape(1, LANES)
    gk2 = jnp.tile(g_k[li], 2).reshape(1, LANES)
    sk = sinks[li]
    sp = _ssm_params(ssm_a_re[li], ssm_a_im[li], ssm_log_dt[li], ssm_b_re[li], ssm_b_im[li], ssm_c_re[li],
                     ssm_c_im[li], ssm_d[li], ssm_w_glu[li], ssm_b_glu[li])
    ga = g_att_out[li].reshape(1, ATTN_WIDTH)
    gs = g_ssm_out[li].reshape(1, SSM_WIDTH)
    wout = w_out[li].astype(BF16)
    ffn_w = (g_ffn[li].reshape(1, D_MODEL), w_gate[li].astype(BF16), w_up[li].astype(BF16), w_down[li].astype(BF16))
    regroup = lambda a, axis: jnp.swapaxes(
        a.reshape(a.shape[:axis] + (2, Q_PER_KV, HEAD_DIM) + a.shape[axis + 1:]), axis, axis + 1).reshape(a.shape)
    win = jnp.concatenate([regroup(win[:, :ATTN_WIDTH], 1), win[:, ATTN_WIDTH:]], axis=1)
    wout = jnp.concatenate([regroup(wout[:ATTN_WIDTH], 0), wout[ATTN_WIDTH:]], axis=0)
    ga = regroup(ga, 1)

    wout_chunks = wout.reshape(D_MODEL, OUT_CHUNKS, OUT_CHUNK).transpose(1, 0, 2)
    hproj, kw_p, vw_p, hp_r, hp_i = _mixer(x_prompt, meta_tokens, sk, gmix, win, gq2, gk2, sp, ga, gs, wout_chunks)
    y_prompt = _ffn_call(x_prompt.reshape(bp * seq, D_MODEL), hproj.reshape(OUT_CHUNKS, bp * seq, OUT_CHUNK), *ffn_w,
                         tm=512).reshape(bp, seq, D_MODEL)

    n_s = db * dseq
    xs_rows = x_sample.reshape(n_s, D_MODEL)
    q_s, k_s, v_s, u_s = _front(xs_rows, gmix, win, gq2, gk2, tm=512)
    ck = cache_k_win[li].reshape(db, WINDOW, KV_WIDTH)
    cv = cache_v_win[li].reshape(db, WINDOW, KV_WIDTH)
    oa_s, kw_s, vw_s = _attn_sample(q_s.reshape(db, dseq, ATTN_WIDTH), k_s.reshape(db, dseq, KV_WIDTH),
                                    v_s.reshape(db, dseq, KV_WIDTH), ck, cv, sk)
    os_s, hs_r, hs_i = _ssm_sample(u_s, state_ssm_re[li].reshape(db, STATE_COLS),
                                   state_ssm_im[li].reshape(db, STATE_COLS), sp, nseq=db, t_steps=dseq)
    y_sample = _back(xs_rows, oa_s.reshape(n_s, ATTN_WIDTH), os_s, ga, gs, wout, *ffn_w, tm=512).reshape(db, dseq, D_MODEL)

    kv5 = lambda a, n: a.reshape(1, n, WINDOW, N_KV_HEADS, HEAD_DIM)
    st4 = lambda a, n: a.reshape(1, n, N_SSM_GROUPS, SSM_STATE)
    return (y_prompt, y_sample, kv5(kw_p, bp), kv5(vw_p, bp), st4(hp_r, bp), st4(hp_i, bp),
            kv5(kw_s, db), kv5(vw_s, db), st4(hs_r, db), st4(hs_i, db))
```

```python
import functools

import jax
import jax.numpy as jnp
from jax import lax
from jax.experimental import pallas as pl
from jax.experimental.pallas import tpu as pltpu

D_MODEL = 1024
N_META = 16
HEAD_DIM = 64
ATTN_WIDTH = 512
N_Q_HEADS = 8
Q_PER_KV = 4
N_KV_HEADS = 2
KV_WIDTH = 128
WINDOW = 128
BLOCK = 128
SSM_WIDTH = 512
SSM_GROUP = 16
N_SSM_GROUPS = 32
SSM_STATE = 64
IN_COLS = ATTN_WIDTH + 2 * KV_WIDTH + SSM_WIDTH
D_FF = 2816
EPS = 1e-6
ATTN_SCALE = HEAD_DIM ** -0.5
STATE_COLS = N_SSM_GROUPS * SSM_STATE
LANES = 128
SUBLANES = 8
VMEM_LIMIT = 56 * 1024 * 1024

F32 = jnp.float32
BF16 = jnp.bfloat16


def _const_spec(shape):
    return pl.BlockSpec(shape, lambda *_: (0,) * len(shape), pipeline_mode=pl.Buffered(1))


def _rms(x, g):
    return x * lax.rsqrt(jnp.mean(x * x, axis=-1, keepdims=True) + EPS) * g


def _left_half():
    return lax.broadcasted_iota(jnp.int32, (1, LANES), 1) < HEAD_DIM


def _pair_norm(zz, g2, left):
    sq = zz * zz
    sl = jnp.sum(jnp.where(left, sq, 0.0), axis=-1, keepdims=True)
    sr = jnp.sum(jnp.where(left, 0.0, sq), axis=-1, keepdims=True)
    inv = jnp.where(left, lax.rsqrt(sl / HEAD_DIM + EPS), lax.rsqrt(sr / HEAD_DIM + EPS))
    return zz * inv * g2


def _front_kernel(x_ref, gmix_ref, win_ref, gq_ref, gk_ref, q_ref, k_ref, v_ref, u_ref):
    xn = _rms(x_ref[...], gmix_ref[...]).astype(BF16)
    z = jnp.dot(xn, win_ref[...], preferred_element_type=F32)
    left = _left_half()
    for p in range(ATTN_WIDTH // LANES):
        q_ref[:, p * LANES:(p + 1) * LANES] = _pair_norm(z[:, p * LANES:(p + 1) * LANES], gq_ref[...], left)
    k_ref[...] = _pair_norm(z[:, ATTN_WIDTH:ATTN_WIDTH + KV_WIDTH], gk_ref[...], left)
    v_ref[...] = z[:, ATTN_WIDTH + KV_WIDTH:ATTN_WIDTH + 2 * KV_WIDTH]
    u_ref[...] = z[:, ATTN_WIDTH + 2 * KV_WIDTH:]


def _front(x_rows, gmix, win_bf, gq2, gk2, tm):
    n = x_rows.shape[0]
    assert n % tm == 0
    row = lambda w: pl.BlockSpec((tm, w), lambda i: (i, 0))
    return pl.pallas_call(
        _front_kernel,
        grid=(n // tm,),
        in_specs=[row(D_MODEL), _const_spec((1, D_MODEL)), _const_spec((D_MODEL, IN_COLS)),
                  _const_spec((1, LANES)), _const_spec((1, LANES))],
        out_specs=[row(ATTN_WIDTH), row(KV_WIDTH), row(KV_WIDTH), row(SSM_WIDTH)],
        out_shape=[jax.ShapeDtypeStruct((n, ATTN_WIDTH), F32), jax.ShapeDtypeStruct((n, KV_WIDTH), F32),
                   jax.ShapeDtypeStruct((n, KV_WIDTH), F32), jax.ShapeDtypeStruct((n, SSM_WIDTH), F32)],
        compiler_params=pltpu.CompilerParams(dimension_semantics=("arbitrary",), vmem_limit_bytes=VMEM_LIMIT),
        name="front",
    )(x_rows, gmix, win_bf, gq2, gk2)


Q_TILES = ATTN_WIDTH // LANES


def _attn_sample_kernel(sink_ref, q_ref, kn_ref, vn_ref, ck_ref, cv_ref, o_ref, kw_ref, vw_ref, *, bb, t):
    left = _left_half()
    n_heads = 2 * Q_TILES
    rows = n_heads * t
    tk = WINDOW + t
    r = lax.broadcasted_iota(jnp.int32, (rows, tk), 0) % t
    c = lax.broadcasted_iota(jnp.int32, (rows, tk), 1)
    bias = jnp.where((c >= r) & (c <= r + WINDOW), 0.0, -jnp.inf)
    hrow = lax.broadcasted_iota(jnp.int32, (rows, 1), 0) // t
    sink = jnp.zeros((rows, 1), F32)
    for g in range(n_heads):
        sink = jnp.where(hrow == g, sink_ref[g // 2 + Q_PER_KV * (g % 2)], sink)
    scores, values = [], []
    for bi in range(bb):
        kk = jnp.concatenate([ck_ref[bi], kn_ref[bi]], axis=0)
        vv = jnp.concatenate([cv_ref[bi], vn_ref[bi]], axis=0)
        kw_ref[bi] = kk[t:]
        vw_ref[bi] = vv[t:]
        q = q_ref[bi] * ATTN_SCALE
        pieces = []
        for tile in range(Q_TILES):
            qt = q[:, tile * LANES:(tile + 1) * LANES]
            pieces += [jnp.where(left, qt, 0.0), jnp.where(left, 0.0, qt)]
        qs = jnp.concatenate(pieces, axis=0).astype(BF16)
        scores.append(lax.dot_general(qs, kk.astype(BF16), (((1,), (1,)), ((), ())), preferred_element_type=F32))
        values.append(vv.astype(BF16))
    probs = []
    for bi in range(bb):
        s = scores[bi] + bias
        m = jnp.maximum(jnp.max(s, axis=-1, keepdims=True), sink)
        p = jnp.exp(s - m)
        inv = 1.0 / (jnp.sum(p, axis=-1, keepdims=True) + jnp.exp(sink - m))
        probs.append((p * inv).astype(BF16))
    for bi in range(bb):
        o = jnp.dot(probs[bi], values[bi], preferred_element_type=F32)
        for tile in range(Q_TILES):
            o_ref[bi, :, tile * LANES:(tile + 1) * LANES] = jnp.where(
                left, o[2 * tile * t:(2 * tile + 1) * t], o[(2 * tile + 1) * t:(2 * tile + 2) * t])


def _attn_sample(q, kn, vn, ck, cv, sinks, bb=8):
    db, t, _ = q.shape
    blk = lambda r, w: pl.BlockSpec((bb, r, w), lambda i: (i, 0, 0))
    return pl.pallas_call(
        functools.partial(_attn_sample_kernel, bb=bb, t=t),
        grid=(db // bb,),
        in_specs=[pl.BlockSpec(memory_space=pltpu.SMEM), blk(t, ATTN_WIDTH), blk(t, KV_WIDTH), blk(t, KV_WIDTH),
                  blk(WINDOW, KV_WIDTH), blk(WINDOW, KV_WIDTH)],
        out_specs=[blk(t, ATTN_WIDTH), blk(WINDOW, KV_WIDTH), blk(WINDOW, KV_WIDTH)],
        out_shape=[jax.ShapeDtypeStruct((db, t, ATTN_WIDTH), F32),
                   jax.ShapeDtypeStruct((db, WINDOW, KV_WIDTH), F32),
                   jax.ShapeDtypeStruct((db, WINDOW, KV_WIDTH), F32)],
        compiler_params=pltpu.CompilerParams(dimension_semantics=("arbitrary",)),
        name="attn_sample",
    )(sinks, q, kn, vn, ck, cv)


N_QUARTERS = 4
Q_GROUPS = N_SSM_GROUPS // N_QUARTERS
Q_STATE = Q_GROUPS * SSM_STATE


def _ssm_params(a_re, a_im, log_dt, b_re, b_im, c_re, c_im, d_skip, w_glu, b_glu):
    dt = jnp.exp(log_dt)[:, None]
    mag = jnp.exp(a_re * dt)
    lr, li = mag * jnp.cos(a_im * dt), mag * jnp.sin(a_im * dt)
    nr, ni = lr - 1.0, li
    den = a_re * a_re + a_im * a_im
    fr, fi = (nr * a_re + ni * a_im) / den, (ni * a_re - nr * a_im) / den
    bbr = fr[..., None] * b_re - fi[..., None] * b_im
    bbi = fr[..., None] * b_im + fi[..., None] * b_re
    eye = jnp.eye(Q_GROUPS, dtype=F32)
    quartered = lambda a: a.reshape((a.shape[0], N_QUARTERS, Q_GROUPS) + a.shape[2:])
    wq = jnp.einsum('rqgph,gm->qghrmp', quartered(jnp.stack([bbr, bbi])), eye).reshape(N_QUARTERS, LANES, 2 * Q_STATE)
    cq = jnp.einsum('rqghp,gm->qrgpmh', quartered(jnp.stack([c_re, -c_im])), eye).reshape(N_QUARTERS, 2 * Q_STATE, LANES)
    wglu = jnp.einsum('qghk,gm->qghmk', w_glu.reshape(N_QUARTERS, Q_GROUPS, SSM_GROUP, SSM_GROUP), eye)
    return dict(wq=wq.astype(BF16), cq=cq.astype(BF16), wgluq=wglu.reshape(N_QUARTERS, LANES, LANES).astype(BF16),
                lr=lr.reshape(1, STATE_COLS), li=li.reshape(1, STATE_COLS), d=d_skip.reshape(1, SSM_WIDTH),
                bglu=b_glu.reshape(1, SSM_WIDTH))


def _scan_step(lr, li, hr, hi, xr, xi):
    return lr * hr - li * hi + xr, lr * hi + li * hr + xi


def _ssm_sample_kernel(u0_ref, u1_ref, u2_ref, u3_ref, h0r_ref, h0i_ref, wq_ref, lr_ref, li_ref, cq_ref, d_ref,
                       wglu_ref, bglu_ref, o_ref, hr_ref, hi_ref, utb_ref, xs_ref, *, nseq, t_steps):
    u_refs = (u0_ref, u1_ref, u2_ref, u3_ref)
    quarters = range(N_QUARTERS)
    n_rg = nseq // SUBLANES
    rg_rows = SUBLANES * t_steps

    for t in range(t_steps):
        for rg in range(n_rg):
            for j in quarters:
                utb_ref[t * nseq + rg * SUBLANES:t * nseq + (rg + 1) * SUBLANES, j * LANES:(j + 1) * LANES] = (
                    u_refs[j][pl.ds(rg * rg_rows + t, SUBLANES, stride=t_steps), :])

    u = [utb_ref[:, q * LANES:(q + 1) * LANES] for q in quarters]
    for q in quarters:
        xs_ref[q] = jnp.dot(u[q].astype(BF16), wq_ref[q], preferred_element_type=F32)
    lam = [(jnp.broadcast_to(lr_ref[:, q * Q_STATE:(q + 1) * Q_STATE], (SUBLANES, Q_STATE)),
            jnp.broadcast_to(li_ref[:, q * Q_STATE:(q + 1) * Q_STATE], (SUBLANES, Q_STATE))) for q in quarters]

    def scan_group(rg, carry):
        r0 = pl.multiple_of(rg * SUBLANES, SUBLANES)
        h = [(h0r_ref[pl.ds(r0, SUBLANES), q * Q_STATE:(q + 1) * Q_STATE],
              h0i_ref[pl.ds(r0, SUBLANES), q * Q_STATE:(q + 1) * Q_STATE]) for q in quarters]
        for t in range(t_steps):
            rows = pl.ds(pl.multiple_of(t * nseq + r0, SUBLANES), SUBLANES)
            for q in quarters:
                h[q] = _scan_step(*lam[q], *h[q], xs_ref[q, rows, 0:Q_STATE], xs_ref[q, rows, Q_STATE:])
                xs_ref[q, rows, 0:Q_STATE] = h[q][0]
                xs_ref[q, rows, Q_STATE:] = h[q][1]
        for q in quarters:
            hr_ref[pl.ds(r0, SUBLANES), q * Q_STATE:(q + 1) * Q_STATE] = h[q][0]
            hi_ref[pl.ds(r0, SUBLANES), q * Q_STATE:(q + 1) * Q_STATE] = h[q][1]
        return carry

    lax.fori_loop(0, n_rg, scan_group, 0)
    y = [jnp.dot(xs_ref[q].astype(BF16), cq_ref[q], preferred_element_type=F32) for q in quarters]
    g = [jax.nn.gelu(y[q] + d_ref[:, q * LANES:(q + 1) * LANES] * u[q]) for q in quarters]
    gate = [jnp.dot(g[q].astype(BF16), wglu_ref[q], preferred_element_type=F32) + bglu_ref[:, q * LANES:(q + 1) * LANES]
            for q in quarters]
    for q in quarters:
        o = g[q] * jax.nn.sigmoid(gate[q])
        for t in range(t_steps):
            for rg in range(n_rg):
                o_ref[q, pl.ds(rg * rg_rows + t, SUBLANES, stride=t_steps), :] = (
                    o[t * nseq + rg * SUBLANES:t * nseq + (rg + 1) * SUBLANES, :])


def _ssm_sample(u_rows, h0r, h0i, sp, nseq, t_steps):
    n = nseq * t_steps
    slab = lambda j: pl.BlockSpec((n, LANES), lambda i, j=j: (0, j))
    st = pl.BlockSpec((nseq, STATE_COLS), lambda i: (0, 0))
    return pl.pallas_call(
        functools.partial(_ssm_sample_kernel, nseq=nseq, t_steps=t_steps),
        grid=(1,),
        in_specs=[slab(0), slab(1), slab(2), slab(3), st, st,
                  _const_spec((N_QUARTERS, LANES, 2 * Q_STATE)), _const_spec((1, STATE_COLS)),
                  _const_spec((1, STATE_COLS)), _const_spec((N_QUARTERS, 2 * Q_STATE, LANES)),
                  _const_spec((1, SSM_WIDTH)), _const_spec((N_QUARTERS, LANES, LANES)), _const_spec((1, SSM_WIDTH))],
        out_specs=[pl.BlockSpec((N_QUARTERS, n, LANES), lambda i: (0, 0, 0)), st, st],
        out_shape=[jax.ShapeDtypeStruct((N_QUARTERS, n, LANES), F32),
                   jax.ShapeDtypeStruct((nseq, STATE_COLS), F32), jax.ShapeDtypeStruct((nseq, STATE_COLS), F32)],
        scratch_shapes=[pltpu.VMEM((n, SSM_WIDTH), F32), pltpu.VMEM((N_QUARTERS, n, 2 * Q_STATE), F32)],
        compiler_params=pltpu.CompilerParams(dimension_semantics=("arbitrary",), vmem_limit_bytes=VMEM_LIMIT),
        name="ssm_sample",
    )(u_rows, u_rows, u_rows, u_rows, h0r, h0i, sp["wq"], sp["lr"], sp["li"], sp["cq"], sp["d"], sp["wgluq"], sp["bglu"])


def _ffn(h, gf_ref, wg_ref, wu_ref, wd_ref):
    f = _rms(h, gf_ref[...]).astype(BF16)
    gate = jnp.dot(f, wg_ref[...], preferred_element_type=F32)
    up = jnp.dot(f, wu_ref[...], preferred_element_type=F32)
    a = (jax.nn.silu(gate) * up).astype(BF16)
    return h + jnp.dot(a, wd_ref[...], preferred_element_type=F32)


def _back_kernel(x_ref, oa_ref, os_ref, ga_ref, gs_ref, wout_ref, gf_ref, wg_ref, wu_ref, wd_ref, y_ref):
    o_ssm = jnp.concatenate([os_ref[j] for j in range(N_QUARTERS)], axis=-1)
    mix = jnp.concatenate([_rms(oa_ref[...], ga_ref[...]), _rms(o_ssm, gs_ref[...])], axis=-1).astype(BF16)
    h = x_ref[...] + jnp.dot(mix, wout_ref[...], preferred_element_type=F32)
    y_ref[...] = _ffn(h, gf_ref, wg_ref, wu_ref, wd_ref)


FF_COLS = ((0, 1024), (1024, 2048), (2048, D_FF))
PRE_ROWS = 128


def _ffn_pre_steps(x_ref, hp_ref, gf_ref, h_ref, f_ref):
    def chunk(r0):
        rows = slice(r0, r0 + PRE_ROWS)
        h = x_ref[rows, :] + jnp.concatenate([hp_ref[c, rows, :] for c in range(hp_ref.shape[0])], axis=-1)
        h_ref[rows, :] = h
        f_ref[rows, :] = _rms(h, gf_ref[...]).astype(BF16)
    return [functools.partial(chunk, r0) for r0 in range(0, x_ref.shape[0], PRE_ROWS)]


def _ffn_mm_steps(h_ref, f_ref, wg_ref, wu_ref, wd_ref, y_ref, row0):
    env = {}

    def gate_up(k):
        lo, hi = FF_COLS[k]
        f = f_ref[...]
        env["gate", k] = jnp.dot(f, wg_ref[:, lo:hi], preferred_element_type=F32)
        env["up", k] = jnp.dot(f, wu_ref[:, lo:hi], preferred_element_type=F32)

    def activate(k):
        env["a", k] = (jax.nn.silu(env.pop(("gate", k))) * env.pop(("up", k))).astype(BF16)

    def down(k):
        lo, hi = FF_COLS[k]
        part = jnp.dot(env.pop(("a", k)), wd_ref[lo:hi, :], preferred_element_type=F32)
        env["acc"] = part if k == 0 else env["acc"] + part

    def store():
        y_ref[row0:row0 + h_ref.shape[0], :] = h_ref[...] + env.pop("acc")

    n = len(FF_COLS)
    steps = [functools.partial(gate_up, 0)]
    for k in range(n):
        if k + 1 < n:
            steps.append(functools.partial(gate_up, k + 1))
        steps += [functools.partial(activate, k), functools.partial(down, k)]
    return steps + [store]


def _ffn_kernel(xa_ref, xb_ref, hpa_ref, hpb_ref, gf_ref, wg_ref, wu_ref, wd_ref, y_ref, hp_s, fp_s, hq_s, fq_s):
    j = pl.program_id(0)
    tm = xa_ref.shape[0]

    @pl.when(j == 0)
    def _():
        for step in _ffn_pre_steps(xb_ref, hpb_ref, gf_ref, hp_s, fp_s):
            step()

    @pl.when(j > 0)
    def _():
        for step in _interleave(_ffn_mm_steps(hp_s, fp_s, wg_ref, wu_ref, wd_ref, y_ref, 0),
                                _ffn_pre_steps(xa_ref, hpa_ref, gf_ref, hq_s, fq_s)):
            step()
        for step in _interleave(_ffn_mm_steps(hq_s, fq_s, wg_ref, wu_ref, wd_ref, y_ref, tm),
                                _ffn_pre_steps(xb_ref, hpb_ref, gf_ref, hp_s, fp_s)):
            step()


def _back(x_rows, oa, os_, ga, gs, wout, gf, wg, wu, wd, tm):
    n = x_rows.shape[0]
    assert n % tm == 0
    row = lambda w: pl.BlockSpec((tm, w), lambda i: (i, 0))
    return pl.pallas_call(
        _back_kernel,
        grid=(n // tm,),
        in_specs=[row(D_MODEL), row(ATTN_WIDTH), pl.BlockSpec((N_QUARTERS, tm, LANES), lambda i: (0, i, 0)),
                  _const_spec((1, ATTN_WIDTH)), _const_spec((1, SSM_WIDTH)), _const_spec((D_MODEL, D_MODEL)),
                  _const_spec((1, D_MODEL)), _const_spec((D_MODEL, D_FF)), _const_spec((D_MODEL, D_FF)),
                  _const_spec((D_FF, D_MODEL))],
        out_specs=row(D_MODEL),
        out_shape=jax.ShapeDtypeStruct((n, D_MODEL), F32),
        compiler_params=pltpu.CompilerParams(dimension_semantics=("arbitrary",), vmem_limit_bytes=VMEM_LIMIT),
        name="back",
    )(x_rows, oa, os_, ga, gs, wout, gf, wg, wu, wd)


def _ffn_call(x_rows, hp_rows, gf, wg, wu, wd, tm):
    n = x_rows.shape[0]
    assert n % (2 * tm) == 0
    last = n // tm - 1
    odd = lambda j: jnp.maximum(2 * j - 1, 0)
    even = lambda j: jnp.minimum(2 * j, last)
    xs = lambda pick: pl.BlockSpec((tm, D_MODEL), lambda j: (pick(j), 0))
    hps = lambda pick: pl.BlockSpec((hp_rows.shape[0], tm, hp_rows.shape[2]), lambda j: (0, pick(j), 0))
    return pl.pallas_call(
        _ffn_kernel,
        grid=(n // (2 * tm) + 1,),
        in_specs=[xs(odd), xs(even), hps(odd), hps(even), _const_spec((1, D_MODEL)),
                  _const_spec((D_MODEL, D_FF)), _const_spec((D_MODEL, D_FF)), _const_spec((D_FF, D_MODEL))],
        out_specs=pl.BlockSpec((2 * tm, D_MODEL), lambda j: (jnp.maximum(j - 1, 0), 0)),
        out_shape=jax.ShapeDtypeStruct((n, D_MODEL), F32),
        scratch_shapes=[pltpu.VMEM((tm, D_MODEL), F32), pltpu.VMEM((tm, D_MODEL), BF16),
                        pltpu.VMEM((tm, D_MODEL), F32), pltpu.VMEM((tm, D_MODEL), BF16)],
        compiler_params=pltpu.CompilerParams(dimension_semantics=("arbitrary",), vmem_limit_bytes=VMEM_LIMIT),
        name="ffn",
    )(x_rows, x_rows, hp_rows, hp_rows, gf, wg, wu, wd)


N_SEQ = 8
ROWS = N_SEQ * BLOCK
PITCH = BLOCK + SUBLANES
CHUNK_T = BLOCK // N_SEQ


def _interleave(*stages):
    keyed = [((i + 0.5) / len(steps), k, i, step) for k, steps in enumerate(stages) for i, step in enumerate(steps)]
    return [step for _, _, _, step in sorted(keyed, key=lambda e: e[:3])]


def _mixer_kernel(sink_ref, x_ref, meta_ref, gmix_ref, win_ref, gq_ref, gk_ref,
                  wq_ref, lr_ref, li_ref, cq_ref, d_ref, wglu_ref, bglu_ref, ga_ref, gs_ref, wout_ref,
                  hp_ref, kw_ref, vw_ref, hr_ref, hi_ref,
                  qs_ref, kk_ref, vv_ref, s_ref, p_ref, oatt_ref, usl_ref, xs_ref, hb_ref, st_ref, osl_ref, mix_ref):
    i = pl.program_id(0)
    n_blocks = pl.num_programs(0) - 1

    @pl.when(i < n_blocks)
    def _():
        _mixer_block(i, sink_ref, x_ref, meta_ref, gmix_ref, win_ref, gq_ref, gk_ref,
                     wq_ref, lr_ref, li_ref, cq_ref, d_ref, wglu_ref, bglu_ref, ga_ref, gs_ref, wout_ref,
                     hp_ref, kw_ref, vw_ref, hr_ref, hi_ref,
                     qs_ref, kk_ref, vv_ref, s_ref, p_ref, oatt_ref, usl_ref, xs_ref, hb_ref, st_ref, osl_ref, mix_ref)

    @pl.when(i == n_blocks)
    def _():
        for step in _output_norm_steps(oatt_ref, osl_ref, ga_ref, gs_ref, mix_ref):
            step()
        for c in range(OUT_CHUNKS):
            _project_chunk(mix_ref, wout_ref, hp_ref, c)


OUT_CHUNKS = 4
OUT_CHUNK = D_MODEL // OUT_CHUNKS


def _project_chunk(mix_ref, wout_ref, hp_ref, c):
    hp_ref[c] = jnp.dot(mix_ref[...], wout_ref[c], preferred_element_type=F32).reshape(N_SEQ, BLOCK, OUT_CHUNK)


def _output_norm_steps(oatt_ref, osl_ref, ga_ref, gs_ref, mix_ref):
    def one(b):
        rows = slice(b * BLOCK, (b + 1) * BLOCK)
        mix_ref[rows, 0:ATTN_WIDTH] = _rms(oatt_ref[b], ga_ref[...]).astype(BF16)
        sl = [osl_ref[j, b * PITCH:b * PITCH + BLOCK, :] for j in range(N_QUARTERS)]
        ms = sum(jnp.sum(s * s, axis=-1, keepdims=True) for s in sl) / SSM_WIDTH
        inv = lax.rsqrt(ms + EPS)
        for j in range(N_QUARTERS):
            mix_ref[rows, ATTN_WIDTH + j * LANES:ATTN_WIDTH + (j + 1) * LANES] = (
                sl[j] * inv * gs_ref[:, j * LANES:(j + 1) * LANES]).astype(BF16)
    return [functools.partial(one, b) for b in range(N_SEQ)]


def _mixer_block(i, sink_ref, x_ref, meta_ref, gmix_ref, win_ref, gq_ref, gk_ref,
                 wq_ref, lr_ref, li_ref, cq_ref, d_ref, wglu_ref, bglu_ref, ga_ref, gs_ref, wout_ref,
                 hp_ref, kw_ref, vw_ref, hr_ref, hi_ref,
                 qs_ref, kk_ref, vv_ref, s_ref, p_ref, oatt_ref, usl_ref, xs_ref, hb_ref, st_ref, osl_ref, mix_ref):
    left = _left_half()
    project = lambda c: [functools.partial(_project_chunk, mix_ref, wout_ref, hp_ref, c)]

    @pl.when(i == 0)
    def _():
        oatt_ref[...] = jnp.zeros((N_SEQ, BLOCK, ATTN_WIDTH), F32)
        osl_ref[...] = jnp.zeros((N_QUARTERS, N_SEQ * PITCH, LANES), F32)
        xm = _rms(meta_ref[...], gmix_ref[...]).astype(BF16)
        zm = jnp.dot(xm, win_ref[:, ATTN_WIDTH:], preferred_element_type=F32)
        km = _pair_norm(zm[:, :KV_WIDTH], gk_ref[...], left).astype(BF16)
        vm = zm[:, KV_WIDTH:2 * KV_WIDTH].astype(BF16)
        lead = jnp.zeros((BLOCK - N_META, KV_WIDTH), BF16)
        for b in range(N_SEQ):
            kk_ref[b, 0:BLOCK - N_META, :] = lead
            vv_ref[b, 0:BLOCK - N_META, :] = lead
            kk_ref[b, BLOCK - N_META:BLOCK, :] = km
            vv_ref[b, BLOCK - N_META:BLOCK, :] = vm
        um = zm[:, 2 * KV_WIDTH:]
        for q in range(N_QUARTERS):
            xm_q = jnp.dot(um[:, q * LANES:(q + 1) * LANES].astype(BF16), wq_ref[q], preferred_element_type=F32)
            lr = jnp.broadcast_to(lr_ref[:, q * Q_STATE:(q + 1) * Q_STATE], (SUBLANES, Q_STATE))
            li = jnp.broadcast_to(li_ref[:, q * Q_STATE:(q + 1) * Q_STATE], (SUBLANES, Q_STATE))
            hr = jnp.zeros((SUBLANES, Q_STATE), F32)
            hi = jnp.zeros((SUBLANES, Q_STATE), F32)
            for t in range(N_META):
                xr = jnp.broadcast_to(xm_q[t:t + 1, 0:Q_STATE], (SUBLANES, Q_STATE))
                xi = jnp.broadcast_to(xm_q[t:t + 1, Q_STATE:], (SUBLANES, Q_STATE))
                hr, hi = _scan_step(lr, li, hr, hi, xr, xi)
            st_ref[:, 2 * q * Q_STATE:(2 * q + 1) * Q_STATE] = hr
            st_ref[:, (2 * q + 1) * Q_STATE:(2 * q + 2) * Q_STATE] = hi

    p1 = {}

    def pre_norm():
        p1["xn"] = _rms(x_ref[...].reshape(ROWS, D_MODEL), gmix_ref[...]).astype(BF16)

    def project_in(key, c0):
        p1[key] = jnp.dot(p1["xn"], win_ref[:, c0:c0 + 2 * LANES], preferred_element_type=F32)

    def finish_q(c):
        zz = p1.pop(("q", c))
        for tt in range(2):
            t = 2 * c + tt
            qn = _pair_norm(zz[:, tt * LANES:(tt + 1) * LANES], gq_ref[...], left) * ATTN_SCALE
            qa = jnp.where(left, qn, 0.0).astype(BF16)
            qb = jnp.where(left, 0.0, qn).astype(BF16)
            for b in range(N_SEQ):
                qs_ref[b, 2 * t * BLOCK:(2 * t + 1) * BLOCK, :] = qa[b * BLOCK:(b + 1) * BLOCK]
                qs_ref[b, (2 * t + 1) * BLOCK:(2 * t + 2) * BLOCK, :] = qb[b * BLOCK:(b + 1) * BLOCK]

    def finish_kv():
        zz = p1.pop("kv")
        kn = _pair_norm(zz[:, :LANES], gk_ref[...], left)
        vn = zz[:, LANES:]
        kw_ref[...] = kn.reshape(N_SEQ, BLOCK, KV_WIDTH)
        vw_ref[...] = vn.reshape(N_SEQ, BLOCK, KV_WIDTH)
        kk_ref[:, BLOCK:, :] = kn.astype(BF16).reshape(N_SEQ, BLOCK, KV_WIDTH)
        vv_ref[:, BLOCK:, :] = vn.astype(BF16).reshape(N_SEQ, BLOCK, KV_WIDTH)

    def finish_u(c):
        zz = p1.pop(("u", c))
        for tt in range(2):
            for b in range(N_SEQ):
                usl_ref[2 * c + tt, b * PITCH:b * PITCH + BLOCK, :] = zz[b * BLOCK:(b + 1) * BLOCK, tt * LANES:(tt + 1) * LANES]

    dot_kv = functools.partial(project_in, "kv", ATTN_WIDTH)
    dot_q = [functools.partial(project_in, ("q", c), c * 2 * LANES) for c in range(2)]
    dot_u = [functools.partial(project_in, ("u", c), ATTN_WIDTH + 2 * KV_WIDTH + c * 2 * LANES) for c in range(2)]
    post_q = [functools.partial(finish_q, c) for c in range(2)]
    post_u = [functools.partial(finish_u, c) for c in range(2)]

    r = lax.broadcasted_iota(jnp.int32, (BLOCK, 2 * BLOCK), 0)
    c = lax.broadcasted_iota(jnp.int32, (BLOCK, 2 * BLOCK), 1)
    valid = (c >= r) & (c <= r + WINDOW) & ((i > 0) | (c >= BLOCK - N_META))
    bias = jnp.where(valid, 0.0, -jnp.inf)

    def gather_u(n):
        t0 = n * CHUNK_T
        return jnp.concatenate(
            [jnp.concatenate([usl_ref[j, pl.ds(t0 + tl, SUBLANES, stride=PITCH), :] for j in range(N_QUARTERS)], axis=1)
             for tl in range(CHUNK_T)], axis=0)

    def feed(n, par):
        env = {}

        def scores():
            s_ref[par] = lax.dot_general(qs_ref[n], kk_ref[n], (((1,), (1,)), ((), ())), preferred_element_type=F32)

        def gather():
            env["u"] = gather_u(n)

        def scan_inputs(q):
            xs_ref[par, :, 2 * q * Q_STATE:(2 * q + 2) * Q_STATE] = jnp.dot(
                env["u"][:, q * LANES:(q + 1) * LANES].astype(BF16), wq_ref[q], preferred_element_type=F32)

        return [scores, gather] + [functools.partial(scan_inputs, q) for q in range(N_QUARTERS)]

    def mid(n, par):
        env = {}

        def row_max(g):
            s = s_ref[par, g * BLOCK:(g + 1) * BLOCK, :] + bias
            env[g] = (s, jnp.maximum(jnp.max(s, axis=-1, keepdims=True), sink_ref[g // 2 + Q_PER_KV * (g % 2)]))

        def exponent(g):
            s, m = env[g]
            p = jnp.exp(s - m)
            sink = sink_ref[g // 2 + Q_PER_KV * (g % 2)]
            env[g] = (p, 1.0 / (jnp.sum(p, axis=-1, keepdims=True) + jnp.exp(sink - m)))

        def normalise(g):
            p, inv = env.pop(g)
            p_ref[par, g * BLOCK:(g + 1) * BLOCK, :] = (p * inv).astype(BF16)

        softmax = [functools.partial(f, g) for g in range(2 * Q_TILES) for f in (row_max, exponent, normalise)]

        def load_state():
            for q in range(N_QUARTERS):
                env["lam", q] = (jnp.broadcast_to(lr_ref[:, q * Q_STATE:(q + 1) * Q_STATE], (SUBLANES, Q_STATE)),
                                 jnp.broadcast_to(li_ref[:, q * Q_STATE:(q + 1) * Q_STATE], (SUBLANES, Q_STATE)))
                env["h", q] = (st_ref[:, 2 * q * Q_STATE:(2 * q + 1) * Q_STATE],
                               st_ref[:, (2 * q + 1) * Q_STATE:(2 * q + 2) * Q_STATE])

        def scan_pair(tp, q):
            lr, li = env["lam", q]
            hr, hi = env["h", q]
            re0, im0 = 2 * q * Q_STATE, (2 * q + 1) * Q_STATE
            pair_r, pair_i = [], []
            for tl in (2 * tp, 2 * tp + 1):
                xr = xs_ref[par, tl * SUBLANES:(tl + 1) * SUBLANES, re0:re0 + Q_STATE]
                xi = xs_ref[par, tl * SUBLANES:(tl + 1) * SUBLANES, im0:im0 + Q_STATE]
                hr, hi = _scan_step(lr, li, hr, hi, xr, xi)
                pair_r.append(hr)
                pair_i.append(hi)
            env["h", q] = (hr, hi)
            rows = slice(2 * tp * SUBLANES, (2 * tp + 2) * SUBLANES)
            hb_ref[par, rows, re0:re0 + Q_STATE] = jnp.concatenate(pair_r, axis=0).astype(BF16)
            hb_ref[par, rows, im0:im0 + Q_STATE] = jnp.concatenate(pair_i, axis=0).astype(BF16)

        def store_state():
            for q in range(N_QUARTERS):
                hr, hi = env["h", q]
                st_ref[:, 2 * q * Q_STATE:(2 * q + 1) * Q_STATE] = hr
                st_ref[:, (2 * q + 1) * Q_STATE:(2 * q + 2) * Q_STATE] = hi

        scan = ([load_state] + [functools.partial(scan_pair, tp, q) for tp in range(CHUNK_T // 2) for q in range(N_QUARTERS)]
                + [store_state])
        return softmax, scan

    def tail(n, par):
        env = {}

        def attn_out():
            o_all = jnp.dot(p_ref[par], vv_ref[n], preferred_element_type=F32)
            for t in range(Q_TILES):
                oatt_ref[n, :, t * LANES:(t + 1) * LANES] = jnp.where(
                    left, o_all[2 * t * BLOCK:(2 * t + 1) * BLOCK], o_all[(2 * t + 1) * BLOCK:(2 * t + 2) * BLOCK])

        def gather():
            env["u"] = gather_u(n)

        def readout(q):
            env["y", q] = jnp.dot(hb_ref[par, :, 2 * q * Q_STATE:(2 * q + 2) * Q_STATE], cq_ref[q],
                                  preferred_element_type=F32)

        def activate(q):
            cols = slice(q * LANES, (q + 1) * LANES)
            env["g", q] = jax.nn.gelu(env.pop(("y", q)) + d_ref[:, cols] * env["u"][:, cols])

        def gate(q):
            cols = slice(q * LANES, (q + 1) * LANES)
            env["gate", q] = jnp.dot(env["g", q].astype(BF16), wglu_ref[q], preferred_element_type=F32) + bglu_ref[:, cols]

        def emit(q):
            o = env.pop(("g", q)) * jax.nn.sigmoid(env.pop(("gate", q)))
            t0 = n * CHUNK_T
            for tl in range(CHUNK_T):
                osl_ref[q, pl.ds(t0 + tl, SUBLANES, stride=PITCH), :] = o[tl * SUBLANES:(tl + 1) * SUBLANES]

        per_quarter = lambda f: [functools.partial(f, q) for q in range(N_QUARTERS)]
        return [attn_out, gather] + per_quarter(readout) + per_quarter(activate) + per_quarter(gate) + per_quarter(emit)

    def run(*stages):
        for step in _interleave(*stages):
            step()

    def steady(k, carry):
        n = 2 * k + 1
        run(feed(n + 1, 0), *mid(n, 1), tail(n - 1, 0), project(k))
        run(feed(n + 2, 1), *mid(n + 1, 0), tail(n, 1))
        return carry

    norms = _output_norm_steps(oatt_ref, osl_ref, ga_ref, gs_ref, mix_ref)
    feed0, feed1 = feed(0, 0), feed(1, 1)
    softmax0, scan0 = mid(0, 0)
    half = len(softmax0) // 2
    run([pre_norm, dot_kv, dot_q[0], finish_kv, dot_q[1], post_q[0], dot_u[0], post_q[1]], norms)
    run(feed0[:1])
    run([dot_u[1]] + feed1[:1] + post_u, softmax0[:half])
    run(feed0[1:], softmax0[half:])
    run(feed1[1:], scan0)
    n_steady = (N_SEQ - 2) // 2
    assert n_steady == OUT_CHUNKS - 1
    lax.fori_loop(0, n_steady, steady, 0)
    run(*mid(N_SEQ - 1, 1), tail(N_SEQ - 2, 0), project(OUT_CHUNKS - 1))
    run(tail(N_SEQ - 1, 1))
    kk_ref[:, 0:BLOCK, :] = kk_ref[:, BLOCK:, :]
    vv_ref[:, 0:BLOCK, :] = vv_ref[:, BLOCK:, :]

    hr_ref[...] = jnp.concatenate([st_ref[:, 2 * q * Q_STATE:(2 * q + 1) * Q_STATE] for q in range(N_QUARTERS)], axis=1)
    hi_ref[...] = jnp.concatenate([st_ref[:, (2 * q + 1) * Q_STATE:(2 * q + 2) * Q_STATE] for q in range(N_QUARTERS)], axis=1)


def _mixer(x_prompt, meta_tokens, sinks, gmix, win, gq2, gk2, sp, ga, gs, wout):
    nseq, seq, _ = x_prompt.shape
    assert nseq == N_SEQ and seq % BLOCK == 0
    n_blocks = seq // BLOCK
    st = pl.BlockSpec((N_SEQ, STATE_COLS), lambda i: (0, 0))
    kvw = pl.BlockSpec((N_SEQ, BLOCK, KV_WIDTH), lambda i: (0, 0, 0))
    return pl.pallas_call(
        _mixer_kernel,
        grid=(n_blocks + 1,),
        in_specs=[pl.BlockSpec(memory_space=pltpu.SMEM),
                  pl.BlockSpec((N_SEQ, BLOCK, D_MODEL), lambda i: (0, jnp.minimum(i, n_blocks - 1), 0)),
                  _const_spec((N_META, D_MODEL)),
                  _const_spec((1, D_MODEL)), _const_spec((D_MODEL, IN_COLS)), _const_spec((1, LANES)),
                  _const_spec((1, LANES)),
                  _const_spec((N_QUARTERS, LANES, 2 * Q_STATE)), _const_spec((1, STATE_COLS)),
                  _const_spec((1, STATE_COLS)), _const_spec((N_QUARTERS, 2 * Q_STATE, LANES)),
                  _const_spec((1, SSM_WIDTH)), _const_spec((N_QUARTERS, LANES, LANES)), _const_spec((1, SSM_WIDTH)),
                  _const_spec((1, ATTN_WIDTH)), _const_spec((1, SSM_WIDTH)),
                  _const_spec((OUT_CHUNKS, D_MODEL, OUT_CHUNK))],
        out_specs=[pl.BlockSpec((OUT_CHUNKS, N_SEQ, BLOCK, OUT_CHUNK), lambda i: (0, 0, jnp.maximum(i - 1, 0), 0)),
                   kvw, kvw, st, st],
        out_shape=[jax.ShapeDtypeStruct((OUT_CHUNKS, N_SEQ, seq, OUT_CHUNK), F32),
                   jax.ShapeDtypeStruct((N_SEQ, BLOCK, KV_WIDTH), F32), jax.ShapeDtypeStruct((N_SEQ, BLOCK, KV_WIDTH), F32),
                   jax.ShapeDtypeStruct((N_SEQ, STATE_COLS), F32), jax.ShapeDtypeStruct((N_SEQ, STATE_COLS), F32)],
        scratch_shapes=[pltpu.VMEM((N_SEQ, 2 * Q_TILES * BLOCK, LANES), BF16),
                        pltpu.VMEM((N_SEQ, 2 * BLOCK, KV_WIDTH), BF16),
                        pltpu.VMEM((N_SEQ, 2 * BLOCK, KV_WIDTH), BF16),
                        pltpu.VMEM((2, 2 * Q_TILES * BLOCK, 2 * BLOCK), F32),
                        pltpu.VMEM((2, 2 * Q_TILES * BLOCK, 2 * BLOCK), BF16),
                        pltpu.VMEM((N_SEQ, BLOCK, ATTN_WIDTH), F32),
                        pltpu.VMEM((N_QUARTERS, N_SEQ * PITCH, LANES), F32),
                        pltpu.VMEM((2, CHUNK_T * N_SEQ, 2 * STATE_COLS), F32),
                        pltpu.VMEM((2, CHUNK_T * N_SEQ, 2 * STATE_COLS), BF16),
                        pltpu.VMEM((N_SEQ, 2 * STATE_COLS), F32),
                        pltpu.VMEM((N_QUARTERS, N_SEQ * PITCH, LANES), F32),
                        pltpu.VMEM((ROWS, D_MODEL), BF16)],
        compiler_params=pltpu.CompilerParams(dimension_semantics=("arbitrary",), vmem_limit_bytes=VMEM_LIMIT),
        name="mixer",
    )(sinks, x_prompt, meta_tokens, gmix, win, gq2, gk2, sp["wq"], sp["lr"], sp["li"], sp["cq"], sp["d"],
      sp["wgluq"], sp["bglu"], ga, gs, wout)


def kernel(x_prompt, x_sample, cache_k_win, cache_v_win, state_ssm_re, state_ssm_im, meta_tokens, g_mix, w_in, g_q,
           g_k, sinks, ssm_a_re, ssm_a_im, ssm_log_dt, ssm_b_re, ssm_b_im, ssm_c_re, ssm_c_im, ssm_d, ssm_w_glu,
           ssm_b_glu, g_att_out, g_ssm_out, w_out, g_ffn, w_gate, w_up, w_down):
    bp, seq, _ = x_prompt.shape
    db, dseq, _ = x_sample.shape
    li = 0
    gmix = g_mix[li].reshape(1, D_MODEL)
    win = w_in[li].astype(BF16)
    gq2 = jnp.tile(g_q[li], 2).reshape(1, LANES)
    gk2 = jnp.tile(g_k[li], 2).reshape(1, LANES)
    sk = sinks[li]
    sp = _ssm_params(ssm_a_re[li], ssm_a_im[li], ssm_log_dt[li], ssm_b_re[li], ssm_b_im[li], ssm_c_re[li],
                     ssm_c_im[li], ssm_d[li], ssm_w_glu[li], ssm_b_glu[li])
    ga = g_att_out[li].reshape(1, ATTN_WIDTH)
    gs = g_ssm_out[li].reshape(1, SSM_WIDTH)
    wout = w_out[li].astype(BF16)
    ffn_w = (g_ffn[li].reshape(1, D_MODEL), w_gate[li].astype(BF16), w_up[li].astype(BF16), w_down[li].astype(BF16))
    regroup = lambda a, axis: jnp.swapaxes(
        a.reshape(a.shape[:axis] + (2, Q_PER_KV, HEAD_DIM) + a.shape[axis + 1:]), axis, axis + 1).reshape(a.shape)
    win = jnp.concatenate([regroup(win[:, :ATTN_WIDTH], 1), win[:, ATTN_WIDTH:]], axis=1)
    wout = jnp.concatenate([regroup(wout[:ATTN_WIDTH], 0), wout[ATTN_WIDTH:]], axis=0)
    ga = regroup(ga, 1)

    wout_chunks = wout.reshape(D_MODEL, OUT_CHUNKS, OUT_CHUNK).transpose(1, 0, 2)
    hproj, kw_p, vw_p, hp_r, hp_i = _mixer(x_prompt, meta_tokens, sk, gmix, win, gq2, gk2, sp, ga, gs, wout_chunks)
    y_prompt = _ffn_call(x_prompt.reshape(bp * seq, D_MODEL), hproj.reshape(OUT_CHUNKS, bp * seq, OUT_CHUNK), *ffn_w,
                         tm=512).reshape(bp, seq, D_MODEL)

    n_s = db * dseq
    xs_rows = x_sample.reshape(n_s, D_MODEL)
    q_s, k_s, v_s, u_s = _front(xs_rows, gmix, win, gq2, gk2, tm=512)
    ck = cache_k_win[li].reshape(db, WINDOW, KV_WIDTH)
    cv = cache_v_win[li].reshape(db, WINDOW, KV_WIDTH)
    oa_s, kw_s, vw_s = _attn_sample(q_s.reshape(db, dseq, ATTN_WIDTH), k_s.reshape(db, dseq, KV_WIDTH),
                                    v_s.reshape(db, dseq, KV_WIDTH), ck, cv, sk)
    os_s, hs_r, hs_i = _ssm_sample(u_s, state_ssm_re[li].reshape(db, STATE_COLS),
                                   state_ssm_im[li].reshape(db, STATE_COLS), sp, nseq=db, t_steps=dseq)
    y_sample = _back(xs_rows, oa_s.reshape(n_s, ATTN_WIDTH), os_s, ga, gs, wout, *ffn_w, tm=512).reshape(db, dseq, D_MODEL)

    kv5 = lambda a, n: a.reshape(1, n, WINDOW, N_KV_HEADS, HEAD_DIM)
    st4 = lambda a, n: a.reshape(1, n, N_SSM_GROUPS, SSM_STATE)
    return (y_prompt, y_sample, kv5(kw_p, bp), kv5(vw_p, bp), st4(hp_r, bp), st4(hp_i, bp),
            kv5(kw_s, db), kv5(vw_s, db), st4(hs_r, db), st4(hs_i, db))
```

```python
import functools

import jax
import jax.numpy as jnp
from jax import lax
from jax.experimental import pallas as pl
from jax.experimental.pallas import tpu as pltpu

D_MODEL = 1024
N_META = 16
HEAD_DIM = 64
ATTN_WIDTH = 512
Q_PER_KV = 4
N_KV_HEADS = 2
KV_WIDTH = 128
WINDOW = 128
BLOCK = 128
SSM_WIDTH = 512
SSM_GROUP = 16
N_SSM_GROUPS = 32
SSM_STATE = 64
IN_COLS = ATTN_WIDTH + 2 * KV_WIDTH + SSM_WIDTH
D_FF = 2816
EPS = 1e-6
ATTN_SCALE = HEAD_DIM ** -0.5
STATE_COLS = N_SSM_GROUPS * SSM_STATE
LANES = 128
SUBLANES = 8
VMEM_LIMIT = 56 * 1024 * 1024

F32 = jnp.float32
BF16 = jnp.bfloat16


def _const_spec(shape):
    return pl.BlockSpec(shape, lambda *_: (0,) * len(shape), pipeline_mode=pl.Buffered(1))


def _rms(x, g):
    return x * lax.rsqrt(jnp.mean(x * x, axis=-1, keepdims=True) + EPS) * g


def _left_half():
    return lax.broadcasted_iota(jnp.int32, (1, LANES), 1) < HEAD_DIM


def _pair_norm(zz, g2, left):
    sq = zz * zz
    sl = jnp.sum(jnp.where(left, sq, 0.0), axis=-1, keepdims=True)
    sr = jnp.sum(jnp.where(left, 0.0, sq), axis=-1, keepdims=True)
    inv = jnp.where(left, lax.rsqrt(sl / HEAD_DIM + EPS), lax.rsqrt(sr / HEAD_DIM + EPS))
    return zz * inv * g2


def _front_kernel(x_ref, gmix_ref, win_ref, gq_ref, gk_ref, q_ref, k_ref, v_ref, u_ref):
    xn = _rms(x_ref[...], gmix_ref[...]).astype(BF16)
    z = jnp.dot(xn, win_ref[...], preferred_element_type=F32)
    left = _left_half()
    for p in range(ATTN_WIDTH // LANES):
        q_ref[:, p * LANES:(p + 1) * LANES] = _pair_norm(z[:, p * LANES:(p + 1) * LANES], gq_ref[...], left)
    k_ref[...] = _pair_norm(z[:, ATTN_WIDTH:ATTN_WIDTH + KV_WIDTH], gk_ref[...], left)
    v_ref[...] = z[:, ATTN_WIDTH + KV_WIDTH:ATTN_WIDTH + 2 * KV_WIDTH]
    u_ref[...] = z[:, ATTN_WIDTH + 2 * KV_WIDTH:]


def _front(x_rows, gmix, win_bf, gq2, gk2, tm):
    n = x_rows.shape[0]
    assert n % tm == 0
    row = lambda w: pl.BlockSpec((tm, w), lambda i: (i, 0))
    return pl.pallas_call(
        _front_kernel,
        grid=(n // tm,),
        in_specs=[row(D_MODEL), _const_spec((1, D_MODEL)), _const_spec((D_MODEL, IN_COLS)),
                  _const_spec((1, LANES)), _const_spec((1, LANES))],
        out_specs=[row(ATTN_WIDTH), row(KV_WIDTH), row(KV_WIDTH), row(SSM_WIDTH)],
        out_shape=[jax.ShapeDtypeStruct((n, ATTN_WIDTH), F32), jax.ShapeDtypeStruct((n, KV_WIDTH), F32),
                   jax.ShapeDtypeStruct((n, KV_WIDTH), F32), jax.ShapeDtypeStruct((n, SSM_WIDTH), F32)],
        compiler_params=pltpu.CompilerParams(dimension_semantics=("arbitrary",), vmem_limit_bytes=VMEM_LIMIT),
        name="front",
    )(x_rows, gmix, win_bf, gq2, gk2)


Q_TILES = ATTN_WIDTH // LANES


def _attn_sample_kernel(sink_ref, q_ref, kn_ref, vn_ref, ck_ref, cv_ref, o_ref, kw_ref, vw_ref, *, bb, t):
    left = _left_half()
    n_heads = 2 * Q_TILES
    rows = n_heads * t
    tk = WINDOW + t
    r = lax.broadcasted_iota(jnp.int32, (rows, tk), 0) % t
    c = lax.broadcasted_iota(jnp.int32, (rows, tk), 1)
    bias = jnp.where((c >= r) & (c <= r + WINDOW), 0.0, -jnp.inf)
    hrow = lax.broadcasted_iota(jnp.int32, (rows, 1), 0) // t
    sink = jnp.zeros((rows, 1), F32)
    for g in range(n_heads):
        sink = jnp.where(hrow == g, sink_ref[g // 2 + Q_PER_KV * (g % 2)], sink)
    scores, values = [], []
    for bi in range(bb):
        kk = jnp.concatenate([ck_ref[bi], kn_ref[bi]], axis=0)
        vv = jnp.concatenate([cv_ref[bi], vn_ref[bi]], axis=0)
        kw_ref[bi] = kk[t:]
        vw_ref[bi] = vv[t:]
        q = q_ref[bi] * ATTN_SCALE
        pieces = []
        for tile in range(Q_TILES):
            qt = q[:, tile * LANES:(tile + 1) * LANES]
            pieces += [jnp.where(left, qt, 0.0), jnp.where(left, 0.0, qt)]
        qs = jnp.concatenate(pieces, axis=0).astype(BF16)
        scores.append(lax.dot_general(qs, kk.astype(BF16), (((1,), (1,)), ((), ())), preferred_element_type=F32))
        values.append(vv.astype(BF16))
    probs = []
    for bi in range(bb):
        s = scores[bi] + bias
        m = jnp.maximum(jnp.max(s, axis=-1, keepdims=True), sink)
        p = jnp.exp(s - m)
        inv = 1.0 / (jnp.sum(p, axis=-1, keepdims=True) + jnp.exp(sink - m))
        probs.append((p * inv).astype(BF16))
    for bi in range(bb):
        o = jnp.dot(probs[bi], values[bi], preferred_element_type=F32)
        for tile in range(Q_TILES):
            o_ref[bi, :, tile * LANES:(tile + 1) * LANES] = jnp.where(
                left, o[2 * tile * t:(2 * tile + 1) * t], o[(2 * tile + 1) * t:(2 * tile + 2) * t])


def _attn_sample(q, kn, vn, ck, cv, sinks, bb=16):
    db, t, _ = q.shape
    blk = lambda r, w: pl.BlockSpec((bb, r, w), lambda i: (i, 0, 0))
    return pl.pallas_call(
        functools.partial(_attn_sample_kernel, bb=bb, t=t),
        grid=(db // bb,),
        in_specs=[pl.BlockSpec(memory_space=pltpu.SMEM), blk(t, ATTN_WIDTH), blk(t, KV_WIDTH), blk(t, KV_WIDTH),
                  blk(WINDOW, KV_WIDTH), blk(WINDOW, KV_WIDTH)],
        out_specs=[blk(t, ATTN_WIDTH), blk(WINDOW, KV_WIDTH), blk(WINDOW, KV_WIDTH)],
        out_shape=[jax.ShapeDtypeStruct((db, t, ATTN_WIDTH), F32),
                   jax.ShapeDtypeStruct((db, WINDOW, KV_WIDTH), F32),
                   jax.ShapeDtypeStruct((db, WINDOW, KV_WIDTH), F32)],
        compiler_params=pltpu.CompilerParams(dimension_semantics=("arbitrary",)),
        name="attn_sample",
    )(sinks, q, kn, vn, ck, cv)


N_QUARTERS = 4
Q_GROUPS = N_SSM_GROUPS // N_QUARTERS
Q_STATE = Q_GROUPS * SSM_STATE


def _ssm_params(a_re, a_im, log_dt, b_re, b_im, c_re, c_im, d_skip, w_glu, b_glu):
    dt = jnp.exp(log_dt)[:, None]
    mag = jnp.exp(a_re * dt)
    lr, li = mag * jnp.cos(a_im * dt), mag * jnp.sin(a_im * dt)
    nr, ni = lr - 1.0, li
    den = a_re * a_re + a_im * a_im
    fr, fi = (nr * a_re + ni * a_im) / den, (ni * a_re - nr * a_im) / den
    bbr = fr[..., None] * b_re - fi[..., None] * b_im
    bbi = fr[..., None] * b_im + fi[..., None] * b_re
    eye = jnp.eye(Q_GROUPS, dtype=F32)
    quartered = lambda a: a.reshape((a.shape[0], N_QUARTERS, Q_GROUPS) + a.shape[2:])
    wq = jnp.einsum('rqgph,gm->qghrmp', quartered(jnp.stack([bbr, bbi])), eye).reshape(N_QUARTERS, LANES, 2 * Q_STATE)
    cq = jnp.einsum('rqghp,gm->qrgpmh', quartered(jnp.stack([c_re, -c_im])), eye).reshape(N_QUARTERS, 2 * Q_STATE, LANES)
    wglu = jnp.einsum('qghk,gm->qghmk', w_glu.reshape(N_QUARTERS, Q_GROUPS, SSM_GROUP, SSM_GROUP), eye)
    return dict(wq=wq.astype(BF16), cq=cq.astype(BF16), wgluq=wglu.reshape(N_QUARTERS, LANES, LANES).astype(BF16),
                lr=lr.reshape(1, STATE_COLS), li=li.reshape(1, STATE_COLS), d=d_skip.reshape(1, SSM_WIDTH),
                bglu=b_glu.reshape(1, SSM_WIDTH))


def _scan_step(lr, li, hr, hi, xr, xi):
    return lr * hr - li * hi + xr, lr * hi + li * hr + xi


def _ssm_sample_kernel(u0_ref, u1_ref, u2_ref, u3_ref, h0r_ref, h0i_ref, wq_ref, lr_ref, li_ref, cq_ref, d_ref,
                       wglu_ref, bglu_ref, o_ref, hr_ref, hi_ref, utb_ref, xs_ref, *, nseq, t_steps):
    u_refs = (u0_ref, u1_ref, u2_ref, u3_ref)
    quarters = range(N_QUARTERS)
    n_rg = nseq // SUBLANES
    rg_rows = SUBLANES * t_steps

    for t in range(t_steps):
        for rg in range(n_rg):
            for j in quarters:
                utb_ref[t * nseq + rg * SUBLANES:t * nseq + (rg + 1) * SUBLANES, j * LANES:(j + 1) * LANES] = (
                    u_refs[j][pl.ds(rg * rg_rows + t, SUBLANES, stride=t_steps), :])

    u = [utb_ref[:, q * LANES:(q + 1) * LANES] for q in quarters]
    for q in quarters:
        xs_ref[q] = jnp.dot(u[q].astype(BF16), wq_ref[q], preferred_element_type=F32)
    lam = [(jnp.broadcast_to(lr_ref[:, q * Q_STATE:(q + 1) * Q_STATE], (SUBLANES, Q_STATE)),
            jnp.broadcast_to(li_ref[:, q * Q_STATE:(q + 1) * Q_STATE], (SUBLANES, Q_STATE))) for q in quarters]

    def scan_group(rg, carry):
        r0 = pl.multiple_of(rg * SUBLANES, SUBLANES)
        h = [(h0r_ref[pl.ds(r0, SUBLANES), q * Q_STATE:(q + 1) * Q_STATE],
              h0i_ref[pl.ds(r0, SUBLANES), q * Q_STATE:(q + 1) * Q_STATE]) for q in quarters]
        for t in range(t_steps):
            rows = pl.ds(pl.multiple_of(t * nseq + r0, SUBLANES), SUBLANES)
            for q in quarters:
                h[q] = _scan_step(*lam[q], *h[q], xs_ref[q, rows, 0:Q_STATE], xs_ref[q, rows, Q_STATE:])
                xs_ref[q, rows, 0:Q_STATE] = h[q][0]
                xs_ref[q, rows, Q_STATE:] = h[q][1]
        for q in quarters:
            hr_ref[pl.ds(r0, SUBLANES), q * Q_STATE:(q + 1) * Q_STATE] = h[q][0]
            hi_ref[pl.ds(r0, SUBLANES), q * Q_STATE:(q + 1) * Q_STATE] = h[q][1]
        return carry

    lax.fori_loop(0, n_rg, scan_group, 0)
    y = [jnp.dot(xs_ref[q].astype(BF16), cq_ref[q], preferred_element_type=F32) for q in quarters]
    g = [jax.nn.gelu(y[q] + d_ref[:, q * LANES:(q + 1) * LANES] * u[q]) for q in quarters]
    gate = [jnp.dot(g[q].astype(BF16), wglu_ref[q], preferred_element_type=F32) + bglu_ref[:, q * LANES:(q + 1) * LANES]
            for q in quarters]
    for q in quarters:
        o = g[q] * jax.nn.sigmoid(gate[q])
        for t in range(t_steps):
            for rg in range(n_rg):
                o_ref[q, pl.ds(rg * rg_rows + t, SUBLANES, stride=t_steps), :] = (
                    o[t * nseq + rg * SUBLANES:t * nseq + (rg + 1) * SUBLANES, :])


def _ssm_sample(u_rows, h0r, h0i, sp, nseq, t_steps):
    n = nseq * t_steps
    slab = lambda j: pl.BlockSpec((n, LANES), lambda i, j=j: (0, j))
    st = pl.BlockSpec((nseq, STATE_COLS), lambda i: (0, 0))
    return pl.pallas_call(
        functools.partial(_ssm_sample_kernel, nseq=nseq, t_steps=t_steps),
        grid=(1,),
        in_specs=[slab(0), slab(1), slab(2), slab(3), st, st,
                  _const_spec((N_QUARTERS, LANES, 2 * Q_STATE)), _const_spec((1, STATE_COLS)),
                  _const_spec((1, STATE_COLS)), _const_spec((N_QUARTERS, 2 * Q_STATE, LANES)),
                  _const_spec((1, SSM_WIDTH)), _const_spec((N_QUARTERS, LANES, LANES)), _const_spec((1, SSM_WIDTH))],
        out_specs=[pl.BlockSpec((N_QUARTERS, n, LANES), lambda i: (0, 0, 0)), st, st],
        out_shape=[jax.ShapeDtypeStruct((N_QUARTERS, n, LANES), F32),
                   jax.ShapeDtypeStruct((nseq, STATE_COLS), F32), jax.ShapeDtypeStruct((nseq, STATE_COLS), F32)],
        scratch_shapes=[pltpu.VMEM((n, SSM_WIDTH), F32), pltpu.VMEM((N_QUARTERS, n, 2 * Q_STATE), F32)],
        compiler_params=pltpu.CompilerParams(dimension_semantics=("arbitrary",), vmem_limit_bytes=VMEM_LIMIT),
        name="ssm_sample",
    )(u_rows, u_rows, u_rows, u_rows, h0r, h0i, sp["wq"], sp["lr"], sp["li"], sp["cq"], sp["d"], sp["wgluq"], sp["bglu"])


def _ffn(h, gf_ref, wg_ref, wu_ref, wd_ref):
    f = _rms(h, gf_ref[...]).astype(BF16)
    gate = jnp.dot(f, wg_ref[...], preferred_element_type=F32)
    up = jnp.dot(f, wu_ref[...], preferred_element_type=F32)
    a = (jax.nn.silu(gate) * up).astype(BF16)
    return h + jnp.dot(a, wd_ref[...], preferred_element_type=F32)


def _back_kernel(x_ref, oa_ref, os_ref, ga_ref, gs_ref, wout_ref, gf_ref, wg_ref, wu_ref, wd_ref, y_ref):
    o_ssm = jnp.concatenate([os_ref[j] for j in range(N_QUARTERS)], axis=-1)
    mix = jnp.concatenate([_rms(oa_ref[...], ga_ref[...]), _rms(o_ssm, gs_ref[...])], axis=-1).astype(BF16)
    h = x_ref[...] + jnp.dot(mix, wout_ref[...], preferred_element_type=F32)
    y_ref[...] = _ffn(h, gf_ref, wg_ref, wu_ref, wd_ref)


def _ffn_kernel(x_ref, hp_ref, gf_ref, wg_ref, wu_ref, wd_ref, y_ref):
    hp = jnp.concatenate([hp_ref[c] for c in range(hp_ref.shape[0])], axis=-1)
    y_ref[...] = _ffn(x_ref[...] + hp, gf_ref, wg_ref, wu_ref, wd_ref)


def _back(x_rows, oa, os_, ga, gs, wout, gf, wg, wu, wd, tm):
    n = x_rows.shape[0]
    assert n % tm == 0
    row = lambda w: pl.BlockSpec((tm, w), lambda i: (i, 0))
    return pl.pallas_call(
        _back_kernel,
        grid=(n // tm,),
        in_specs=[row(D_MODEL), row(ATTN_WIDTH), pl.BlockSpec((N_QUARTERS, tm, LANES), lambda i: (0, i, 0)),
                  _const_spec((1, ATTN_WIDTH)), _const_spec((1, SSM_WIDTH)), _const_spec((D_MODEL, D_MODEL)),
                  _const_spec((1, D_MODEL)), _const_spec((D_MODEL, D_FF)), _const_spec((D_MODEL, D_FF)),
                  _const_spec((D_FF, D_MODEL))],
        out_specs=row(D_MODEL),
        out_shape=jax.ShapeDtypeStruct((n, D_MODEL), F32),
        compiler_params=pltpu.CompilerParams(dimension_semantics=("arbitrary",), vmem_limit_bytes=VMEM_LIMIT),
        name="back",
    )(x_rows, oa, os_, ga, gs, wout, gf, wg, wu, wd)


def _ffn_call(x_rows, hp_rows, gf, wg, wu, wd, tm):
    n = x_rows.shape[0]
    assert n % tm == 0
    row = lambda w: pl.BlockSpec((tm, w), lambda i: (i, 0))
    return pl.pallas_call(
        _ffn_kernel,
        grid=(n // tm,),
        in_specs=[row(D_MODEL), pl.BlockSpec((hp_rows.shape[0], tm, hp_rows.shape[2]), lambda i: (0, i, 0)),
                  _const_spec((1, D_MODEL)),
                  _const_spec((D_MODEL, D_FF)), _const_spec((D_MODEL, D_FF)), _const_spec((D_FF, D_MODEL))],
        out_specs=row(D_MODEL),
        out_shape=jax.ShapeDtypeStruct((n, D_MODEL), F32),
        compiler_params=pltpu.CompilerParams(dimension_semantics=("arbitrary",), vmem_limit_bytes=VMEM_LIMIT),
        name="ffn",
    )(x_rows, hp_rows, gf, wg, wu, wd)


N_SEQ = 8
ROWS = N_SEQ * BLOCK
PITCH = BLOCK + SUBLANES
CHUNK_T = BLOCK // N_SEQ


def _interleave(*stages):
    keyed = [((i + 0.5) / len(steps), k, i, step) for k, steps in enumerate(stages) for i, step in enumerate(steps)]
    return [step for _, _, _, step in sorted(keyed, key=lambda e: e[:3])]


def _mixer_kernel(sink_ref, x_ref, meta_ref, gmix_ref, win_ref, gq_ref, gk_ref,
                  wq_ref, lr_ref, li_ref, cq_ref, d_ref, wglu_ref, bglu_ref, ga_ref, gs_ref, wout_ref,
                  hp_ref, kw_ref, vw_ref, hr_ref, hi_ref,
                  qs_ref, kk_ref, vv_ref, s_ref, p_ref, oatt_ref, usl_ref, xs_ref, hb_ref, st_ref, osl_ref, mix_ref):
    i = pl.program_id(0)
    n_blocks = pl.num_programs(0) - 1

    @pl.when(i < n_blocks)
    def _():
        _mixer_block(i, sink_ref, x_ref, meta_ref, gmix_ref, win_ref, gq_ref, gk_ref,
                     wq_ref, lr_ref, li_ref, cq_ref, d_ref, wglu_ref, bglu_ref, ga_ref, gs_ref, wout_ref,
                     hp_ref, kw_ref, vw_ref, hr_ref, hi_ref,
                     qs_ref, kk_ref, vv_ref, s_ref, p_ref, oatt_ref, usl_ref, xs_ref, hb_ref, st_ref, osl_ref, mix_ref)

    @pl.when(i == n_blocks)
    def _():
        for step in _output_norm_steps(oatt_ref, osl_ref, ga_ref, gs_ref, mix_ref):
            step()
        for c in range(OUT_CHUNKS):
            _project_chunk(mix_ref, wout_ref, hp_ref, c)


OUT_CHUNKS = 4
OUT_CHUNK = D_MODEL // OUT_CHUNKS


def _project_chunk(mix_ref, wout_ref, hp_ref, c):
    hp_ref[c] = jnp.dot(mix_ref[...], wout_ref[c], preferred_element_type=F32).reshape(N_SEQ, BLOCK, OUT_CHUNK)


def _output_norm_steps(oatt_ref, osl_ref, ga_ref, gs_ref, mix_ref):
    def one(b):
        rows = slice(b * BLOCK, (b + 1) * BLOCK)
        mix_ref[rows, 0:ATTN_WIDTH] = _rms(oatt_ref[b], ga_ref[...]).astype(BF16)
        sl = [osl_ref[j, b * PITCH:b * PITCH + BLOCK, :] for j in range(N_QUARTERS)]
        ms = sum(jnp.sum(s * s, axis=-1, keepdims=True) for s in sl) / SSM_WIDTH
        inv = lax.rsqrt(ms + EPS)
        for j in range(N_QUARTERS):
            mix_ref[rows, ATTN_WIDTH + j * LANES:ATTN_WIDTH + (j + 1) * LANES] = (
                sl[j] * inv * gs_ref[:, j * LANES:(j + 1) * LANES]).astype(BF16)
    return [functools.partial(one, b) for b in range(N_SEQ)]


def _mixer_block(i, sink_ref, x_ref, meta_ref, gmix_ref, win_ref, gq_ref, gk_ref,
                 wq_ref, lr_ref, li_ref, cq_ref, d_ref, wglu_ref, bglu_ref, ga_ref, gs_ref, wout_ref,
                 hp_ref, kw_ref, vw_ref, hr_ref, hi_ref,
                 qs_ref, kk_ref, vv_ref, s_ref, p_ref, oatt_ref, usl_ref, xs_ref, hb_ref, st_ref, osl_ref, mix_ref):
    left = _left_half()
    project = lambda c: [functools.partial(_project_chunk, mix_ref, wout_ref, hp_ref, c)]

    @pl.when(i == 0)
    def _():
        oatt_ref[...] = jnp.zeros((N_SEQ, BLOCK, ATTN_WIDTH), F32)
        osl_ref[...] = jnp.zeros((N_QUARTERS, N_SEQ * PITCH, LANES), F32)
        xm = _rms(meta_ref[...], gmix_ref[...]).astype(BF16)
        zm = jnp.dot(xm, win_ref[:, ATTN_WIDTH:], preferred_element_type=F32)
        km = _pair_norm(zm[:, :KV_WIDTH], gk_ref[...], left).astype(BF16)
        vm = zm[:, KV_WIDTH:2 * KV_WIDTH].astype(BF16)
        lead = jnp.zeros((BLOCK - N_META, KV_WIDTH), BF16)
        for b in range(N_SEQ):
            kk_ref[b, 0:BLOCK - N_META, :] = lead
            vv_ref[b, 0:BLOCK - N_META, :] = lead
            kk_ref[b, BLOCK - N_META:BLOCK, :] = km
            vv_ref[b, BLOCK - N_META:BLOCK, :] = vm
        um = zm[:, 2 * KV_WIDTH:]
        for q in range(N_QUARTERS):
            xm_q = jnp.dot(um[:, q * LANES:(q + 1) * LANES].astype(BF16), wq_ref[q], preferred_element_type=F32)
            lr = jnp.broadcast_to(lr_ref[:, q * Q_STATE:(q + 1) * Q_STATE], (SUBLANES, Q_STATE))
            li = jnp.broadcast_to(li_ref[:, q * Q_STATE:(q + 1) * Q_STATE], (SUBLANES, Q_STATE))
            hr = jnp.zeros((SUBLANES, Q_STATE), F32)
            hi = jnp.zeros((SUBLANES, Q_STATE), F32)
            for t in range(N_META):
                xr = jnp.broadcast_to(xm_q[t:t + 1, 0:Q_STATE], (SUBLANES, Q_STATE))
                xi = jnp.broadcast_to(xm_q[t:t + 1, Q_STATE:], (SUBLANES, Q_STATE))
                hr, hi = _scan_step(lr, li, hr, hi, xr, xi)
            st_ref[:, 2 * q * Q_STATE:(2 * q + 1) * Q_STATE] = hr
            st_ref[:, (2 * q + 1) * Q_STATE:(2 * q + 2) * Q_STATE] = hi

    p1 = {}

    def pre_norm():
        p1["xn"] = _rms(x_ref[...].reshape(ROWS, D_MODEL), gmix_ref[...]).astype(BF16)

    def project_in(key, c0):
        p1[key] = jnp.dot(p1["xn"], win_ref[:, c0:c0 + 2 * LANES], preferred_element_type=F32)

    def finish_q(c):
        zz = p1.pop(("q", c))
        for tt in range(2):
            t = 2 * c + tt
            qn = _pair_norm(zz[:, tt * LANES:(tt + 1) * LANES], gq_ref[...], left) * ATTN_SCALE
            qa = jnp.where(left, qn, 0.0).astype(BF16)
            qb = jnp.where(left, 0.0, qn).astype(BF16)
            for b in range(N_SEQ):
                qs_ref[b, 2 * t * BLOCK:(2 * t + 1) * BLOCK, :] = qa[b * BLOCK:(b + 1) * BLOCK]
                qs_ref[b, (2 * t + 1) * BLOCK:(2 * t + 2) * BLOCK, :] = qb[b * BLOCK:(b + 1) * BLOCK]

    def finish_kv():
        zz = p1.pop("kv")
        kn = _pair_norm(zz[:, :LANES], gk_ref[...], left)
        vn = zz[:, LANES:]
        kw_ref[...] = kn.reshape(N_SEQ, BLOCK, KV_WIDTH)
        vw_ref[...] = vn.reshape(N_SEQ, BLOCK, KV_WIDTH)
        kk_ref[:, BLOCK:, :] = kn.astype(BF16).reshape(N_SEQ, BLOCK, KV_WIDTH)
        vv_ref[:, BLOCK:, :] = vn.astype(BF16).reshape(N_SEQ, BLOCK, KV_WIDTH)

    def finish_u(c):
        zz = p1.pop(("u", c))
        for tt in range(2):
            for b in range(N_SEQ):
                usl_ref[2 * c + tt, b * PITCH:b * PITCH + BLOCK, :] = zz[b * BLOCK:(b + 1) * BLOCK, tt * LANES:(tt + 1) * LANES]

    dot_kv = functools.partial(project_in, "kv", ATTN_WIDTH)
    dot_q = [functools.partial(project_in, ("q", c), c * 2 * LANES) for c in range(2)]
    dot_u = [functools.partial(project_in, ("u", c), ATTN_WIDTH + 2 * KV_WIDTH + c * 2 * LANES) for c in range(2)]
    post_q = [functools.partial(finish_q, c) for c in range(2)]
    post_u = [functools.partial(finish_u, c) for c in range(2)]

    r = lax.broadcasted_iota(jnp.int32, (BLOCK, 2 * BLOCK), 0)
    c = lax.broadcasted_iota(jnp.int32, (BLOCK, 2 * BLOCK), 1)
    valid = (c >= r) & (c <= r + WINDOW) & ((i > 0) | (c >= BLOCK - N_META))
    bias = jnp.where(valid, 0.0, -jnp.inf)

    def gather_u(n):
        t0 = n * CHUNK_T
        return jnp.concatenate(
            [jnp.concatenate([usl_ref[j, pl.ds(t0 + tl, SUBLANES, stride=PITCH), :] for j in range(N_QUARTERS)], axis=1)
             for tl in range(CHUNK_T)], axis=0)

    def feed(n, par):
        env = {}

        def scores():
            s_ref[par] = lax.dot_general(qs_ref[n], kk_ref[n], (((1,), (1,)), ((), ())), preferred_element_type=F32)

        def gather():
            env["u"] = gather_u(n)

        def scan_inputs(q):
            xs_ref[par, :, 2 * q * Q_STATE:(2 * q + 2) * Q_STATE] = jnp.dot(
                env["u"][:, q * LANES:(q + 1) * LANES].astype(BF16), wq_ref[q], preferred_element_type=F32)

        return [scores, gather] + [functools.partial(scan_inputs, q) for q in range(N_QUARTERS)]

    def mid(n, par):
        env = {}

        def row_max(g):
            s = s_ref[par, g * BLOCK:(g + 1) * BLOCK, :] + bias
            env[g] = (s, jnp.maximum(jnp.max(s, axis=-1, keepdims=True), sink_ref[g // 2 + Q_PER_KV * (g % 2)]))

        def exponent(g):
            s, m = env[g]
            p = jnp.exp(s - m)
            sink = sink_ref[g // 2 + Q_PER_KV * (g % 2)]
            env[g] = (p, 1.0 / (jnp.sum(p, axis=-1, keepdims=True) + jnp.exp(sink - m)))

        def normalise(g):
            p, inv = env.pop(g)
            p_ref[par, g * BLOCK:(g + 1) * BLOCK, :] = (p * inv).astype(BF16)

        softmax = [functools.partial(f, g) for g in range(2 * Q_TILES) for f in (row_max, exponent, normalise)]

        def load_state():
            for q in range(N_QUARTERS):
                env["lam", q] = (jnp.broadcast_to(lr_ref[:, q * Q_STATE:(q + 1) * Q_STATE], (SUBLANES, Q_STATE)),
                                 jnp.broadcast_to(li_ref[:, q * Q_STATE:(q + 1) * Q_STATE], (SUBLANES, Q_STATE)))
                env["h", q] = (st_ref[:, 2 * q * Q_STATE:(2 * q + 1) * Q_STATE],
                               st_ref[:, (2 * q + 1) * Q_STATE:(2 * q + 2) * Q_STATE])

        def scan_pair(tp, q):
            lr, li = env["lam", q]
            hr, hi = env["h", q]
            re0, im0 = 2 * q * Q_STATE, (2 * q + 1) * Q_STATE
            pair_r, pair_i = [], []
            for tl in (2 * tp, 2 * tp + 1):
                xr = xs_ref[par, tl * SUBLANES:(tl + 1) * SUBLANES, re0:re0 + Q_STATE]
                xi = xs_ref[par, tl * SUBLANES:(tl + 1) * SUBLANES, im0:im0 + Q_STATE]
                hr, hi = _scan_step(lr, li, hr, hi, xr, xi)
                pair_r.append(hr)
                pair_i.append(hi)
            env["h", q] = (hr, hi)
            rows = slice(2 * tp * SUBLANES, (2 * tp + 2) * SUBLANES)
            hb_ref[par, rows, re0:re0 + Q_STATE] = jnp.concatenate(pair_r, axis=0).astype(BF16)
            hb_ref[par, rows, im0:im0 + Q_STATE] = jnp.concatenate(pair_i, axis=0).astype(BF16)

        def store_state():
            for q in range(N_QUARTERS):
                hr, hi = env["h", q]
                st_ref[:, 2 * q * Q_STATE:(2 * q + 1) * Q_STATE] = hr
                st_ref[:, (2 * q + 1) * Q_STATE:(2 * q + 2) * Q_STATE] = hi

        scan = ([load_state] + [functools.partial(scan_pair, tp, q) for tp in range(CHUNK_T // 2) for q in range(N_QUARTERS)]
                + [store_state])
        return softmax, scan

    def tail(n, par):
        env = {}

        def attn_out():
            o_all = jnp.dot(p_ref[par], vv_ref[n], preferred_element_type=F32)
            for t in range(Q_TILES):
                oatt_ref[n, :, t * LANES:(t + 1) * LANES] = jnp.where(
                    left, o_all[2 * t * BLOCK:(2 * t + 1) * BLOCK], o_all[(2 * t + 1) * BLOCK:(2 * t + 2) * BLOCK])

        def gather():
            env["u"] = gather_u(n)

        def readout(q):
            env["y", q] = jnp.dot(hb_ref[par, :, 2 * q * Q_STATE:(2 * q + 2) * Q_STATE], cq_ref[q],
                                  preferred_element_type=F32)

        def activate(q):
            cols = slice(q * LANES, (q + 1) * LANES)
            env["g", q] = jax.nn.gelu(env.pop(("y", q)) + d_ref[:, cols] * env["u"][:, cols])

        def gate(q):
            cols = slice(q * LANES, (q + 1) * LANES)
            env["gate", q] = jnp.dot(env["g", q].astype(BF16), wglu_ref[q], preferred_element_type=F32) + bglu_ref[:, cols]

        def emit(q):
            o = env.pop(("g", q)) * jax.nn.sigmoid(env.pop(("gate", q)))
            t0 = n * CHUNK_T
            for tl in range(CHUNK_T):
                osl_ref[q, pl.ds(t0 + tl, SUBLANES, stride=PITCH), :] = o[tl * SUBLANES:(tl + 1) * SUBLANES]

        per_quarter = lambda f: [functools.partial(f, q) for q in range(N_QUARTERS)]
        return [attn_out, gather] + per_quarter(readout) + per_quarter(activate) + per_quarter(gate) + per_quarter(emit)

    def run(*stages):
        for step in _interleave(*stages):
            step()

    def steady(k, carry):
        n = 2 * k + 1
        run(feed(n + 1, 0), *mid(n, 1), tail(n - 1, 0), project(k))
        run(feed(n + 2, 1), *mid(n + 1, 0), tail(n, 1))
        return carry

    norms = _output_norm_steps(oatt_ref, osl_ref, ga_ref, gs_ref, mix_ref)
    feed0, feed1 = feed(0, 0), feed(1, 1)
    softmax0, scan0 = mid(0, 0)
    half = len(softmax0) // 2
    run([pre_norm, dot_kv, dot_q[0], finish_kv, dot_q[1], post_q[0], dot_u[0], post_q[1]], norms)
    run(feed0[:1])
    run([dot_u[1]] + feed1[:1] + post_u, softmax0[:half])
    run(feed0[1:], softmax0[half:])
    run(feed1[1:], scan0)
    n_steady = (N_SEQ - 2) // 2
    assert n_steady == OUT_CHUNKS - 1
    lax.fori_loop(0, n_steady, steady, 0)
    run(*mid(N_SEQ - 1, 1), tail(N_SEQ - 2, 0), project(OUT_CHUNKS - 1))
    run(tail(N_SEQ - 1, 1))
    kk_ref[:, 0:BLOCK, :] = kk_ref[:, BLOCK:, :]
    vv_ref[:, 0:BLOCK, :] = vv_ref[:, BLOCK:, :]

    hr_ref[...] = jnp.concatenate([st_ref[:, 2 * q * Q_STATE:(2 * q + 1) * Q_STATE] for q in range(N_QUARTERS)], axis=1)
    hi_ref[...] = jnp.concatenate([st_ref[:, (2 * q + 1) * Q_STATE:(2 * q + 2) * Q_STATE] for q in range(N_QUARTERS)], axis=1)


def _mixer(x_prompt, meta_tokens, sinks, gmix, win, gq2, gk2, sp, ga, gs, wout):
    nseq, seq, _ = x_prompt.shape
    assert nseq == N_SEQ and seq % BLOCK == 0
    n_blocks = seq // BLOCK
    st = pl.BlockSpec((N_SEQ, STATE_COLS), lambda i: (0, 0))
    kvw = pl.BlockSpec((N_SEQ, BLOCK, KV_WIDTH), lambda i: (0, 0, 0))
    return pl.pallas_call(
        _mixer_kernel,
        grid=(n_blocks + 1,),
        in_specs=[pl.BlockSpec(memory_space=pltpu.SMEM),
                  pl.BlockSpec((N_SEQ, BLOCK, D_MODEL), lambda i: (0, jnp.minimum(i, n_blocks - 1), 0)),
                  _const_spec((N_META, D_MODEL)),
                  _const_spec((1, D_MODEL)), _const_spec((D_MODEL, IN_COLS)), _const_spec((1, LANES)),
                  _const_spec((1, LANES)),
                  _const_spec((N_QUARTERS, LANES, 2 * Q_STATE)), _const_spec((1, STATE_COLS)),
                  _const_spec((1, STATE_COLS)), _const_spec((N_QUARTERS, 2 * Q_STATE, LANES)),
                  _const_spec((1, SSM_WIDTH)), _const_spec((N_QUARTERS, LANES, LANES)), _const_spec((1, SSM_WIDTH)),
                  _const_spec((1, ATTN_WIDTH)), _const_spec((1, SSM_WIDTH)),
                  _const_spec((OUT_CHUNKS, D_MODEL, OUT_CHUNK))],
        out_specs=[pl.BlockSpec((OUT_CHUNKS, N_SEQ, BLOCK, OUT_CHUNK), lambda i: (0, 0, jnp.maximum(i - 1, 0), 0)),
                   kvw, kvw, st, st],
        out_shape=[jax.ShapeDtypeStruct((OUT_CHUNKS, N_SEQ, seq, OUT_CHUNK), F32),
                   jax.ShapeDtypeStruct((N_SEQ, BLOCK, KV_WIDTH), F32), jax.ShapeDtypeStruct((N_SEQ, BLOCK, KV_WIDTH), F32),
                   jax.ShapeDtypeStruct((N_SEQ, STATE_COLS), F32), jax.ShapeDtypeStruct((N_SEQ, STATE_COLS), F32)],
        scratch_shapes=[pltpu.VMEM((N_SEQ, 2 * Q_TILES * BLOCK, LANES), BF16),
                        pltpu.VMEM((N_SEQ, 2 * BLOCK, KV_WIDTH), BF16),
                        pltpu.VMEM((N_SEQ, 2 * BLOCK, KV_WIDTH), BF16),
                        pltpu.VMEM((2, 2 * Q_TILES * BLOCK, 2 * BLOCK), F32),
                        pltpu.VMEM((2, 2 * Q_TILES * BLOCK, 2 * BLOCK), BF16),
                        pltpu.VMEM((N_SEQ, BLOCK, ATTN_WIDTH), F32),
                        pltpu.VMEM((N_QUARTERS, N_SEQ * PITCH, LANES), F32),
                        pltpu.VMEM((2, CHUNK_T * N_SEQ, 2 * STATE_COLS), F32),
                        pltpu.VMEM((2, CHUNK_T * N_SEQ, 2 * STATE_COLS), BF16),
                        pltpu.VMEM((N_SEQ, 2 * STATE_COLS), F32),
                        pltpu.VMEM((N_QUARTERS, N_SEQ * PITCH, LANES), F32),
                        pltpu.VMEM((ROWS, D_MODEL), BF16)],
        compiler_params=pltpu.CompilerParams(dimension_semantics=("arbitrary",), vmem_limit_bytes=VMEM_LIMIT),
        name="mixer",
    )(sinks, x_prompt, meta_tokens, gmix, win, gq2, gk2, sp["wq"], sp["lr"], sp["li"], sp["cq"], sp["d"],
      sp["wgluq"], sp["bglu"], ga, gs, wout)


def kernel(x_prompt, x_sample, cache_k_win, cache_v_win, state_ssm_re, state_ssm_im, meta_tokens, g_mix, w_in, g_q,
           g_k, sinks, ssm_a_re, ssm_a_im, ssm_log_dt, ssm_b_re, ssm_b_im, ssm_c_re, ssm_c_im, ssm_d, ssm_w_glu,
           ssm_b_glu, g_att_out, g_ssm_out, w_out, g_ffn, w_gate, w_up, w_down):
    bp, seq, _ = x_prompt.shape
    db, dseq, _ = x_sample.shape
    li = 0
    gmix = g_mix[li].reshape(1, D_MODEL)
    win = w_in[li].astype(BF16)
    gq2 = jnp.tile(g_q[li], 2).reshape(1, LANES)
    gk2 = jnp.tile(g_k[li], 2).reshape(1, LANES)
    sk = sinks[li]
    sp = _ssm_params(ssm_a_re[li], ssm_a_im[li], ssm_log_dt[li], ssm_b_re[li], ssm_b_im[li], ssm_c_re[li],
                     ssm_c_im[li], ssm_d[li], ssm_w_glu[li], ssm_b_glu[li])
    ga = g_att_out[li].reshape(1, ATTN_WIDTH)
    gs = g_ssm_out[li].reshape(1, SSM_WIDTH)
    wout = w_out[li].astype(BF16)
    ffn_w = (g_ffn[li].reshape(1, D_MODEL), w_gate[li].astype(BF16), w_up[li].astype(BF16), w_down[li].astype(BF16))
    regroup = lambda a, axis: jnp.swapaxes(
        a.reshape(a.shape[:axis] + (2, Q_PER_KV, HEAD_DIM) + a.shape[axis + 1:]), axis, axis + 1).reshape(a.shape)
    win = jnp.concatenate([regroup(win[:, :ATTN_WIDTH], 1), win[:, ATTN_WIDTH:]], axis=1)
    wout = jnp.concatenate([regroup(wout[:ATTN_WIDTH], 0), wout[ATTN_WIDTH:]], axis=0)
    ga = regroup(ga, 1)

    wout_chunks = wout.reshape(D_MODEL, OUT_CHUNKS, OUT_CHUNK).transpose(1, 0, 2)
    hproj, kw_p, vw_p, hp_r, hp_i = _mixer(x_prompt, meta_tokens, sk, gmix, win, gq2, gk2, sp, ga, gs, wout_chunks)
    y_prompt = _ffn_call(x_prompt.reshape(bp * seq, D_MODEL), hproj.reshape(OUT_CHUNKS, bp * seq, OUT_CHUNK), *ffn_w,
                         tm=512).reshape(bp, seq, D_MODEL)

    n_s = db * dseq
    xs_rows = x_sample.reshape(n_s, D_MODEL)
    q_s, k_s, v_s, u_s = _front(xs_rows, gmix, win, gq2, gk2, tm=512)
    ck = cache_k_win[li].reshape(db, WINDOW, KV_WIDTH)
    cv = cache_v_win[li].reshape(db, WINDOW, KV_WIDTH)
    oa_s, kw_s, vw_s = _attn_sample(q_s.reshape(db, dseq, ATTN_WIDTH), k_s.reshape(db, dseq, KV_WIDTH),
                                    v_s.reshape(db, dseq, KV_WIDTH), ck, cv, sk)
    os_s, hs_r, hs_i = _ssm_sample(u_s, state_ssm_re[li].reshape(db, STATE_COLS),
                                   state_ssm_im[li].reshape(db, STATE_COLS), sp, nseq=db, t_steps=dseq)
    y_sample = _back(xs_rows, oa_s.reshape(n_s, ATTN_WIDTH), os_s, ga, gs, wout, *ffn_w, tm=512).reshape(db, dseq, D_MODEL)

    kv5 = lambda a, n: a.reshape(1, n, WINDOW, N_KV_HEADS, HEAD_DIM)
    st4 = lambda a, n: a.reshape(1, n, N_SSM_GROUPS, SSM_STATE)
    return (y_prompt, y_sample, kv5(kw_p, bp), kv5(vw_p, bp), st4(hp_r, bp), st4(hp_i, bp),
            kv5(kw_s, db), kv5(vw_s, db), st4(hs_r, db), st4(hs_i, db))
```

```python
import functools

import jax
import jax.numpy as jnp
from jax import lax
from jax.experimental import pallas as pl
from jax.experimental.pallas import tpu as pltpu

D_MODEL = 1024
N_META = 16
HEAD_DIM = 64
ATTN_WIDTH = 512
Q_PER_KV = 4
N_KV_HEADS = 2
KV_WIDTH = 128
WINDOW = 128
BLOCK = 128
SSM_WIDTH = 512
SSM_GROUP = 16
N_SSM_GROUPS = 32
SSM_STATE = 64
IN_COLS = ATTN_WIDTH + 2 * KV_WIDTH + SSM_WIDTH
D_FF = 2816
EPS = 1e-6
ATTN_SCALE = HEAD_DIM ** -0.5
STATE_COLS = N_SSM_GROUPS * SSM_STATE
LANES = 128
SUBLANES = 8
VMEM_LIMIT = 56 * 1024 * 1024

F32 = jnp.float32
BF16 = jnp.bfloat16


def _const_spec(shape):
    return pl.BlockSpec(shape, lambda *_: (0,) * len(shape), pipeline_mode=pl.Buffered(1))


def _rms(x, g):
    return x * lax.rsqrt(jnp.mean(x * x, axis=-1, keepdims=True) + EPS) * g


def _left_half():
    return lax.broadcasted_iota(jnp.int32, (1, LANES), 1) < HEAD_DIM


def _pair_norm(zz, g2, left):
    sq = zz * zz
    sl = jnp.sum(jnp.where(left, sq, 0.0), axis=-1, keepdims=True)
    sr = jnp.sum(jnp.where(left, 0.0, sq), axis=-1, keepdims=True)
    inv = jnp.where(left, lax.rsqrt(sl / HEAD_DIM + EPS), lax.rsqrt(sr / HEAD_DIM + EPS))
    return zz * inv * g2


def _front_kernel(x_ref, gmix_ref, win_ref, gq_ref, gk_ref, q_ref, k_ref, v_ref, u_ref):
    xn = _rms(x_ref[...], gmix_ref[...]).astype(BF16)
    z = jnp.dot(xn, win_ref[...], preferred_element_type=F32)
    left = _left_half()
    for p in range(ATTN_WIDTH // LANES):
        q_ref[:, p * LANES:(p + 1) * LANES] = _pair_norm(z[:, p * LANES:(p + 1) * LANES], gq_ref[...], left)
    k_ref[...] = _pair_norm(z[:, ATTN_WIDTH:ATTN_WIDTH + KV_WIDTH], gk_ref[...], left)
    v_ref[...] = z[:, ATTN_WIDTH + KV_WIDTH:ATTN_WIDTH + 2 * KV_WIDTH]
    u_ref[...] = z[:, ATTN_WIDTH + 2 * KV_WIDTH:]


def _front(x_rows, gmix, win_bf, gq2, gk2, tm):
    n = x_rows.shape[0]
    assert n % tm == 0
    row = lambda w: pl.BlockSpec((tm, w), lambda i: (i, 0))
    return pl.pallas_call(
        _front_kernel,
        grid=(n // tm,),
        in_specs=[row(D_MODEL), _const_spec((1, D_MODEL)), _const_spec((D_MODEL, IN_COLS)),
                  _const_spec((1, LANES)), _const_spec((1, LANES))],
        out_specs=[row(ATTN_WIDTH), row(KV_WIDTH), row(KV_WIDTH), row(SSM_WIDTH)],
        out_shape=[jax.ShapeDtypeStruct((n, ATTN_WIDTH), F32), jax.ShapeDtypeStruct((n, KV_WIDTH), F32),
                   jax.ShapeDtypeStruct((n, KV_WIDTH), F32), jax.ShapeDtypeStruct((n, SSM_WIDTH), F32)],
        compiler_params=pltpu.CompilerParams(dimension_semantics=("arbitrary",), vmem_limit_bytes=VMEM_LIMIT),
        name="front",
    )(x_rows, gmix, win_bf, gq2, gk2)


Q_TILES = ATTN_WIDTH // LANES


def _attn_sample_kernel(sink_ref, q_ref, kn_ref, vn_ref, ck_ref, cv_ref, o_ref, kw_ref, vw_ref, *, bb, t):
    left = _left_half()
    n_heads = 2 * Q_TILES
    rows = n_heads * t
    tk = WINDOW + t
    r = lax.broadcasted_iota(jnp.int32, (rows, tk), 0) % t
    c = lax.broadcasted_iota(jnp.int32, (rows, tk), 1)
    bias = jnp.where((c >= r) & (c <= r + WINDOW), 0.0, -jnp.inf)
    hrow = lax.broadcasted_iota(jnp.int32, (rows, 1), 0) // t
    sink = jnp.zeros((rows, 1), F32)
    for g in range(n_heads):
        sink = jnp.where(hrow == g, sink_ref[g // 2 + Q_PER_KV * (g % 2)], sink)
    scores, values = [], []
    for bi in range(bb):
        kk = jnp.concatenate([ck_ref[bi], kn_ref[bi]], axis=0)
        vv = jnp.concatenate([cv_ref[bi], vn_ref[bi]], axis=0)
        kw_ref[bi] = kk[t:]
        vw_ref[bi] = vv[t:]
        q = q_ref[bi] * ATTN_SCALE
        pieces = []
        for tile in range(Q_TILES):
            qt = q[:, tile * LANES:(tile + 1) * LANES]
            pieces += [jnp.where(left, qt, 0.0), jnp.where(left, 0.0, qt)]
        qs = jnp.concatenate(pieces, axis=0).astype(BF16)
        scores.append(lax.dot_general(qs, kk.astype(BF16), (((1,), (1,)), ((), ())), preferred_element_type=F32))
        values.append(vv.astype(BF16))
    probs = []
    for bi in range(bb):
        s = scores[bi] + bias
        m = jnp.maximum(jnp.max(s, axis=-1, keepdims=True), sink)
        p = jnp.exp(s - m)
        inv = 1.0 / (jnp.sum(p, axis=-1, keepdims=True) + jnp.exp(sink - m))
        probs.append((p * inv).astype(BF16))
    for bi in range(bb):
        o = jnp.dot(probs[bi], values[bi], preferred_element_type=F32)
        for tile in range(Q_TILES):
            o_ref[bi, :, tile * LANES:(tile + 1) * LANES] = jnp.where(
                left, o[2 * tile * t:(2 * tile + 1) * t], o[(2 * tile + 1) * t:(2 * tile + 2) * t])


def _attn_sample(q, kn, vn, ck, cv, sinks, bb=16):
    db, t, _ = q.shape
    blk = lambda r, w: pl.BlockSpec((bb, r, w), lambda i: (i, 0, 0))
    return pl.pallas_call(
        functools.partial(_attn_sample_kernel, bb=bb, t=t),
        grid=(db // bb,),
        in_specs=[pl.BlockSpec(memory_space=pltpu.SMEM), blk(t, ATTN_WIDTH), blk(t, KV_WIDTH), blk(t, KV_WIDTH),
                  blk(WINDOW, KV_WIDTH), blk(WINDOW, KV_WIDTH)],
        out_specs=[blk(t, ATTN_WIDTH), blk(WINDOW, KV_WIDTH), blk(WINDOW, KV_WIDTH)],
        out_shape=[jax.ShapeDtypeStruct((db, t, ATTN_WIDTH), F32),
                   jax.ShapeDtypeStruct((db, WINDOW, KV_WIDTH), F32),
                   jax.ShapeDtypeStruct((db, WINDOW, KV_WIDTH), F32)],
        compiler_params=pltpu.CompilerParams(dimension_semantics=("arbitrary",)),
        name="attn_sample",
    )(sinks, q, kn, vn, ck, cv)


N_QUARTERS = 4
Q_GROUPS = N_SSM_GROUPS // N_QUARTERS
Q_STATE = Q_GROUPS * SSM_STATE


def _zoh_kernel(are_ref, aim_ref, logdt_ref, bre_ref, bim_ref, lr_ref, li_ref, bbr_ref, bbi_ref):
    ar, ai = are_ref[...], aim_ref[...]
    dt = jnp.exp(logdt_ref[...])
    mag = jnp.exp(ar * dt)
    lr, li = mag * jnp.cos(ai * dt), mag * jnp.sin(ai * dt)
    nr, ni = lr - 1.0, li
    den = ar * ar + ai * ai
    fr, fi = ((nr * ar + ni * ai) / den)[:, None, :], ((ni * ar - nr * ai) / den)[:, None, :]
    br, bi = bre_ref[...], bim_ref[...]
    lr_ref[...] = lr
    li_ref[...] = li
    bbr_ref[...] = fr * br - fi * bi
    bbi_ref[...] = fr * bi + fi * br


def _ssm_params(a_re, a_im, log_dt, b_re, b_im, c_re, c_im, d_skip, w_glu, b_glu):
    g, p, h = b_re.shape
    lr, li, bbr, bbi = pl.pallas_call(
        _zoh_kernel,
        out_shape=[jax.ShapeDtypeStruct((g, p), F32), jax.ShapeDtypeStruct((g, p), F32),
                   jax.ShapeDtypeStruct((g, h, p), F32), jax.ShapeDtypeStruct((g, h, p), F32)],
        name="zoh",
    )(a_re, a_im, log_dt.reshape(g, 1), jnp.swapaxes(b_re, 1, 2), jnp.swapaxes(b_im, 1, 2))
    eye = jnp.eye(Q_GROUPS, dtype=F32)
    quartered = lambda a: a.reshape((a.shape[0], N_QUARTERS, Q_GROUPS) + a.shape[2:])
    wq = jnp.einsum('rqghp,gm->qghrmp', quartered(jnp.stack([bbr, bbi])), eye).reshape(N_QUARTERS, LANES, 2 * Q_STATE)
    cq = jnp.einsum('rqghp,gm->qrgpmh', quartered(jnp.stack([c_re, -c_im])), eye).reshape(N_QUARTERS, 2 * Q_STATE, LANES)
    wglu = jnp.einsum('qghk,gm->qghmk', w_glu.reshape(N_QUARTERS, Q_GROUPS, SSM_GROUP, SSM_GROUP), eye)
    return dict(wq=wq.astype(BF16), cq=cq.astype(BF16), wgluq=wglu.reshape(N_QUARTERS, LANES, LANES).astype(BF16),
                lr=lr.reshape(1, STATE_COLS), li=li.reshape(1, STATE_COLS), d=d_skip.reshape(1, SSM_WIDTH),
                bglu=b_glu.reshape(1, SSM_WIDTH))


def _scan_step(lr, li, hr, hi, xr, xi):
    return lr * hr - li * hi + xr, lr * hi + li * hr + xi


def _ssm_sample_kernel(u0_ref, u1_ref, u2_ref, u3_ref, h0r_ref, h0i_ref, wq_ref, lr_ref, li_ref, cq_ref, d_ref,
                       wglu_ref, bglu_ref, o_ref, hr_ref, hi_ref, utb_ref, xs_ref, *, nseq, t_steps):
    u_refs = (u0_ref, u1_ref, u2_ref, u3_ref)
    quarters = range(N_QUARTERS)
    n_rg = nseq // SUBLANES
    rg_rows = SUBLANES * t_steps

    for t in range(t_steps):
        for rg in range(n_rg):
            for j in quarters:
                utb_ref[t * nseq + rg * SUBLANES:t * nseq + (rg + 1) * SUBLANES, j * LANES:(j + 1) * LANES] = (
                    u_refs[j][pl.ds(rg * rg_rows + t, SUBLANES, stride=t_steps), :])

    u = [utb_ref[:, q * LANES:(q + 1) * LANES] for q in quarters]
    for q in quarters:
        xs_ref[q] = jnp.dot(u[q].astype(BF16), wq_ref[q], preferred_element_type=F32)
    lam = [(jnp.broadcast_to(lr_ref[:, q * Q_STATE:(q + 1) * Q_STATE], (SUBLANES, Q_STATE)),
            jnp.broadcast_to(li_ref[:, q * Q_STATE:(q + 1) * Q_STATE], (SUBLANES, Q_STATE))) for q in quarters]

    def scan_group(rg, carry):
        r0 = pl.multiple_of(rg * SUBLANES, SUBLANES)
        h = [(h0r_ref[pl.ds(r0, SUBLANES), q * Q_STATE:(q + 1) * Q_STATE],
              h0i_ref[pl.ds(r0, SUBLANES), q * Q_STATE:(q + 1) * Q_STATE]) for q in quarters]
        for t in range(t_steps):
            rows = pl.ds(pl.multiple_of(t * nseq + r0, SUBLANES), SUBLANES)
            for q in quarters:
                h[q] = _scan_step(*lam[q], *h[q], xs_ref[q, rows, 0:Q_STATE], xs_ref[q, rows, Q_STATE:])
                xs_ref[q, rows, 0:Q_STATE] = h[q][0]
                xs_ref[q, rows, Q_STATE:] = h[q][1]
        for q in quarters:
            hr_ref[pl.ds(r0, SUBLANES), q * Q_STATE:(q + 1) * Q_STATE] = h[q][0]
            hi_ref[pl.ds(r0, SUBLANES), q * Q_STATE:(q + 1) * Q_STATE] = h[q][1]
        return carry

    lax.fori_loop(0, n_rg, scan_group, 0)
    y = [jnp.dot(xs_ref[q].astype(BF16), cq_ref[q], preferred_element_type=F32) for q in quarters]
    g = [jax.nn.gelu(y[q] + d_ref[:, q * LANES:(q + 1) * LANES] * u[q]) for q in quarters]
    gate = [jnp.dot(g[q].astype(BF16), wglu_ref[q], preferred_element_type=F32) + bglu_ref[:, q * LANES:(q + 1) * LANES]
            for q in quarters]
    for q in quarters:
        o = g[q] * jax.nn.sigmoid(gate[q])
        for t in range(t_steps):
            for rg in range(n_rg):
                o_ref[q, pl.ds(rg * rg_rows + t, SUBLANES, stride=t_steps), :] = (
                    o[t * nseq + rg * SUBLANES:t * nseq + (rg + 1) * SUBLANES, :])


def _ssm_sample(u_rows, h0r, h0i, sp, nseq, t_steps):
    n = nseq * t_steps
    slab = lambda j: pl.BlockSpec((n, LANES), lambda i, j=j: (0, j))
    st = pl.BlockSpec((nseq, STATE_COLS), lambda i: (0, 0))
    return pl.pallas_call(
        functools.partial(_ssm_sample_kernel, nseq=nseq, t_steps=t_steps),
        grid=(1,),
        in_specs=[slab(0), slab(1), slab(2), slab(3), st, st,
                  _const_spec((N_QUARTERS, LANES, 2 * Q_STATE)), _const_spec((1, STATE_COLS)),
                  _const_spec((1, STATE_COLS)), _const_spec((N_QUARTERS, 2 * Q_STATE, LANES)),
                  _const_spec((1, SSM_WIDTH)), _const_spec((N_QUARTERS, LANES, LANES)), _const_spec((1, SSM_WIDTH))],
        out_specs=[pl.BlockSpec((N_QUARTERS, n, LANES), lambda i: (0, 0, 0)), st, st],
        out_shape=[jax.ShapeDtypeStruct((N_QUARTERS, n, LANES), F32),
                   jax.ShapeDtypeStruct((nseq, STATE_COLS), F32), jax.ShapeDtypeStruct((nseq, STATE_COLS), F32)],
        scratch_shapes=[pltpu.VMEM((n, SSM_WIDTH), F32), pltpu.VMEM((N_QUARTERS, n, 2 * Q_STATE), F32)],
        compiler_params=pltpu.CompilerParams(dimension_semantics=("arbitrary",), vmem_limit_bytes=VMEM_LIMIT),
        name="ssm_sample",
    )(u_rows, u_rows, u_rows, u_rows, h0r, h0i, sp["wq"], sp["lr"], sp["li"], sp["cq"], sp["d"], sp["wgluq"], sp["bglu"])


def _ffn(h, gf_ref, wg_ref, wu_ref, wd_ref):
    f = _rms(h, gf_ref[...]).astype(BF16)
    gate = jnp.dot(f, wg_ref[...], preferred_element_type=F32)
    up = jnp.dot(f, wu_ref[...], preferred_element_type=F32)
    a = (jax.nn.silu(gate) * up).astype(BF16)
    return h + jnp.dot(a, wd_ref[...], preferred_element_type=F32)


def _back_kernel(x_ref, oa_ref, os_ref, ga_ref, gs_ref, wout_ref, gf_ref, wg_ref, wu_ref, wd_ref, y_ref):
    o_ssm = jnp.concatenate([os_ref[j] for j in range(N_QUARTERS)], axis=-1)
    mix = jnp.concatenate([_rms(oa_ref[...], ga_ref[...]), _rms(o_ssm, gs_ref[...])], axis=-1).astype(BF16)
    h = x_ref[...] + jnp.dot(mix, wout_ref[...], preferred_element_type=F32)
    y_ref[...] = _ffn(h, gf_ref, wg_ref, wu_ref, wd_ref)


def _ffn_kernel(x_ref, hp_ref, gf_ref, wg_ref, wu_ref, wd_ref, y_ref):
    hp = jnp.concatenate([hp_ref[c] for c in range(hp_ref.shape[0])], axis=-1)
    y_ref[...] = _ffn(x_ref[...] + hp, gf_ref, wg_ref, wu_ref, wd_ref)


def _back(x_rows, oa, os_, ga, gs, wout, gf, wg, wu, wd, tm):
    n = x_rows.shape[0]
    assert n % tm == 0
    row = lambda w: pl.BlockSpec((tm, w), lambda i: (i, 0))
    return pl.pallas_call(
        _back_kernel,
        grid=(n // tm,),
        in_specs=[row(D_MODEL), row(ATTN_WIDTH), pl.BlockSpec((N_QUARTERS, tm, LANES), lambda i: (0, i, 0)),
                  _const_spec((1, ATTN_WIDTH)), _const_spec((1, SSM_WIDTH)), _const_spec((D_MODEL, D_MODEL)),
                  _const_spec((1, D_MODEL)), _const_spec((D_MODEL, D_FF)), _const_spec((D_MODEL, D_FF)),
                  _const_spec((D_FF, D_MODEL))],
        out_specs=row(D_MODEL),
        out_shape=jax.ShapeDtypeStruct((n, D_MODEL), F32),
        compiler_params=pltpu.CompilerParams(dimension_semantics=("arbitrary",), vmem_limit_bytes=VMEM_LIMIT),
        name="back",
    )(x_rows, oa, os_, ga, gs, wout, gf, wg, wu, wd)


def _ffn_call(x_rows, hp_rows, gf, wg, wu, wd, tm):
    n = x_rows.shape[0]
    assert n % tm == 0
    row = lambda w: pl.BlockSpec((tm, w), lambda i: (i, 0))
    return pl.pallas_call(
        _ffn_kernel,
        grid=(n // tm,),
        in_specs=[row(D_MODEL), pl.BlockSpec((hp_rows.shape[0], tm, hp_rows.shape[2]), lambda i: (0, i, 0)),
                  _const_spec((1, D_MODEL)),
                  _const_spec((D_MODEL, D_FF)), _const_spec((D_MODEL, D_FF)), _const_spec((D_FF, D_MODEL))],
        out_specs=row(D_MODEL),
        out_shape=jax.ShapeDtypeStruct((n, D_MODEL), F32),
        compiler_params=pltpu.CompilerParams(dimension_semantics=("arbitrary",), vmem_limit_bytes=VMEM_LIMIT),
        name="ffn",
    )(x_rows, hp_rows, gf, wg, wu, wd)


N_SEQ = 8
ROWS = N_SEQ * BLOCK
PITCH = BLOCK + SUBLANES
CHUNK_T = BLOCK // N_SEQ


def _interleave(*stages):
    keyed = [((i + 0.5) / len(steps), k, i, step) for k, steps in enumerate(stages) for i, step in enumerate(steps)]
    return [step for _, _, _, step in sorted(keyed, key=lambda e: e[:3])]


def _mixer_kernel(sink_ref, x_ref, meta_ref, gmix_ref, win_ref, gq_ref, gk_ref,
                  wq_ref, lr_ref, li_ref, cq_ref, d_ref, wglu_ref, bglu_ref, ga_ref, gs_ref, wout_ref,
                  hp_ref, kw_ref, vw_ref, hr_ref, hi_ref,
                  qs_ref, kk_ref, vv_ref, s_ref, p_ref, oatt_ref, usl_ref, xs_ref, hb_ref, st_ref, osl_ref, mix_ref):
    i = pl.program_id(0)
    n_blocks = pl.num_programs(0) - 1

    @pl.when(i < n_blocks)
    def _():
        _mixer_block(i, sink_ref, x_ref, meta_ref, gmix_ref, win_ref, gq_ref, gk_ref,
                     wq_ref, lr_ref, li_ref, cq_ref, d_ref, wglu_ref, bglu_ref, ga_ref, gs_ref, wout_ref,
                     hp_ref, kw_ref, vw_ref, hr_ref, hi_ref,
                     qs_ref, kk_ref, vv_ref, s_ref, p_ref, oatt_ref, usl_ref, xs_ref, hb_ref, st_ref, osl_ref, mix_ref)

    @pl.when(i == n_blocks)
    def _():
        for step in _output_norm_steps(oatt_ref, osl_ref, ga_ref, gs_ref, mix_ref):
            step()
        for c in range(OUT_CHUNKS):
            _project_chunk(mix_ref, wout_ref, hp_ref, c)


OUT_CHUNKS = 4
OUT_CHUNK = D_MODEL // OUT_CHUNKS


def _project_chunk(mix_ref, wout_ref, hp_ref, c):
    hp_ref[c] = jnp.dot(mix_ref[...], wout_ref[c], preferred_element_type=F32).reshape(N_SEQ, BLOCK, OUT_CHUNK)


def _output_norm_steps(oatt_ref, osl_ref, ga_ref, gs_ref, mix_ref):
    def one(b):
        rows = slice(b * BLOCK, (b + 1) * BLOCK)
        mix_ref[rows, 0:ATTN_WIDTH] = _rms(oatt_ref[b], ga_ref[...]).astype(BF16)
        sl = [osl_ref[j, b * PITCH:b * PITCH + BLOCK, :] for j in range(N_QUARTERS)]
        ms = sum(jnp.sum(s * s, axis=-1, keepdims=True) for s in sl) / SSM_WIDTH
        inv = lax.rsqrt(ms + EPS)
        for j in range(N_QUARTERS):
            mix_ref[rows, ATTN_WIDTH + j * LANES:ATTN_WIDTH + (j + 1) * LANES] = (
                sl[j] * inv * gs_ref[:, j * LANES:(j + 1) * LANES]).astype(BF16)
    return [functools.partial(one, b) for b in range(N_SEQ)]


def _mixer_block(i, sink_ref, x_ref, meta_ref, gmix_ref, win_ref, gq_ref, gk_ref,
                 wq_ref, lr_ref, li_ref, cq_ref, d_ref, wglu_ref, bglu_ref, ga_ref, gs_ref, wout_ref,
                 hp_ref, kw_ref, vw_ref, hr_ref, hi_ref,
                 qs_ref, kk_ref, vv_ref, s_ref, p_ref, oatt_ref, usl_ref, xs_ref, hb_ref, st_ref, osl_ref, mix_ref):
    left = _left_half()
    project = lambda c: [functools.partial(_project_chunk, mix_ref, wout_ref, hp_ref, c)]

    @pl.when(i == 0)
    def _():
        oatt_ref[...] = jnp.zeros((N_SEQ, BLOCK, ATTN_WIDTH), F32)
        osl_ref[...] = jnp.zeros((N_QUARTERS, N_SEQ * PITCH, LANES), F32)
        xm = _rms(meta_ref[...], gmix_ref[...]).astype(BF16)
        zm = jnp.dot(xm, win_ref[:, ATTN_WIDTH:], preferred_element_type=F32)
        km = _pair_norm(zm[:, :KV_WIDTH], gk_ref[...], left).astype(BF16)
        vm = zm[:, KV_WIDTH:2 * KV_WIDTH].astype(BF16)
        lead = jnp.zeros((BLOCK - N_META, KV_WIDTH), BF16)
        for b in range(N_SEQ):
            kk_ref[b, 0:BLOCK - N_META, :] = lead
            vv_ref[b, 0:BLOCK - N_META, :] = lead
            kk_ref[b, BLOCK - N_META:BLOCK, :] = km
            vv_ref[b, BLOCK - N_META:BLOCK, :] = vm
        um = zm[:, 2 * KV_WIDTH:]
        for q in range(N_QUARTERS):
            xm_q = jnp.dot(um[:, q * LANES:(q + 1) * LANES].astype(BF16), wq_ref[q], preferred_element_type=F32)
            lr = jnp.broadcast_to(lr_ref[:, q * Q_STATE:(q + 1) * Q_STATE], (SUBLANES, Q_STATE))
            li = jnp.broadcast_to(li_ref[:, q * Q_STATE:(q + 1) * Q_STATE], (SUBLANES, Q_STATE))
            hr = jnp.zeros((SUBLANES, Q_STATE), F32)
            hi = jnp.zeros((SUBLANES, Q_STATE), F32)
            for t in range(N_META):
                xr = jnp.broadcast_to(xm_q[t:t + 1, 0:Q_STATE], (SUBLANES, Q_STATE))
                xi = jnp.broadcast_to(xm_q[t:t + 1, Q_STATE:], (SUBLANES, Q_STATE))
                hr, hi = _scan_step(lr, li, hr, hi, xr, xi)
            st_ref[:, 2 * q * Q_STATE:(2 * q + 1) * Q_STATE] = hr
            st_ref[:, (2 * q + 1) * Q_STATE:(2 * q + 2) * Q_STATE] = hi

    p1 = {}

    def pre_norm():
        p1["xn"] = _rms(x_ref[...].reshape(ROWS, D_MODEL), gmix_ref[...]).astype(BF16)

    def project_in(key, c0):
        p1[key] = jnp.dot(p1["xn"], win_ref[:, c0:c0 + 2 * LANES], preferred_element_type=F32)

    def finish_q(c):
        zz = p1.pop(("q", c))
        for tt in range(2):
            t = 2 * c + tt
            qn = _pair_norm(zz[:, tt * LANES:(tt + 1) * LANES], gq_ref[...], left) * ATTN_SCALE
            qa = jnp.where(left, qn, 0.0).astype(BF16)
            qb = jnp.where(left, 0.0, qn).astype(BF16)
            for b in range(N_SEQ):
                qs_ref[b, 2 * t * BLOCK:(2 * t + 1) * BLOCK, :] = qa[b * BLOCK:(b + 1) * BLOCK]
                qs_ref[b, (2 * t + 1) * BLOCK:(2 * t + 2) * BLOCK, :] = qb[b * BLOCK:(b + 1) * BLOCK]

    def finish_kv():
        zz = p1.pop("kv")
        kn = _pair_norm(zz[:, :LANES], gk_ref[...], left)
        vn = zz[:, LANES:]
        kw_ref[...] = kn.reshape(N_SEQ, BLOCK, KV_WIDTH)
        vw_ref[...] = vn.reshape(N_SEQ, BLOCK, KV_WIDTH)
        kk_ref[:, BLOCK:, :] = kn.astype(BF16).reshape(N_SEQ, BLOCK, KV_WIDTH)
        vv_ref[:, BLOCK:, :] = vn.astype(BF16).reshape(N_SEQ, BLOCK, KV_WIDTH)

    def finish_u(c):
        zz = p1.pop(("u", c))
        for tt in range(2):
            for b in range(N_SEQ):
                usl_ref[2 * c + tt, b * PITCH:b * PITCH + BLOCK, :] = zz[b * BLOCK:(b + 1) * BLOCK, tt * LANES:(tt + 1) * LANES]

    dot_kv = functools.partial(project_in, "kv", ATTN_WIDTH)
    dot_q = [functools.partial(project_in, ("q", c), c * 2 * LANES) for c in range(2)]
    dot_u = [functools.partial(project_in, ("u", c), ATTN_WIDTH + 2 * KV_WIDTH + c * 2 * LANES) for c in range(2)]
    post_q = [functools.partial(finish_q, c) for c in range(2)]
    post_u = [functools.partial(finish_u, c) for c in range(2)]

    r = lax.broadcasted_iota(jnp.int32, (BLOCK, 2 * BLOCK), 0)
    c = lax.broadcasted_iota(jnp.int32, (BLOCK, 2 * BLOCK), 1)
    valid = (c >= r) & (c <= r + WINDOW) & ((i > 0) | (c >= BLOCK - N_META))
    bias = jnp.where(valid, 0.0, -jnp.inf)

    def gather_u(n):
        t0 = n * CHUNK_T
        return jnp.concatenate(
            [jnp.concatenate([usl_ref[j, pl.ds(t0 + tl, SUBLANES, stride=PITCH), :] for j in range(N_QUARTERS)], axis=1)
             for tl in range(CHUNK_T)], axis=0)

    def feed(n, par):
        env = {}

        def scores():
            s_ref[par] = lax.dot_general(qs_ref[n], kk_ref[n], (((1,), (1,)), ((), ())), preferred_element_type=F32)

        def gather():
            env["u"] = gather_u(n)

        def scan_inputs(q):
            xs_ref[par, :, 2 * q * Q_STATE:(2 * q + 2) * Q_STATE] = jnp.dot(
                env["u"][:, q * LANES:(q + 1) * LANES].astype(BF16), wq_ref[q], preferred_element_type=F32)

        return [scores, gather] + [functools.partial(scan_inputs, q) for q in range(N_QUARTERS)]

    def mid(n, par):
        env = {}

        def row_max(g):
            s = s_ref[par, g * BLOCK:(g + 1) * BLOCK, :] + bias
            env[g] = (s, jnp.maximum(jnp.max(s, axis=-1, keepdims=True), sink_ref[g // 2 + Q_PER_KV * (g % 2)]))

        def exponent(g):
            s, m = env[g]
            p = jnp.exp(s - m)
            sink = sink_ref[g // 2 + Q_PER_KV * (g % 2)]
            env[g] = (p, 1.0 / (jnp.sum(p, axis=-1, keepdims=True) + jnp.exp(sink - m)))

        def normalise(g):
            p, inv = env.pop(g)
            p_ref[par, g * BLOCK:(g + 1) * BLOCK, :] = (p * inv).astype(BF16)

        softmax = [functools.partial(f, g) for g in range(2 * Q_TILES) for f in (row_max, exponent, normalise)]

        def load_state():
            for q in range(N_QUARTERS):
                env["lam", q] = (jnp.broadcast_to(lr_ref[:, q * Q_STATE:(q + 1) * Q_STATE], (SUBLANES, Q_STATE)),
                                 jnp.broadcast_to(li_ref[:, q * Q_STATE:(q + 1) * Q_STATE], (SUBLANES, Q_STATE)))
                env["h", q] = (st_ref[:, 2 * q * Q_STATE:(2 * q + 1) * Q_STATE],
                               st_ref[:, (2 * q + 1) * Q_STATE:(2 * q + 2) * Q_STATE])

        def scan_pair(tp, q):
            lr, li = env["lam", q]
            hr, hi = env["h", q]
            re0, im0 = 2 * q * Q_STATE, (2 * q + 1) * Q_STATE
            pair_r, pair_i = [], []
            for tl in (2 * tp, 2 * tp + 1):
                xr = xs_ref[par, tl * SUBLANES:(tl + 1) * SUBLANES, re0:re0 + Q_STATE]
                xi = xs_ref[par, tl * SUBLANES:(tl + 1) * SUBLANES, im0:im0 + Q_STATE]
                hr, hi = _scan_step(lr, li, hr, hi, xr, xi)
                pair_r.append(hr)
                pair_i.append(hi)
            env["h", q] = (hr, hi)
            rows = slice(2 * tp * SUBLANES, (2 * tp + 2) * SUBLANES)
            hb_ref[par, rows, re0:re0 + Q_STATE] = jnp.concatenate(pair_r, axis=0).astype(BF16)
            hb_ref[par, rows, im0:im0 + Q_STATE] = jnp.concatenate(pair_i, axis=0).astype(BF16)

        def store_state():
            for q in range(N_QUARTERS):
                hr, hi = env["h", q]
                st_ref[:, 2 * q * Q_STATE:(2 * q + 1) * Q_STATE] = hr
                st_ref[:, (2 * q + 1) * Q_STATE:(2 * q + 2) * Q_STATE] = hi

        scan = ([load_state] + [functools.partial(scan_pair, tp, q) for tp in range(CHUNK_T // 2) for q in range(N_QUARTERS)]
                + [store_state])
        return softmax, scan

    def tail(n, par):
        env = {}

        def attn_out():
            o_all = jnp.dot(p_ref[par], vv_ref[n], preferred_element_type=F32)
            for t in range(Q_TILES):
                oatt_ref[n, :, t * LANES:(t + 1) * LANES] = jnp.where(
                    left, o_all[2 * t * BLOCK:(2 * t + 1) * BLOCK], o_all[(2 * t + 1) * BLOCK:(2 * t + 2) * BLOCK])

        def gather():
            env["u"] = gather_u(n)

        def readout(q):
            env["y", q] = jnp.dot(hb_ref[par, :, 2 * q * Q_STATE:(2 * q + 2) * Q_STATE], cq_ref[q],
                                  preferred_element_type=F32)

        def activate(q):
            cols = slice(q * LANES, (q + 1) * LANES)
            env["g", q] = jax.nn.gelu(env.pop(("y", q)) + d_ref[:, cols] * env["u"][:, cols])

        def gate(q):
            cols = slice(q * LANES, (q + 1) * LANES)
            env["gate", q] = jnp.dot(env["g", q].astype(BF16), wglu_ref[q], preferred_element_type=F32) + bglu_ref[:, cols]

        def emit(q):
            o = env.pop(("g", q)) * jax.nn.sigmoid(env.pop(("gate", q)))
            t0 = n * CHUNK_T
            for tl in range(CHUNK_T):
                osl_ref[q, pl.ds(t0 + tl, SUBLANES, stride=PITCH), :] = o[tl * SUBLANES:(tl + 1) * SUBLANES]

        per_quarter = lambda f: [functools.partial(f, q) for q in range(N_QUARTERS)]
        return [attn_out, gather] + per_quarter(readout) + per_quarter(activate) + per_quarter(gate) + per_quarter(emit)

    def run(*stages):
        for step in _interleave(*stages):
            step()

    def steady(k, carry):
        n = 2 * k + 1
        run(feed(n + 1, 0), *mid(n, 1), tail(n - 1, 0), project(k))
        run(feed(n + 2, 1), *mid(n + 1, 0), tail(n, 1))
        return carry

    norms = _output_norm_steps(oatt_ref, osl_ref, ga_ref, gs_ref, mix_ref)
    feed0, feed1 = feed(0, 0), feed(1, 1)
    softmax0, scan0 = mid(0, 0)
    half = len(softmax0) // 2
    run([pre_norm, dot_kv, dot_q[0], finish_kv, dot_q[1], post_q[0], dot_u[0], post_q[1]], norms)
    run(feed0[:1])
    run([dot_u[1]] + feed1[:1] + post_u, softmax0[:half])
    run(feed0[1:], softmax0[half:])
    run(feed1[1:], scan0)
    n_steady = (N_SEQ - 2) // 2
    assert n_steady == OUT_CHUNKS - 1
    lax.fori_loop(0, n_steady, steady, 0)
    run(*mid(N_SEQ - 1, 1), tail(N_SEQ - 2, 0), project(OUT_CHUNKS - 1))
    run(tail(N_SEQ - 1, 1))
    kk_ref[:, 0:BLOCK, :] = kk_ref[:, BLOCK:, :]
    vv_ref[:, 0:BLOCK, :] = vv_ref[:, BLOCK:, :]

    hr_ref[...] = jnp.concatenate([st_ref[:, 2 * q * Q_STATE:(2 * q + 1) * Q_STATE] for q in range(N_QUARTERS)], axis=1)
    hi_ref[...] = jnp.concatenate([st_ref[:, (2 * q + 1) * Q_STATE:(2 * q + 2) * Q_STATE] for q in range(N_QUARTERS)], axis=1)


def _mixer(x_prompt, meta_tokens, sinks, gmix, win, gq2, gk2, sp, ga, gs, wout):
    nseq, seq, _ = x_prompt.shape
    assert nseq == N_SEQ and seq % BLOCK == 0
    n_blocks = seq // BLOCK
    st = pl.BlockSpec((N_SEQ, STATE_COLS), lambda i: (0, 0))
    kvw = pl.BlockSpec((N_SEQ, BLOCK, KV_WIDTH), lambda i: (0, 0, 0))
    return pl.pallas_call(
        _mixer_kernel,
        grid=(n_blocks + 1,),
        in_specs=[pl.BlockSpec(memory_space=pltpu.SMEM),
                  pl.BlockSpec((N_SEQ, BLOCK, D_MODEL), lambda i: (0, jnp.minimum(i, n_blocks - 1), 0)),
                  _const_spec((N_META, D_MODEL)),
                  _const_spec((1, D_MODEL)), _const_spec((D_MODEL, IN_COLS)), _const_spec((1, LANES)),
                  _const_spec((1, LANES)),
                  _const_spec((N_QUARTERS, LANES, 2 * Q_STATE)), _const_spec((1, STATE_COLS)),
                  _const_spec((1, STATE_COLS)), _const_spec((N_QUARTERS, 2 * Q_STATE, LANES)),
                  _const_spec((1, SSM_WIDTH)), _const_spec((N_QUARTERS, LANES, LANES)), _const_spec((1, SSM_WIDTH)),
                  _const_spec((1, ATTN_WIDTH)), _const_spec((1, SSM_WIDTH)),
                  _const_spec((OUT_CHUNKS, D_MODEL, OUT_CHUNK))],
        out_specs=[pl.BlockSpec((OUT_CHUNKS, N_SEQ, BLOCK, OUT_CHUNK), lambda i: (0, 0, jnp.maximum(i - 1, 0), 0)),
                   kvw, kvw, st, st],
        out_shape=[jax.ShapeDtypeStruct((OUT_CHUNKS, N_SEQ, seq, OUT_CHUNK), F32),
                   jax.ShapeDtypeStruct((N_SEQ, BLOCK, KV_WIDTH), F32), jax.ShapeDtypeStruct((N_SEQ, BLOCK, KV_WIDTH), F32),
                   jax.ShapeDtypeStruct((N_SEQ, STATE_COLS), F32), jax.ShapeDtypeStruct((N_SEQ, STATE_COLS), F32)],
        scratch_shapes=[pltpu.VMEM((N_SEQ, 2 * Q_TILES * BLOCK, LANES), BF16),
                        pltpu.VMEM((N_SEQ, 2 * BLOCK, KV_WIDTH), BF16),
                        pltpu.VMEM((N_SEQ, 2 * BLOCK, KV_WIDTH), BF16),
                        pltpu.VMEM((2, 2 * Q_TILES * BLOCK, 2 * BLOCK), F32),
                        pltpu.VMEM((2, 2 * Q_TILES * BLOCK, 2 * BLOCK), BF16),
                        pltpu.VMEM((N_SEQ, BLOCK, ATTN_WIDTH), F32),
                        pltpu.VMEM((N_QUARTERS, N_SEQ * PITCH, LANES), F32),
                        pltpu.VMEM((2, CHUNK_T * N_SEQ, 2 * STATE_COLS), F32),
                        pltpu.VMEM((2, CHUNK_T * N_SEQ, 2 * STATE_COLS), BF16),
                        pltpu.VMEM((N_SEQ, 2 * STATE_COLS), F32),
                        pltpu.VMEM((N_QUARTERS, N_SEQ * PITCH, LANES), F32),
                        pltpu.VMEM((ROWS, D_MODEL), BF16)],
        compiler_params=pltpu.CompilerParams(dimension_semantics=("arbitrary",), vmem_limit_bytes=VMEM_LIMIT),
        name="mixer",
    )(sinks, x_prompt, meta_tokens, gmix, win, gq2, gk2, sp["wq"], sp["lr"], sp["li"], sp["cq"], sp["d"],
      sp["wgluq"], sp["bglu"], ga, gs, wout)


def kernel(x_prompt, x_sample, cache_k_win, cache_v_win, state_ssm_re, state_ssm_im, meta_tokens, g_mix, w_in, g_q,
           g_k, sinks, ssm_a_re, ssm_a_im, ssm_log_dt, ssm_b_re, ssm_b_im, ssm_c_re, ssm_c_im, ssm_d, ssm_w_glu,
           ssm_b_glu, g_att_out, g_ssm_out, w_out, g_ffn, w_gate, w_up, w_down):
    bp, seq, _ = x_prompt.shape
    db, dseq, _ = x_sample.shape
    li = 0
    gmix = g_mix[li].reshape(1, D_MODEL)
    win = w_in[li].astype(BF16)
    gq2 = jnp.tile(g_q[li], 2).reshape(1, LANES)
    gk2 = jnp.tile(g_k[li], 2).reshape(1, LANES)
    sk = sinks[li]
    sp = _ssm_params(ssm_a_re[li], ssm_a_im[li], ssm_log_dt[li], ssm_b_re[li], ssm_b_im[li], ssm_c_re[li],
                     ssm_c_im[li], ssm_d[li], ssm_w_glu[li], ssm_b_glu[li])
    ga = g_att_out[li].reshape(1, ATTN_WIDTH)
    gs = g_ssm_out[li].reshape(1, SSM_WIDTH)
    wout = w_out[li].astype(BF16)
    ffn_w = (g_ffn[li].reshape(1, D_MODEL), w_gate[li].astype(BF16), w_up[li].astype(BF16), w_down[li].astype(BF16))
    regroup = lambda a, axis: jnp.swapaxes(
        a.reshape(a.shape[:axis] + (2, Q_PER_KV, HEAD_DIM) + a.shape[axis + 1:]), axis, axis + 1).reshape(a.shape)
    win = jnp.concatenate([regroup(win[:, :ATTN_WIDTH], 1), win[:, ATTN_WIDTH:]], axis=1)
    wout = jnp.concatenate([regroup(wout[:ATTN_WIDTH], 0), wout[ATTN_WIDTH:]], axis=0)
    ga = regroup(ga, 1)

    wout_chunks = wout.reshape(D_MODEL, OUT_CHUNKS, OUT_CHUNK).transpose(1, 0, 2)
    hproj, kw_p, vw_p, hp_r, hp_i = _mixer(x_prompt, meta_tokens, sk, gmix, win, gq2, gk2, sp, ga, gs, wout_chunks)
    y_prompt = _ffn_call(x_prompt.reshape(bp * seq, D_MODEL), hproj.reshape(OUT_CHUNKS, bp * seq, OUT_CHUNK), *ffn_w,
                         tm=512).reshape(bp, seq, D_MODEL)

    n_s = db * dseq
    xs_rows = x_sample.reshape(n_s, D_MODEL)
    q_s, k_s, v_s, u_s = _front(xs_rows, gmix, win, gq2, gk2, tm=512)
    ck = cache_k_win[li].reshape(db, WINDOW, KV_WIDTH)
    cv = cache_v_win[li].reshape(db, WINDOW, KV_WIDTH)
    oa_s, kw_s, vw_s = _attn_sample(q_s.reshape(db, dseq, ATTN_WIDTH), k_s.reshape(db, dseq, KV_WIDTH),
                                    v_s.reshape(db, dseq, KV_WIDTH), ck, cv, sk)
    os_s, hs_r, hs_i = _ssm_sample(u_s, state_ssm_re[li].reshape(db, STATE_COLS),
                                   state_ssm_im[li].reshape(db, STATE_COLS), sp, nseq=db, t_steps=dseq)
    y_sample = _back(xs_rows, oa_s.reshape(n_s, ATTN_WIDTH), os_s, ga, gs, wout, *ffn_w, tm=512).reshape(db, dseq, D_MODEL)

    kv5 = lambda a, n: a.reshape(1, n, WINDOW, N_KV_HEADS, HEAD_DIM)
    st4 = lambda a, n: a.reshape(1, n, N_SSM_GROUPS, SSM_STATE)
    return (y_prompt, y_sample, kv5(kw_p, bp), kv5(vw_p, bp), st4(hp_r, bp), st4(hp_i, bp),
            kv5(kw_s, db), kv5(vw_s, db), st4(hs_r, db), st4(hs_i, db))
```

```python
import functools

import jax
import jax.numpy as jnp
from jax import lax
from jax.experimental import pallas as pl
from jax.experimental.pallas import tpu as pltpu

D_MODEL = 1024
N_META = 16
HEAD_DIM = 64
ATTN_WIDTH = 512
Q_PER_KV = 4
N_KV_HEADS = 2
KV_WIDTH = 128
WINDOW = 128
BLOCK = 128
SSM_WIDTH = 512
SSM_GROUP = 16
N_SSM_GROUPS = 32
SSM_STATE = 64
IN_COLS = ATTN_WIDTH + 2 * KV_WIDTH + SSM_WIDTH
D_FF = 2816
EPS = 1e-6
ATTN_SCALE = HEAD_DIM ** -0.5
STATE_COLS = N_SSM_GROUPS * SSM_STATE
LANES = 128
SUBLANES = 8
VMEM_LIMIT = 56 * 1024 * 1024

F32 = jnp.float32
BF16 = jnp.bfloat16


def _const_spec(shape):
    return pl.BlockSpec(shape, lambda *_: (0,) * len(shape), pipeline_mode=pl.Buffered(1))


def _rms(x, g):
    return x * lax.rsqrt(jnp.mean(x * x, axis=-1, keepdims=True) + EPS) * g


def _left_half():
    return lax.broadcasted_iota(jnp.int32, (1, LANES), 1) < HEAD_DIM


def _pair_norm(zz, g2, left):
    sq = zz * zz
    sl = jnp.sum(jnp.where(left, sq, 0.0), axis=-1, keepdims=True)
    sr = jnp.sum(jnp.where(left, 0.0, sq), axis=-1, keepdims=True)
    inv = jnp.where(left, lax.rsqrt(sl / HEAD_DIM + EPS), lax.rsqrt(sr / HEAD_DIM + EPS))
    return zz * inv * g2


def _front_kernel(x_ref, gmix_ref, win_ref, gq_ref, gk_ref, q_ref, k_ref, v_ref, u_ref):
    xn = _rms(x_ref[...], gmix_ref[...]).astype(BF16)
    z = jnp.dot(xn, win_ref[...], preferred_element_type=F32)
    left = _left_half()
    for p in range(ATTN_WIDTH // LANES):
        q_ref[:, p * LANES:(p + 1) * LANES] = _pair_norm(z[:, p * LANES:(p + 1) * LANES], gq_ref[...], left)
    k_ref[...] = _pair_norm(z[:, ATTN_WIDTH:ATTN_WIDTH + KV_WIDTH], gk_ref[...], left)
    v_ref[...] = z[:, ATTN_WIDTH + KV_WIDTH:ATTN_WIDTH + 2 * KV_WIDTH]
    u_ref[...] = z[:, ATTN_WIDTH + 2 * KV_WIDTH:]


def _front(x_rows, gmix, win_bf, gq2, gk2, tm):
    n = x_rows.shape[0]
    assert n % tm == 0
    row = lambda w: pl.BlockSpec((tm, w), lambda i: (i, 0))
    return pl.pallas_call(
        _front_kernel,
        grid=(n // tm,),
        in_specs=[row(D_MODEL), _const_spec((1, D_MODEL)), _const_spec((D_MODEL, IN_COLS)),
                  _const_spec((1, LANES)), _const_spec((1, LANES))],
        out_specs=[row(ATTN_WIDTH), row(KV_WIDTH), row(KV_WIDTH), row(SSM_WIDTH)],
        out_shape=[jax.ShapeDtypeStruct((n, ATTN_WIDTH), F32), jax.ShapeDtypeStruct((n, KV_WIDTH), F32),
                   jax.ShapeDtypeStruct((n, KV_WIDTH), F32), jax.ShapeDtypeStruct((n, SSM_WIDTH), F32)],
        compiler_params=pltpu.CompilerParams(dimension_semantics=("arbitrary",), vmem_limit_bytes=VMEM_LIMIT),
        name="front",
    )(x_rows, gmix, win_bf, gq2, gk2)


Q_TILES = ATTN_WIDTH // LANES


def _attn_sample_kernel(sink_ref, q_ref, kn_ref, vn_ref, ck_ref, cv_ref, o_ref, kw_ref, vw_ref, *, bb, t):
    left = _left_half()
    n_heads = 2 * Q_TILES
    rows = n_heads * t
    tk = WINDOW + t
    r = lax.broadcasted_iota(jnp.int32, (rows, tk), 0) % t
    c = lax.broadcasted_iota(jnp.int32, (rows, tk), 1)
    bias = jnp.where((c >= r) & (c <= r + WINDOW), 0.0, -jnp.inf)
    hrow = lax.broadcasted_iota(jnp.int32, (rows, 1), 0) // t
    sink = jnp.zeros((rows, 1), F32)
    for g in range(n_heads):
        sink = jnp.where(hrow == g, sink_ref[g // 2 + Q_PER_KV * (g % 2)], sink)
    scores, values = [], []
    for bi in range(bb):
        kk = jnp.concatenate([ck_ref[bi], kn_ref[bi]], axis=0)
        vv = jnp.concatenate([cv_ref[bi], vn_ref[bi]], axis=0)
        kw_ref[bi] = kk[t:]
        vw_ref[bi] = vv[t:]
        q = q_ref[bi] * ATTN_SCALE
        pieces = []
        for tile in range(Q_TILES):
            qt = q[:, tile * LANES:(tile + 1) * LANES]
            pieces += [jnp.where(left, qt, 0.0), jnp.where(left, 0.0, qt)]
        qs = jnp.concatenate(pieces, axis=0).astype(BF16)
        scores.append(lax.dot_general(qs, kk.astype(BF16), (((1,), (1,)), ((), ())), preferred_element_type=F32))
        values.append(vv.astype(BF16))
    probs = []
    for bi in range(bb):
        s = scores[bi] + bias
        m = jnp.maximum(jnp.max(s, axis=-1, keepdims=True), sink)
        p = jnp.exp(s - m)
        inv = 1.0 / (jnp.sum(p, axis=-1, keepdims=True) + jnp.exp(sink - m))
        probs.append((p * inv).astype(BF16))
    for bi in range(bb):
        o = jnp.dot(probs[bi], values[bi], preferred_element_type=F32)
        for tile in range(Q_TILES):
            o_ref[bi, :, tile * LANES:(tile + 1) * LANES] = jnp.where(
                left, o[2 * tile * t:(2 * tile + 1) * t], o[(2 * tile + 1) * t:(2 * tile + 2) * t])


def _attn_sample(q, kn, vn, ck, cv, sinks, bb=16):
    db, t, _ = q.shape
    blk = lambda r, w: pl.BlockSpec((bb, r, w), lambda i: (i, 0, 0))
    return pl.pallas_call(
        functools.partial(_attn_sample_kernel, bb=bb, t=t),
        grid=(db // bb,),
        in_specs=[pl.BlockSpec(memory_space=pltpu.SMEM), blk(t, ATTN_WIDTH), blk(t, KV_WIDTH), blk(t, KV_WIDTH),
                  blk(WINDOW, KV_WIDTH), blk(WINDOW, KV_WIDTH)],
        out_specs=[blk(t, ATTN_WIDTH), blk(WINDOW, KV_WIDTH), blk(WINDOW, KV_WIDTH)],
        out_shape=[jax.ShapeDtypeStruct((db, t, ATTN_WIDTH), F32),
                   jax.ShapeDtypeStruct((db, WINDOW, KV_WIDTH), F32),
                   jax.ShapeDtypeStruct((db, WINDOW, KV_WIDTH), F32)],
        compiler_params=pltpu.CompilerParams(dimension_semantics=("arbitrary",)),
        name="attn_sample",
    )(sinks, q, kn, vn, ck, cv)


N_QUARTERS = 4
Q_GROUPS = N_SSM_GROUPS // N_QUARTERS
Q_STATE = Q_GROUPS * SSM_STATE


def _zoh_kernel(are_ref, aim_ref, logdt_ref, bre_ref, bim_ref, lr_ref, li_ref, bbr_ref, bbi_ref):
    ar, ai = are_ref[...], aim_ref[...]
    dt = jnp.exp(logdt_ref[...])
    mag = jnp.exp(ar * dt)
    lr, li = mag * jnp.cos(ai * dt), mag * jnp.sin(ai * dt)
    nr, ni = lr - 1.0, li
    den = ar * ar + ai * ai
    fr, fi = ((nr * ar + ni * ai) / den)[:, None, :], ((ni * ar - nr * ai) / den)[:, None, :]
    br, bi = bre_ref[...], bim_ref[...]
    lr_ref[...] = lr
    li_ref[...] = li
    bbr_ref[...] = fr * br - fi * bi
    bbi_ref[...] = fr * bi + fi * br


def _ssm_params(a_re, a_im, log_dt, b_re, b_im, c_re, c_im, d_skip, w_glu, b_glu):
    g, p, h = b_re.shape
    lr, li, bbr, bbi = pl.pallas_call(
        _zoh_kernel,
        out_shape=[jax.ShapeDtypeStruct((g, p), F32), jax.ShapeDtypeStruct((g, p), F32),
                   jax.ShapeDtypeStruct((g, h, p), F32), jax.ShapeDtypeStruct((g, h, p), F32)],
        name="zoh",
    )(a_re, a_im, log_dt.reshape(g, 1), jnp.swapaxes(b_re, 1, 2), jnp.swapaxes(b_im, 1, 2))
    eye = jnp.eye(Q_GROUPS, dtype=F32)
    quartered = lambda a: a.reshape((a.shape[0], N_QUARTERS, Q_GROUPS) + a.shape[2:])
    wq = jnp.einsum('rqghp,gm->qghrmp', quartered(jnp.stack([bbr, bbi])), eye).reshape(N_QUARTERS, LANES, 2 * Q_STATE)
    cq = jnp.einsum('rqghp,gm->qrgpmh', quartered(jnp.stack([c_re, -c_im])), eye).reshape(N_QUARTERS, 2 * Q_STATE, LANES)
    wglu = jnp.einsum('qghk,gm->qghmk', w_glu.reshape(N_QUARTERS, Q_GROUPS, SSM_GROUP, SSM_GROUP), eye)
    return dict(wq=wq.astype(BF16), cq=cq.astype(BF16), wgluq=wglu.reshape(N_QUARTERS, LANES, LANES).astype(BF16),
                lr=lr.reshape(1, STATE_COLS), li=li.reshape(1, STATE_COLS), d=d_skip.reshape(1, SSM_WIDTH),
                bglu=b_glu.reshape(1, SSM_WIDTH))


def _scan_step(lr, li, hr, hi, xr, xi):
    return lr * hr - li * hi + xr, lr * hi + li * hr + xi


def _ssm_sample_kernel(u0_ref, u1_ref, u2_ref, u3_ref, h0r_ref, h0i_ref, wq_ref, lr_ref, li_ref, cq_ref, d_ref,
                       wglu_ref, bglu_ref, oa_ref, ga_ref, gs_ref, wout_ref, hp_ref, hr_ref, hi_ref,
                       utb_ref, xs_ref, osl_ref, *, nseq, t_steps):
    u_refs = (u0_ref, u1_ref, u2_ref, u3_ref)
    quarters = range(N_QUARTERS)
    n_rg = nseq // SUBLANES
    rg_rows = SUBLANES * t_steps

    for t in range(t_steps):
        for rg in range(n_rg):
            for j in quarters:
                utb_ref[t * nseq + rg * SUBLANES:t * nseq + (rg + 1) * SUBLANES, j * LANES:(j + 1) * LANES] = (
                    u_refs[j][pl.ds(rg * rg_rows + t, SUBLANES, stride=t_steps), :])

    u = [utb_ref[:, q * LANES:(q + 1) * LANES] for q in quarters]
    for q in quarters:
        xs_ref[q] = jnp.dot(u[q].astype(BF16), wq_ref[q], preferred_element_type=F32)
    lam = [(jnp.broadcast_to(lr_ref[:, q * Q_STATE:(q + 1) * Q_STATE], (SUBLANES, Q_STATE)),
            jnp.broadcast_to(li_ref[:, q * Q_STATE:(q + 1) * Q_STATE], (SUBLANES, Q_STATE))) for q in quarters]

    def scan_group(rg, carry):
        r0 = pl.multiple_of(rg * SUBLANES, SUBLANES)
        h = [(h0r_ref[pl.ds(r0, SUBLANES), q * Q_STATE:(q + 1) * Q_STATE],
              h0i_ref[pl.ds(r0, SUBLANES), q * Q_STATE:(q + 1) * Q_STATE]) for q in quarters]
        for t in range(t_steps):
            rows = pl.ds(pl.multiple_of(t * nseq + r0, SUBLANES), SUBLANES)
            for q in quarters:
                h[q] = _scan_step(*lam[q], *h[q], xs_ref[q, rows, 0:Q_STATE], xs_ref[q, rows, Q_STATE:])
                xs_ref[q, rows, 0:Q_STATE] = h[q][0]
                xs_ref[q, rows, Q_STATE:] = h[q][1]
        for q in quarters:
            hr_ref[pl.ds(r0, SUBLANES), q * Q_STATE:(q + 1) * Q_STATE] = h[q][0]
            hi_ref[pl.ds(r0, SUBLANES), q * Q_STATE:(q + 1) * Q_STATE] = h[q][1]
        return carry

    lax.fori_loop(0, n_rg, scan_group, 0)
    y = [jnp.dot(xs_ref[q].astype(BF16), cq_ref[q], preferred_element_type=F32) for q in quarters]
    g = [jax.nn.gelu(y[q] + d_ref[:, q * LANES:(q + 1) * LANES] * u[q]) for q in quarters]
    gate = [jnp.dot(g[q].astype(BF16), wglu_ref[q], preferred_element_type=F32) + bglu_ref[:, q * LANES:(q + 1) * LANES]
            for q in quarters]
    for q in quarters:
        o = g[q] * jax.nn.sigmoid(gate[q])
        for t in range(t_steps):
            for rg in range(n_rg):
                osl_ref[q, pl.ds(rg * rg_rows + t, SUBLANES, stride=t_steps), :] = (
                    o[t * nseq + rg * SUBLANES:t * nseq + (rg + 1) * SUBLANES, :])
    o_ssm = jnp.concatenate([osl_ref[q] for q in quarters], axis=-1)
    mix = jnp.concatenate([_rms(oa_ref[...], ga_ref[...]), _rms(o_ssm, gs_ref[...])], axis=-1).astype(BF16)
    for c in range(wout_ref.shape[0]):
        hp_ref[c] = jnp.dot(mix, wout_ref[c], preferred_element_type=F32)


def _ssm_sample(u_rows, h0r, h0i, sp, oa_rows, ga, gs, wout_chunks, nseq, t_steps):
    n = nseq * t_steps
    n_chunks, _, chunk = wout_chunks.shape
    slab = lambda j: pl.BlockSpec((n, LANES), lambda i, j=j: (0, j))
    st = pl.BlockSpec((nseq, STATE_COLS), lambda i: (0, 0))
    return pl.pallas_call(
        functools.partial(_ssm_sample_kernel, nseq=nseq, t_steps=t_steps),
        grid=(1,),
        in_specs=[slab(0), slab(1), slab(2), slab(3), st, st,
                  _const_spec((N_QUARTERS, LANES, 2 * Q_STATE)), _const_spec((1, STATE_COLS)),
                  _const_spec((1, STATE_COLS)), _const_spec((N_QUARTERS, 2 * Q_STATE, LANES)),
                  _const_spec((1, SSM_WIDTH)), _const_spec((N_QUARTERS, LANES, LANES)), _const_spec((1, SSM_WIDTH)),
                  _const_spec((n, ATTN_WIDTH)), _const_spec((1, ATTN_WIDTH)), _const_spec((1, SSM_WIDTH)),
                  _const_spec((n_chunks, D_MODEL, chunk))],
        out_specs=[pl.BlockSpec((n_chunks, n, chunk), lambda i: (0, 0, 0)), st, st],
        out_shape=[jax.ShapeDtypeStruct((n_chunks, n, chunk), F32),
                   jax.ShapeDtypeStruct((nseq, STATE_COLS), F32), jax.ShapeDtypeStruct((nseq, STATE_COLS), F32)],
        scratch_shapes=[pltpu.VMEM((n, SSM_WIDTH), F32), pltpu.VMEM((N_QUARTERS, n, 2 * Q_STATE), F32),
                        pltpu.VMEM((N_QUARTERS, n, LANES), F32)],
        compiler_params=pltpu.CompilerParams(dimension_semantics=("arbitrary",), vmem_limit_bytes=VMEM_LIMIT),
        name="ssm_sample",
    )(u_rows, u_rows, u_rows, u_rows, h0r, h0i, sp["wq"], sp["lr"], sp["li"], sp["cq"], sp["d"], sp["wgluq"], sp["bglu"],
      oa_rows, ga, gs, wout_chunks)


def _ffn(h, gf_ref, wg_ref, wu_ref, wd_ref):
    f = _rms(h, gf_ref[...]).astype(BF16)
    gate = jnp.dot(f, wg_ref[...], preferred_element_type=F32)
    up = jnp.dot(f, wu_ref[...], preferred_element_type=F32)
    a = (jax.nn.silu(gate) * up).astype(BF16)
    return h + jnp.dot(a, wd_ref[...], preferred_element_type=F32)


def _ffn_kernel(xp_ref, xs_ref, hpp_ref, hps_ref, gf_ref, wg_ref, wu_ref, wd_ref, yp_ref, ys_ref, *, n_prompt_tiles):
    i = pl.program_id(0)

    def tile(x_ref, hp_ref, y_ref):
        hp = jnp.concatenate([hp_ref[c] for c in range(hp_ref.shape[0])], axis=-1)
        y_ref[...] = _ffn(x_ref[...] + hp, gf_ref, wg_ref, wu_ref, wd_ref)

    @pl.when(i < n_prompt_tiles)
    def _():
        tile(xp_ref, hpp_ref, yp_ref)

    @pl.when(i >= n_prompt_tiles)
    def _():
        tile(xs_ref, hps_ref, ys_ref)


def _ffn_call(xp_rows, hpp_rows, xs_rows, hps_rows, gf, wg, wu, wd, tm):
    n_p, n_s = xp_rows.shape[0], xs_rows.shape[0]
    assert n_p % tm == 0 and n_s % tm == 0
    tp = n_p // tm
    prompt = lambda i: jnp.minimum(i, tp - 1)
    sample = lambda i: jnp.maximum(i - tp, 0)
    row = lambda pick: pl.BlockSpec((tm, D_MODEL), lambda i: (pick(i), 0))
    hp = lambda a, pick: pl.BlockSpec((a.shape[0], tm, a.shape[2]), lambda i: (0, pick(i), 0))
    return pl.pallas_call(
        functools.partial(_ffn_kernel, n_prompt_tiles=tp),
        grid=(tp + n_s // tm,),
        in_specs=[row(prompt), row(sample), hp(hpp_rows, prompt), hp(hps_rows, sample), _const_spec((1, D_MODEL)),
                  _const_spec((D_MODEL, D_FF)), _const_spec((D_MODEL, D_FF)), _const_spec((D_FF, D_MODEL))],
        out_specs=[row(prompt), row(sample)],
        out_shape=[jax.ShapeDtypeStruct((n_p, D_MODEL), F32), jax.ShapeDtypeStruct((n_s, D_MODEL), F32)],
        compiler_params=pltpu.CompilerParams(dimension_semantics=("arbitrary",), vmem_limit_bytes=VMEM_LIMIT),
        name="ffn",
    )(xp_rows, xs_rows, hpp_rows, hps_rows, gf, wg, wu, wd)


N_SEQ = 8
ROWS = N_SEQ * BLOCK
PITCH = BLOCK + SUBLANES
CHUNK_T = BLOCK // N_SEQ


def _interleave(*stages):
    keyed = [((i + 0.5) / len(steps), k, i, step) for k, steps in enumerate(stages) for i, step in enumerate(steps)]
    return [step for _, _, _, step in sorted(keyed, key=lambda e: e[:3])]


def _mixer_kernel(sink_ref, x_ref, meta_ref, gmix_ref, win_ref, gq_ref, gk_ref,
                  wq_ref, lr_ref, li_ref, cq_ref, d_ref, wglu_ref, bglu_ref, ga_ref, gs_ref, wout_ref,
                  hp_ref, kw_ref, vw_ref, hr_ref, hi_ref,
                  qs_ref, kk_ref, vv_ref, s_ref, p_ref, oatt_ref, usl_ref, xs_ref, hb_ref, st_ref, osl_ref, mix_ref):
    i = pl.program_id(0)
    n_blocks = pl.num_programs(0) - 1

    @pl.when(i < n_blocks)
    def _():
        _mixer_block(i, sink_ref, x_ref, meta_ref, gmix_ref, win_ref, gq_ref, gk_ref,
                     wq_ref, lr_ref, li_ref, cq_ref, d_ref, wglu_ref, bglu_ref, ga_ref, gs_ref, wout_ref,
                     hp_ref, kw_ref, vw_ref, hr_ref, hi_ref,
                     qs_ref, kk_ref, vv_ref, s_ref, p_ref, oatt_ref, usl_ref, xs_ref, hb_ref, st_ref, osl_ref, mix_ref)

    @pl.when(i == n_blocks)
    def _():
        for step in _output_norm_steps(oatt_ref, osl_ref, ga_ref, gs_ref, mix_ref):
            step()
        for c in range(OUT_CHUNKS):
            _project_chunk(mix_ref, wout_ref, hp_ref, c)


OUT_CHUNKS = 4
OUT_CHUNK = D_MODEL // OUT_CHUNKS


def _project_chunk(mix_ref, wout_ref, hp_ref, c):
    hp_ref[c] = jnp.dot(mix_ref[...], wout_ref[c], preferred_element_type=F32).reshape(N_SEQ, BLOCK, OUT_CHUNK)


def _output_norm_steps(oatt_ref, osl_ref, ga_ref, gs_ref, mix_ref):
    def one(b):
        rows = slice(b * BLOCK, (b + 1) * BLOCK)
        mix_ref[rows, 0:ATTN_WIDTH] = _rms(oatt_ref[b], ga_ref[...]).astype(BF16)
        sl = [osl_ref[j, b * PITCH:b * PITCH + BLOCK, :] for j in range(N_QUARTERS)]
        ms = sum(jnp.sum(s * s, axis=-1, keepdims=True) for s in sl) / SSM_WIDTH
        inv = lax.rsqrt(ms + EPS)
        for j in range(N_QUARTERS):
            mix_ref[rows, ATTN_WIDTH + j * LANES:ATTN_WIDTH + (j + 1) * LANES] = (
                sl[j] * inv * gs_ref[:, j * LANES:(j + 1) * LANES]).astype(BF16)
    return [functools.partial(one, b) for b in range(N_SEQ)]


def _mixer_block(i, sink_ref, x_ref, meta_ref, gmix_ref, win_ref, gq_ref, gk_ref,
                 wq_ref, lr_ref, li_ref, cq_ref, d_ref, wglu_ref, bglu_ref, ga_ref, gs_ref, wout_ref,
                 hp_ref, kw_ref, vw_ref, hr_ref, hi_ref,
                 qs_ref, kk_ref, vv_ref, s_ref, p_ref, oatt_ref, usl_ref, xs_ref, hb_ref, st_ref, osl_ref, mix_ref):
    left = _left_half()
    project = lambda c: [functools.partial(_project_chunk, mix_ref, wout_ref, hp_ref, c)]

    @pl.when(i == 0)
    def _():
        oatt_ref[...] = jnp.zeros((N_SEQ, BLOCK, ATTN_WIDTH), F32)
        osl_ref[...] = jnp.zeros((N_QUARTERS, N_SEQ * PITCH, LANES), F32)
        xm = _rms(meta_ref[...], gmix_ref[...]).astype(BF16)
        zm = jnp.dot(xm, win_ref[:, ATTN_WIDTH:], preferred_element_type=F32)
        km = _pair_norm(zm[:, :KV_WIDTH], gk_ref[...], left).astype(BF16)
        vm = zm[:, KV_WIDTH:2 * KV_WIDTH].astype(BF16)
        lead = jnp.zeros((BLOCK - N_META, KV_WIDTH), BF16)
        for b in range(N_SEQ):
            kk_ref[b, 0:BLOCK - N_META, :] = lead
            vv_ref[b, 0:BLOCK - N_META, :] = lead
            kk_ref[b, BLOCK - N_META:BLOCK, :] = km
            vv_ref[b, BLOCK - N_META:BLOCK, :] = vm
        um = zm[:, 2 * KV_WIDTH:]
        for q in range(N_QUARTERS):
            xm_q = jnp.dot(um[:, q * LANES:(q + 1) * LANES].astype(BF16), wq_ref[q], preferred_element_type=F32)
            lr = jnp.broadcast_to(lr_ref[:, q * Q_STATE:(q + 1) * Q_STATE], (SUBLANES, Q_STATE))
            li = jnp.broadcast_to(li_ref[:, q * Q_STATE:(q + 1) * Q_STATE], (SUBLANES, Q_STATE))
            hr = jnp.zeros((SUBLANES, Q_STATE), F32)
            hi = jnp.zeros((SUBLANES, Q_STATE), F32)
            for t in range(N_META):
                xr = jnp.broadcast_to(xm_q[t:t + 1, 0:Q_STATE], (SUBLANES, Q_STATE))
                xi = jnp.broadcast_to(xm_q[t:t + 1, Q_STATE:], (SUBLANES, Q_STATE))
                hr, hi = _scan_step(lr, li, hr, hi, xr, xi)
            st_ref[:, 2 * q * Q_STATE:(2 * q + 1) * Q_STATE] = hr
            st_ref[:, (2 * q + 1) * Q_STATE:(2 * q + 2) * Q_STATE] = hi

    p1 = {}

    def pre_norm():
        p1["xn"] = _rms(x_ref[...].reshape(ROWS, D_MODEL), gmix_ref[...]).astype(BF16)

    def project_in(key, c0):
        p1[key] = jnp.dot(p1["xn"], win_ref[:, c0:c0 + 2 * LANES], preferred_element_type=F32)

    def finish_q(c):
        zz = p1.pop(("q", c))
        for tt in range(2):
            t = 2 * c + tt
            qn = _pair_norm(zz[:, tt * LANES:(tt + 1) * LANES], gq_ref[...], left) * ATTN_SCALE
            qa = jnp.where(left, qn, 0.0).astype(BF16)
            qb = jnp.where(left, 0.0, qn).astype(BF16)
            for b in range(N_SEQ):
                qs_ref[b, 2 * t * BLOCK:(2 * t + 1) * BLOCK, :] = qa[b * BLOCK:(b + 1) * BLOCK]
                qs_ref[b, (2 * t + 1) * BLOCK:(2 * t + 2) * BLOCK, :] = qb[b * BLOCK:(b + 1) * BLOCK]

    def finish_kv():
        zz = p1.pop("kv")
        kn = _pair_norm(zz[:, :LANES], gk_ref[...], left)
        vn = zz[:, LANES:]
        kw_ref[...] = kn.reshape(N_SEQ, BLOCK, KV_WIDTH)
        vw_ref[...] = vn.reshape(N_SEQ, BLOCK, KV_WIDTH)
        kk_ref[:, BLOCK:, :] = kn.astype(BF16).reshape(N_SEQ, BLOCK, KV_WIDTH)
        vv_ref[:, BLOCK:, :] = vn.astype(BF16).reshape(N_SEQ, BLOCK, KV_WIDTH)

    def finish_u(c):
        zz = p1.pop(("u", c))
        for tt in range(2):
            for b in range(N_SEQ):
                usl_ref[2 * c + tt, b * PITCH:b * PITCH + BLOCK, :] = zz[b * BLOCK:(b + 1) * BLOCK, tt * LANES:(tt + 1) * LANES]

    dot_kv = functools.partial(project_in, "kv", ATTN_WIDTH)
    dot_q = [functools.partial(project_in, ("q", c), c * 2 * LANES) for c in range(2)]
    dot_u = [functools.partial(project_in, ("u", c), ATTN_WIDTH + 2 * KV_WIDTH + c * 2 * LANES) for c in range(2)]
    post_q = [functools.partial(finish_q, c) for c in range(2)]
    post_u = [functools.partial(finish_u, c) for c in range(2)]

    r = lax.broadcasted_iota(jnp.int32, (BLOCK, 2 * BLOCK), 0)
    c = lax.broadcasted_iota(jnp.int32, (BLOCK, 2 * BLOCK), 1)
    valid = (c >= r) & (c <= r + WINDOW) & ((i > 0) | (c >= BLOCK - N_META))
    bias = jnp.where(valid, 0.0, -jnp.inf)

    def gather_u(n):
        t0 = n * CHUNK_T
        return jnp.concatenate(
            [jnp.concatenate([usl_ref[j, pl.ds(t0 + tl, SUBLANES, stride=PITCH), :] for j in range(N_QUARTERS)], axis=1)
             for tl in range(CHUNK_T)], axis=0)

    def feed(n, par):
        env = {}

        def scores():
            s_ref[par] = lax.dot_general(qs_ref[n], kk_ref[n], (((1,), (1,)), ((), ())), preferred_element_type=F32)

        def gather():
            env["u"] = gather_u(n)

        def scan_inputs(q):
            xs_ref[par, :, 2 * q * Q_STATE:(2 * q + 2) * Q_STATE] = jnp.dot(
                env["u"][:, q * LANES:(q + 1) * LANES].astype(BF16), wq_ref[q], preferred_element_type=F32)

        return [scores, gather] + [functools.partial(scan_inputs, q) for q in range(N_QUARTERS)]

    def mid(n, par):
        env = {}

        def row_max(g):
            s = s_ref[par, g * BLOCK:(g + 1) * BLOCK, :] + bias
            env[g] = (s, jnp.maximum(jnp.max(s, axis=-1, keepdims=True), sink_ref[g // 2 + Q_PER_KV * (g % 2)]))

        def exponent(g):
            s, m = env[g]
            p = jnp.exp(s - m)
            sink = sink_ref[g // 2 + Q_PER_KV * (g % 2)]
            env[g] = (p, 1.0 / (jnp.sum(p, axis=-1, keepdims=True) + jnp.exp(sink - m)))

        def normalise(g):
            p, inv = env.pop(g)
            p_ref[par, g * BLOCK:(g + 1) * BLOCK, :] = (p * inv).astype(BF16)

        softmax = [functools.partial(f, g) for g in range(2 * Q_TILES) for f in (row_max, exponent, normalise)]

        def load_state():
            for q in range(N_QUARTERS):
                env["lam", q] = (jnp.broadcast_to(lr_ref[:, q * Q_STATE:(q + 1) * Q_STATE], (SUBLANES, Q_STATE)),
                                 jnp.broadcast_to(li_ref[:, q * Q_STATE:(q + 1) * Q_STATE], (SUBLANES, Q_STATE)))
                env["h", q] = (st_ref[:, 2 * q * Q_STATE:(2 * q + 1) * Q_STATE],
                               st_ref[:, (2 * q + 1) * Q_STATE:(2 * q + 2) * Q_STATE])

        def scan_pair(tp, q):
            lr, li = env["lam", q]
            hr, hi = env["h", q]
            re0, im0 = 2 * q * Q_STATE, (2 * q + 1) * Q_STATE
            pair_r, pair_i = [], []
            for tl in (2 * tp, 2 * tp + 1):
                xr = xs_ref[par, tl * SUBLANES:(tl + 1) * SUBLANES, re0:re0 + Q_STATE]
                xi = xs_ref[par, tl * SUBLANES:(tl + 1) * SUBLANES, im0:im0 + Q_STATE]
                hr, hi = _scan_step(lr, li, hr, hi, xr, xi)
                pair_r.append(hr)
                pair_i.append(hi)
            env["h", q] = (hr, hi)
            rows = slice(2 * tp * SUBLANES, (2 * tp + 2) * SUBLANES)
            hb_ref[par, rows, re0:re0 + Q_STATE] = jnp.concatenate(pair_r, axis=0).astype(BF16)
            hb_ref[par, rows, im0:im0 + Q_STATE] = jnp.concatenate(pair_i, axis=0).astype(BF16)

        def store_state():
            for q in range(N_QUARTERS):
                hr, hi = env["h", q]
                st_ref[:, 2 * q * Q_STATE:(2 * q + 1) * Q_STATE] = hr
                st_ref[:, (2 * q + 1) * Q_STATE:(2 * q + 2) * Q_STATE] = hi

        scan = ([load_state] + [functools.partial(scan_pair, tp, q) for tp in range(CHUNK_T // 2) for q in range(N_QUARTERS)]
                + [store_state])
        return softmax, scan

    def tail(n, par):
        env = {}

        def attn_out():
            o_all = jnp.dot(p_ref[par], vv_ref[n], preferred_element_type=F32)
            for t in range(Q_TILES):
                oatt_ref[n, :, t * LANES:(t + 1) * LANES] = jnp.where(
                    left, o_all[2 * t * BLOCK:(2 * t + 1) * BLOCK], o_all[(2 * t + 1) * BLOCK:(2 * t + 2) * BLOCK])

        def gather():
            env["u"] = gather_u(n)

        def readout(q):
            env["y", q] = jnp.dot(hb_ref[par, :, 2 * q * Q_STATE:(2 * q + 2) * Q_STATE], cq_ref[q],
                                  preferred_element_type=F32)

        def activate(q):
            cols = slice(q * LANES, (q + 1) * LANES)
            env["g", q] = jax.nn.gelu(env.pop(("y", q)) + d_ref[:, cols] * env["u"][:, cols])

        def gate(q):
            cols = slice(q * LANES, (q + 1) * LANES)
            env["gate", q] = jnp.dot(env["g", q].astype(BF16), wglu_ref[q], preferred_element_type=F32) + bglu_ref[:, cols]

        def emit(q):
            o = env.pop(("g", q)) * jax.nn.sigmoid(env.pop(("gate", q)))
            t0 = n * CHUNK_T
            for tl in range(CHUNK_T):
                osl_ref[q, pl.ds(t0 + tl, SUBLANES, stride=PITCH), :] = o[tl * SUBLANES:(tl + 1) * SUBLANES]

        per_quarter = lambda f: [functools.partial(f, q) for q in range(N_QUARTERS)]
        return [attn_out, gather] + per_quarter(readout) + per_quarter(activate) + per_quarter(gate) + per_quarter(emit)

    def run(*stages):
        for step in _interleave(*stages):
            step()

    def steady(k, carry):
        n = 2 * k + 1
        run(feed(n + 1, 0), *mid(n, 1), tail(n - 1, 0), project(k))
        run(feed(n + 2, 1), *mid(n + 1, 0), tail(n, 1))
        return carry

    norms = _output_norm_steps(oatt_ref, osl_ref, ga_ref, gs_ref, mix_ref)
    feed0, feed1 = feed(0, 0), feed(1, 1)
    softmax0, scan0 = mid(0, 0)
    half = len(softmax0) // 2
    run([pre_norm, dot_kv, dot_q[0], finish_kv, dot_q[1], post_q[0], dot_u[0], post_q[1]], norms)
    run(feed0[:1])
    run([dot_u[1]] + feed1[:1] + post_u, softmax0[:half])
    run(feed0[1:], softmax0[half:])
    run(feed1[1:], scan0)
    n_steady = (N_SEQ - 2) // 2
    assert n_steady == OUT_CHUNKS - 1
    lax.fori_loop(0, n_steady, steady, 0)
    run(*mid(N_SEQ - 1, 1), tail(N_SEQ - 2, 0), project(OUT_CHUNKS - 1))
    run(tail(N_SEQ - 1, 1))
    kk_ref[:, 0:BLOCK, :] = kk_ref[:, BLOCK:, :]
    vv_ref[:, 0:BLOCK, :] = vv_ref[:, BLOCK:, :]

    hr_ref[...] = jnp.concatenate([st_ref[:, 2 * q * Q_STATE:(2 * q + 1) * Q_STATE] for q in range(N_QUARTERS)], axis=1)
    hi_ref[...] = jnp.concatenate([st_ref[:, (2 * q + 1) * Q_STATE:(2 * q + 2) * Q_STATE] for q in range(N_QUARTERS)], axis=1)


def _mixer(x_prompt, meta_tokens, sinks, gmix, win, gq2, gk2, sp, ga, gs, wout):
    nseq, seq, _ = x_prompt.shape
    assert nseq == N_SEQ and seq % BLOCK == 0
    n_blocks = seq // BLOCK
    st = pl.BlockSpec((N_SEQ, STATE_COLS), lambda i: (0, 0))
    kvw = pl.BlockSpec((N_SEQ, BLOCK, KV_WIDTH), lambda i: (0, 0, 0))
    return pl.pallas_call(
        _mixer_kernel,
        grid=(n_blocks + 1,),
        in_specs=[pl.BlockSpec(memory_space=pltpu.SMEM),
                  pl.BlockSpec((N_SEQ, BLOCK, D_MODEL), lambda i: (0, jnp.minimum(i, n_blocks - 1), 0)),
                  _const_spec((N_META, D_MODEL)),
                  _const_spec((1, D_MODEL)), _const_spec((D_MODEL, IN_COLS)), _const_spec((1, LANES)),
                  _const_spec((1, LANES)),
                  _const_spec((N_QUARTERS, LANES, 2 * Q_STATE)), _const_spec((1, STATE_COLS)),
                  _const_spec((1, STATE_COLS)), _const_spec((N_QUARTERS, 2 * Q_STATE, LANES)),
                  _const_spec((1, SSM_WIDTH)), _const_spec((N_QUARTERS, LANES, LANES)), _const_spec((1, SSM_WIDTH)),
                  _const_spec((1, ATTN_WIDTH)), _const_spec((1, SSM_WIDTH)),
                  _const_spec((OUT_CHUNKS, D_MODEL, OUT_CHUNK))],
        out_specs=[pl.BlockSpec((OUT_CHUNKS, N_SEQ, BLOCK, OUT_CHUNK), lambda i: (0, 0, jnp.maximum(i - 1, 0), 0)),
                   kvw, kvw, st, st],
        out_shape=[jax.ShapeDtypeStruct((OUT_CHUNKS, N_SEQ, seq, OUT_CHUNK), F32),
                   jax.ShapeDtypeStruct((N_SEQ, BLOCK, KV_WIDTH), F32), jax.ShapeDtypeStruct((N_SEQ, BLOCK, KV_WIDTH), F32),
                   jax.ShapeDtypeStruct((N_SEQ, STATE_COLS), F32), jax.ShapeDtypeStruct((N_SEQ, STATE_COLS), F32)],
        scratch_shapes=[pltpu.VMEM((N_SEQ, 2 * Q_TILES * BLOCK, LANES), BF16),
                        pltpu.VMEM((N_SEQ, 2 * BLOCK, KV_WIDTH), BF16),
                        pltpu.VMEM((N_SEQ, 2 * BLOCK, KV_WIDTH), BF16),
                        pltpu.VMEM((2, 2 * Q_TILES * BLOCK, 2 * BLOCK), F32),
                        pltpu.VMEM((2, 2 * Q_TILES * BLOCK, 2 * BLOCK), BF16),
                        pltpu.VMEM((N_SEQ, BLOCK, ATTN_WIDTH), F32),
                        pltpu.VMEM((N_QUARTERS, N_SEQ * PITCH, LANES), F32),
                        pltpu.VMEM((2, CHUNK_T * N_SEQ, 2 * STATE_COLS), F32),
                        pltpu.VMEM((2, CHUNK_T * N_SEQ, 2 * STATE_COLS), BF16),
                        pltpu.VMEM((N_SEQ, 2 * STATE_COLS), F32),
                        pltpu.VMEM((N_QUARTERS, N_SEQ * PITCH, LANES), F32),
                        pltpu.VMEM((ROWS, D_MODEL), BF16)],
        compiler_params=pltpu.CompilerParams(dimension_semantics=("arbitrary",), vmem_limit_bytes=VMEM_LIMIT),
        name="mixer",
    )(sinks, x_prompt, meta_tokens, gmix, win, gq2, gk2, sp["wq"], sp["lr"], sp["li"], sp["cq"], sp["d"],
      sp["wgluq"], sp["bglu"], ga, gs, wout)


def kernel(x_prompt, x_sample, cache_k_win, cache_v_win, state_ssm_re, state_ssm_im, meta_tokens, g_mix, w_in, g_q,
           g_k, sinks, ssm_a_re, ssm_a_im, ssm_log_dt, ssm_b_re, ssm_b_im, ssm_c_re, ssm_c_im, ssm_d, ssm_w_glu,
           ssm_b_glu, g_att_out, g_ssm_out, w_out, g_ffn, w_gate, w_up, w_down):
    bp, seq, _ = x_prompt.shape
    db, dseq, _ = x_sample.shape
    li = 0
    gmix = g_mix[li].reshape(1, D_MODEL)
    win = w_in[li].astype(BF16)
    gq2 = jnp.tile(g_q[li], 2).reshape(1, LANES)
    gk2 = jnp.tile(g_k[li], 2).reshape(1, LANES)
    sk = sinks[li]
    sp = _ssm_params(ssm_a_re[li], ssm_a_im[li], ssm_log_dt[li], ssm_b_re[li], ssm_b_im[li], ssm_c_re[li],
                     ssm_c_im[li], ssm_d[li], ssm_w_glu[li], ssm_b_glu[li])
    ga = g_att_out[li].reshape(1, ATTN_WIDTH)
    gs = g_ssm_out[li].reshape(1, SSM_WIDTH)
    wout = w_out[li].astype(BF16)
    ffn_w = (g_ffn[li].reshape(1, D_MODEL), w_gate[li].astype(BF16), w_up[li].astype(BF16), w_down[li].astype(BF16))
    regroup = lambda a, axis: jnp.swapaxes(
        a.reshape(a.shape[:axis] + (2, Q_PER_KV, HEAD_DIM) + a.shape[axis + 1:]), axis, axis + 1).reshape(a.shape)
    win = jnp.concatenate([regroup(win[:, :ATTN_WIDTH], 1), win[:, ATTN_WIDTH:]], axis=1)
    wout = jnp.concatenate([regroup(wout[:ATTN_WIDTH], 0), wout[ATTN_WIDTH:]], axis=0)
    ga = regroup(ga, 1)

    wout_chunks = wout.reshape(D_MODEL, OUT_CHUNKS, OUT_CHUNK).transpose(1, 0, 2)
    hproj, kw_p, vw_p, hp_r, hp_i = _mixer(x_prompt, meta_tokens, sk, gmix, win, gq2, gk2, sp, ga, gs, wout_chunks)

    n_s = db * dseq
    xs_rows = x_sample.reshape(n_s, D_MODEL)
    q_s, k_s, v_s, u_s = _front(xs_rows, gmix, win, gq2, gk2, tm=512)
    ck = cache_k_win[li].reshape(db, WINDOW, KV_WIDTH)
    cv = cache_v_win[li].reshape(db, WINDOW, KV_WIDTH)
    oa_s, kw_s, vw_s = _attn_sample(q_s.reshape(db, dseq, ATTN_WIDTH), k_s.reshape(db, dseq, KV_WIDTH),
                                    v_s.reshape(db, dseq, KV_WIDTH), ck, cv, sk)
    hproj_s, hs_r, hs_i = _ssm_sample(u_s, state_ssm_re[li].reshape(db, STATE_COLS),
                                      state_ssm_im[li].reshape(db, STATE_COLS), sp, oa_s.reshape(n_s, ATTN_WIDTH),
                                      ga, gs, wout_chunks, nseq=db, t_steps=dseq)

    y_prompt, y_sample = _ffn_call(x_prompt.reshape(bp * seq, D_MODEL), hproj.reshape(OUT_CHUNKS, bp * seq, OUT_CHUNK),
                                   xs_rows, hproj_s, *ffn_w, tm=512)
    y_prompt = y_prompt.reshape(bp, seq, D_MODEL)
    y_sample = y_sample.reshape(db, dseq, D_MODEL)

    kv5 = lambda a, n: a.reshape(1, n, WINDOW, N_KV_HEADS, HEAD_DIM)
    st4 = lambda a, n: a.reshape(1, n, N_SSM_GROUPS, SSM_STATE)
    return (y_prompt, y_sample, kv5(kw_p, bp), kv5(vw_p, bp), st4(hp_r, bp), st4(hp_i, bp),
            kv5(kw_s, db), kv5(vw_s, db), st4(hs_r, db), st4(hs_i, db))
```

```python
import functools

import jax
import jax.numpy as jnp
from jax import lax
from jax.experimental import pallas as pl
from jax.experimental.pallas import tpu as pltpu

D_MODEL = 1024
N_META = 16
HEAD_DIM = 64
ATTN_WIDTH = 512
Q_PER_KV = 4
N_KV_HEADS = 2
KV_WIDTH = 128
WINDOW = 128
BLOCK = 128
SSM_WIDTH = 512
SSM_GROUP = 16
N_SSM_GROUPS = 32
SSM_STATE = 64
IN_COLS = ATTN_WIDTH + 2 * KV_WIDTH + SSM_WIDTH
D_FF = 2816
EPS = 1e-6
ATTN_SCALE = HEAD_DIM ** -0.5
STATE_COLS = N_SSM_GROUPS * SSM_STATE
LANES = 128
SUBLANES = 8
VMEM_LIMIT = 56 * 1024 * 1024

F32 = jnp.float32
BF16 = jnp.bfloat16


def _const_spec(shape):
    return pl.BlockSpec(shape, lambda *_: (0,) * len(shape), pipeline_mode=pl.Buffered(1))


def _rms(x, g):
    return x * lax.rsqrt(jnp.mean(x * x, axis=-1, keepdims=True) + EPS) * g


def _left_half():
    return lax.broadcasted_iota(jnp.int32, (1, LANES), 1) < HEAD_DIM


def _pair_norm(zz, g2, left):
    sq = zz * zz
    sl = jnp.sum(jnp.where(left, sq, 0.0), axis=-1, keepdims=True)
    sr = jnp.sum(jnp.where(left, 0.0, sq), axis=-1, keepdims=True)
    inv = jnp.where(left, lax.rsqrt(sl / HEAD_DIM + EPS), lax.rsqrt(sr / HEAD_DIM + EPS))
    return zz * inv * g2


def _front_kernel(x_ref, gmix_ref, win_ref, gq_ref, gk_ref, q_ref, k_ref, v_ref, u_ref):
    xn = _rms(x_ref[...], gmix_ref[...]).astype(BF16)
    z = jnp.dot(xn, win_ref[...], preferred_element_type=F32)
    left = _left_half()
    for p in range(ATTN_WIDTH // LANES):
        q_ref[:, p * LANES:(p + 1) * LANES] = _pair_norm(z[:, p * LANES:(p + 1) * LANES], gq_ref[...], left)
    k_ref[...] = _pair_norm(z[:, ATTN_WIDTH:ATTN_WIDTH + KV_WIDTH], gk_ref[...], left)
    v_ref[...] = z[:, ATTN_WIDTH + KV_WIDTH:ATTN_WIDTH + 2 * KV_WIDTH]
    u_ref[...] = z[:, ATTN_WIDTH + 2 * KV_WIDTH:]


def _front(x_rows, gmix, win_bf, gq2, gk2, tm):
    n = x_rows.shape[0]
    assert n % tm == 0
    row = lambda w: pl.BlockSpec((tm, w), lambda i: (i, 0))
    return pl.pallas_call(
        _front_kernel,
        grid=(n // tm,),
        in_specs=[row(D_MODEL), _const_spec((1, D_MODEL)), _const_spec((D_MODEL, IN_COLS)),
                  _const_spec((1, LANES)), _const_spec((1, LANES))],
        out_specs=[row(ATTN_WIDTH), row(KV_WIDTH), row(KV_WIDTH), row(SSM_WIDTH)],
        out_shape=[jax.ShapeDtypeStruct((n, ATTN_WIDTH), F32), jax.ShapeDtypeStruct((n, KV_WIDTH), F32),
                   jax.ShapeDtypeStruct((n, KV_WIDTH), F32), jax.ShapeDtypeStruct((n, SSM_WIDTH), F32)],
        compiler_params=pltpu.CompilerParams(dimension_semantics=("arbitrary",), vmem_limit_bytes=VMEM_LIMIT),
        name="front",
    )(x_rows, gmix, win_bf, gq2, gk2)


Q_TILES = ATTN_WIDTH // LANES


def _attn_sample_kernel(sink_ref, q_ref, kn_ref, vn_ref, ck_ref, cv_ref, o_ref, kw_ref, vw_ref, *, bb, t):
    left = _left_half()
    n_heads = 2 * Q_TILES
    rows = n_heads * t
    tk = WINDOW + t
    r = lax.broadcasted_iota(jnp.int32, (rows, tk), 0) % t
    c = lax.broadcasted_iota(jnp.int32, (rows, tk), 1)
    bias = jnp.where((c >= r) & (c <= r + WINDOW), 0.0, -jnp.inf)
    hrow = lax.broadcasted_iota(jnp.int32, (rows, 1), 0) // t
    sink = jnp.zeros((rows, 1), F32)
    for g in range(n_heads):
        sink = jnp.where(hrow == g, sink_ref[g // 2 + Q_PER_KV * (g % 2)], sink)
    scores, values = [], []
    for bi in range(bb):
        kk = jnp.concatenate([ck_ref[bi], kn_ref[bi]], axis=0)
        vv = jnp.concatenate([cv_ref[bi], vn_ref[bi]], axis=0)
        kw_ref[bi] = kk[t:]
        vw_ref[bi] = vv[t:]
        q = q_ref[bi] * ATTN_SCALE
        pieces = []
        for tile in range(Q_TILES):
            qt = q[:, tile * LANES:(tile + 1) * LANES]
            pieces += [jnp.where(left, qt, 0.0), jnp.where(left, 0.0, qt)]
        qs = jnp.concatenate(pieces, axis=0).astype(BF16)
        scores.append(lax.dot_general(qs, kk.astype(BF16), (((1,), (1,)), ((), ())), preferred_element_type=F32))
        values.append(vv.astype(BF16))
    probs = []
    for bi in range(bb):
        s = scores[bi] + bias
        m = jnp.maximum(jnp.max(s, axis=-1, keepdims=True), sink)
        p = jnp.exp(s - m)
        inv = 1.0 / (jnp.sum(p, axis=-1, keepdims=True) + jnp.exp(sink - m))
        probs.append((p * inv).astype(BF16))
    for bi in range(bb):
        o = jnp.dot(probs[bi], values[bi], preferred_element_type=F32)
        for tile in range(Q_TILES):
            o_ref[bi, :, tile * LANES:(tile + 1) * LANES] = jnp.where(
                left, o[2 * tile * t:(2 * tile + 1) * t], o[(2 * tile + 1) * t:(2 * tile + 2) * t])


def _attn_sample(q, kn, vn, ck, cv, sinks, bb=16):
    db, t, _ = q.shape
    blk = lambda r, w: pl.BlockSpec((bb, r, w), lambda i: (i, 0, 0))
    return pl.pallas_call(
        functools.partial(_attn_sample_kernel, bb=bb, t=t),
        grid=(db // bb,),
        in_specs=[pl.BlockSpec(memory_space=pltpu.SMEM), blk(t, ATTN_WIDTH), blk(t, KV_WIDTH), blk(t, KV_WIDTH),
                  blk(WINDOW, KV_WIDTH), blk(WINDOW, KV_WIDTH)],
        out_specs=[blk(t, ATTN_WIDTH), blk(WINDOW, KV_WIDTH), blk(WINDOW, KV_WIDTH)],
        out_shape=[jax.ShapeDtypeStruct((db, t, ATTN_WIDTH), F32),
                   jax.ShapeDtypeStruct((db, WINDOW, KV_WIDTH), F32),
                   jax.ShapeDtypeStruct((db, WINDOW, KV_WIDTH), F32)],
        compiler_params=pltpu.CompilerParams(dimension_semantics=("arbitrary",)),
        name="attn_sample",
    )(sinks, q, kn, vn, ck, cv)


N_QUARTERS = 4
Q_GROUPS = N_SSM_GROUPS // N_QUARTERS
Q_STATE = Q_GROUPS * SSM_STATE


def _zoh_kernel(are_ref, aim_ref, logdt_ref, bre_ref, bim_ref, lr_ref, li_ref, bbr_ref, bbi_ref):
    ar, ai = are_ref[...], aim_ref[...]
    dt = jnp.exp(logdt_ref[...])
    mag = jnp.exp(ar * dt)
    lr, li = mag * jnp.cos(ai * dt), mag * jnp.sin(ai * dt)
    nr, ni = lr - 1.0, li
    den = ar * ar + ai * ai
    fr, fi = ((nr * ar + ni * ai) / den)[:, None, :], ((ni * ar - nr * ai) / den)[:, None, :]
    br, bi = bre_ref[...], bim_ref[...]
    lr_ref[...] = lr
    li_ref[...] = li
    bbr_ref[...] = fr * br - fi * bi
    bbi_ref[...] = fr * bi + fi * br


def _ssm_params(a_re, a_im, log_dt, b_re, b_im, c_re, c_im, d_skip, w_glu, b_glu):
    g, p, h = b_re.shape
    lr, li, bbr, bbi = pl.pallas_call(
        _zoh_kernel,
        out_shape=[jax.ShapeDtypeStruct((g, p), F32), jax.ShapeDtypeStruct((g, p), F32),
                   jax.ShapeDtypeStruct((g, h, p), F32), jax.ShapeDtypeStruct((g, h, p), F32)],
        name="zoh",
    )(a_re, a_im, log_dt.reshape(g, 1), jnp.swapaxes(b_re, 1, 2), jnp.swapaxes(b_im, 1, 2))
    eye = jnp.eye(Q_GROUPS, dtype=F32)
    quartered = lambda a: a.reshape((a.shape[0], N_QUARTERS, Q_GROUPS) + a.shape[2:])
    wq = jnp.einsum('rqghp,gm->qghrmp', quartered(jnp.stack([bbr, bbi])), eye).reshape(N_QUARTERS, LANES, 2 * Q_STATE)
    cq = jnp.einsum('rqghp,gm->qrgpmh', quartered(jnp.stack([c_re, -c_im])), eye).reshape(N_QUARTERS, 2 * Q_STATE, LANES)
    wglu = jnp.einsum('qghk,gm->qghmk', w_glu.reshape(N_QUARTERS, Q_GROUPS, SSM_GROUP, SSM_GROUP), eye)
    return dict(wq=wq.astype(BF16), cq=cq.astype(BF16), wgluq=wglu.reshape(N_QUARTERS, LANES, LANES).astype(BF16),
                lr=lr.reshape(1, STATE_COLS), li=li.reshape(1, STATE_COLS), d=d_skip.reshape(1, SSM_WIDTH),
                bglu=b_glu.reshape(1, SSM_WIDTH))


def _scan_step(lr, li, hr, hi, xr, xi):
    return lr * hr - li * hi + xr, lr * hi + li * hr + xi


def _ssm_sample_kernel(u0_ref, u1_ref, u2_ref, u3_ref, h0r_ref, h0i_ref, wq_ref, lr_ref, li_ref, cq_ref, d_ref,
                       wglu_ref, bglu_ref, o_ref, hr_ref, hi_ref, utb_ref, xs_ref, *, nseq, t_steps):
    u_refs = (u0_ref, u1_ref, u2_ref, u3_ref)
    quarters = range(N_QUARTERS)
    n_rg = nseq // SUBLANES
    rg_rows = SUBLANES * t_steps

    for t in range(t_steps):
        for rg in range(n_rg):
            for j in quarters:
                utb_ref[t * nseq + rg * SUBLANES:t * nseq + (rg + 1) * SUBLANES, j * LANES:(j + 1) * LANES] = (
                    u_refs[j][pl.ds(rg * rg_rows + t, SUBLANES, stride=t_steps), :])

    u = [utb_ref[:, q * LANES:(q + 1) * LANES] for q in quarters]
    for q in quarters:
        xs_ref[q] = jnp.dot(u[q].astype(BF16), wq_ref[q], preferred_element_type=F32)
    lam = [(jnp.broadcast_to(lr_ref[:, q * Q_STATE:(q + 1) * Q_STATE], (SUBLANES, Q_STATE)),
            jnp.broadcast_to(li_ref[:, q * Q_STATE:(q + 1) * Q_STATE], (SUBLANES, Q_STATE))) for q in quarters]

    def scan_group(rg, carry):
        r0 = pl.multiple_of(rg * SUBLANES, SUBLANES)
        h = [(h0r_ref[pl.ds(r0, SUBLANES), q * Q_STATE:(q + 1) * Q_STATE],
              h0i_ref[pl.ds(r0, SUBLANES), q * Q_STATE:(q + 1) * Q_STATE]) for q in quarters]
        for t in range(t_steps):
            rows = pl.ds(pl.multiple_of(t * nseq + r0, SUBLANES), SUBLANES)
            for q in quarters:
                h[q] = _scan_step(*lam[q], *h[q], xs_ref[q, rows, 0:Q_STATE], xs_ref[q, rows, Q_STATE:])
                xs_ref[q, rows, 0:Q_STATE] = h[q][0]
                xs_ref[q, rows, Q_STATE:] = h[q][1]
        for q in quarters:
            hr_ref[pl.ds(r0, SUBLANES), q * Q_STATE:(q + 1) * Q_STATE] = h[q][0]
            hi_ref[pl.ds(r0, SUBLANES), q * Q_STATE:(q + 1) * Q_STATE] = h[q][1]
        return carry

    lax.fori_loop(0, n_rg, scan_group, 0)
    y = [jnp.dot(xs_ref[q].astype(BF16), cq_ref[q], preferred_element_type=F32) for q in quarters]
    g = [jax.nn.gelu(y[q] + d_ref[:, q * LANES:(q + 1) * LANES] * u[q]) for q in quarters]
    gate = [jnp.dot(g[q].astype(BF16), wglu_ref[q], preferred_element_type=F32) + bglu_ref[:, q * LANES:(q + 1) * LANES]
            for q in quarters]
    for q in quarters:
        o = g[q] * jax.nn.sigmoid(gate[q])
        for t in range(t_steps):
            for rg in range(n_rg):
                o_ref[q, pl.ds(rg * rg_rows + t, SUBLANES, stride=t_steps), :] = (
                    o[t * nseq + rg * SUBLANES:t * nseq + (rg + 1) * SUBLANES, :])


def _ssm_sample(u_rows, h0r, h0i, sp, nseq, t_steps):
    n = nseq * t_steps
    slab = lambda j: pl.BlockSpec((n, LANES), lambda i, j=j: (0, j))
    st = pl.BlockSpec((nseq, STATE_COLS), lambda i: (0, 0))
    return pl.pallas_call(
        functools.partial(_ssm_sample_kernel, nseq=nseq, t_steps=t_steps),
        grid=(1,),
        in_specs=[slab(0), slab(1), slab(2), slab(3), st, st,
                  _const_spec((N_QUARTERS, LANES, 2 * Q_STATE)), _const_spec((1, STATE_COLS)),
                  _const_spec((1, STATE_COLS)), _const_spec((N_QUARTERS, 2 * Q_STATE, LANES)),
                  _const_spec((1, SSM_WIDTH)), _const_spec((N_QUARTERS, LANES, LANES)), _const_spec((1, SSM_WIDTH))],
        out_specs=[pl.BlockSpec((N_QUARTERS, n, LANES), lambda i: (0, 0, 0)), st, st],
        out_shape=[jax.ShapeDtypeStruct((N_QUARTERS, n, LANES), F32),
                   jax.ShapeDtypeStruct((nseq, STATE_COLS), F32), jax.ShapeDtypeStruct((nseq, STATE_COLS), F32)],
        scratch_shapes=[pltpu.VMEM((n, SSM_WIDTH), F32), pltpu.VMEM((N_QUARTERS, n, 2 * Q_STATE), F32)],
        compiler_params=pltpu.CompilerParams(dimension_semantics=("arbitrary",), vmem_limit_bytes=VMEM_LIMIT),
        name="ssm_sample",
    )(u_rows, u_rows, u_rows, u_rows, h0r, h0i, sp["wq"], sp["lr"], sp["li"], sp["cq"], sp["d"], sp["wgluq"], sp["bglu"])


def _ffn(h, gf_ref, wg_ref, wu_ref, wd_ref):
    f = _rms(h, gf_ref[...]).astype(BF16)
    gate = jnp.dot(f, wg_ref[...], preferred_element_type=F32)
    up = jnp.dot(f, wu_ref[...], preferred_element_type=F32)
    a = (jax.nn.silu(gate) * up).astype(BF16)
    return h + jnp.dot(a, wd_ref[...], preferred_element_type=F32)


def _back_kernel(x_ref, oa_ref, os_ref, ga_ref, gs_ref, wout_ref, gf_ref, wg_ref, wu_ref, wd_ref, y_ref):
    o_ssm = jnp.concatenate([os_ref[j] for j in range(N_QUARTERS)], axis=-1)
    mix = jnp.concatenate([_rms(oa_ref[...], ga_ref[...]), _rms(o_ssm, gs_ref[...])], axis=-1).astype(BF16)
    h = x_ref[...] + jnp.dot(mix, wout_ref[...], preferred_element_type=F32)
    y_ref[...] = _ffn(h, gf_ref, wg_ref, wu_ref, wd_ref)


def _ffn_kernel(x_ref, hp_ref, gf_ref, wg_ref, wu_ref, wd_ref, y_ref):
    hp = jnp.concatenate([hp_ref[c] for c in range(hp_ref.shape[0])], axis=-1)
    y_ref[...] = _ffn(x_ref[...] + hp, gf_ref, wg_ref, wu_ref, wd_ref)


def _back(x_rows, oa, os_, ga, gs, wout, gf, wg, wu, wd, tm):
    n = x_rows.shape[0]
    assert n % tm == 0
    row = lambda w: pl.BlockSpec((tm, w), lambda i: (i, 0))
    return pl.pallas_call(
        _back_kernel,
        grid=(n // tm,),
        in_specs=[row(D_MODEL), row(ATTN_WIDTH), pl.BlockSpec((N_QUARTERS, tm, LANES), lambda i: (0, i, 0)),
                  _const_spec((1, ATTN_WIDTH)), _const_spec((1, SSM_WIDTH)), _const_spec((D_MODEL, D_MODEL)),
                  _const_spec((1, D_MODEL)), _const_spec((D_MODEL, D_FF)), _const_spec((D_MODEL, D_FF)),
                  _const_spec((D_FF, D_MODEL))],
        out_specs=row(D_MODEL),
        out_shape=jax.ShapeDtypeStruct((n, D_MODEL), F32),
        compiler_params=pltpu.CompilerParams(dimension_semantics=("arbitrary",), vmem_limit_bytes=VMEM_LIMIT),
        name="back",
    )(x_rows, oa, os_, ga, gs, wout, gf, wg, wu, wd)


def _ffn_call(x_rows, hp_rows, gf, wg, wu, wd, tm):
    n = x_rows.shape[0]
    assert n % tm == 0
    row = lambda w: pl.BlockSpec((tm, w), lambda i: (i, 0))
    return pl.pallas_call(
        _ffn_kernel,
        grid=(n // tm,),
        in_specs=[row(D_MODEL), pl.BlockSpec((hp_rows.shape[0], tm, hp_rows.shape[2]), lambda i: (0, i, 0)),
                  _const_spec((1, D_MODEL)),
                  _const_spec((D_MODEL, D_FF)), _const_spec((D_MODEL, D_FF)), _const_spec((D_FF, D_MODEL))],
        out_specs=row(D_MODEL),
        out_shape=jax.ShapeDtypeStruct((n, D_MODEL), F32),
        compiler_params=pltpu.CompilerParams(dimension_semantics=("arbitrary",), vmem_limit_bytes=VMEM_LIMIT),
        name="ffn",
    )(x_rows, hp_rows, gf, wg, wu, wd)


N_SEQ = 8
ROWS = N_SEQ * BLOCK
PITCH = BLOCK + SUBLANES // 2
CHUNK_T = BLOCK // N_SEQ


def _interleave(*stages):
    keyed = [((i + 0.5) / len(steps), k, i, step) for k, steps in enumerate(stages) for i, step in enumerate(steps)]
    return [step for _, _, _, step in sorted(keyed, key=lambda e: e[:3])]


def _mixer_kernel(sink_ref, x_ref, meta_ref, gmix_ref, win_ref, gq_ref, gk_ref,
                  wq_ref, lr_ref, li_ref, cq_ref, d_ref, wglu_ref, bglu_ref, ga_ref, gs_ref, wout_ref,
                  hp_ref, kw_ref, vw_ref, hr_ref, hi_ref,
                  qs_ref, kk_ref, vv_ref, s_ref, p_ref, oatt_ref, usl_ref, xs_ref, hb_ref, st_ref, osl_ref, mix_ref):
    i = pl.program_id(0)
    n_blocks = pl.num_programs(0) - 1

    @pl.when(i < n_blocks)
    def _():
        _mixer_block(i, sink_ref, x_ref, meta_ref, gmix_ref, win_ref, gq_ref, gk_ref,
                     wq_ref, lr_ref, li_ref, cq_ref, d_ref, wglu_ref, bglu_ref, ga_ref, gs_ref, wout_ref,
                     hp_ref, kw_ref, vw_ref, hr_ref, hi_ref,
                     qs_ref, kk_ref, vv_ref, s_ref, p_ref, oatt_ref, usl_ref, xs_ref, hb_ref, st_ref, osl_ref, mix_ref)

    @pl.when(i == n_blocks)
    def _():
        for step in _output_norm_steps(oatt_ref, osl_ref, ga_ref, gs_ref, mix_ref):
            step()
        for c in range(OUT_CHUNKS):
            _project_chunk(mix_ref, wout_ref, hp_ref, c)


OUT_CHUNKS = 4
OUT_CHUNK = D_MODEL // OUT_CHUNKS


def _project_chunk(mix_ref, wout_ref, hp_ref, c):
    hp_ref[c] = jnp.dot(mix_ref[...], wout_ref[c], preferred_element_type=F32).reshape(N_SEQ, BLOCK, OUT_CHUNK)


def _output_norm_steps(oatt_ref, osl_ref, ga_ref, gs_ref, mix_ref):
    def one(b):
        rows = slice(b * BLOCK, (b + 1) * BLOCK)
        mix_ref[rows, 0:ATTN_WIDTH] = _rms(oatt_ref[b], ga_ref[...]).astype(BF16)
        sl = [osl_ref[j, b * PITCH:b * PITCH + BLOCK, :] for j in range(N_QUARTERS)]
        ms = sum(jnp.sum(s * s, axis=-1, keepdims=True) for s in sl) / SSM_WIDTH
        inv = lax.rsqrt(ms + EPS)
        for j in range(N_QUARTERS):
            mix_ref[rows, ATTN_WIDTH + j * LANES:ATTN_WIDTH + (j + 1) * LANES] = (
                sl[j] * inv * gs_ref[:, j * LANES:(j + 1) * LANES]).astype(BF16)
    return [functools.partial(one, b) for b in range(N_SEQ)]


def _mixer_block(i, sink_ref, x_ref, meta_ref, gmix_ref, win_ref, gq_ref, gk_ref,
                 wq_ref, lr_ref, li_ref, cq_ref, d_ref, wglu_ref, bglu_ref, ga_ref, gs_ref, wout_ref,
                 hp_ref, kw_ref, vw_ref, hr_ref, hi_ref,
                 qs_ref, kk_ref, vv_ref, s_ref, p_ref, oatt_ref, usl_ref, xs_ref, hb_ref, st_ref, osl_ref, mix_ref):
    left = _left_half()
    project = lambda c: [functools.partial(_project_chunk, mix_ref, wout_ref, hp_ref, c)]

    @pl.when(i == 0)
    def _():
        oatt_ref[...] = jnp.zeros((N_SEQ, BLOCK, ATTN_WIDTH), F32)
        osl_ref[...] = jnp.zeros((N_QUARTERS, N_SEQ * PITCH, LANES), F32)
        xm = _rms(meta_ref[...], gmix_ref[...]).astype(BF16)
        zm = jnp.dot(xm, win_ref[:, ATTN_WIDTH:], preferred_element_type=F32)
        km = _pair_norm(zm[:, :KV_WIDTH], gk_ref[...], left).astype(BF16)
        vm = zm[:, KV_WIDTH:2 * KV_WIDTH].astype(BF16)
        lead = jnp.zeros((BLOCK - N_META, KV_WIDTH), BF16)
        for b in range(N_SEQ):
            kk_ref[b, 0:BLOCK - N_META, :] = lead
            vv_ref[b, 0:BLOCK - N_META, :] = lead
            kk_ref[b, BLOCK - N_META:BLOCK, :] = km
            vv_ref[b, BLOCK - N_META:BLOCK, :] = vm
        um = zm[:, 2 * KV_WIDTH:]
        for q in range(N_QUARTERS):
            xm_q = jnp.dot(um[:, q * LANES:(q + 1) * LANES].astype(BF16), wq_ref[q], preferred_element_type=F32)
            lr = jnp.broadcast_to(lr_ref[:, q * Q_STATE:(q + 1) * Q_STATE], (SUBLANES, Q_STATE))
            li = jnp.broadcast_to(li_ref[:, q * Q_STATE:(q + 1) * Q_STATE], (SUBLANES, Q_STATE))
            hr = jnp.zeros((SUBLANES, Q_STATE), F32)
            hi = jnp.zeros((SUBLANES, Q_STATE), F32)
            for t in range(N_META):
                xr = jnp.broadcast_to(xm_q[t:t + 1, 0:Q_STATE], (SUBLANES, Q_STATE))
                xi = jnp.broadcast_to(xm_q[t:t + 1, Q_STATE:], (SUBLANES, Q_STATE))
                hr, hi = _scan_step(lr, li, hr, hi, xr, xi)
            st_ref[:, 2 * q * Q_STATE:(2 * q + 1) * Q_STATE] = hr
            st_ref[:, (2 * q + 1) * Q_STATE:(2 * q + 2) * Q_STATE] = hi

    p1 = {}

    def pre_norm():
        p1["xn"] = _rms(x_ref[...].reshape(ROWS, D_MODEL), gmix_ref[...]).astype(BF16)

    def project_in(key, c0):
        p1[key] = jnp.dot(p1["xn"], win_ref[:, c0:c0 + 2 * LANES], preferred_element_type=F32)

    def finish_q(c):
        zz = p1.pop(("q", c))
        for tt in range(2):
            t = 2 * c + tt
            qn = _pair_norm(zz[:, tt * LANES:(tt + 1) * LANES], gq_ref[...], left) * ATTN_SCALE
            qa = jnp.where(left, qn, 0.0).astype(BF16)
            qb = jnp.where(left, 0.0, qn).astype(BF16)
            for b in range(N_SEQ):
                qs_ref[b, 2 * t * BLOCK:(2 * t + 1) * BLOCK, :] = qa[b * BLOCK:(b + 1) * BLOCK]
                qs_ref[b, (2 * t + 1) * BLOCK:(2 * t + 2) * BLOCK, :] = qb[b * BLOCK:(b + 1) * BLOCK]

    def finish_kv():
        zz = p1.pop("kv")
        kn = _pair_norm(zz[:, :LANES], gk_ref[...], left)
        vn = zz[:, LANES:]
        kw_ref[...] = kn.reshape(N_SEQ, BLOCK, KV_WIDTH)
        vw_ref[...] = vn.reshape(N_SEQ, BLOCK, KV_WIDTH)
        kk_ref[:, BLOCK:, :] = kn.astype(BF16).reshape(N_SEQ, BLOCK, KV_WIDTH)
        vv_ref[:, BLOCK:, :] = vn.astype(BF16).reshape(N_SEQ, BLOCK, KV_WIDTH)

    def finish_u(c):
        zz = p1.pop(("u", c))
        for tt in range(2):
            for b in range(N_SEQ):
                usl_ref[2 * c + tt, b * PITCH:b * PITCH + BLOCK, :] = zz[b * BLOCK:(b + 1) * BLOCK, tt * LANES:(tt + 1) * LANES]

    dot_kv = functools.partial(project_in, "kv", ATTN_WIDTH)
    dot_q = [functools.partial(project_in, ("q", c), c * 2 * LANES) for c in range(2)]
    dot_u = [functools.partial(project_in, ("u", c), ATTN_WIDTH + 2 * KV_WIDTH + c * 2 * LANES) for c in range(2)]
    post_q = [functools.partial(finish_q, c) for c in range(2)]
    post_u = [functools.partial(finish_u, c) for c in range(2)]

    r = lax.broadcasted_iota(jnp.int32, (BLOCK, 2 * BLOCK), 0)
    c = lax.broadcasted_iota(jnp.int32, (BLOCK, 2 * BLOCK), 1)
    valid = (c >= r) & (c <= r + WINDOW) & ((i > 0) | (c >= BLOCK - N_META))
    bias = jnp.where(valid, 0.0, -jnp.inf)

    def gather_u(n):
        t0 = n * CHUNK_T
        return jnp.concatenate(
            [jnp.concatenate([usl_ref[j, pl.ds(t0 + tl, SUBLANES, stride=PITCH), :] for j in range(N_QUARTERS)], axis=1)
             for tl in range(CHUNK_T)], axis=0)

    def feed(n, par):
        env = {}

        def scores():
            s_ref[par] = lax.dot_general(qs_ref[n], kk_ref[n], (((1,), (1,)), ((), ())), preferred_element_type=F32)

        def gather():
            env["u"] = gather_u(n)

        def scan_inputs(q):
            xs_ref[par, :, 2 * q * Q_STATE:(2 * q + 2) * Q_STATE] = jnp.dot(
                env["u"][:, q * LANES:(q + 1) * LANES].astype(BF16), wq_ref[q], preferred_element_type=F32)

        return [scores, gather] + [functools.partial(scan_inputs, q) for q in range(N_QUARTERS)]

    def mid(n, par):
        env = {}

        def row_max(g):
            s = s_ref[par, g * BLOCK:(g + 1) * BLOCK, :] + bias
            env[g] = (s, jnp.maximum(jnp.max(s, axis=-1, keepdims=True), sink_ref[g // 2 + Q_PER_KV * (g % 2)]))

        def exponent(g):
            s, m = env[g]
            p = jnp.exp(s - m)
            sink = sink_ref[g // 2 + Q_PER_KV * (g % 2)]
            env[g] = (p, 1.0 / (jnp.sum(p, axis=-1, keepdims=True) + jnp.exp(sink - m)))

        def normalise(g):
            p, inv = env.pop(g)
            p_ref[par, g * BLOCK:(g + 1) * BLOCK, :] = (p * inv).astype(BF16)

        softmax = [functools.partial(f, g) for g in range(2 * Q_TILES) for f in (row_max, exponent, normalise)]

        def load_state():
            for q in range(N_QUARTERS):
                env["lam", q] = (jnp.broadcast_to(lr_ref[:, q * Q_STATE:(q + 1) * Q_STATE], (SUBLANES, Q_STATE)),
                                 jnp.broadcast_to(li_ref[:, q * Q_STATE:(q + 1) * Q_STATE], (SUBLANES, Q_STATE)))
                env["h", q] = (st_ref[:, 2 * q * Q_STATE:(2 * q + 1) * Q_STATE],
                               st_ref[:, (2 * q + 1) * Q_STATE:(2 * q + 2) * Q_STATE])

        def scan_pair(tp, q):
            lr, li = env["lam", q]
            hr, hi = env["h", q]
            re0, im0 = 2 * q * Q_STATE, (2 * q + 1) * Q_STATE
            pair_r, pair_i = [], []
            for tl in (2 * tp, 2 * tp + 1):
                xr = xs_ref[par, tl * SUBLANES:(tl + 1) * SUBLANES, re0:re0 + Q_STATE]
                xi = xs_ref[par, tl * SUBLANES:(tl + 1) * SUBLANES, im0:im0 + Q_STATE]
                hr, hi = _scan_step(lr, li, hr, hi, xr, xi)
                pair_r.append(hr)
                pair_i.append(hi)
            env["h", q] = (hr, hi)
            rows = slice(2 * tp * SUBLANES, (2 * tp + 2) * SUBLANES)
            hb_ref[par, rows, re0:re0 + Q_STATE] = jnp.concatenate(pair_r, axis=0).astype(BF16)
            hb_ref[par, rows, im0:im0 + Q_STATE] = jnp.concatenate(pair_i, axis=0).astype(BF16)

        def store_state():
            for q in range(N_QUARTERS):
                hr, hi = env["h", q]
                st_ref[:, 2 * q * Q_STATE:(2 * q + 1) * Q_STATE] = hr
                st_ref[:, (2 * q + 1) * Q_STATE:(2 * q + 2) * Q_STATE] = hi

        scan = ([load_state] + [functools.partial(scan_pair, tp, q) for tp in range(CHUNK_T // 2) for q in range(N_QUARTERS)]
                + [store_state])
        return softmax, scan

    def tail(n, par):
        env = {}

        def attn_out():
            o_all = jnp.dot(p_ref[par], vv_ref[n], preferred_element_type=F32)
            for t in range(Q_TILES):
                oatt_ref[n, :, t * LANES:(t + 1) * LANES] = jnp.where(
                    left, o_all[2 * t * BLOCK:(2 * t + 1) * BLOCK], o_all[(2 * t + 1) * BLOCK:(2 * t + 2) * BLOCK])

        def gather():
            env["u"] = gather_u(n)

        def readout(q):
            env["y", q] = jnp.dot(hb_ref[par, :, 2 * q * Q_STATE:(2 * q + 2) * Q_STATE], cq_ref[q],
                                  preferred_element_type=F32)

        def activate(q):
            cols = slice(q * LANES, (q + 1) * LANES)
            env["g", q] = jax.nn.gelu(env.pop(("y", q)) + d_ref[:, cols] * env["u"][:, cols])

        def gate(q):
            cols = slice(q * LANES, (q + 1) * LANES)
            env["gate", q] = jnp.dot(env["g", q].astype(BF16), wglu_ref[q], preferred_element_type=F32) + bglu_ref[:, cols]

        def emit(q):
            o = env.pop(("g", q)) * jax.nn.sigmoid(env.pop(("gate", q)))
            t0 = n * CHUNK_T
            for tl in range(CHUNK_T):
                osl_ref[q, pl.ds(t0 + tl, SUBLANES, stride=PITCH), :] = o[tl * SUBLANES:(tl + 1) * SUBLANES]

        per_quarter = lambda f: [functools.partial(f, q) for q in range(N_QUARTERS)]
        return [attn_out, gather] + per_quarter(readout) + per_quarter(activate) + per_quarter(gate) + per_quarter(emit)

    def run(*stages):
        for step in _interleave(*stages):
            step()

    def steady(k, carry):
        n = 2 * k + 1
        run(project(k) + tail(n - 1, 0), *mid(n, 1), feed(n + 1, 0))
        run(feed(n + 2, 1), *mid(n + 1, 0), tail(n, 1))
        return carry

    norms = _output_norm_steps(oatt_ref, osl_ref, ga_ref, gs_ref, mix_ref)
    feed0, feed1 = feed(0, 0), feed(1, 1)
    softmax0, scan0 = mid(0, 0)
    half = len(softmax0) // 2
    run([pre_norm, dot_kv, dot_q[0], finish_kv, dot_q[1], post_q[0], dot_u[0], post_q[1]], norms)
    run(feed0[:1])
    run([dot_u[1]] + feed1[:1] + post_u, softmax0[:half])
    run(feed0[1:], softmax0[half:])
    run(feed1[1:], scan0)
    n_steady = (N_SEQ - 2) // 2
    assert n_steady == OUT_CHUNKS - 1
    lax.fori_loop(0, n_steady, steady, 0)
    run(*mid(N_SEQ - 1, 1), tail(N_SEQ - 2, 0), project(OUT_CHUNKS - 1))
    run(tail(N_SEQ - 1, 1))
    kk_ref[:, 0:BLOCK, :] = kk_ref[:, BLOCK:, :]
    vv_ref[:, 0:BLOCK, :] = vv_ref[:, BLOCK:, :]

    hr_ref[...] = jnp.concatenate([st_ref[:, 2 * q * Q_STATE:(2 * q + 1) * Q_STATE] for q in range(N_QUARTERS)], axis=1)
    hi_ref[...] = jnp.concatenate([st_ref[:, (2 * q + 1) * Q_STATE:(2 * q + 2) * Q_STATE] for q in range(N_QUARTERS)], axis=1)


def _mixer(x_prompt, meta_tokens, sinks, gmix, win, gq2, gk2, sp, ga, gs, wout):
    nseq, seq, _ = x_prompt.shape
    assert nseq == N_SEQ and seq % BLOCK == 0
    n_blocks = seq // BLOCK
    st = pl.BlockSpec((N_SEQ, STATE_COLS), lambda i: (0, 0))
    kvw = pl.BlockSpec((N_SEQ, BLOCK, KV_WIDTH), lambda i: (0, 0, 0))
    return pl.pallas_call(
        _mixer_kernel,
        grid=(n_blocks + 1,),
        in_specs=[pl.BlockSpec(memory_space=pltpu.SMEM),
                  pl.BlockSpec((N_SEQ, BLOCK, D_MODEL), lambda i: (0, jnp.minimum(i, n_blocks - 1), 0)),
                  _const_spec((N_META, D_MODEL)),
                  _const_spec((1, D_MODEL)), _const_spec((D_MODEL, IN_COLS)), _const_spec((1, LANES)),
                  _const_spec((1, LANES)),
                  _const_spec((N_QUARTERS, LANES, 2 * Q_STATE)), _const_spec((1, STATE_COLS)),
                  _const_spec((1, STATE_COLS)), _const_spec((N_QUARTERS, 2 * Q_STATE, LANES)),
                  _const_spec((1, SSM_WIDTH)), _const_spec((N_QUARTERS, LANES, LANES)), _const_spec((1, SSM_WIDTH)),
                  _const_spec((1, ATTN_WIDTH)), _const_spec((1, SSM_WIDTH)),
                  _const_spec((OUT_CHUNKS, D_MODEL, OUT_CHUNK))],
        out_specs=[pl.BlockSpec((OUT_CHUNKS, N_SEQ, BLOCK, OUT_CHUNK), lambda i: (0, 0, jnp.maximum(i - 1, 0), 0)),
                   kvw, kvw, st, st],
        out_shape=[jax.ShapeDtypeStruct((OUT_CHUNKS, N_SEQ, seq, OUT_CHUNK), F32),
                   jax.ShapeDtypeStruct((N_SEQ, BLOCK, KV_WIDTH), F32), jax.ShapeDtypeStruct((N_SEQ, BLOCK, KV_WIDTH), F32),
                   jax.ShapeDtypeStruct((N_SEQ, STATE_COLS), F32), jax.ShapeDtypeStruct((N_SEQ, STATE_COLS), F32)],
        scratch_shapes=[pltpu.VMEM((N_SEQ, 2 * Q_TILES * BLOCK, LANES), BF16),
                        pltpu.VMEM((N_SEQ, 2 * BLOCK, KV_WIDTH), BF16),
                        pltpu.VMEM((N_SEQ, 2 * BLOCK, KV_WIDTH), BF16),
                        pltpu.VMEM((2, 2 * Q_TILES * BLOCK, 2 * BLOCK), F32),
                        pltpu.VMEM((2, 2 * Q_TILES * BLOCK, 2 * BLOCK), BF16),
                        pltpu.VMEM((N_SEQ, BLOCK, ATTN_WIDTH), F32),
                        pltpu.VMEM((N_QUARTERS, N_SEQ * PITCH, LANES), F32),
                        pltpu.VMEM((2, CHUNK_T * N_SEQ, 2 * STATE_COLS), F32),
                        pltpu.VMEM((2, CHUNK_T * N_SEQ, 2 * STATE_COLS), BF16),
                        pltpu.VMEM((N_SEQ, 2 * STATE_COLS), F32),
                        pltpu.VMEM((N_QUARTERS, N_SEQ * PITCH, LANES), F32),
                        pltpu.VMEM((ROWS, D_MODEL), BF16)],
        compiler_params=pltpu.CompilerParams(dimension_semantics=("arbitrary",), vmem_limit_bytes=VMEM_LIMIT),
        name="mixer",
    )(sinks, x_prompt, meta_tokens, gmix, win, gq2, gk2, sp["wq"], sp["lr"], sp["li"], sp["cq"], sp["d"],
      sp["wgluq"], sp["bglu"], ga, gs, wout)


def kernel(x_prompt, x_sample, cache_k_win, cache_v_win, state_ssm_re, state_ssm_im, meta_tokens, g_mix, w_in, g_q,
           g_k, sinks, ssm_a_re, ssm_a_im, ssm_log_dt, ssm_b_re, ssm_b_im, ssm_c_re, ssm_c_im, ssm_d, ssm_w_glu,
           ssm_b_glu, g_att_out, g_ssm_out, w_out, g_ffn, w_gate, w_up, w_down):
    bp, seq, _ = x_prompt.shape
    db, dseq, _ = x_sample.shape
    li = 0
    gmix = g_mix[li].reshape(1, D_MODEL)
    win = w_in[li].astype(BF16)
    gq2 = jnp.tile(g_q[li], 2).reshape(1, LANES)
    gk2 = jnp.tile(g_k[li], 2).reshape(1, LANES)
    sk = sinks[li]
    sp = _ssm_params(ssm_a_re[li], ssm_a_im[li], ssm_log_dt[li], ssm_b_re[li], ssm_b_im[li], ssm_c_re[li],
                     ssm_c_im[li], ssm_d[li], ssm_w_glu[li], ssm_b_glu[li])
    ga = g_att_out[li].reshape(1, ATTN_WIDTH)
    gs = g_ssm_out[li].reshape(1, SSM_WIDTH)
    wout = w_out[li].astype(BF16)
    ffn_w = (g_ffn[li].reshape(1, D_MODEL), w_gate[li].astype(BF16), w_up[li].astype(BF16), w_down[li].astype(BF16))
    regroup = lambda a, axis: jnp.swapaxes(
        a.reshape(a.shape[:axis] + (2, Q_PER_KV, HEAD_DIM) + a.shape[axis + 1:]), axis, axis + 1).reshape(a.shape)
    win = jnp.concatenate([regroup(win[:, :ATTN_WIDTH], 1), win[:, ATTN_WIDTH:]], axis=1)
    wout = jnp.concatenate([regroup(wout[:ATTN_WIDTH], 0), wout[ATTN_WIDTH:]], axis=0)
    ga = regroup(ga, 1)

    wout_chunks = wout.reshape(D_MODEL, OUT_CHUNKS, OUT_CHUNK).transpose(1, 0, 2)
    hproj, kw_p, vw_p, hp_r, hp_i = _mixer(x_prompt, meta_tokens, sk, gmix, win, gq2, gk2, sp, ga, gs, wout_chunks)
    y_prompt = _ffn_call(x_prompt.reshape(bp * seq, D_MODEL), hproj.reshape(OUT_CHUNKS, bp * seq, OUT_CHUNK), *ffn_w,
                         tm=512).reshape(bp, seq, D_MODEL)

    n_s = db * dseq
    xs_rows = x_sample.reshape(n_s, D_MODEL)
    q_s, k_s, v_s, u_s = _front(xs_rows, gmix, win, gq2, gk2, tm=512)
    ck = cache_k_win[li].reshape(db, WINDOW, KV_WIDTH)
    cv = cache_v_win[li].reshape(db, WINDOW, KV_WIDTH)
    oa_s, kw_s, vw_s = _attn_sample(q_s.reshape(db, dseq, ATTN_WIDTH), k_s.reshape(db, dseq, KV_WIDTH),
                                    v_s.reshape(db, dseq, KV_WIDTH), ck, cv, sk)
    os_s, hs_r, hs_i = _ssm_sample(u_s, state_ssm_re[li].reshape(db, STATE_COLS),
                                   state_ssm_im[li].reshape(db, STATE_COLS), sp, nseq=db, t_steps=dseq)
    y_sample = _back(xs_rows, oa_s.reshape(n_s, ATTN_WIDTH), os_s, ga, gs, wout, *ffn_w, tm=512).reshape(db, dseq, D_MODEL)

    kv5 = lambda a, n: a.reshape(1, n, WINDOW, N_KV_HEADS, HEAD_DIM)
    st4 = lambda a, n: a.reshape(1, n, N_SSM_GROUPS, SSM_STATE)
    return (y_prompt, y_sample, kv5(kw_p, bp), kv5(vw_p, bp), st4(hp_r, bp), st4(hp_i, bp),
            kv5(kw_s, db), kv5(vw_s, db), st4(hs_r, db), st4(hs_i, db))
```

```python
import functools

import jax
import jax.numpy as jnp
from jax import lax
from jax.experimental import pallas as pl
from jax.experimental.pallas import tpu as pltpu

D_MODEL = 1024
N_META = 16
HEAD_DIM = 64
ATTN_WIDTH = 512
Q_PER_KV = 4
N_KV_HEADS = 2
KV_WIDTH = 128
WINDOW = 128
BLOCK = 128
SSM_WIDTH = 512
SSM_GROUP = 16
N_SSM_GROUPS = 32
SSM_STATE = 64
IN_COLS = ATTN_WIDTH + 2 * KV_WIDTH + SSM_WIDTH
D_FF = 2816
EPS = 1e-6
ATTN_SCALE = HEAD_DIM ** -0.5
STATE_COLS = N_SSM_GROUPS * SSM_STATE
LANES = 128
SUBLANES = 8
VMEM_LIMIT = 56 * 1024 * 1024

F32 = jnp.float32
BF16 = jnp.bfloat16


def _const_spec(shape):
    return pl.BlockSpec(shape, lambda *_: (0,) * len(shape), pipeline_mode=pl.Buffered(1))


def _rms(x, g):
    return x * lax.rsqrt(jnp.mean(x * x, axis=-1, keepdims=True) + EPS) * g


def _left_half():
    return lax.broadcasted_iota(jnp.int32, (1, LANES), 1) < HEAD_DIM


def _pair_norm(zz, g2, left):
    sq = zz * zz
    sl = jnp.sum(jnp.where(left, sq, 0.0), axis=-1, keepdims=True)
    sr = jnp.sum(jnp.where(left, 0.0, sq), axis=-1, keepdims=True)
    inv = jnp.where(left, lax.rsqrt(sl / HEAD_DIM + EPS), lax.rsqrt(sr / HEAD_DIM + EPS))
    return zz * inv * g2


def _front_kernel(x_ref, gmix_ref, win_ref, wkvt_ref, gq_ref, gkt_ref, q_ref, kt_ref, vt_ref, u_ref):
    xn = _rms(x_ref[...], gmix_ref[...]).astype(BF16)
    z = jnp.dot(xn, win_ref[...], preferred_element_type=F32)
    left = _left_half()
    for p in range(ATTN_WIDTH // LANES):
        q_ref[:, p * LANES:(p + 1) * LANES] = _pair_norm(z[:, p * LANES:(p + 1) * LANES], gq_ref[...], left)
    u_ref[...] = z[:, ATTN_WIDTH + 2 * KV_WIDTH:]
    zt = lax.dot_general(wkvt_ref[...], xn, (((1,), (1,)), ((), ())), preferred_element_type=F32)
    heads = []
    for h in range(N_KV_HEADS):
        kh = zt[h * HEAD_DIM:(h + 1) * HEAD_DIM]
        heads.append(kh * lax.rsqrt(jnp.mean(kh * kh, axis=0, keepdims=True) + EPS))
    kt_ref[...] = jnp.concatenate(heads, axis=0) * gkt_ref[...]
    vt_ref[...] = zt[KV_WIDTH:]


def _front(x_rows, gmix, win_bf, wkvt_bf, gq2, gkt, tm):
    n = x_rows.shape[0]
    assert n % tm == 0
    row = lambda w: pl.BlockSpec((tm, w), lambda i: (i, 0))
    col = pl.BlockSpec((KV_WIDTH, tm), lambda i: (0, i))
    return pl.pallas_call(
        _front_kernel,
        grid=(n // tm,),
        in_specs=[row(D_MODEL), _const_spec((1, D_MODEL)), _const_spec((D_MODEL, IN_COLS)),
                  _const_spec((2 * KV_WIDTH, D_MODEL)), _const_spec((1, LANES)), _const_spec((KV_WIDTH, 1))],
        out_specs=[row(ATTN_WIDTH), col, col, row(SSM_WIDTH)],
        out_shape=[jax.ShapeDtypeStruct((n, ATTN_WIDTH), F32), jax.ShapeDtypeStruct((KV_WIDTH, n), F32),
                   jax.ShapeDtypeStruct((KV_WIDTH, n), F32), jax.ShapeDtypeStruct((n, SSM_WIDTH), F32)],
        compiler_params=pltpu.CompilerParams(dimension_semantics=("arbitrary",), vmem_limit_bytes=VMEM_LIMIT),
        name="front",
    )(x_rows, gmix, win_bf, wkvt_bf, gq2, gkt)


Q_TILES = ATTN_WIDTH // LANES


def _attn_sample_kernel(sink_ref, q_ref, knt_ref, vnt_ref, ckt_ref, cvt_ref, o_ref, kwt_ref, vwt_ref, *, bb, t):
    left = _left_half()
    n_heads = 2 * Q_TILES
    rows = n_heads * t
    nt_dims = (((1,), (1,)), ((), ()))
    r = lax.broadcasted_iota(jnp.int32, (rows, WINDOW), 0) % t
    c = lax.broadcasted_iota(jnp.int32, (rows, WINDOW), 1)
    bias_cache = jnp.where(c >= r, 0.0, -jnp.inf)
    lane = lax.broadcasted_iota(jnp.int32, (KV_WIDTH, WINDOW), 1)
    hrow = lax.broadcasted_iota(jnp.int32, (rows, 1), 0) // t
    sink = jnp.zeros((rows, 1), F32)
    for g in range(n_heads):
        sink = jnp.where(hrow == g, sink_ref[g // 2 + Q_PER_KV * (g % 2)], sink)
    knt, vnt = knt_ref[...], vnt_ref[...]
    knt_bf, vnt_bf = knt.astype(BF16), vnt.astype(BF16)
    scores, values = [], []
    for bi in range(bb):
        ckt, cvt = ckt_ref[bi], cvt_ref[bi]
        shift = (WINDOW - t - bi * t) % WINDOW
        new_k, new_v = (knt, vnt) if shift == 0 else (pltpu.roll(knt, shift, 1), pltpu.roll(vnt, shift, 1))
        kwt_ref[bi] = jnp.where(lane < WINDOW - t, pltpu.roll(ckt, WINDOW - t, 1), new_k)
        vwt_ref[bi] = jnp.where(lane < WINDOW - t, pltpu.roll(cvt, WINDOW - t, 1), new_v)
        q = q_ref[bi] * ATTN_SCALE
        pieces = []
        for tile in range(Q_TILES):
            qt = q[:, tile * LANES:(tile + 1) * LANES]
            pieces += [jnp.where(left, qt, 0.0), jnp.where(left, 0.0, qt)]
        qs = jnp.concatenate(pieces, axis=0).astype(BF16)
        scores.append((jnp.dot(qs, ckt.astype(BF16), preferred_element_type=F32),
                       jnp.dot(qs, knt_bf, preferred_element_type=F32)))
        values.append(cvt.astype(BF16))
    probs = []
    for bi in range(bb):
        own = (c >= bi * t) & (c <= bi * t + r)
        s = jnp.concatenate([scores[bi][0] + bias_cache, jnp.where(own, scores[bi][1], -jnp.inf)], axis=1)
        m = jnp.maximum(jnp.max(s, axis=-1, keepdims=True), sink)
        p = jnp.exp(s - m)
        inv = 1.0 / (jnp.sum(p, axis=-1, keepdims=True) + jnp.exp(sink - m))
        probs.append((p * inv).astype(BF16))
    for bi in range(bb):
        o = (lax.dot_general(probs[bi][:, :WINDOW], values[bi], nt_dims, preferred_element_type=F32)
             + lax.dot_general(probs[bi][:, WINDOW:], vnt_bf, nt_dims, preferred_element_type=F32))
        for tile in range(Q_TILES):
            o_ref[bi, :, tile * LANES:(tile + 1) * LANES] = jnp.where(
                left, o[2 * tile * t:(2 * tile + 1) * t], o[(2 * tile + 1) * t:(2 * tile + 2) * t])


def _attn_sample(q, knt, vnt, ckt, cvt, sinks):
    db, t, _ = q.shape
    bb = WINDOW // t
    assert WINDOW == LANES and bb * t == LANES and db % bb == 0
    blk = lambda r, w: pl.BlockSpec((bb, r, w), lambda i: (i, 0, 0))
    new = pl.BlockSpec((KV_WIDTH, bb * t), lambda i: (0, i))
    return pl.pallas_call(
        functools.partial(_attn_sample_kernel, bb=bb, t=t),
        grid=(db // bb,),
        in_specs=[pl.BlockSpec(memory_space=pltpu.SMEM), blk(t, ATTN_WIDTH), new, new,
                  blk(KV_WIDTH, WINDOW), blk(KV_WIDTH, WINDOW)],
        out_specs=[blk(t, ATTN_WIDTH), blk(KV_WIDTH, WINDOW), blk(KV_WIDTH, WINDOW)],
        out_shape=[jax.ShapeDtypeStruct((db, t, ATTN_WIDTH), F32),
                   jax.ShapeDtypeStruct((db, KV_WIDTH, WINDOW), F32),
                   jax.ShapeDtypeStruct((db, KV_WIDTH, WINDOW), F32)],
        compiler_params=pltpu.CompilerParams(dimension_semantics=("arbitrary",)),
        name="attn_sample",
    )(sinks, q, knt, vnt, ckt, cvt)


N_QUARTERS = 4
Q_GROUPS = N_SSM_GROUPS // N_QUARTERS
Q_STATE = Q_GROUPS * SSM_STATE


def _zoh_kernel(are_ref, aim_ref, logdt_ref, bre_ref, bim_ref, lr_ref, li_ref, bbr_ref, bbi_ref):
    ar, ai = are_ref[...], aim_ref[...]
    dt = jnp.exp(logdt_ref[...])
    mag = jnp.exp(ar * dt)
    lr, li = mag * jnp.cos(ai * dt), mag * jnp.sin(ai * dt)
    nr, ni = lr - 1.0, li
    den = ar * ar + ai * ai
    fr, fi = ((nr * ar + ni * ai) / den)[:, None, :], ((ni * ar - nr * ai) / den)[:, None, :]
    br, bi = bre_ref[...], bim_ref[...]
    lr_ref[...] = lr
    li_ref[...] = li
    bbr_ref[...] = fr * br - fi * bi
    bbi_ref[...] = fr * bi + fi * br


def _ssm_params(a_re, a_im, log_dt, b_re, b_im, c_re, c_im, d_skip, w_glu, b_glu):
    g, p, h = b_re.shape
    lr, li, bbr, bbi = pl.pallas_call(
        _zoh_kernel,
        out_shape=[jax.ShapeDtypeStruct((g, p), F32), jax.ShapeDtypeStruct((g, p), F32),
                   jax.ShapeDtypeStruct((g, h, p), F32), jax.ShapeDtypeStruct((g, h, p), F32)],
        name="zoh",
    )(a_re, a_im, log_dt.reshape(g, 1), jnp.swapaxes(b_re, 1, 2), jnp.swapaxes(b_im, 1, 2))
    eye = jnp.eye(Q_GROUPS, dtype=F32)
    quartered = lambda a: a.reshape((a.shape[0], N_QUARTERS, Q_GROUPS) + a.shape[2:])
    wq = jnp.einsum('rqghp,gm->qghrmp', quartered(jnp.stack([bbr, bbi])), eye).reshape(N_QUARTERS, LANES, 2 * Q_STATE)
    cq = jnp.einsum('rqghp,gm->qrgpmh', quartered(jnp.stack([c_re, -c_im])), eye).reshape(N_QUARTERS, 2 * Q_STATE, LANES)
    wglu = jnp.einsum('qghk,gm->qghmk', w_glu.reshape(N_QUARTERS, Q_GROUPS, SSM_GROUP, SSM_GROUP), eye)
    return dict(wq=wq.astype(BF16), cq=cq.astype(BF16), wgluq=wglu.reshape(N_QUARTERS, LANES, LANES).astype(BF16),
                lr=lr.reshape(1, STATE_COLS), li=li.reshape(1, STATE_COLS), d=d_skip.reshape(1, SSM_WIDTH),
                bglu=b_glu.reshape(1, SSM_WIDTH))


def _scan_step(lr, li, hr, hi, xr, xi):
    return lr * hr - li * hi + xr, lr * hi + li * hr + xi


def _ssm_sample_kernel(u0_ref, u1_ref, u2_ref, u3_ref, h0r_ref, h0i_ref, wq_ref, lr_ref, li_ref, cq_ref, d_ref,
                       wglu_ref, bglu_ref, o_ref, hr_ref, hi_ref, utb_ref, xs_ref, *, nseq, t_steps):
    u_refs = (u0_ref, u1_ref, u2_ref, u3_ref)
    quarters = range(N_QUARTERS)
    n_rg = nseq // SUBLANES
    rg_rows = SUBLANES * t_steps

    for t in range(t_steps):
        for rg in range(n_rg):
            for j in quarters:
                utb_ref[t * nseq + rg * SUBLANES:t * nseq + (rg + 1) * SUBLANES, j * LANES:(j + 1) * LANES] = (
                    u_refs[j][pl.ds(rg * rg_rows + t, SUBLANES, stride=t_steps), :])

    u = [utb_ref[:, q * LANES:(q + 1) * LANES] for q in quarters]
    for q in quarters:
        xs_ref[q] = jnp.dot(u[q].astype(BF16), wq_ref[q], preferred_element_type=F32)
    lam = [(jnp.broadcast_to(lr_ref[:, q * Q_STATE:(q + 1) * Q_STATE], (SUBLANES, Q_STATE)),
            jnp.broadcast_to(li_ref[:, q * Q_STATE:(q + 1) * Q_STATE], (SUBLANES, Q_STATE))) for q in quarters]

    def scan_group(rg, carry):
        r0 = pl.multiple_of(rg * SUBLANES, SUBLANES)
        h = [(h0r_ref[pl.ds(r0, SUBLANES), q * Q_STATE:(q + 1) * Q_STATE],
              h0i_ref[pl.ds(r0, SUBLANES), q * Q_STATE:(q + 1) * Q_STATE]) for q in quarters]
        for t in range(t_steps):
            rows = pl.ds(pl.multiple_of(t * nseq + r0, SUBLANES), SUBLANES)
            for q in quarters:
                h[q] = _scan_step(*lam[q], *h[q], xs_ref[q, rows, 0:Q_STATE], xs_ref[q, rows, Q_STATE:])
                xs_ref[q, rows, 0:Q_STATE] = h[q][0]
                xs_ref[q, rows, Q_STATE:] = h[q][1]
        for q in quarters:
            hr_ref[pl.ds(r0, SUBLANES), q * Q_STATE:(q + 1) * Q_STATE] = h[q][0]
            hi_ref[pl.ds(r0, SUBLANES), q * Q_STATE:(q + 1) * Q_STATE] = h[q][1]
        return carry

    lax.fori_loop(0, n_rg, scan_group, 0)
    y = [jnp.dot(xs_ref[q].astype(BF16), cq_ref[q], preferred_element_type=F32) for q in quarters]
    g = [jax.nn.gelu(y[q] + d_ref[:, q * LANES:(q + 1) * LANES] * u[q]) for q in quarters]
    gate = [jnp.dot(g[q].astype(BF16), wglu_ref[q], preferred_element_type=F32) + bglu_ref[:, q * LANES:(q + 1) * LANES]
            for q in quarters]
    for q in quarters:
        o = g[q] * jax.nn.sigmoid(gate[q])
        for t in range(t_steps):
            for rg in range(n_rg):
                o_ref[q, pl.ds(rg * rg_rows + t, SUBLANES, stride=t_steps), :] = (
                    o[t * nseq + rg * SUBLANES:t * nseq + (rg + 1) * SUBLANES, :])


def _ssm_sample(u_rows, h0r, h0i, sp, nseq, t_steps):
    n = nseq * t_steps
    slab = lambda j: pl.BlockSpec((n, LANES), lambda i, j=j: (0, j))
    st = pl.BlockSpec((nseq, STATE_COLS), lambda i: (0, 0))
    return pl.pallas_call(
        functools.partial(_ssm_sample_kernel, nseq=nseq, t_steps=t_steps),
        grid=(1,),
        in_specs=[slab(0), slab(1), slab(2), slab(3), st, st,
                  _const_spec((N_QUARTERS, LANES, 2 * Q_STATE)), _const_spec((1, STATE_COLS)),
                  _const_spec((1, STATE_COLS)), _const_spec((N_QUARTERS, 2 * Q_STATE, LANES)),
                  _const_spec((1, SSM_WIDTH)), _const_spec((N_QUARTERS, LANES, LANES)), _const_spec((1, SSM_WIDTH))],
        out_specs=[pl.BlockSpec((N_QUARTERS, n, LANES), lambda i: (0, 0, 0)), st, st],
        out_shape=[jax.ShapeDtypeStruct((N_QUARTERS, n, LANES), F32),
                   jax.ShapeDtypeStruct((nseq, STATE_COLS), F32), jax.ShapeDtypeStruct((nseq, STATE_COLS), F32)],
        scratch_shapes=[pltpu.VMEM((n, SSM_WIDTH), F32), pltpu.VMEM((N_QUARTERS, n, 2 * Q_STATE), F32)],
        compiler_params=pltpu.CompilerParams(dimension_semantics=("arbitrary",), vmem_limit_bytes=VMEM_LIMIT),
        name="ssm_sample",
    )(u_rows, u_rows, u_rows, u_rows, h0r, h0i, sp["wq"], sp["lr"], sp["li"], sp["cq"], sp["d"], sp["wgluq"], sp["bglu"])


def _ffn(h, gf_ref, wg_ref, wu_ref, wd_ref):
    f = _rms(h, gf_ref[...]).astype(BF16)
    gate = jnp.dot(f, wg_ref[...], preferred_element_type=F32)
    up = jnp.dot(f, wu_ref[...], preferred_element_type=F32)
    a = (jax.nn.silu(gate) * up).astype(BF16)
    return h + jnp.dot(a, wd_ref[...], preferred_element_type=F32)


def _back_kernel(x_ref, oa_ref, os_ref, ga_ref, gs_ref, wout_ref, gf_ref, wg_ref, wu_ref, wd_ref, y_ref):
    o_ssm = jnp.concatenate([os_ref[j] for j in range(N_QUARTERS)], axis=-1)
    mix = jnp.concatenate([_rms(oa_ref[...], ga_ref[...]), _rms(o_ssm, gs_ref[...])], axis=-1).astype(BF16)
    h = x_ref[...] + jnp.dot(mix, wout_ref[...], preferred_element_type=F32)
    y_ref[...] = _ffn(h, gf_ref, wg_ref, wu_ref, wd_ref)


def _ffn_kernel(x_ref, hp_ref, gf_ref, wg_ref, wu_ref, wd_ref, y_ref):
    hp = jnp.concatenate([hp_ref[c] for c in range(hp_ref.shape[0])], axis=-1)
    y_ref[...] = _ffn(x_ref[...] + hp, gf_ref, wg_ref, wu_ref, wd_ref)


def _back(x_rows, oa, os_, ga, gs, wout, gf, wg, wu, wd, tm):
    n = x_rows.shape[0]
    assert n % tm == 0
    row = lambda w: pl.BlockSpec((tm, w), lambda i: (i, 0))
    return pl.pallas_call(
        _back_kernel,
        grid=(n // tm,),
        in_specs=[row(D_MODEL), row(ATTN_WIDTH), pl.BlockSpec((N_QUARTERS, tm, LANES), lambda i: (0, i, 0)),
                  _const_spec((1, ATTN_WIDTH)), _const_spec((1, SSM_WIDTH)), _const_spec((D_MODEL, D_MODEL)),
                  _const_spec((1, D_MODEL)), _const_spec((D_MODEL, D_FF)), _const_spec((D_MODEL, D_FF)),
                  _const_spec((D_FF, D_MODEL))],
        out_specs=row(D_MODEL),
        out_shape=jax.ShapeDtypeStruct((n, D_MODEL), F32),
        compiler_params=pltpu.CompilerParams(dimension_semantics=("arbitrary",), vmem_limit_bytes=VMEM_LIMIT),
        name="back",
    )(x_rows, oa, os_, ga, gs, wout, gf, wg, wu, wd)


def _ffn_call(x_rows, hp_rows, gf, wg, wu, wd, tm):
    n = x_rows.shape[0]
    assert n % tm == 0
    row = lambda w: pl.BlockSpec((tm, w), lambda i: (i, 0))
    return pl.pallas_call(
        _ffn_kernel,
        grid=(n // tm,),
        in_specs=[row(D_MODEL), pl.BlockSpec((hp_rows.shape[0], tm, hp_rows.shape[2]), lambda i: (0, i, 0)),
                  _const_spec((1, D_MODEL)),
                  _const_spec((D_MODEL, D_FF)), _const_spec((D_MODEL, D_FF)), _const_spec((D_FF, D_MODEL))],
        out_specs=row(D_MODEL),
        out_shape=jax.ShapeDtypeStruct((n, D_MODEL), F32),
        compiler_params=pltpu.CompilerParams(dimension_semantics=("arbitrary",), vmem_limit_bytes=VMEM_LIMIT),
        name="ffn",
    )(x_rows, hp_rows, gf, wg, wu, wd)


N_SEQ = 8
ROWS = N_SEQ * BLOCK
PITCH = BLOCK + SUBLANES // 2
CHUNK_T = BLOCK // N_SEQ


def _interleave(*stages):
    keyed = [((i + 0.5) / len(steps), k, i, step) for k, steps in enumerate(stages) for i, step in enumerate(steps)]
    return [step for _, _, _, step in sorted(keyed, key=lambda e: e[:3])]


def _mixer_kernel(sink_ref, x_ref, meta_ref, gmix_ref, win_ref, gq_ref, gk_ref,
                  wq_ref, lr_ref, li_ref, cq_ref, d_ref, wglu_ref, bglu_ref, ga_ref, gs_ref, wout_ref,
                  hp_ref, kw_ref, vw_ref, hr_ref, hi_ref,
                  qs_ref, kk_ref, vv_ref, s_ref, p_ref, oatt_ref, usl_ref, xs_ref, hb_ref, st_ref, osl_ref, mix_ref):
    i = pl.program_id(0)
    n_blocks = pl.num_programs(0) - 1

    @pl.when(i < n_blocks)
    def _():
        _mixer_block(i, sink_ref, x_ref, meta_ref, gmix_ref, win_ref, gq_ref, gk_ref,
                     wq_ref, lr_ref, li_ref, cq_ref, d_ref, wglu_ref, bglu_ref, ga_ref, gs_ref, wout_ref,
                     hp_ref, kw_ref, vw_ref, hr_ref, hi_ref,
                     qs_ref, kk_ref, vv_ref, s_ref, p_ref, oatt_ref, usl_ref, xs_ref, hb_ref, st_ref, osl_ref, mix_ref)

    @pl.when(i == n_blocks)
    def _():
        for step in _output_norm_steps(oatt_ref, osl_ref, ga_ref, gs_ref, mix_ref):
            step()
        for c in range(OUT_CHUNKS):
            _project_chunk(mix_ref, wout_ref, hp_ref, c)


OUT_CHUNKS = 4
OUT_CHUNK = D_MODEL // OUT_CHUNKS


def _project_chunk(mix_ref, wout_ref, hp_ref, c):
    hp_ref[c] = jnp.dot(mix_ref[...], wout_ref[c], preferred_element_type=F32).reshape(N_SEQ, BLOCK, OUT_CHUNK)


def _output_norm_steps(oatt_ref, osl_ref, ga_ref, gs_ref, mix_ref):
    def one(b):
        rows = slice(b * BLOCK, (b + 1) * BLOCK)
        mix_ref[rows, 0:ATTN_WIDTH] = _rms(oatt_ref[b], ga_ref[...]).astype(BF16)
        sl = [osl_ref[j, b * PITCH:b * PITCH + BLOCK, :] for j in range(N_QUARTERS)]
        ms = sum(jnp.sum(s * s, axis=-1, keepdims=True) for s in sl) / SSM_WIDTH
        inv = lax.rsqrt(ms + EPS)
        for j in range(N_QUARTERS):
            mix_ref[rows, ATTN_WIDTH + j * LANES:ATTN_WIDTH + (j + 1) * LANES] = (
                sl[j] * inv * gs_ref[:, j * LANES:(j + 1) * LANES]).astype(BF16)
    return [functools.partial(one, b) for b in range(N_SEQ)]


def _mixer_block(i, sink_ref, x_ref, meta_ref, gmix_ref, win_ref, gq_ref, gk_ref,
                 wq_ref, lr_ref, li_ref, cq_ref, d_ref, wglu_ref, bglu_ref, ga_ref, gs_ref, wout_ref,
                 hp_ref, kw_ref, vw_ref, hr_ref, hi_ref,
                 qs_ref, kk_ref, vv_ref, s_ref, p_ref, oatt_ref, usl_ref, xs_ref, hb_ref, st_ref, osl_ref, mix_ref):
    left = _left_half()
    project = lambda c: [functools.partial(_project_chunk, mix_ref, wout_ref, hp_ref, c)]

    @pl.when(i == 0)
    def _():
        oatt_ref[...] = jnp.zeros((N_SEQ, BLOCK, ATTN_WIDTH), F32)
        osl_ref[...] = jnp.zeros((N_QUARTERS, N_SEQ * PITCH, LANES), F32)
        xm = _rms(meta_ref[...], gmix_ref[...]).astype(BF16)
        zm = jnp.dot(xm, win_ref[:, ATTN_WIDTH:], preferred_element_type=F32)
        km = _pair_norm(zm[:, :KV_WIDTH], gk_ref[...], left).astype(BF16)
        vm = zm[:, KV_WIDTH:2 * KV_WIDTH].astype(BF16)
        lead = jnp.zeros((BLOCK - N_META, KV_WIDTH), BF16)
        for b in range(N_SEQ):
            kk_ref[b, 0:BLOCK - N_META, :] = lead
            vv_ref[b, 0:BLOCK - N_META, :] = lead
            kk_ref[b, BLOCK - N_META:BLOCK, :] = km
            vv_ref[b, BLOCK - N_META:BLOCK, :] = vm
        um = zm[:, 2 * KV_WIDTH:]
        for q in range(N_QUARTERS):
            xm_q = jnp.dot(um[:, q * LANES:(q + 1) * LANES].astype(BF16), wq_ref[q], preferred_element_type=F32)
            lr = jnp.broadcast_to(lr_ref[:, q * Q_STATE:(q + 1) * Q_STATE], (SUBLANES, Q_STATE))
            li = jnp.broadcast_to(li_ref[:, q * Q_STATE:(q + 1) * Q_STATE], (SUBLANES, Q_STATE))
            hr = jnp.zeros((SUBLANES, Q_STATE), F32)
            hi = jnp.zeros((SUBLANES, Q_STATE), F32)
            for t in range(N_META):
                xr = jnp.broadcast_to(xm_q[t:t + 1, 0:Q_STATE], (SUBLANES, Q_STATE))
                xi = jnp.broadcast_to(xm_q[t:t + 1, Q_STATE:], (SUBLANES, Q_STATE))
                hr, hi = _scan_step(lr, li, hr, hi, xr, xi)
            st_ref[:, 2 * q * Q_STATE:(2 * q + 1) * Q_STATE] = hr
            st_ref[:, (2 * q + 1) * Q_STATE:(2 * q + 2) * Q_STATE] = hi

    p1 = {}

    def pre_norm():
        p1["xn"] = _rms(x_ref[...].reshape(ROWS, D_MODEL), gmix_ref[...]).astype(BF16)

    def project_in(key, c0):
        p1[key] = jnp.dot(p1["xn"], win_ref[:, c0:c0 + 2 * LANES], preferred_element_type=F32)

    def finish_q(c):
        zz = p1.pop(("q", c))
        for tt in range(2):
            t = 2 * c + tt
            qn = _pair_norm(zz[:, tt * LANES:(tt + 1) * LANES], gq_ref[...], left) * ATTN_SCALE
            qa = jnp.where(left, qn, 0.0).astype(BF16)
            qb = jnp.where(left, 0.0, qn).astype(BF16)
            for b in range(N_SEQ):
                qs_ref[b, 2 * t * BLOCK:(2 * t + 1) * BLOCK, :] = qa[b * BLOCK:(b + 1) * BLOCK]
                qs_ref[b, (2 * t + 1) * BLOCK:(2 * t + 2) * BLOCK, :] = qb[b * BLOCK:(b + 1) * BLOCK]

    def finish_kv():
        zz = p1.pop("kv")
        kn = _pair_norm(zz[:, :LANES], gk_ref[...], left)
        vn = zz[:, LANES:]
        kw_ref[...] = kn.reshape(N_SEQ, BLOCK, KV_WIDTH)
        vw_ref[...] = vn.reshape(N_SEQ, BLOCK, KV_WIDTH)
        kk_ref[:, BLOCK:, :] = kn.astype(BF16).reshape(N_SEQ, BLOCK, KV_WIDTH)
        vv_ref[:, BLOCK:, :] = vn.astype(BF16).reshape(N_SEQ, BLOCK, KV_WIDTH)

    def finish_u(c):
        zz = p1.pop(("u", c))
        for tt in range(2):
            for b in range(N_SEQ):
                usl_ref[2 * c + tt, b * PITCH:b * PITCH + BLOCK, :] = zz[b * BLOCK:(b + 1) * BLOCK, tt * LANES:(tt + 1) * LANES]

    dot_kv = functools.partial(project_in, "kv", ATTN_WIDTH)
    dot_q = [functools.partial(project_in, ("q", c), c * 2 * LANES) for c in range(2)]
    dot_u = [functools.partial(project_in, ("u", c), ATTN_WIDTH + 2 * KV_WIDTH + c * 2 * LANES) for c in range(2)]
    post_q = [functools.partial(finish_q, c) for c in range(2)]
    post_u = [functools.partial(finish_u, c) for c in range(2)]

    r = lax.broadcasted_iota(jnp.int32, (BLOCK, 2 * BLOCK), 0)
    c = lax.broadcasted_iota(jnp.int32, (BLOCK, 2 * BLOCK), 1)
    valid = (c >= r) & (c <= r + WINDOW) & ((i > 0) | (c >= BLOCK - N_META))
    bias = jnp.where(valid, 0.0, -jnp.inf)

    def gather_u(n):
        t0 = n * CHUNK_T
        return jnp.concatenate(
            [jnp.concatenate([usl_ref[j, pl.ds(t0 + tl, SUBLANES, stride=PITCH), :] for j in range(N_QUARTERS)], axis=1)
             for tl in range(CHUNK_T)], axis=0)

    def feed(n, par):
        env = {}

        def scores():
            s_ref[par] = lax.dot_general(qs_ref[n], kk_ref[n], (((1,), (1,)), ((), ())), preferred_element_type=F32)

        def gather():
            env["u"] = gather_u(n)

        def scan_inputs(q):
            xs_ref[par, :, 2 * q * Q_STATE:(2 * q + 2) * Q_STATE] = jnp.dot(
                env["u"][:, q * LANES:(q + 1) * LANES].astype(BF16), wq_ref[q], preferred_element_type=F32)

        return [scores, gather] + [functools.partial(scan_inputs, q) for q in range(N_QUARTERS)]

    def mid(n, par):
        env = {}

        def row_max(g):
            s = s_ref[par, g * BLOCK:(g + 1) * BLOCK, :] + bias
            env[g] = (s, jnp.maximum(jnp.max(s, axis=-1, keepdims=True), sink_ref[g // 2 + Q_PER_KV * (g % 2)]))

        def exponent(g):
            s, m = env[g]
            p = jnp.exp(s - m)
            sink = sink_ref[g // 2 + Q_PER_KV * (g % 2)]
            env[g] = (p, 1.0 / (jnp.sum(p, axis=-1, keepdims=True) + jnp.exp(sink - m)))

        def normalise(g):
            p, inv = env.pop(g)
            p_ref[par, g * BLOCK:(g + 1) * BLOCK, :] = (p * inv).astype(BF16)

        softmax = [functools.partial(f, g) for g in range(2 * Q_TILES) for f in (row_max, exponent, normalise)]

        def load_state():
            for q in range(N_QUARTERS):
                env["lam", q] = (jnp.broadcast_to(lr_ref[:, q * Q_STATE:(q + 1) * Q_STATE], (SUBLANES, Q_STATE)),
                                 jnp.broadcast_to(li_ref[:, q * Q_STATE:(q + 1) * Q_STATE], (SUBLANES, Q_STATE)))
                env["h", q] = (st_ref[:, 2 * q * Q_STATE:(2 * q + 1) * Q_STATE],
                               st_ref[:, (2 * q + 1) * Q_STATE:(2 * q + 2) * Q_STATE])

        def scan_pair(tp, q):
            lr, li = env["lam", q]
            hr, hi = env["h", q]
            re0, im0 = 2 * q * Q_STATE, (2 * q + 1) * Q_STATE
            pair_r, pair_i = [], []
            for tl in (2 * tp, 2 * tp + 1):
                xr = xs_ref[par, tl * SUBLANES:(tl + 1) * SUBLANES, re0:re0 + Q_STATE]
                xi = xs_ref[par, tl * SUBLANES:(tl + 1) * SUBLANES, im0:im0 + Q_STATE]
                hr, hi = _scan_step(lr, li, hr, hi, xr, xi)
                pair_r.append(hr)
                pair_i.append(hi)
            env["h", q] = (hr, hi)
            rows = slice(2 * tp * SUBLANES, (2 * tp + 2) * SUBLANES)
            hb_ref[par, rows, re0:re0 + Q_STATE] = jnp.concatenate(pair_r, axis=0).astype(BF16)
            hb_ref[par, rows, im0:im0 + Q_STATE] = jnp.concatenate(pair_i, axis=0).astype(BF16)

        def store_state():
            for q in range(N_QUARTERS):
                hr, hi = env["h", q]
                st_ref[:, 2 * q * Q_STATE:(2 * q + 1) * Q_STATE] = hr
                st_ref[:, (2 * q + 1) * Q_STATE:(2 * q + 2) * Q_STATE] = hi

        scan = ([load_state] + [functools.partial(scan_pair, tp, q) for tp in range(CHUNK_T // 2) for q in range(N_QUARTERS)]
                + [store_state])
        return softmax, scan

    def tail(n, par):
        env = {}

        def attn_out():
            o_all = jnp.dot(p_ref[par], vv_ref[n], preferred_element_type=F32)
            for t in range(Q_TILES):
                oatt_ref[n, :, t * LANES:(t + 1) * LANES] = jnp.where(
                    left, o_all[2 * t * BLOCK:(2 * t + 1) * BLOCK], o_all[(2 * t + 1) * BLOCK:(2 * t + 2) * BLOCK])

        def gather():
            env["u"] = gather_u(n)

        def readout(q):
            env["y", q] = jnp.dot(hb_ref[par, :, 2 * q * Q_STATE:(2 * q + 2) * Q_STATE], cq_ref[q],
                                  preferred_element_type=F32)

        def activate(q):
            cols = slice(q * LANES, (q + 1) * LANES)
            env["g", q] = jax.nn.gelu(env.pop(("y", q)) + d_ref[:, cols] * env["u"][:, cols])

        def gate(q):
            cols = slice(q * LANES, (q + 1) * LANES)
            env["gate", q] = jnp.dot(env["g", q].astype(BF16), wglu_ref[q], preferred_element_type=F32) + bglu_ref[:, cols]

        def emit(q):
            o = env.pop(("g", q)) * jax.nn.sigmoid(env.pop(("gate", q)))
            t0 = n * CHUNK_T
            for tl in range(CHUNK_T):
                osl_ref[q, pl.ds(t0 + tl, SUBLANES, stride=PITCH), :] = o[tl * SUBLANES:(tl + 1) * SUBLANES]

        per_quarter = lambda f: [functools.partial(f, q) for q in range(N_QUARTERS)]
        return [attn_out, gather] + per_quarter(readout) + per_quarter(activate) + per_quarter(gate) + per_quarter(emit)

    def run(*stages):
        for step in _interleave(*stages):
            step()

    def steady(k, carry):
        n = 2 * k + 1
        run(project(k) + tail(n - 1, 0), *mid(n, 1), feed(n + 1, 0))
        run(feed(n + 2, 1), *mid(n + 1, 0), tail(n, 1))
        return carry

    norms = _output_norm_steps(oatt_ref, osl_ref, ga_ref, gs_ref, mix_ref)
    feed0, feed1 = feed(0, 0), feed(1, 1)
    softmax0, scan0 = mid(0, 0)
    half = len(softmax0) // 2
    run([pre_norm, dot_kv, dot_q[0], finish_kv, dot_q[1], post_q[0], dot_u[0], post_q[1]], norms)
    run(feed0[:1])
    run([dot_u[1]] + feed1[:1] + post_u, softmax0[:half])
    run(feed0[1:], softmax0[half:])
    run(feed1[1:], scan0)
    n_steady = (N_SEQ - 2) // 2
    assert n_steady == OUT_CHUNKS - 1
    lax.fori_loop(0, n_steady, steady, 0)
    run(*mid(N_SEQ - 1, 1), tail(N_SEQ - 2, 0), project(OUT_CHUNKS - 1))
    run(tail(N_SEQ - 1, 1))
    kk_ref[:, 0:BLOCK, :] = kk_ref[:, BLOCK:, :]
    vv_ref[:, 0:BLOCK, :] = vv_ref[:, BLOCK:, :]

    hr_ref[...] = jnp.concatenate([st_ref[:, 2 * q * Q_STATE:(2 * q + 1) * Q_STATE] for q in range(N_QUARTERS)], axis=1)
    hi_ref[...] = jnp.concatenate([st_ref[:, (2 * q + 1) * Q_STATE:(2 * q + 2) * Q_STATE] for q in range(N_QUARTERS)], axis=1)


def _mixer(x_prompt, meta_tokens, sinks, gmix, win, gq2, gk2, sp, ga, gs, wout):
    nseq, seq, _ = x_prompt.shape
    assert nseq == N_SEQ and seq % BLOCK == 0
    n_blocks = seq // BLOCK
    st = pl.BlockSpec((N_SEQ, STATE_COLS), lambda i: (0, 0))
    kvw = pl.BlockSpec((N_SEQ, BLOCK, KV_WIDTH), lambda i: (0, 0, 0))
    return pl.pallas_call(
        _mixer_kernel,
        grid=(n_blocks + 1,),
        in_specs=[pl.BlockSpec(memory_space=pltpu.SMEM),
                  pl.BlockSpec((N_SEQ, BLOCK, D_MODEL), lambda i: (0, jnp.minimum(i, n_blocks - 1), 0)),
                  _const_spec((N_META, D_MODEL)),
                  _const_spec((1, D_MODEL)), _const_spec((D_MODEL, IN_COLS)), _const_spec((1, LANES)),
                  _const_spec((1, LANES)),
                  _const_spec((N_QUARTERS, LANES, 2 * Q_STATE)), _const_spec((1, STATE_COLS)),
                  _const_spec((1, STATE_COLS)), _const_spec((N_QUARTERS, 2 * Q_STATE, LANES)),
                  _const_spec((1, SSM_WIDTH)), _const_spec((N_QUARTERS, LANES, LANES)), _const_spec((1, SSM_WIDTH)),
                  _const_spec((1, ATTN_WIDTH)), _const_spec((1, SSM_WIDTH)),
                  _const_spec((OUT_CHUNKS, D_MODEL, OUT_CHUNK))],
        out_specs=[pl.BlockSpec((OUT_CHUNKS, N_SEQ, BLOCK, OUT_CHUNK), lambda i: (0, 0, jnp.maximum(i - 1, 0), 0)),
                   kvw, kvw, st, st],
        out_shape=[jax.ShapeDtypeStruct((OUT_CHUNKS, N_SEQ, seq, OUT_CHUNK), F32),
                   jax.ShapeDtypeStruct((N_SEQ, BLOCK, KV_WIDTH), F32), jax.ShapeDtypeStruct((N_SEQ, BLOCK, KV_WIDTH), F32),
                   jax.ShapeDtypeStruct((N_SEQ, STATE_COLS), F32), jax.ShapeDtypeStruct((N_SEQ, STATE_COLS), F32)],
        scratch_shapes=[pltpu.VMEM((N_SEQ, 2 * Q_TILES * BLOCK, LANES), BF16),
                        pltpu.VMEM((N_SEQ, 2 * BLOCK, KV_WIDTH), BF16),
                        pltpu.VMEM((N_SEQ, 2 * BLOCK, KV_WIDTH), BF16),
                        pltpu.VMEM((2, 2 * Q_TILES * BLOCK, 2 * BLOCK), F32),
                        pltpu.VMEM((2, 2 * Q_TILES * BLOCK, 2 * BLOCK), BF16),
                        pltpu.VMEM((N_SEQ, BLOCK, ATTN_WIDTH), F32),
                        pltpu.VMEM((N_QUARTERS, N_SEQ * PITCH, LANES), F32),
                        pltpu.VMEM((2, CHUNK_T * N_SEQ, 2 * STATE_COLS), F32),
                        pltpu.VMEM((2, CHUNK_T * N_SEQ, 2 * STATE_COLS), BF16),
                        pltpu.VMEM((N_SEQ, 2 * STATE_COLS), F32),
                        pltpu.VMEM((N_QUARTERS, N_SEQ * PITCH, LANES), F32),
                        pltpu.VMEM((ROWS, D_MODEL), BF16)],
        compiler_params=pltpu.CompilerParams(dimension_semantics=("arbitrary",), vmem_limit_bytes=VMEM_LIMIT),
        name="mixer",
    )(sinks, x_prompt, meta_tokens, gmix, win, gq2, gk2, sp["wq"], sp["lr"], sp["li"], sp["cq"], sp["d"],
      sp["wgluq"], sp["bglu"], ga, gs, wout)


def kernel(x_prompt, x_sample, cache_k_win, cache_v_win, state_ssm_re, state_ssm_im, meta_tokens, g_mix, w_in, g_q,
           g_k, sinks, ssm_a_re, ssm_a_im, ssm_log_dt, ssm_b_re, ssm_b_im, ssm_c_re, ssm_c_im, ssm_d, ssm_w_glu,
           ssm_b_glu, g_att_out, g_ssm_out, w_out, g_ffn, w_gate, w_up, w_down):
    bp, seq, _ = x_prompt.shape
    db, dseq, _ = x_sample.shape
    li = 0
    gmix = g_mix[li].reshape(1, D_MODEL)
    win = w_in[li].astype(BF16)
    gq2 = jnp.tile(g_q[li], 2).reshape(1, LANES)
    gk2 = jnp.tile(g_k[li], 2).reshape(1, LANES)
    sk = sinks[li]
    sp = _ssm_params(ssm_a_re[li], ssm_a_im[li], ssm_log_dt[li], ssm_b_re[li], ssm_b_im[li], ssm_c_re[li],
                     ssm_c_im[li], ssm_d[li], ssm_w_glu[li], ssm_b_glu[li])
    ga = g_att_out[li].reshape(1, ATTN_WIDTH)
    gs = g_ssm_out[li].reshape(1, SSM_WIDTH)
    wout = w_out[li].astype(BF16)
    ffn_w = (g_ffn[li].reshape(1, D_MODEL), w_gate[li].astype(BF16), w_up[li].astype(BF16), w_down[li].astype(BF16))
    regroup = lambda a, axis: jnp.swapaxes(
        a.reshape(a.shape[:axis] + (2, Q_PER_KV, HEAD_DIM) + a.shape[axis + 1:]), axis, axis + 1).reshape(a.shape)
    win = jnp.concatenate([regroup(win[:, :ATTN_WIDTH], 1), win[:, ATTN_WIDTH:]], axis=1)
    wout = jnp.concatenate([regroup(wout[:ATTN_WIDTH], 0), wout[ATTN_WIDTH:]], axis=0)
    ga = regroup(ga, 1)

    wout_chunks = wout.reshape(D_MODEL, OUT_CHUNKS, OUT_CHUNK).transpose(1, 0, 2)
    hproj, kw_p, vw_p, hp_r, hp_i = _mixer(x_prompt, meta_tokens, sk, gmix, win, gq2, gk2, sp, ga, gs, wout_chunks)
    y_prompt = _ffn_call(x_prompt.reshape(bp * seq, D_MODEL), hproj.reshape(OUT_CHUNKS, bp * seq, OUT_CHUNK), *ffn_w,
                         tm=512).reshape(bp, seq, D_MODEL)

    n_s = db * dseq
    xs_rows = x_sample.reshape(n_s, D_MODEL)
    wkvt = win[:, ATTN_WIDTH:ATTN_WIDTH + 2 * KV_WIDTH].T
    q_s, kt_s, vt_s, u_s = _front(xs_rows, gmix, win, wkvt, gq2, gk2.reshape(KV_WIDTH, 1), tm=512)
    to_t = lambda a: jnp.transpose(a, (0, 2, 3, 1)).reshape(db, KV_WIDTH, WINDOW)
    from_t = lambda a: jnp.transpose(a.reshape(db, N_KV_HEADS, HEAD_DIM, WINDOW), (0, 3, 1, 2))[None]
    oa_s, kwt_s, vwt_s = _attn_sample(q_s.reshape(db, dseq, ATTN_WIDTH), kt_s, vt_s, to_t(cache_k_win[li]),
                                      to_t(cache_v_win[li]), sk)
    os_s, hs_r, hs_i = _ssm_sample(u_s, state_ssm_re[li].reshape(db, STATE_COLS),
                                   state_ssm_im[li].reshape(db, STATE_COLS), sp, nseq=db, t_steps=dseq)
    y_sample = _back(xs_rows, oa_s.reshape(n_s, ATTN_WIDTH), os_s, ga, gs, wout, *ffn_w, tm=512).reshape(db, dseq, D_MODEL)

    kv5 = lambda a, n: a.reshape(1, n, WINDOW, N_KV_HEADS, HEAD_DIM)
    st4 = lambda a, n: a.reshape(1, n, N_SSM_GROUPS, SSM_STATE)
    return (y_prompt, y_sample, kv5(kw_p, bp), kv5(vw_p, bp), st4(hp_r, bp), st4(hp_i, bp),
            from_t(kwt_s), from_t(vwt_s), st4(hs_r, db), st4(hs_i, db))
```

```python
import functools

import jax
import jax.numpy as jnp
from jax import lax
from jax.experimental import pallas as pl
from jax.experimental.pallas import tpu as pltpu

D_MODEL = 1024
N_META = 16
HEAD_DIM = 64
ATTN_WIDTH = 512
Q_PER_KV = 4
N_KV_HEADS = 2
KV_WIDTH = 128
WINDOW = 128
BLOCK = 128
SSM_WIDTH = 512
SSM_GROUP = 16
N_SSM_GROUPS = 32
SSM_STATE = 64
IN_COLS = ATTN_WIDTH + 2 * KV_WIDTH + SSM_WIDTH
D_FF = 2816
EPS = 1e-6
ATTN_SCALE = HEAD_DIM ** -0.5
STATE_COLS = N_SSM_GROUPS * SSM_STATE
LANES = 128
SUBLANES = 8
VMEM_LIMIT = 56 * 1024 * 1024

F32 = jnp.float32
BF16 = jnp.bfloat16


def _const_spec(shape):
    return pl.BlockSpec(shape, lambda *_: (0,) * len(shape), pipeline_mode=pl.Buffered(1))


def _rms(x, g):
    return x * lax.rsqrt(jnp.mean(x * x, axis=-1, keepdims=True) + EPS) * g


def _left_half():
    return lax.broadcasted_iota(jnp.int32, (1, LANES), 1) < HEAD_DIM


def _pair_norm(zz, g2, left):
    sq = zz * zz
    sl = jnp.sum(jnp.where(left, sq, 0.0), axis=-1, keepdims=True)
    sr = jnp.sum(jnp.where(left, 0.0, sq), axis=-1, keepdims=True)
    inv = jnp.where(left, lax.rsqrt(sl / HEAD_DIM + EPS), lax.rsqrt(sr / HEAD_DIM + EPS))
    return zz * inv * g2


def _front_kernel(x_ref, gmix_ref, win_ref, gq_ref, gk_ref, q_ref, kt_ref, vt_ref, u_ref):
    xn = _rms(x_ref[...], gmix_ref[...]).astype(BF16)
    z = jnp.dot(xn, win_ref[...], preferred_element_type=F32)
    left = _left_half()
    for p in range(ATTN_WIDTH // LANES):
        q_ref[:, p * LANES:(p + 1) * LANES] = _pair_norm(z[:, p * LANES:(p + 1) * LANES], gq_ref[...], left)
    kt_ref[...] = _pair_norm(z[:, ATTN_WIDTH:ATTN_WIDTH + KV_WIDTH], gk_ref[...], left).T
    vt_ref[...] = z[:, ATTN_WIDTH + KV_WIDTH:ATTN_WIDTH + 2 * KV_WIDTH].T
    u_ref[...] = z[:, ATTN_WIDTH + 2 * KV_WIDTH:]


def _front(x_rows, gmix, win_bf, gq2, gk2, tm):
    n = x_rows.shape[0]
    assert n % tm == 0
    row = lambda w: pl.BlockSpec((tm, w), lambda i: (i, 0))
    col = pl.BlockSpec((KV_WIDTH, tm), lambda i: (0, i))
    return pl.pallas_call(
        _front_kernel,
        grid=(n // tm,),
        in_specs=[row(D_MODEL), _const_spec((1, D_MODEL)), _const_spec((D_MODEL, IN_COLS)),
                  _const_spec((1, LANES)), _const_spec((1, LANES))],
        out_specs=[row(ATTN_WIDTH), col, col, row(SSM_WIDTH)],
        out_shape=[jax.ShapeDtypeStruct((n, ATTN_WIDTH), F32), jax.ShapeDtypeStruct((KV_WIDTH, n), F32),
                   jax.ShapeDtypeStruct((KV_WIDTH, n), F32), jax.ShapeDtypeStruct((n, SSM_WIDTH), F32)],
        compiler_params=pltpu.CompilerParams(dimension_semantics=("arbitrary",), vmem_limit_bytes=VMEM_LIMIT),
        name="front",
    )(x_rows, gmix, win_bf, gq2, gk2)


Q_TILES = ATTN_WIDTH // LANES


def _attn_sample_kernel(sink_ref, q_ref, knt_ref, vnt_ref, ckt_ref, cvt_ref, o_ref, kwt_ref, vwt_ref, *, bb, t):
    left = _left_half()
    n_heads = 2 * Q_TILES
    rows = n_heads * t
    nt_dims = (((1,), (1,)), ((), ()))
    r = lax.broadcasted_iota(jnp.int32, (rows, WINDOW), 0) % t
    c = lax.broadcasted_iota(jnp.int32, (rows, WINDOW), 1)
    bias_cache = jnp.where(c >= r, 0.0, -jnp.inf)
    lane = lax.broadcasted_iota(jnp.int32, (KV_WIDTH, WINDOW), 1)
    hrow = lax.broadcasted_iota(jnp.int32, (rows, 1), 0) // t
    sink = jnp.zeros((rows, 1), F32)
    for g in range(n_heads):
        sink = jnp.where(hrow == g, sink_ref[g // 2 + Q_PER_KV * (g % 2)], sink)
    knt, vnt = knt_ref[...], vnt_ref[...]
    knt_bf, vnt_bf = knt.astype(BF16), vnt.astype(BF16)
    scores, values = [], []
    for bi in range(bb):
        ckt, cvt = ckt_ref[bi], cvt_ref[bi]
        shift = (WINDOW - t - bi * t) % WINDOW
        new_k, new_v = (knt, vnt) if shift == 0 else (pltpu.roll(knt, shift, 1), pltpu.roll(vnt, shift, 1))
        kwt_ref[bi] = jnp.where(lane < WINDOW - t, pltpu.roll(ckt, WINDOW - t, 1), new_k)
        vwt_ref[bi] = jnp.where(lane < WINDOW - t, pltpu.roll(cvt, WINDOW - t, 1), new_v)
        q = q_ref[bi] * ATTN_SCALE
        pieces = []
        for tile in range(Q_TILES):
            qt = q[:, tile * LANES:(tile + 1) * LANES]
            pieces += [jnp.where(left, qt, 0.0), jnp.where(left, 0.0, qt)]
        qs = jnp.concatenate(pieces, axis=0).astype(BF16)
        scores.append((jnp.dot(qs, ckt.astype(BF16), preferred_element_type=F32),
                       jnp.dot(qs, knt_bf, preferred_element_type=F32)))
        values.append(cvt.astype(BF16))
    probs = []
    for bi in range(bb):
        own = (c >= bi * t) & (c <= bi * t + r)
        s = jnp.concatenate([scores[bi][0] + bias_cache, jnp.where(own, scores[bi][1], -jnp.inf)], axis=1)
        m = jnp.maximum(jnp.max(s, axis=-1, keepdims=True), sink)
        p = jnp.exp(s - m)
        inv = 1.0 / (jnp.sum(p, axis=-1, keepdims=True) + jnp.exp(sink - m))
        probs.append((p * inv).astype(BF16))
    for bi in range(bb):
        o = (lax.dot_general(probs[bi][:, :WINDOW], values[bi], nt_dims, preferred_element_type=F32)
             + lax.dot_general(probs[bi][:, WINDOW:], vnt_bf, nt_dims, preferred_element_type=F32))
        for tile in range(Q_TILES):
            o_ref[bi, :, tile * LANES:(tile + 1) * LANES] = jnp.where(
                left, o[2 * tile * t:(2 * tile + 1) * t], o[(2 * tile + 1) * t:(2 * tile + 2) * t])


def _attn_sample(q, knt, vnt, ckt, cvt, sinks):
    db, t, _ = q.shape
    bb = WINDOW // t
    assert WINDOW == LANES and bb * t == LANES and db % bb == 0
    blk = lambda r, w: pl.BlockSpec((bb, r, w), lambda i: (i, 0, 0))
    new = pl.BlockSpec((KV_WIDTH, bb * t), lambda i: (0, i))
    return pl.pallas_call(
        functools.partial(_attn_sample_kernel, bb=bb, t=t),
        grid=(db // bb,),
        in_specs=[pl.BlockSpec(memory_space=pltpu.SMEM), blk(t, ATTN_WIDTH), new, new,
                  blk(KV_WIDTH, WINDOW), blk(KV_WIDTH, WINDOW)],
        out_specs=[blk(t, ATTN_WIDTH), blk(KV_WIDTH, WINDOW), blk(KV_WIDTH, WINDOW)],
        out_shape=[jax.ShapeDtypeStruct((db, t, ATTN_WIDTH), F32),
                   jax.ShapeDtypeStruct((db, KV_WIDTH, WINDOW), F32),
                   jax.ShapeDtypeStruct((db, KV_WIDTH, WINDOW), F32)],
        compiler_params=pltpu.CompilerParams(dimension_semantics=("arbitrary",)),
        name="attn_sample",
    )(sinks, q, knt, vnt, ckt, cvt)


N_QUARTERS = 4
Q_GROUPS = N_SSM_GROUPS // N_QUARTERS
Q_STATE = Q_GROUPS * SSM_STATE


def _zoh_kernel(are_ref, aim_ref, logdt_ref, bre_ref, bim_ref, lr_ref, li_ref, bbr_ref, bbi_ref):
    ar, ai = are_ref[...], aim_ref[...]
    dt = jnp.exp(logdt_ref[...])
    mag = jnp.exp(ar * dt)
    lr, li = mag * jnp.cos(ai * dt), mag * jnp.sin(ai * dt)
    nr, ni = lr - 1.0, li
    den = ar * ar + ai * ai
    fr, fi = ((nr * ar + ni * ai) / den)[:, None, :], ((ni * ar - nr * ai) / den)[:, None, :]
    br, bi = bre_ref[...], bim_ref[...]
    lr_ref[...] = lr
    li_ref[...] = li
    bbr_ref[...] = fr * br - fi * bi
    bbi_ref[...] = fr * bi + fi * br


def _ssm_params(a_re, a_im, log_dt, b_re, b_im, c_re, c_im, d_skip, w_glu, b_glu):
    g, p, h = b_re.shape
    lr, li, bbr, bbi = pl.pallas_call(
        _zoh_kernel,
        out_shape=[jax.ShapeDtypeStruct((g, p), F32), jax.ShapeDtypeStruct((g, p), F32),
                   jax.ShapeDtypeStruct((g, h, p), F32), jax.ShapeDtypeStruct((g, h, p), F32)],
        name="zoh",
    )(a_re, a_im, log_dt.reshape(g, 1), jnp.swapaxes(b_re, 1, 2), jnp.swapaxes(b_im, 1, 2))
    eye = jnp.eye(Q_GROUPS, dtype=F32)
    quartered = lambda a: a.reshape((a.shape[0], N_QUARTERS, Q_GROUPS) + a.shape[2:])
    wq = jnp.einsum('rqghp,gm->qghrmp', quartered(jnp.stack([bbr, bbi])), eye).reshape(N_QUARTERS, LANES, 2 * Q_STATE)
    cq = jnp.einsum('rqghp,gm->qrgpmh', quartered(jnp.stack([c_re, -c_im])), eye).reshape(N_QUARTERS, 2 * Q_STATE, LANES)
    wglu = jnp.einsum('qghk,gm->qghmk', w_glu.reshape(N_QUARTERS, Q_GROUPS, SSM_GROUP, SSM_GROUP), eye)
    return dict(wq=wq.astype(BF16), cq=cq.astype(BF16), wgluq=wglu.reshape(N_QUARTERS, LANES, LANES).astype(BF16),
                lr=lr.reshape(1, STATE_COLS), li=li.reshape(1, STATE_COLS), d=d_skip.reshape(1, SSM_WIDTH),
                bglu=b_glu.reshape(1, SSM_WIDTH))


def _scan_step(lr, li, hr, hi, xr, xi):
    return lr * hr - li * hi + xr, lr * hi + li * hr + xi


def _ssm_sample_kernel(u0_ref, u1_ref, u2_ref, u3_ref, h0r_ref, h0i_ref, wq_ref, lr_ref, li_ref, cq_ref, d_ref,
                       wglu_ref, bglu_ref, o_ref, hr_ref, hi_ref, utb_ref, xs_ref, *, nseq, t_steps):
    u_refs = (u0_ref, u1_ref, u2_ref, u3_ref)
    quarters = range(N_QUARTERS)
    n_rg = nseq // SUBLANES
    rg_rows = SUBLANES * t_steps

    for t in range(t_steps):
        for rg in range(n_rg):
            for j in quarters:
                utb_ref[t * nseq + rg * SUBLANES:t * nseq + (rg + 1) * SUBLANES, j * LANES:(j + 1) * LANES] = (
                    u_refs[j][pl.ds(rg * rg_rows + t, SUBLANES, stride=t_steps), :])

    u = [utb_ref[:, q * LANES:(q + 1) * LANES] for q in quarters]
    for q in quarters:
        xs_ref[q] = jnp.dot(u[q].astype(BF16), wq_ref[q], preferred_element_type=F32)
    lam = [(jnp.broadcast_to(lr_ref[:, q * Q_STATE:(q + 1) * Q_STATE], (SUBLANES, Q_STATE)),
            jnp.broadcast_to(li_ref[:, q * Q_STATE:(q + 1) * Q_STATE], (SUBLANES, Q_STATE))) for q in quarters]

    def scan_group(rg, carry):
        r0 = pl.multiple_of(rg * SUBLANES, SUBLANES)
        h = [(h0r_ref[pl.ds(r0, SUBLANES), q * Q_STATE:(q + 1) * Q_STATE],
              h0i_ref[pl.ds(r0, SUBLANES), q * Q_STATE:(q + 1) * Q_STATE]) for q in quarters]
        for t in range(t_steps):
            rows = pl.ds(pl.multiple_of(t * nseq + r0, SUBLANES), SUBLANES)
            for q in quarters:
                h[q] = _scan_step(*lam[q], *h[q], xs_ref[q, rows, 0:Q_STATE], xs_ref[q, rows, Q_STATE:])
                xs_ref[q, rows, 0:Q_STATE] = h[q][0]
                xs_ref[q, rows, Q_STATE:] = h[q][1]
        for q in quarters:
            hr_ref[pl.ds(r0, SUBLANES), q * Q_STATE:(q + 1) * Q_STATE] = h[q][0]
            hi_ref[pl.ds(r0, SUBLANES), q * Q_STATE:(q + 1) * Q_STATE] = h[q][1]
        return carry

    lax.fori_loop(0, n_rg, scan_group, 0)
    y = [jnp.dot(xs_ref[q].astype(BF16), cq_ref[q], preferred_element_type=F32) for q in quarters]
    g = [jax.nn.gelu(y[q] + d_ref[:, q * LANES:(q + 1) * LANES] * u[q]) for q in quarters]
    gate = [jnp.dot(g[q].astype(BF16), wglu_ref[q], preferred_element_type=F32) + bglu_ref[:, q * LANES:(q + 1) * LANES]
            for q in quarters]
    for q in quarters:
        o = g[q] * jax.nn.sigmoid(gate[q])
        for t in range(t_steps):
            for rg in range(n_rg):
                o_ref[q, pl.ds(rg * rg_rows + t, SUBLANES, stride=t_steps), :] = (
                    o[t * nseq + rg * SUBLANES:t * nseq + (rg + 1) * SUBLANES, :])


def _ssm_sample(u_rows, h0r, h0i, sp, nseq, t_steps):
    n = nseq * t_steps
    slab = lambda j: pl.BlockSpec((n, LANES), lambda i, j=j: (0, j))
    st = pl.BlockSpec((nseq, STATE_COLS), lambda i: (0, 0))
    return pl.pallas_call(
        functools.partial(_ssm_sample_kernel, nseq=nseq, t_steps=t_steps),
        grid=(1,),
        in_specs=[slab(0), slab(1), slab(2), slab(3), st, st,
                  _const_spec((N_QUARTERS, LANES, 2 * Q_STATE)), _const_spec((1, STATE_COLS)),
                  _const_spec((1, STATE_COLS)), _const_spec((N_QUARTERS, 2 * Q_STATE, LANES)),
                  _const_spec((1, SSM_WIDTH)), _const_spec((N_QUARTERS, LANES, LANES)), _const_spec((1, SSM_WIDTH))],
        out_specs=[pl.BlockSpec((N_QUARTERS, n, LANES), lambda i: (0, 0, 0)), st, st],
        out_shape=[jax.ShapeDtypeStruct((N_QUARTERS, n, LANES), F32),
                   jax.ShapeDtypeStruct((nseq, STATE_COLS), F32), jax.ShapeDtypeStruct((nseq, STATE_COLS), F32)],
        scratch_shapes=[pltpu.VMEM((n, SSM_WIDTH), F32), pltpu.VMEM((N_QUARTERS, n, 2 * Q_STATE), F32)],
        compiler_params=pltpu.CompilerParams(dimension_semantics=("arbitrary",), vmem_limit_bytes=VMEM_LIMIT),
        name="ssm_sample",
    )(u_rows, u_rows, u_rows, u_rows, h0r, h0i, sp["wq"], sp["lr"], sp["li"], sp["cq"], sp["d"], sp["wgluq"], sp["bglu"])


def _ffn(h, gf_ref, wg_ref, wu_ref, wd_ref):
    f = _rms(h, gf_ref[...]).astype(BF16)
    gate = jnp.dot(f, wg_ref[...], preferred_element_type=F32)
    up = jnp.dot(f, wu_ref[...], preferred_element_type=F32)
    a = (jax.nn.silu(gate) * up).astype(BF16)
    return h + jnp.dot(a, wd_ref[...], preferred_element_type=F32)


def _back_kernel(x_ref, oa_ref, os_ref, ga_ref, gs_ref, wout_ref, gf_ref, wg_ref, wu_ref, wd_ref, y_ref):
    o_ssm = jnp.concatenate([os_ref[j] for j in range(N_QUARTERS)], axis=-1)
    mix = jnp.concatenate([_rms(oa_ref[...], ga_ref[...]), _rms(o_ssm, gs_ref[...])], axis=-1).astype(BF16)
    h = x_ref[...] + jnp.dot(mix, wout_ref[...], preferred_element_type=F32)
    y_ref[...] = _ffn(h, gf_ref, wg_ref, wu_ref, wd_ref)


def _ffn_kernel(x_ref, hp_ref, gf_ref, wg_ref, wu_ref, wd_ref, y_ref):
    hp = jnp.concatenate([hp_ref[c] for c in range(hp_ref.shape[0])], axis=-1)
    y_ref[...] = _ffn(x_ref[...] + hp, gf_ref, wg_ref, wu_ref, wd_ref)


def _back(x_rows, oa, os_, ga, gs, wout, gf, wg, wu, wd, tm):
    n = x_rows.shape[0]
    assert n % tm == 0
    row = lambda w: pl.BlockSpec((tm, w), lambda i: (i, 0))
    return pl.pallas_call(
        _back_kernel,
        grid=(n // tm,),
        in_specs=[row(D_MODEL), row(ATTN_WIDTH), pl.BlockSpec((N_QUARTERS, tm, LANES), lambda i: (0, i, 0)),
                  _const_spec((1, ATTN_WIDTH)), _const_spec((1, SSM_WIDTH)), _const_spec((D_MODEL, D_MODEL)),
                  _const_spec((1, D_MODEL)), _const_spec((D_MODEL, D_FF)), _const_spec((D_MODEL, D_FF)),
                  _const_spec((D_FF, D_MODEL))],
        out_specs=row(D_MODEL),
        out_shape=jax.ShapeDtypeStruct((n, D_MODEL), F32),
        compiler_params=pltpu.CompilerParams(dimension_semantics=("arbitrary",), vmem_limit_bytes=VMEM_LIMIT),
        name="back",
    )(x_rows, oa, os_, ga, gs, wout, gf, wg, wu, wd)


def _ffn_call(x_rows, hp_rows, gf, wg, wu, wd, tm):
    n = x_rows.shape[0]
    assert n % tm == 0
    row = lambda w: pl.BlockSpec((tm, w), lambda i: (i, 0))
    return pl.pallas_call(
        _ffn_kernel,
        grid=(n // tm,),
        in_specs=[row(D_MODEL), pl.BlockSpec((hp_rows.shape[0], tm, hp_rows.shape[2]), lambda i: (0, i, 0)),
                  _const_spec((1, D_MODEL)),
                  _const_spec((D_MODEL, D_FF)), _const_spec((D_MODEL, D_FF)), _const_spec((D_FF, D_MODEL))],
        out_specs=row(D_MODEL),
        out_shape=jax.ShapeDtypeStruct((n, D_MODEL), F32),
        compiler_params=pltpu.CompilerParams(dimension_semantics=("arbitrary",), vmem_limit_bytes=VMEM_LIMIT),
        name="ffn",
    )(x_rows, hp_rows, gf, wg, wu, wd)


N_SEQ = 8
ROWS = N_SEQ * BLOCK
PITCH = BLOCK + SUBLANES // 2
CHUNK_T = BLOCK // N_SEQ


def _interleave(*stages):
    keyed = [((i + 0.5) / len(steps), k, i, step) for k, steps in enumerate(stages) for i, step in enumerate(steps)]
    return [step for _, _, _, step in sorted(keyed, key=lambda e: e[:3])]


def _mixer_kernel(sink_ref, x_ref, meta_ref, gmix_ref, win_ref, gq_ref, gk_ref,
                  wq_ref, lr_ref, li_ref, cq_ref, d_ref, wglu_ref, bglu_ref, ga_ref, gs_ref, wout_ref,
                  hp_ref, kw_ref, vw_ref, hr_ref, hi_ref,
                  qs_ref, kk_ref, vv_ref, s_ref, p_ref, oatt_ref, usl_ref, xs_ref, hb_ref, st_ref, osl_ref, mix_ref):
    i = pl.program_id(0)
    n_blocks = pl.num_programs(0) - 1

    @pl.when(i < n_blocks)
    def _():
        _mixer_block(i, sink_ref, x_ref, meta_ref, gmix_ref, win_ref, gq_ref, gk_ref,
                     wq_ref, lr_ref, li_ref, cq_ref, d_ref, wglu_ref, bglu_ref, ga_ref, gs_ref, wout_ref,
                     hp_ref, kw_ref, vw_ref, hr_ref, hi_ref,
                     qs_ref, kk_ref, vv_ref, s_ref, p_ref, oatt_ref, usl_ref, xs_ref, hb_ref, st_ref, osl_ref, mix_ref)

    @pl.when(i == n_blocks)
    def _():
        for step in _output_norm_steps(oatt_ref, osl_ref, ga_ref, gs_ref, mix_ref):
            step()
        for c in range(OUT_CHUNKS):
            _project_chunk(mix_ref, wout_ref, hp_ref, c)


OUT_CHUNKS = 4
OUT_CHUNK = D_MODEL // OUT_CHUNKS


def _project_chunk(mix_ref, wout_ref, hp_ref, c):
    hp_ref[c] = jnp.dot(mix_ref[...], wout_ref[c], preferred_element_type=F32).reshape(N_SEQ, BLOCK, OUT_CHUNK)


def _output_norm_steps(oatt_ref, osl_ref, ga_ref, gs_ref, mix_ref):
    def one(b):
        rows = slice(b * BLOCK, (b + 1) * BLOCK)
        mix_ref[rows, 0:ATTN_WIDTH] = _rms(oatt_ref[b], ga_ref[...]).astype(BF16)
        sl = [osl_ref[j, b * PITCH:b * PITCH + BLOCK, :] for j in range(N_QUARTERS)]
        ms = sum(jnp.sum(s * s, axis=-1, keepdims=True) for s in sl) / SSM_WIDTH
        inv = lax.rsqrt(ms + EPS)
        for j in range(N_QUARTERS):
            mix_ref[rows, ATTN_WIDTH + j * LANES:ATTN_WIDTH + (j + 1) * LANES] = (
                sl[j] * inv * gs_ref[:, j * LANES:(j + 1) * LANES]).astype(BF16)
    return [functools.partial(one, b) for b in range(N_SEQ)]


def _mixer_block(i, sink_ref, x_ref, meta_ref, gmix_ref, win_ref, gq_ref, gk_ref,
                 wq_ref, lr_ref, li_ref, cq_ref, d_ref, wglu_ref, bglu_ref, ga_ref, gs_ref, wout_ref,
                 hp_ref, kw_ref, vw_ref, hr_ref, hi_ref,
                 qs_ref, kk_ref, vv_ref, s_ref, p_ref, oatt_ref, usl_ref, xs_ref, hb_ref, st_ref, osl_ref, mix_ref):
    left = _left_half()
    project = lambda c: [functools.partial(_project_chunk, mix_ref, wout_ref, hp_ref, c)]

    @pl.when(i == 0)
    def _():
        oatt_ref[...] = jnp.zeros((N_SEQ, BLOCK, ATTN_WIDTH), F32)
        osl_ref[...] = jnp.zeros((N_QUARTERS, N_SEQ * PITCH, LANES), F32)
        xm = _rms(meta_ref[...], gmix_ref[...]).astype(BF16)
        zm = jnp.dot(xm, win_ref[:, ATTN_WIDTH:], preferred_element_type=F32)
        km = _pair_norm(zm[:, :KV_WIDTH], gk_ref[...], left).astype(BF16)
        vm = zm[:, KV_WIDTH:2 * KV_WIDTH].astype(BF16)
        lead = jnp.zeros((BLOCK - N_META, KV_WIDTH), BF16)
        for b in range(N_SEQ):
            kk_ref[b, 0:BLOCK - N_META, :] = lead
            vv_ref[b, 0:BLOCK - N_META, :] = lead
            kk_ref[b, BLOCK - N_META:BLOCK, :] = km
            vv_ref[b, BLOCK - N_META:BLOCK, :] = vm
        um = zm[:, 2 * KV_WIDTH:]
        for q in range(N_QUARTERS):
            xm_q = jnp.dot(um[:, q * LANES:(q + 1) * LANES].astype(BF16), wq_ref[q], preferred_element_type=F32)
            lr = jnp.broadcast_to(lr_ref[:, q * Q_STATE:(q + 1) * Q_STATE], (SUBLANES, Q_STATE))
            li = jnp.broadcast_to(li_ref[:, q * Q_STATE:(q + 1) * Q_STATE], (SUBLANES, Q_STATE))
            hr = jnp.zeros((SUBLANES, Q_STATE), F32)
            hi = jnp.zeros((SUBLANES, Q_STATE), F32)
            for t in range(N_META):
                xr = jnp.broadcast_to(xm_q[t:t + 1, 0:Q_STATE], (SUBLANES, Q_STATE))
                xi = jnp.broadcast_to(xm_q[t:t + 1, Q_STATE:], (SUBLANES, Q_STATE))
                hr, hi = _scan_step(lr, li, hr, hi, xr, xi)
            st_ref[:, 2 * q * Q_STATE:(2 * q + 1) * Q_STATE] = hr
            st_ref[:, (2 * q + 1) * Q_STATE:(2 * q + 2) * Q_STATE] = hi

    p1 = {}

    def pre_norm():
        p1["xn"] = _rms(x_ref[...].reshape(ROWS, D_MODEL), gmix_ref[...]).astype(BF16)

    def project_in(key, c0):
        p1[key] = jnp.dot(p1["xn"], win_ref[:, c0:c0 + 2 * LANES], preferred_element_type=F32)

    def finish_q(c):
        zz = p1.pop(("q", c))
        for tt in range(2):
            t = 2 * c + tt
            qn = _pair_norm(zz[:, tt * LANES:(tt + 1) * LANES], gq_ref[...], left) * ATTN_SCALE
            qa = jnp.where(left, qn, 0.0).astype(BF16)
            qb = jnp.where(left, 0.0, qn).astype(BF16)
            for b in range(N_SEQ):
                qs_ref[b, 2 * t * BLOCK:(2 * t + 1) * BLOCK, :] = qa[b * BLOCK:(b + 1) * BLOCK]
                qs_ref[b, (2 * t + 1) * BLOCK:(2 * t + 2) * BLOCK, :] = qb[b * BLOCK:(b + 1) * BLOCK]

    def finish_kv():
        zz = p1.pop("kv")
        kn = _pair_norm(zz[:, :LANES], gk_ref[...], left)
        vn = zz[:, LANES:]
        kw_ref[...] = kn.reshape(N_SEQ, BLOCK, KV_WIDTH)
        vw_ref[...] = vn.reshape(N_SEQ, BLOCK, KV_WIDTH)
        kk_ref[:, BLOCK:, :] = kn.astype(BF16).reshape(N_SEQ, BLOCK, KV_WIDTH)
        vv_ref[:, BLOCK:, :] = vn.astype(BF16).reshape(N_SEQ, BLOCK, KV_WIDTH)

    def finish_u(c):
        zz = p1.pop(("u", c))
        for tt in range(2):
            for b in range(N_SEQ):
                usl_ref[2 * c + tt, b * PITCH:b * PITCH + BLOCK, :] = zz[b * BLOCK:(b + 1) * BLOCK, tt * LANES:(tt + 1) * LANES]

    dot_kv = functools.partial(project_in, "kv", ATTN_WIDTH)
    dot_q = [functools.partial(project_in, ("q", c), c * 2 * LANES) for c in range(2)]
    dot_u = [functools.partial(project_in, ("u", c), ATTN_WIDTH + 2 * KV_WIDTH + c * 2 * LANES) for c in range(2)]
    post_q = [functools.partial(finish_q, c) for c in range(2)]
    post_u = [functools.partial(finish_u, c) for c in range(2)]

    r = lax.broadcasted_iota(jnp.int32, (BLOCK, 2 * BLOCK), 0)
    c = lax.broadcasted_iota(jnp.int32, (BLOCK, 2 * BLOCK), 1)
    valid = (c >= r) & (c <= r + WINDOW) & ((i > 0) | (c >= BLOCK - N_META))
    bias = jnp.where(valid, 0.0, -jnp.inf)

    def gather_u(n):
        t0 = n * CHUNK_T
        return jnp.concatenate(
            [jnp.concatenate([usl_ref[j, pl.ds(t0 + tl, SUBLANES, stride=PITCH), :] for j in range(N_QUARTERS)], axis=1)
             for tl in range(CHUNK_T)], axis=0)

    def feed(n, par):
        env = {}

        def scores():
            s_ref[par] = lax.dot_general(qs_ref[n], kk_ref[n], (((1,), (1,)), ((), ())), preferred_element_type=F32)

        def gather():
            env["u"] = gather_u(n)

        def scan_inputs(q):
            xs_ref[par, :, 2 * q * Q_STATE:(2 * q + 2) * Q_STATE] = jnp.dot(
                env["u"][:, q * LANES:(q + 1) * LANES].astype(BF16), wq_ref[q], preferred_element_type=F32)

        return [scores, gather] + [functools.partial(scan_inputs, q) for q in range(N_QUARTERS)]

    def mid(n, par):
        env = {}

        def row_max(g):
            s = s_ref[par, g * BLOCK:(g + 1) * BLOCK, :] + bias
            env[g] = (s, jnp.maximum(jnp.max(s, axis=-1, keepdims=True), sink_ref[g // 2 + Q_PER_KV * (g % 2)]))

        def exponent(g):
            s, m = env[g]
            p = jnp.exp(s - m)
            sink = sink_ref[g // 2 + Q_PER_KV * (g % 2)]
            env[g] = (p, 1.0 / (jnp.sum(p, axis=-1, keepdims=True) + jnp.exp(sink - m)))

        def normalise(g):
            p, inv = env.pop(g)
            p_ref[par, g * BLOCK:(g + 1) * BLOCK, :] = (p * inv).astype(BF16)

        softmax = [functools.partial(f, g) for g in range(2 * Q_TILES) for f in (row_max, exponent, normalise)]

        def load_state():
            for q in range(N_QUARTERS):
                env["lam", q] = (jnp.broadcast_to(lr_ref[:, q * Q_STATE:(q + 1) * Q_STATE], (SUBLANES, Q_STATE)),
                                 jnp.broadcast_to(li_ref[:, q * Q_STATE:(q + 1) * Q_STATE], (SUBLANES, Q_STATE)))
                env["h", q] = (st_ref[:, 2 * q * Q_STATE:(2 * q + 1) * Q_STATE],
                               st_ref[:, (2 * q + 1) * Q_STATE:(2 * q + 2) * Q_STATE])

        def scan_pair(tp, q):
            lr, li = env["lam", q]
            hr, hi = env["h", q]
            re0, im0 = 2 * q * Q_STATE, (2 * q + 1) * Q_STATE
            pair_r, pair_i = [], []
            for tl in (2 * tp, 2 * tp + 1):
                xr = xs_ref[par, tl * SUBLANES:(tl + 1) * SUBLANES, re0:re0 + Q_STATE]
                xi = xs_ref[par, tl * SUBLANES:(tl + 1) * SUBLANES, im0:im0 + Q_STATE]
                hr, hi = _scan_step(lr, li, hr, hi, xr, xi)
                pair_r.append(hr)
                pair_i.append(hi)
            env["h", q] = (hr, hi)
            rows = slice(2 * tp * SUBLANES, (2 * tp + 2) * SUBLANES)
            hb_ref[par, rows, re0:re0 + Q_STATE] = jnp.concatenate(pair_r, axis=0).astype(BF16)
            hb_ref[par, rows, im0:im0 + Q_STATE] = jnp.concatenate(pair_i, axis=0).astype(BF16)

        def store_state():
            for q in range(N_QUARTERS):
                hr, hi = env["h", q]
                st_ref[:, 2 * q * Q_STATE:(2 * q + 1) * Q_STATE] = hr
                st_ref[:, (2 * q + 1) * Q_STATE:(2 * q + 2) * Q_STATE] = hi

        scan = ([load_state] + [functools.partial(scan_pair, tp, q) for tp in range(CHUNK_T // 2) for q in range(N_QUARTERS)]
                + [store_state])
        return softmax, scan

    def tail(n, par):
        env = {}

        def attn_out():
            o_all = jnp.dot(p_ref[par], vv_ref[n], preferred_element_type=F32)
            for t in range(Q_TILES):
                oatt_ref[n, :, t * LANES:(t + 1) * LANES] = jnp.where(
                    left, o_all[2 * t * BLOCK:(2 * t + 1) * BLOCK], o_all[(2 * t + 1) * BLOCK:(2 * t + 2) * BLOCK])

        def gather():
            env["u"] = gather_u(n)

        def readout(q):
            env["y", q] = jnp.dot(hb_ref[par, :, 2 * q * Q_STATE:(2 * q + 2) * Q_STATE], cq_ref[q],
                                  preferred_element_type=F32)

        def activate(q):
            cols = slice(q * LANES, (q + 1) * LANES)
            env["g", q] = jax.nn.gelu(env.pop(("y", q)) + d_ref[:, cols] * env["u"][:, cols])

        def gate(q):
            cols = slice(q * LANES, (q + 1) * LANES)
            env["gate", q] = jnp.dot(env["g", q].astype(BF16), wglu_ref[q], preferred_element_type=F32) + bglu_ref[:, cols]

        def emit(q):
            o = env.pop(("g", q)) * jax.nn.sigmoid(env.pop(("gate", q)))
            t0 = n * CHUNK_T
            for tl in range(CHUNK_T):
                osl_ref[q, pl.ds(t0 + tl, SUBLANES, stride=PITCH), :] = o[tl * SUBLANES:(tl + 1) * SUBLANES]

        per_quarter = lambda f: [functools.partial(f, q) for q in range(N_QUARTERS)]
        return [attn_out, gather] + per_quarter(readout) + per_quarter(activate) + per_quarter(gate) + per_quarter(emit)

    def run(*stages):
        for step in _interleave(*stages):
            step()

    def steady(k, carry):
        n = 2 * k + 1
        run(project(k) + tail(n - 1, 0), *mid(n, 1), feed(n + 1, 0))
        run(feed(n + 2, 1), *mid(n + 1, 0), tail(n, 1))
        return carry

    norms = _output_norm_steps(oatt_ref, osl_ref, ga_ref, gs_ref, mix_ref)
    feed0, feed1 = feed(0, 0), feed(1, 1)
    softmax0, scan0 = mid(0, 0)
    half = len(softmax0) // 2
    run([pre_norm, dot_kv, dot_q[0], finish_kv, dot_q[1], post_q[0], dot_u[0], post_q[1]], norms)
    run(feed0[:1])
    run([dot_u[1]] + feed1[:1] + post_u, softmax0[:half])
    run(feed0[1:], softmax0[half:])
    run(feed1[1:], scan0)
    n_steady = (N_SEQ - 2) // 2
    assert n_steady == OUT_CHUNKS - 1
    lax.fori_loop(0, n_steady, steady, 0)
    run(*mid(N_SEQ - 1, 1), tail(N_SEQ - 2, 0), project(OUT_CHUNKS - 1))
    run(tail(N_SEQ - 1, 1))
    kk_ref[:, 0:BLOCK, :] = kk_ref[:, BLOCK:, :]
    vv_ref[:, 0:BLOCK, :] = vv_ref[:, BLOCK:, :]

    hr_ref[...] = jnp.concatenate([st_ref[:, 2 * q * Q_STATE:(2 * q + 1) * Q_STATE] for q in range(N_QUARTERS)], axis=1)
    hi_ref[...] = jnp.concatenate([st_ref[:, (2 * q + 1) * Q_STATE:(2 * q + 2) * Q_STATE] for q in range(N_QUARTERS)], axis=1)


def _mixer(x_prompt, meta_tokens, sinks, gmix, win, gq2, gk2, sp, ga, gs, wout):
    nseq, seq, _ = x_prompt.shape
    assert nseq == N_SEQ and seq % BLOCK == 0
    n_blocks = seq // BLOCK
    st = pl.BlockSpec((N_SEQ, STATE_COLS), lambda i: (0, 0))
    kvw = pl.BlockSpec((N_SEQ, BLOCK, KV_WIDTH), lambda i: (0, 0, 0))
    return pl.pallas_call(
        _mixer_kernel,
        grid=(n_blocks + 1,),
        in_specs=[pl.BlockSpec(memory_space=pltpu.SMEM),
                  pl.BlockSpec((N_SEQ, BLOCK, D_MODEL), lambda i: (0, jnp.minimum(i, n_blocks - 1), 0)),
                  _const_spec((N_META, D_MODEL)),
                  _const_spec((1, D_MODEL)), _const_spec((D_MODEL, IN_COLS)), _const_spec((1, LANES)),
                  _const_spec((1, LANES)),
                  _const_spec((N_QUARTERS, LANES, 2 * Q_STATE)), _const_spec((1, STATE_COLS)),
                  _const_spec((1, STATE_COLS)), _const_spec((N_QUARTERS, 2 * Q_STATE, LANES)),
                  _const_spec((1, SSM_WIDTH)), _const_spec((N_QUARTERS, LANES, LANES)), _const_spec((1, SSM_WIDTH)),
                  _const_spec((1, ATTN_WIDTH)), _const_spec((1, SSM_WIDTH)),
                  _const_spec((OUT_CHUNKS, D_MODEL, OUT_CHUNK))],
        out_specs=[pl.BlockSpec((OUT_CHUNKS, N_SEQ, BLOCK, OUT_CHUNK), lambda i: (0, 0, jnp.maximum(i - 1, 0), 0)),
                   kvw, kvw, st, st],
        out_shape=[jax.ShapeDtypeStruct((OUT_CHUNKS, N_SEQ, seq, OUT_CHUNK), F32),
                   jax.ShapeDtypeStruct((N_SEQ, BLOCK, KV_WIDTH), F32), jax.ShapeDtypeStruct((N_SEQ, BLOCK, KV_WIDTH), F32),
                   jax.ShapeDtypeStruct((N_SEQ, STATE_COLS), F32), jax.ShapeDtypeStruct((N_SEQ, STATE_COLS), F32)],
        scratch_shapes=[pltpu.VMEM((N_SEQ, 2 * Q_TILES * BLOCK, LANES), BF16),
                        pltpu.VMEM((N_SEQ, 2 * BLOCK, KV_WIDTH), BF16),
                        pltpu.VMEM((N_SEQ, 2 * BLOCK, KV_WIDTH), BF16),
                        pltpu.VMEM((2, 2 * Q_TILES * BLOCK, 2 * BLOCK), F32),
                        pltpu.VMEM((2, 2 * Q_TILES * BLOCK, 2 * BLOCK), BF16),
                        pltpu.VMEM((N_SEQ, BLOCK, ATTN_WIDTH), F32),
                        pltpu.VMEM((N_QUARTERS, N_SEQ * PITCH, LANES), F32),
                        pltpu.VMEM((2, CHUNK_T * N_SEQ, 2 * STATE_COLS), F32),
                        pltpu.VMEM((2, CHUNK_T * N_SEQ, 2 * STATE_COLS), BF16),
                        pltpu.VMEM((N_SEQ, 2 * STATE_COLS), F32),
                        pltpu.VMEM((N_QUARTERS, N_SEQ * PITCH, LANES), F32),
                        pltpu.VMEM((ROWS, D_MODEL), BF16)],
        compiler_params=pltpu.CompilerParams(dimension_semantics=("arbitrary",), vmem_limit_bytes=VMEM_LIMIT),
        name="mixer",
    )(sinks, x_prompt, meta_tokens, gmix, win, gq2, gk2, sp["wq"], sp["lr"], sp["li"], sp["cq"], sp["d"],
      sp["wgluq"], sp["bglu"], ga, gs, wout)


def kernel(x_prompt, x_sample, cache_k_win, cache_v_win, state_ssm_re, state_ssm_im, meta_tokens, g_mix, w_in, g_q,
           g_k, sinks, ssm_a_re, ssm_a_im, ssm_log_dt, ssm_b_re, ssm_b_im, ssm_c_re, ssm_c_im, ssm_d, ssm_w_glu,
           ssm_b_glu, g_att_out, g_ssm_out, w_out, g_ffn, w_gate, w_up, w_down):
    bp, seq, _ = x_prompt.shape
    db, dseq, _ = x_sample.shape
    li = 0
    gmix = g_mix[li].reshape(1, D_MODEL)
    win = w_in[li].astype(BF16)
    gq2 = jnp.tile(g_q[li], 2).reshape(1, LANES)
    gk2 = jnp.tile(g_k[li], 2).reshape(1, LANES)
    sk = sinks[li]
    sp = _ssm_params(ssm_a_re[li], ssm_a_im[li], ssm_log_dt[li], ssm_b_re[li], ssm_b_im[li], ssm_c_re[li],
                     ssm_c_im[li], ssm_d[li], ssm_w_glu[li], ssm_b_glu[li])
    ga = g_att_out[li].reshape(1, ATTN_WIDTH)
    gs = g_ssm_out[li].reshape(1, SSM_WIDTH)
    wout = w_out[li].astype(BF16)
    ffn_w = (g_ffn[li].reshape(1, D_MODEL), w_gate[li].astype(BF16), w_up[li].astype(BF16), w_down[li].astype(BF16))
    regroup = lambda a, axis: jnp.swapaxes(
        a.reshape(a.shape[:axis] + (2, Q_PER_KV, HEAD_DIM) + a.shape[axis + 1:]), axis, axis + 1).reshape(a.shape)
    win = jnp.concatenate([regroup(win[:, :ATTN_WIDTH], 1), win[:, ATTN_WIDTH:]], axis=1)
    wout = jnp.concatenate([regroup(wout[:ATTN_WIDTH], 0), wout[ATTN_WIDTH:]], axis=0)
    ga = regroup(ga, 1)

    wout_chunks = wout.reshape(D_MODEL, OUT_CHUNKS, OUT_CHUNK).transpose(1, 0, 2)
    hproj, kw_p, vw_p, hp_r, hp_i = _mixer(x_prompt, meta_tokens, sk, gmix, win, gq2, gk2, sp, ga, gs, wout_chunks)
    y_prompt = _ffn_call(x_prompt.reshape(bp * seq, D_MODEL), hproj.reshape(OUT_CHUNKS, bp * seq, OUT_CHUNK), *ffn_w,
                         tm=512).reshape(bp, seq, D_MODEL)

    n_s = db * dseq
    xs_rows = x_sample.reshape(n_s, D_MODEL)
    q_s, kt_s, vt_s, u_s = _front(xs_rows, gmix, win, gq2, gk2, tm=512)
    to_t = lambda a: jnp.transpose(a, (0, 2, 3, 1)).reshape(db, KV_WIDTH, WINDOW)
    from_t = lambda a: jnp.transpose(a.reshape(db, N_KV_HEADS, HEAD_DIM, WINDOW), (0, 3, 1, 2))[None]
    oa_s, kwt_s, vwt_s = _attn_sample(q_s.reshape(db, dseq, ATTN_WIDTH), kt_s, vt_s, to_t(cache_k_win[li]),
                                      to_t(cache_v_win[li]), sk)
    os_s, hs_r, hs_i = _ssm_sample(u_s, state_ssm_re[li].reshape(db, STATE_COLS),
                                   state_ssm_im[li].reshape(db, STATE_COLS), sp, nseq=db, t_steps=dseq)
    y_sample = _back(xs_rows, oa_s.reshape(n_s, ATTN_WIDTH), os_s, ga, gs, wout, *ffn_w, tm=512).reshape(db, dseq, D_MODEL)

    kv5 = lambda a, n: a.reshape(1, n, WINDOW, N_KV_HEADS, HEAD_DIM)
    st4 = lambda a, n: a.reshape(1, n, N_SSM_GROUPS, SSM_STATE)
    return (y_prompt, y_sample, kv5(kw_p, bp), kv5(vw_p, bp), st4(hp_r, bp), st4(hp_i, bp),
            from_t(kwt_s), from_t(vwt_s), st4(hs_r, db), st4(hs_i, db))
```

```python
import functools

import jax
import jax.numpy as jnp
from jax import lax
from jax.experimental import pallas as pl
from jax.experimental.pallas import tpu as pltpu

D_MODEL = 1024
N_META = 16
HEAD_DIM = 64
ATTN_WIDTH = 512
Q_PER_KV = 4
N_KV_HEADS = 2
KV_WIDTH = 128
WINDOW = 128
BLOCK = 128
SSM_WIDTH = 512
SSM_GROUP = 16
N_SSM_GROUPS = 32
SSM_STATE = 64
IN_COLS = ATTN_WIDTH + 2 * KV_WIDTH + SSM_WIDTH
D_FF = 2816
EPS = 1e-6
ATTN_SCALE = HEAD_DIM ** -0.5
STATE_COLS = N_SSM_GROUPS * SSM_STATE
LANES = 128
SUBLANES = 8
VMEM_LIMIT = 56 * 1024 * 1024

F32 = jnp.float32
BF16 = jnp.bfloat16


def _const_spec(shape):
    return pl.BlockSpec(shape, lambda *_: (0,) * len(shape), pipeline_mode=pl.Buffered(1))


def _rms(x, g):
    return x * lax.rsqrt(jnp.mean(x * x, axis=-1, keepdims=True) + EPS) * g


def _left_half():
    return lax.broadcasted_iota(jnp.int32, (1, LANES), 1) < HEAD_DIM


def _pair_norm(zz, g2, left):
    sq = zz * zz
    sl = jnp.sum(jnp.where(left, sq, 0.0), axis=-1, keepdims=True)
    sr = jnp.sum(jnp.where(left, 0.0, sq), axis=-1, keepdims=True)
    inv = jnp.where(left, lax.rsqrt(sl / HEAD_DIM + EPS), lax.rsqrt(sr / HEAD_DIM + EPS))
    return zz * inv * g2


def _front_kernel(x_ref, gmix_ref, win_ref, gq_ref, gk_ref, q_ref, kt_ref, vt_ref, u_ref):
    xn = _rms(x_ref[...], gmix_ref[...]).astype(BF16)
    z = jnp.dot(xn, win_ref[...], preferred_element_type=F32)
    left = _left_half()
    for p in range(ATTN_WIDTH // LANES):
        q_ref[:, p * LANES:(p + 1) * LANES] = _pair_norm(z[:, p * LANES:(p + 1) * LANES], gq_ref[...], left)
    kt_ref[...] = _pair_norm(z[:, ATTN_WIDTH:ATTN_WIDTH + KV_WIDTH], gk_ref[...], left).T
    vt_ref[...] = z[:, ATTN_WIDTH + KV_WIDTH:ATTN_WIDTH + 2 * KV_WIDTH].T
    u_ref[...] = z[:, ATTN_WIDTH + 2 * KV_WIDTH:]


def _front(x_rows, gmix, win_bf, gq2, gk2, tm):
    n = x_rows.shape[0]
    assert n % tm == 0
    row = lambda w: pl.BlockSpec((tm, w), lambda i: (i, 0))
    col = pl.BlockSpec((KV_WIDTH, tm), lambda i: (0, i))
    return pl.pallas_call(
        _front_kernel,
        grid=(n // tm,),
        in_specs=[row(D_MODEL), _const_spec((1, D_MODEL)), _const_spec((D_MODEL, IN_COLS)),
                  _const_spec((1, LANES)), _const_spec((1, LANES))],
        out_specs=[row(ATTN_WIDTH), col, col, row(SSM_WIDTH)],
        out_shape=[jax.ShapeDtypeStruct((n, ATTN_WIDTH), F32), jax.ShapeDtypeStruct((KV_WIDTH, n), F32),
                   jax.ShapeDtypeStruct((KV_WIDTH, n), F32), jax.ShapeDtypeStruct((n, SSM_WIDTH), F32)],
        compiler_params=pltpu.CompilerParams(dimension_semantics=("arbitrary",), vmem_limit_bytes=VMEM_LIMIT),
        name="front",
    )(x_rows, gmix, win_bf, gq2, gk2)


Q_TILES = ATTN_WIDTH // LANES


def _attn_sample_kernel(sink_ref, q_ref, knt_ref, vnt_ref, ckt_ref, cvt_ref, o_ref, kwt_ref, vwt_ref, *, bb, t):
    left = _left_half()
    n_heads = 2 * Q_TILES
    rows = n_heads * t
    nt_dims = (((1,), (1,)), ((), ()))
    r = lax.broadcasted_iota(jnp.int32, (rows, WINDOW), 0) % t
    c = lax.broadcasted_iota(jnp.int32, (rows, WINDOW), 1)
    bias_cache = jnp.where(c >= r, 0.0, -jnp.inf)
    lane = lax.broadcasted_iota(jnp.int32, (KV_WIDTH, WINDOW), 1)
    hrow = lax.broadcasted_iota(jnp.int32, (rows, 1), 0) // t
    sink = jnp.zeros((rows, 1), F32)
    for g in range(n_heads):
        sink = jnp.where(hrow == g, sink_ref[g // 2 + Q_PER_KV * (g % 2)], sink)
    knt, vnt = knt_ref[...], vnt_ref[...]
    knt_bf, vnt_bf = knt.astype(BF16), vnt.astype(BF16)
    scores, values = [], []
    for bi in range(bb):
        ckt, cvt = ckt_ref[bi], cvt_ref[bi]
        shift = (WINDOW - t - bi * t) % WINDOW
        new_k, new_v = (knt, vnt) if shift == 0 else (pltpu.roll(knt, shift, 1), pltpu.roll(vnt, shift, 1))
        kwt_ref[bi] = jnp.where(lane < WINDOW - t, pltpu.roll(ckt, WINDOW - t, 1), new_k)
        vwt_ref[bi] = jnp.where(lane < WINDOW - t, pltpu.roll(cvt, WINDOW - t, 1), new_v)
        q = q_ref[bi] * ATTN_SCALE
        pieces = []
        for tile in range(Q_TILES):
            qt = q[:, tile * LANES:(tile + 1) * LANES]
            pieces += [jnp.where(left, qt, 0.0), jnp.where(left, 0.0, qt)]
        qs = jnp.concatenate(pieces, axis=0).astype(BF16)
        scores.append((jnp.dot(qs, ckt.astype(BF16), preferred_element_type=F32),
                       jnp.dot(qs, knt_bf, preferred_element_type=F32)))
        values.append(cvt.astype(BF16))
    probs = []
    for bi in range(bb):
        own = (c >= bi * t) & (c <= bi * t + r)
        s = jnp.concatenate([scores[bi][0] + bias_cache, jnp.where(own, scores[bi][1], -jnp.inf)], axis=1)
        m = jnp.maximum(jnp.max(s, axis=-1, keepdims=True), sink)
        p = jnp.exp(s - m)
        inv = 1.0 / (jnp.sum(p, axis=-1, keepdims=True) + jnp.exp(sink - m))
        probs.append((p * inv).astype(BF16))
    for bi in range(bb):
        o = (lax.dot_general(probs[bi][:, :WINDOW], values[bi], nt_dims, preferred_element_type=F32)
             + lax.dot_general(probs[bi][:, WINDOW:], vnt_bf, nt_dims, preferred_element_type=F32))
        for tile in range(Q_TILES):
            o_ref[bi, :, tile * LANES:(tile + 1) * LANES] = jnp.where(
                left, o[2 * tile * t:(2 * tile + 1) * t], o[(2 * tile + 1) * t:(2 * tile + 2) * t])


def _attn_sample(q, knt, vnt, ckt, cvt, sinks):
    db, t, _ = q.shape
    bb = WINDOW // t
    assert WINDOW == LANES and bb * t == LANES and db % bb == 0
    blk = lambda r, w: pl.BlockSpec((bb, r, w), lambda i: (i, 0, 0))
    new = pl.BlockSpec((KV_WIDTH, bb * t), lambda i: (0, i))
    return pl.pallas_call(
        functools.partial(_attn_sample_kernel, bb=bb, t=t),
        grid=(db // bb,),
        in_specs=[pl.BlockSpec(memory_space=pltpu.SMEM), blk(t, ATTN_WIDTH), new, new,
                  blk(KV_WIDTH, WINDOW), blk(KV_WIDTH, WINDOW)],
        out_specs=[blk(t, ATTN_WIDTH), blk(KV_WIDTH, WINDOW), blk(KV_WIDTH, WINDOW)],
        out_shape=[jax.ShapeDtypeStruct((db, t, ATTN_WIDTH), F32),
                   jax.ShapeDtypeStruct((db, KV_WIDTH, WINDOW), F32),
                   jax.ShapeDtypeStruct((db, KV_WIDTH, WINDOW), F32)],
        compiler_params=pltpu.CompilerParams(dimension_semantics=("arbitrary",)),
        name="attn_sample",
    )(sinks, q, knt, vnt, ckt, cvt)


N_QUARTERS = 4
Q_GROUPS = N_SSM_GROUPS // N_QUARTERS
Q_STATE = Q_GROUPS * SSM_STATE


def _zoh_kernel(are_ref, aim_ref, logdt_ref, bre_ref, bim_ref, cre_ref, cim_ref, wg_ref,
                lr_ref, li_ref, wq_ref, cq_ref, wglu_ref):
    ar, ai = are_ref[...], aim_ref[...]
    dt = jnp.exp(logdt_ref[...])
    mag = jnp.exp(ar * dt)
    lr, li = mag * jnp.cos(ai * dt), mag * jnp.sin(ai * dt)
    nr, ni = lr - 1.0, li
    den = ar * ar + ai * ai
    fr, fi = ((nr * ar + ni * ai) / den)[:, None, :], ((ni * ar - nr * ai) / den)[:, None, :]
    br, bi = bre_ref[...], bim_ref[...]
    lr_ref[...] = lr
    li_ref[...] = li
    bbr = fr * br - fi * bi
    bbi = fr * bi + fi * br

    def block_diag(a, c):
        k = lax.broadcasted_iota(jnp.int32, (a.shape[1], Q_GROUPS * c), 0)
        j = lax.broadcasted_iota(jnp.int32, (a.shape[1], Q_GROUPS * c), 1)
        wide = jnp.dot(a, (j % c == k).astype(F32), precision=lax.Precision.HIGHEST, preferred_element_type=F32)
        r = lax.broadcasted_iota(jnp.int32, wide.shape, 0) // (a.shape[0] // Q_GROUPS)
        m = lax.broadcasted_iota(jnp.int32, wide.shape, 1) // c
        return jnp.where(r == m, wide, 0.0)

    quarters = lambda a: a.reshape(N_QUARTERS, Q_GROUPS * a.shape[1], a.shape[2])
    bqr, bqi, cqr, cqi, wg = (quarters(a) for a in (bbr, bbi, cre_ref[...], -cim_ref[...], wg_ref[...]))
    for q in range(N_QUARTERS):
        wq_ref[q] = jnp.concatenate([block_diag(bqr[q], SSM_STATE), block_diag(bqi[q], SSM_STATE)], axis=1).astype(BF16)
        cq_ref[q] = jnp.concatenate([block_diag(cqr[q], SSM_STATE).T, block_diag(cqi[q], SSM_STATE).T], axis=0).astype(BF16)
        wglu_ref[q] = block_diag(wg[q], SSM_GROUP).astype(BF16)


def _ssm_params(a_re, a_im, log_dt, b_re, b_im, c_re, c_im, d_skip, w_glu, b_glu):
    g, p, h = b_re.shape
    assert (g, p, h) == (N_SSM_GROUPS, SSM_STATE, SSM_GROUP) and Q_GROUPS * h == LANES
    lr, li, wq, cq, wglu = pl.pallas_call(
        _zoh_kernel,
        out_shape=[jax.ShapeDtypeStruct((g, p), F32), jax.ShapeDtypeStruct((g, p), F32),
                   jax.ShapeDtypeStruct((N_QUARTERS, LANES, 2 * Q_STATE), BF16),
                   jax.ShapeDtypeStruct((N_QUARTERS, 2 * Q_STATE, LANES), BF16),
                   jax.ShapeDtypeStruct((N_QUARTERS, LANES, LANES), BF16)],
        name="zoh",
    )(a_re, a_im, log_dt.reshape(g, 1), jnp.swapaxes(b_re, 1, 2), jnp.swapaxes(b_im, 1, 2), c_re, c_im,
      jnp.pad(w_glu, ((0, 0), (0, 0), (0, LANES - h))))
    return dict(wq=wq, cq=cq, wgluq=wglu,
                lr=lr.reshape(1, STATE_COLS), li=li.reshape(1, STATE_COLS), d=d_skip.reshape(1, SSM_WIDTH),
                bglu=b_glu.reshape(1, SSM_WIDTH))


def _scan_step(lr, li, hr, hi, xr, xi):
    return lr * hr - li * hi + xr, lr * hi + li * hr + xi


def _ssm_sample_kernel(u0_ref, u1_ref, u2_ref, u3_ref, h0r_ref, h0i_ref, wq_ref, lr_ref, li_ref, cq_ref, d_ref,
                       wglu_ref, bglu_ref, o_ref, hr_ref, hi_ref, utb_ref, xs_ref, *, nseq, t_steps):
    u_refs = (u0_ref, u1_ref, u2_ref, u3_ref)
    quarters = range(N_QUARTERS)
    n_rg = nseq // SUBLANES
    rg_rows = SUBLANES * t_steps

    for t in range(t_steps):
        for rg in range(n_rg):
            for j in quarters:
                utb_ref[t * nseq + rg * SUBLANES:t * nseq + (rg + 1) * SUBLANES, j * LANES:(j + 1) * LANES] = (
                    u_refs[j][pl.ds(rg * rg_rows + t, SUBLANES, stride=t_steps), :])

    u = [utb_ref[:, q * LANES:(q + 1) * LANES] for q in quarters]
    for q in quarters:
        xs_ref[q] = jnp.dot(u[q].astype(BF16), wq_ref[q], preferred_element_type=F32)
    lam = [(jnp.broadcast_to(lr_ref[:, q * Q_STATE:(q + 1) * Q_STATE], (SUBLANES, Q_STATE)),
            jnp.broadcast_to(li_ref[:, q * Q_STATE:(q + 1) * Q_STATE], (SUBLANES, Q_STATE))) for q in quarters]

    def scan_group(rg, carry):
        r0 = pl.multiple_of(rg * SUBLANES, SUBLANES)
        h = [(h0r_ref[pl.ds(r0, SUBLANES), q * Q_STATE:(q + 1) * Q_STATE],
              h0i_ref[pl.ds(r0, SUBLANES), q * Q_STATE:(q + 1) * Q_STATE]) for q in quarters]
        for t in range(t_steps):
            rows = pl.ds(pl.multiple_of(t * nseq + r0, SUBLANES), SUBLANES)
            for q in quarters:
                h[q] = _scan_step(*lam[q], *h[q], xs_ref[q, rows, 0:Q_STATE], xs_ref[q, rows, Q_STATE:])
                xs_ref[q, rows, 0:Q_STATE] = h[q][0]
                xs_ref[q, rows, Q_STATE:] = h[q][1]
        for q in quarters:
            hr_ref[pl.ds(r0, SUBLANES), q * Q_STATE:(q + 1) * Q_STATE] = h[q][0]
            hi_ref[pl.ds(r0, SUBLANES), q * Q_STATE:(q + 1) * Q_STATE] = h[q][1]
        return carry

    lax.fori_loop(0, n_rg, scan_group, 0)
    y = [jnp.dot(xs_ref[q].astype(BF16), cq_ref[q], preferred_element_type=F32) for q in quarters]
    g = [jax.nn.gelu(y[q] + d_ref[:, q * LANES:(q + 1) * LANES] * u[q]) for q in quarters]
    gate = [jnp.dot(g[q].astype(BF16), wglu_ref[q], preferred_element_type=F32) + bglu_ref[:, q * LANES:(q + 1) * LANES]
            for q in quarters]
    for q in quarters:
        o = g[q] * jax.nn.sigmoid(gate[q])
        for t in range(t_steps):
            for rg in range(n_rg):
                o_ref[q, pl.ds(rg * rg_rows + t, SUBLANES, stride=t_steps), :] = (
                    o[t * nseq + rg * SUBLANES:t * nseq + (rg + 1) * SUBLANES, :])


def _ssm_sample(u_rows, h0r, h0i, sp, nseq, t_steps):
    n = nseq * t_steps
    slab = lambda j: pl.BlockSpec((n, LANES), lambda i, j=j: (0, j))
    st = pl.BlockSpec((nseq, STATE_COLS), lambda i: (0, 0))
    return pl.pallas_call(
        functools.partial(_ssm_sample_kernel, nseq=nseq, t_steps=t_steps),
        grid=(1,),
        in_specs=[slab(0), slab(1), slab(2), slab(3), st, st,
                  _const_spec((N_QUARTERS, LANES, 2 * Q_STATE)), _const_spec((1, STATE_COLS)),
                  _const_spec((1, STATE_COLS)), _const_spec((N_QUARTERS, 2 * Q_STATE, LANES)),
                  _const_spec((1, SSM_WIDTH)), _const_spec((N_QUARTERS, LANES, LANES)), _const_spec((1, SSM_WIDTH))],
        out_specs=[pl.BlockSpec((N_QUARTERS, n, LANES), lambda i: (0, 0, 0)), st, st],
        out_shape=[jax.ShapeDtypeStruct((N_QUARTERS, n, LANES), F32),
                   jax.ShapeDtypeStruct((nseq, STATE_COLS), F32), jax.ShapeDtypeStruct((nseq, STATE_COLS), F32)],
        scratch_shapes=[pltpu.VMEM((n, SSM_WIDTH), F32), pltpu.VMEM((N_QUARTERS, n, 2 * Q_STATE), F32)],
        compiler_params=pltpu.CompilerParams(dimension_semantics=("arbitrary",), vmem_limit_bytes=VMEM_LIMIT),
        name="ssm_sample",
    )(u_rows, u_rows, u_rows, u_rows, h0r, h0i, sp["wq"], sp["lr"], sp["li"], sp["cq"], sp["d"], sp["wgluq"], sp["bglu"])


def _ffn(h, gf_ref, wg_ref, wu_ref, wd_ref):
    f = _rms(h, gf_ref[...]).astype(BF16)
    gate = jnp.dot(f, wg_ref[...], preferred_element_type=F32)
    up = jnp.dot(f, wu_ref[...], preferred_element_type=F32)
    a = (jax.nn.silu(gate) * up).astype(BF16)
    return h + jnp.dot(a, wd_ref[...], preferred_element_type=F32)


def _back_kernel(x_ref, oa_ref, os_ref, ga_ref, gs_ref, wout_ref, gf_ref, wg_ref, wu_ref, wd_ref, y_ref):
    o_ssm = jnp.concatenate([os_ref[j] for j in range(N_QUARTERS)], axis=-1)
    mix = jnp.concatenate([_rms(oa_ref[...], ga_ref[...]), _rms(o_ssm, gs_ref[...])], axis=-1).astype(BF16)
    h = x_ref[...] + jnp.dot(mix, wout_ref[...], preferred_element_type=F32)
    y_ref[...] = _ffn(h, gf_ref, wg_ref, wu_ref, wd_ref)


def _ffn_kernel(x_ref, hp_ref, gf_ref, wg_ref, wu_ref, wd_ref, y_ref):
    hp = jnp.concatenate([hp_ref[c] for c in range(hp_ref.shape[0])], axis=-1)
    y_ref[...] = _ffn(x_ref[...] + hp, gf_ref, wg_ref, wu_ref, wd_ref)


def _back(x_rows, oa, os_, ga, gs, wout, gf, wg, wu, wd, tm):
    n = x_rows.shape[0]
    assert n % tm == 0
    row = lambda w: pl.BlockSpec((tm, w), lambda i: (i, 0))
    return pl.pallas_call(
        _back_kernel,
        grid=(n // tm,),
        in_specs=[row(D_MODEL), row(ATTN_WIDTH), pl.BlockSpec((N_QUARTERS, tm, LANES), lambda i: (0, i, 0)),
                  _const_spec((1, ATTN_WIDTH)), _const_spec((1, SSM_WIDTH)), _const_spec((D_MODEL, D_MODEL)),
                  _const_spec((1, D_MODEL)), _const_spec((D_MODEL, D_FF)), _const_spec((D_MODEL, D_FF)),
                  _const_spec((D_FF, D_MODEL))],
        out_specs=row(D_MODEL),
        out_shape=jax.ShapeDtypeStruct((n, D_MODEL), F32),
        compiler_params=pltpu.CompilerParams(dimension_semantics=("arbitrary",), vmem_limit_bytes=VMEM_LIMIT),
        name="back",
    )(x_rows, oa, os_, ga, gs, wout, gf, wg, wu, wd)


def _ffn_call(x_rows, hp_rows, gf, wg, wu, wd, tm):
    n = x_rows.shape[0]
    assert n % tm == 0
    row = lambda w: pl.BlockSpec((tm, w), lambda i: (i, 0))
    return pl.pallas_call(
        _ffn_kernel,
        grid=(n // tm,),
        in_specs=[row(D_MODEL), pl.BlockSpec((hp_rows.shape[0], tm, hp_rows.shape[2]), lambda i: (0, i, 0)),
                  _const_spec((1, D_MODEL)),
                  _const_spec((D_MODEL, D_FF)), _const_spec((D_MODEL, D_FF)), _const_spec((D_FF, D_MODEL))],
        out_specs=row(D_MODEL),
        out_shape=jax.ShapeDtypeStruct((n, D_MODEL), F32),
        compiler_params=pltpu.CompilerParams(dimension_semantics=("arbitrary",), vmem_limit_bytes=VMEM_LIMIT),
        name="ffn",
    )(x_rows, hp_rows, gf, wg, wu, wd)


N_SEQ = 8
ROWS = N_SEQ * BLOCK
PITCH = BLOCK + SUBLANES // 2
CHUNK_T = BLOCK // N_SEQ


def _interleave(*stages):
    keyed = [((i + 0.5) / len(steps), k, i, step) for k, steps in enumerate(stages) for i, step in enumerate(steps)]
    return [step for _, _, _, step in sorted(keyed, key=lambda e: e[:3])]


def _mixer_kernel(sink_ref, x_ref, meta_ref, gmix_ref, win_ref, gq_ref, gk_ref,
                  wq_ref, lr_ref, li_ref, cq_ref, d_ref, wglu_ref, bglu_ref, ga_ref, gs_ref, wout_ref,
                  hp_ref, kw_ref, vw_ref, hr_ref, hi_ref,
                  qs_ref, kk_ref, vv_ref, s_ref, p_ref, oatt_ref, usl_ref, xs_ref, hb_ref, st_ref, osl_ref, mix_ref):
    i = pl.program_id(0)
    n_blocks = pl.num_programs(0) - 1

    @pl.when(i < n_blocks)
    def _():
        _mixer_block(i, sink_ref, x_ref, meta_ref, gmix_ref, win_ref, gq_ref, gk_ref,
                     wq_ref, lr_ref, li_ref, cq_ref, d_ref, wglu_ref, bglu_ref, ga_ref, gs_ref, wout_ref,
                     hp_ref, kw_ref, vw_ref, hr_ref, hi_ref,
                     qs_ref, kk_ref, vv_ref, s_ref, p_ref, oatt_ref, usl_ref, xs_ref, hb_ref, st_ref, osl_ref, mix_ref)

    @pl.when(i == n_blocks)
    def _():
        for step in _output_norm_steps(oatt_ref, osl_ref, ga_ref, gs_ref, mix_ref):
            step()
        for c in range(OUT_CHUNKS):
            _project_chunk(mix_ref, wout_ref, hp_ref, c)


OUT_CHUNKS = 4
OUT_CHUNK = D_MODEL // OUT_CHUNKS


def _project_chunk(mix_ref, wout_ref, hp_ref, c):
    hp_ref[c] = jnp.dot(mix_ref[...], wout_ref[c], preferred_element_type=F32).reshape(N_SEQ, BLOCK, OUT_CHUNK)


def _output_norm_steps(oatt_ref, osl_ref, ga_ref, gs_ref, mix_ref):
    def one(b):
        rows = slice(b * BLOCK, (b + 1) * BLOCK)
        mix_ref[rows, 0:ATTN_WIDTH] = _rms(oatt_ref[b], ga_ref[...]).astype(BF16)
        sl = [osl_ref[j, b * PITCH:b * PITCH + BLOCK, :] for j in range(N_QUARTERS)]
        ms = sum(jnp.sum(s * s, axis=-1, keepdims=True) for s in sl) / SSM_WIDTH
        inv = lax.rsqrt(ms + EPS)
        for j in range(N_QUARTERS):
            mix_ref[rows, ATTN_WIDTH + j * LANES:ATTN_WIDTH + (j + 1) * LANES] = (
                sl[j] * inv * gs_ref[:, j * LANES:(j + 1) * LANES]).astype(BF16)
    return [functools.partial(one, b) for b in range(N_SEQ)]


def _mixer_block(i, sink_ref, x_ref, meta_ref, gmix_ref, win_ref, gq_ref, gk_ref,
                 wq_ref, lr_ref, li_ref, cq_ref, d_ref, wglu_ref, bglu_ref, ga_ref, gs_ref, wout_ref,
                 hp_ref, kw_ref, vw_ref, hr_ref, hi_ref,
                 qs_ref, kk_ref, vv_ref, s_ref, p_ref, oatt_ref, usl_ref, xs_ref, hb_ref, st_ref, osl_ref, mix_ref):
    left = _left_half()
    project = lambda c: [functools.partial(_project_chunk, mix_ref, wout_ref, hp_ref, c)]

    @pl.when(i == 0)
    def _():
        oatt_ref[...] = jnp.zeros((N_SEQ, BLOCK, ATTN_WIDTH), F32)
        osl_ref[...] = jnp.zeros((N_QUARTERS, N_SEQ * PITCH, LANES), F32)
        xm = _rms(meta_ref[...], gmix_ref[...]).astype(BF16)
        zm = jnp.dot(xm, win_ref[:, ATTN_WIDTH:], preferred_element_type=F32)
        km = _pair_norm(zm[:, :KV_WIDTH], gk_ref[...], left).astype(BF16)
        vm = zm[:, KV_WIDTH:2 * KV_WIDTH].astype(BF16)
        lead = jnp.zeros((BLOCK - N_META, KV_WIDTH), BF16)
        for b in range(N_SEQ):
            kk_ref[b, 0:BLOCK - N_META, :] = lead
            vv_ref[b, 0:BLOCK - N_META, :] = lead
            kk_ref[b, BLOCK - N_META:BLOCK, :] = km
            vv_ref[b, BLOCK - N_META:BLOCK, :] = vm
        um = zm[:, 2 * KV_WIDTH:]
        for q in range(N_QUARTERS):
            xm_q = jnp.dot(um[:, q * LANES:(q + 1) * LANES].astype(BF16), wq_ref[q], preferred_element_type=F32)
            lr = jnp.broadcast_to(lr_ref[:, q * Q_STATE:(q + 1) * Q_STATE], (SUBLANES, Q_STATE))
            li = jnp.broadcast_to(li_ref[:, q * Q_STATE:(q + 1) * Q_STATE], (SUBLANES, Q_STATE))
            hr = jnp.zeros((SUBLANES, Q_STATE), F32)
            hi = jnp.zeros((SUBLANES, Q_STATE), F32)
            for t in range(N_META):
                xr = jnp.broadcast_to(xm_q[t:t + 1, 0:Q_STATE], (SUBLANES, Q_STATE))
                xi = jnp.broadcast_to(xm_q[t:t + 1, Q_STATE:], (SUBLANES, Q_STATE))
                hr, hi = _scan_step(lr, li, hr, hi, xr, xi)
            st_ref[:, 2 * q * Q_STATE:(2 * q + 1) * Q_STATE] = hr
            st_ref[:, (2 * q + 1) * Q_STATE:(2 * q + 2) * Q_STATE] = hi

    p1 = {}

    def pre_norm():
        p1["xn"] = _rms(x_ref[...].reshape(ROWS, D_MODEL), gmix_ref[...]).astype(BF16)

    def project_in(key, c0):
        p1[key] = jnp.dot(p1["xn"], win_ref[:, c0:c0 + 2 * LANES], preferred_element_type=F32)

    def finish_q(c):
        zz = p1.pop(("q", c))
        for tt in range(2):
            t = 2 * c + tt
            qn = _pair_norm(zz[:, tt * LANES:(tt + 1) * LANES], gq_ref[...], left) * ATTN_SCALE
            qa = jnp.where(left, qn, 0.0).astype(BF16)
            qb = jnp.where(left, 0.0, qn).astype(BF16)
            for b in range(N_SEQ):
                qs_ref[b, 2 * t * BLOCK:(2 * t + 1) * BLOCK, :] = qa[b * BLOCK:(b + 1) * BLOCK]
                qs_ref[b, (2 * t + 1) * BLOCK:(2 * t + 2) * BLOCK, :] = qb[b * BLOCK:(b + 1) * BLOCK]

    def finish_kv():
        zz = p1.pop("kv")
        kn = _pair_norm(zz[:, :LANES], gk_ref[...], left)
        vn = zz[:, LANES:]
        kw_ref[...] = kn.reshape(N_SEQ, BLOCK, KV_WIDTH)
        vw_ref[...] = vn.reshape(N_SEQ, BLOCK, KV_WIDTH)
        kk_ref[:, BLOCK:, :] = kn.astype(BF16).reshape(N_SEQ, BLOCK, KV_WIDTH)
        vv_ref[:, BLOCK:, :] = vn.astype(BF16).reshape(N_SEQ, BLOCK, KV_WIDTH)

    def finish_u(c):
        zz = p1.pop(("u", c))
        for tt in range(2):
            for b in range(N_SEQ):
                usl_ref[2 * c + tt, b * PITCH:b * PITCH + BLOCK, :] = zz[b * BLOCK:(b + 1) * BLOCK, tt * LANES:(tt + 1) * LANES]

    dot_kv = functools.partial(project_in, "kv", ATTN_WIDTH)
    dot_q = [functools.partial(project_in, ("q", c), c * 2 * LANES) for c in range(2)]
    dot_u = [functools.partial(project_in, ("u", c), ATTN_WIDTH + 2 * KV_WIDTH + c * 2 * LANES) for c in range(2)]
    post_q = [functools.partial(finish_q, c) for c in range(2)]
    post_u = [functools.partial(finish_u, c) for c in range(2)]

    r = lax.broadcasted_iota(jnp.int32, (BLOCK, 2 * BLOCK), 0)
    c = lax.broadcasted_iota(jnp.int32, (BLOCK, 2 * BLOCK), 1)
    valid = (c >= r) & (c <= r + WINDOW) & ((i > 0) | (c >= BLOCK - N_META))
    bias = jnp.where(valid, 0.0, -jnp.inf)

    def gather_u(n):
        t0 = n * CHUNK_T
        return jnp.concatenate(
            [jnp.concatenate([usl_ref[j, pl.ds(t0 + tl, SUBLANES, stride=PITCH), :] for j in range(N_QUARTERS)], axis=1)
             for tl in range(CHUNK_T)], axis=0)

    def feed(n, par):
        env = {}

        def scores():
            s_ref[par] = lax.dot_general(qs_ref[n], kk_ref[n], (((1,), (1,)), ((), ())), preferred_element_type=F32)

        def gather():
            env["u"] = gather_u(n)

        def scan_inputs(q):
            xs_ref[par, :, 2 * q * Q_STATE:(2 * q + 2) * Q_STATE] = jnp.dot(
                env["u"][:, q * LANES:(q + 1) * LANES].astype(BF16), wq_ref[q], preferred_element_type=F32)

        return [scores, gather] + [functools.partial(scan_inputs, q) for q in range(N_QUARTERS)]

    def mid(n, par):
        env = {}

        def row_max(g):
            s = s_ref[par, g * BLOCK:(g + 1) * BLOCK, :] + bias
            env[g] = (s, jnp.maximum(jnp.max(s, axis=-1, keepdims=True), sink_ref[g // 2 + Q_PER_KV * (g % 2)]))

        def exponent(g):
            s, m = env[g]
            p = jnp.exp(s - m)
            sink = sink_ref[g // 2 + Q_PER_KV * (g % 2)]
            env[g] = (p, 1.0 / (jnp.sum(p, axis=-1, keepdims=True) + jnp.exp(sink - m)))

        def normalise(g):
            p, inv = env.pop(g)
            p_ref[par, g * BLOCK:(g + 1) * BLOCK, :] = (p * inv).astype(BF16)

        softmax = [functools.partial(f, g) for g in range(2 * Q_TILES) for f in (row_max, exponent, normalise)]

        def load_state():
            for q in range(N_QUARTERS):
                env["lam", q] = (jnp.broadcast_to(lr_ref[:, q * Q_STATE:(q + 1) * Q_STATE], (SUBLANES, Q_STATE)),
                                 jnp.broadcast_to(li_ref[:, q * Q_STATE:(q + 1) * Q_STATE], (SUBLANES, Q_STATE)))
                env["h", q] = (st_ref[:, 2 * q * Q_STATE:(2 * q + 1) * Q_STATE],
                               st_ref[:, (2 * q + 1) * Q_STATE:(2 * q + 2) * Q_STATE])

        def scan_pair(tp, q):
            lr, li = env["lam", q]
            hr, hi = env["h", q]
            re0, im0 = 2 * q * Q_STATE, (2 * q + 1) * Q_STATE
            pair_r, pair_i = [], []
            for tl in (2 * tp, 2 * tp + 1):
                xr = xs_ref[par, tl * SUBLANES:(tl + 1) * SUBLANES, re0:re0 + Q_STATE]
                xi = xs_ref[par, tl * SUBLANES:(tl + 1) * SUBLANES, im0:im0 + Q_STATE]
                hr, hi = _scan_step(lr, li, hr, hi, xr, xi)
                pair_r.append(hr)
                pair_i.append(hi)
            env["h", q] = (hr, hi)
            rows = slice(2 * tp * SUBLANES, (2 * tp + 2) * SUBLANES)
            hb_ref[par, rows, re0:re0 + Q_STATE] = jnp.concatenate(pair_r, axis=0).astype(BF16)
            hb_ref[par, rows, im0:im0 + Q_STATE] = jnp.concatenate(pair_i, axis=0).astype(BF16)

        def store_state():
            for q in range(N_QUARTERS):
                hr, hi = env["h", q]
                st_ref[:, 2 * q * Q_STATE:(2 * q + 1) * Q_STATE] = hr
                st_ref[:, (2 * q + 1) * Q_STATE:(2 * q + 2) * Q_STATE] = hi

        scan = ([load_state] + [functools.partial(scan_pair, tp, q) for tp in range(CHUNK_T // 2) for q in range(N_QUARTERS)]
                + [store_state])
        return softmax, scan

    def tail(n, par):
        env = {}

        def attn_out():
            o_all = jnp.dot(p_ref[par], vv_ref[n], preferred_element_type=F32)
            for t in range(Q_TILES):
                oatt_ref[n, :, t * LANES:(t + 1) * LANES] = jnp.where(
                    left, o_all[2 * t * BLOCK:(2 * t + 1) * BLOCK], o_all[(2 * t + 1) * BLOCK:(2 * t + 2) * BLOCK])

        def gather():
            env["u"] = gather_u(n)

        def readout(q):
            env["y", q] = jnp.dot(hb_ref[par, :, 2 * q * Q_STATE:(2 * q + 2) * Q_STATE], cq_ref[q],
                                  preferred_element_type=F32)

        def activate(q):
            cols = slice(q * LANES, (q + 1) * LANES)
            env["g", q] = jax.nn.gelu(env.pop(("y", q)) + d_ref[:, cols] * env["u"][:, cols])

        def gate(q):
            cols = slice(q * LANES, (q + 1) * LANES)
            env["gate", q] = jnp.dot(env["g", q].astype(BF16), wglu_ref[q], preferred_element_type=F32) + bglu_ref[:, cols]

        def emit(q):
            o = env.pop(("g", q)) * jax.nn.sigmoid(env.pop(("gate", q)))
            t0 = n * CHUNK_T
            for tl in range(CHUNK_T):
                osl_ref[q, pl.ds(t0 + tl, SUBLANES, stride=PITCH), :] = o[tl * SUBLANES:(tl + 1) * SUBLANES]

        per_quarter = lambda f: [functools.partial(f, q) for q in range(N_QUARTERS)]
        return [attn_out, gather] + per_quarter(readout) + per_quarter(activate) + per_quarter(gate) + per_quarter(emit)

    def run(*stages):
        for step in _interleave(*stages):
            step()

    def steady(k, carry):
        n = 2 * k + 1
        run(project(k) + tail(n - 1, 0), *mid(n, 1), feed(n + 1, 0))
        run(feed(n + 2, 1), *mid(n + 1, 0), tail(n, 1))
        return carry

    norms = _output_norm_steps(oatt_ref, osl_ref, ga_ref, gs_ref, mix_ref)
    feed0, feed1 = feed(0, 0), feed(1, 1)
    softmax0, scan0 = mid(0, 0)
    half = len(softmax0) // 2
    run([pre_norm, dot_kv, dot_q[0], finish_kv, dot_q[1], post_q[0], dot_u[0], post_q[1]], norms)
    run(feed0[:1])
    run([dot_u[1]] + feed1[:1] + post_u, softmax0[:half])
    run(feed0[1:], softmax0[half:])
    run(feed1[1:], scan0)
    n_steady = (N_SEQ - 2) // 2
    assert n_steady == OUT_CHUNKS - 1
    lax.fori_loop(0, n_steady, steady, 0)
    run(*mid(N_SEQ - 1, 1), tail(N_SEQ - 2, 0), project(OUT_CHUNKS - 1))
    run(tail(N_SEQ - 1, 1))
    kk_ref[:, 0:BLOCK, :] = kk_ref[:, BLOCK:, :]
    vv_ref[:, 0:BLOCK, :] = vv_ref[:, BLOCK:, :]

    hr_ref[...] = jnp.concatenate([st_ref[:, 2 * q * Q_STATE:(2 * q + 1) * Q_STATE] for q in range(N_QUARTERS)], axis=1)
    hi_ref[...] = jnp.concatenate([st_ref[:, (2 * q + 1) * Q_STATE:(2 * q + 2) * Q_STATE] for q in range(N_QUARTERS)], axis=1)


def _mixer(x_prompt, meta_tokens, sinks, gmix, win, gq2, gk2, sp, ga, gs, wout):
    nseq, seq, _ = x_prompt.shape
    assert nseq == N_SEQ and seq % BLOCK == 0
    n_blocks = seq // BLOCK
    st = pl.BlockSpec((N_SEQ, STATE_COLS), lambda i: (0, 0))
    kvw = pl.BlockSpec((N_SEQ, BLOCK, KV_WIDTH), lambda i: (0, 0, 0))
    return pl.pallas_call(
        _mixer_kernel,
        grid=(n_blocks + 1,),
        in_specs=[pl.BlockSpec(memory_space=pltpu.SMEM),
                  pl.BlockSpec((N_SEQ, BLOCK, D_MODEL), lambda i: (0, jnp.minimum(i, n_blocks - 1), 0)),
                  _const_spec((N_META, D_MODEL)),
                  _const_spec((1, D_MODEL)), _const_spec((D_MODEL, IN_COLS)), _const_spec((1, LANES)),
                  _const_spec((1, LANES)),
                  _const_spec((N_QUARTERS, LANES, 2 * Q_STATE)), _const_spec((1, STATE_COLS)),
                  _const_spec((1, STATE_COLS)), _const_spec((N_QUARTERS, 2 * Q_STATE, LANES)),
                  _const_spec((1, SSM_WIDTH)), _const_spec((N_QUARTERS, LANES, LANES)), _const_spec((1, SSM_WIDTH)),
                  _const_spec((1, ATTN_WIDTH)), _const_spec((1, SSM_WIDTH)),
                  _const_spec((OUT_CHUNKS, D_MODEL, OUT_CHUNK))],
        out_specs=[pl.BlockSpec((OUT_CHUNKS, N_SEQ, BLOCK, OUT_CHUNK), lambda i: (0, 0, jnp.maximum(i - 1, 0), 0)),
                   kvw, kvw, st, st],
        out_shape=[jax.ShapeDtypeStruct((OUT_CHUNKS, N_SEQ, seq, OUT_CHUNK), F32),
                   jax.ShapeDtypeStruct((N_SEQ, BLOCK, KV_WIDTH), F32), jax.ShapeDtypeStruct((N_SEQ, BLOCK, KV_WIDTH), F32),
                   jax.ShapeDtypeStruct((N_SEQ, STATE_COLS), F32), jax.ShapeDtypeStruct((N_SEQ, STATE_COLS), F32)],
        scratch_shapes=[pltpu.VMEM((N_SEQ, 2 * Q_TILES * BLOCK, LANES), BF16),
                        pltpu.VMEM((N_SEQ, 2 * BLOCK, KV_WIDTH), BF16),
                        pltpu.VMEM((N_SEQ, 2 * BLOCK, KV_WIDTH), BF16),
                        pltpu.VMEM((2, 2 * Q_TILES * BLOCK, 2 * BLOCK), F32),
                        pltpu.VMEM((2, 2 * Q_TILES * BLOCK, 2 * BLOCK), BF16),
                        pltpu.VMEM((N_SEQ, BLOCK, ATTN_WIDTH), F32),
                        pltpu.VMEM((N_QUARTERS, N_SEQ * PITCH, LANES), F32),
                        pltpu.VMEM((2, CHUNK_T * N_SEQ, 2 * STATE_COLS), F32),
                        pltpu.VMEM((2, CHUNK_T * N_SEQ, 2 * STATE_COLS), BF16),
                        pltpu.VMEM((N_SEQ, 2 * STATE_COLS), F32),
                        pltpu.VMEM((N_QUARTERS, N_SEQ * PITCH, LANES), F32),
                        pltpu.VMEM((ROWS, D_MODEL), BF16)],
        compiler_params=pltpu.CompilerParams(dimension_semantics=("arbitrary",), vmem_limit_bytes=VMEM_LIMIT),
        name="mixer",
    )(sinks, x_prompt, meta_tokens, gmix, win, gq2, gk2, sp["wq"], sp["lr"], sp["li"], sp["cq"], sp["d"],
      sp["wgluq"], sp["bglu"], ga, gs, wout)


def kernel(x_prompt, x_sample, cache_k_win, cache_v_win, state_ssm_re, state_ssm_im, meta_tokens, g_mix, w_in, g_q,
           g_k, sinks, ssm_a_re, ssm_a_im, ssm_log_dt, ssm_b_re, ssm_b_im, ssm_c_re, ssm_c_im, ssm_d, ssm_w_glu,
           ssm_b_glu, g_att_out, g_ssm_out, w_out, g_ffn, w_gate, w_up, w_down):
    bp, seq, _ = x_prompt.shape
    db, dseq, _ = x_sample.shape
    li = 0
    gmix = g_mix[li].reshape(1, D_MODEL)
    win = w_in[li].astype(BF16)
    gq2 = jnp.tile(g_q[li], 2).reshape(1, LANES)
    gk2 = jnp.tile(g_k[li], 2).reshape(1, LANES)
    sk = sinks[li]
    sp = _ssm_params(ssm_a_re[li], ssm_a_im[li], ssm_log_dt[li], ssm_b_re[li], ssm_b_im[li], ssm_c_re[li],
                     ssm_c_im[li], ssm_d[li], ssm_w_glu[li], ssm_b_glu[li])
    ga = g_att_out[li].reshape(1, ATTN_WIDTH)
    gs = g_ssm_out[li].reshape(1, SSM_WIDTH)
    wout = w_out[li].astype(BF16)
    ffn_w = (g_ffn[li].reshape(1, D_MODEL), w_gate[li].astype(BF16), w_up[li].astype(BF16), w_down[li].astype(BF16))
    regroup = lambda a, axis: jnp.swapaxes(
        a.reshape(a.shape[:axis] + (2, Q_PER_KV, HEAD_DIM) + a.shape[axis + 1:]), axis, axis + 1).reshape(a.shape)
    order = [j * Q_PER_KV + t for t in range(Q_PER_KV) for j in range(N_KV_HEADS)]
    head = lambda h: slice(h * HEAD_DIM, (h + 1) * HEAD_DIM)
    win = jnp.concatenate([win[:, head(h)] for h in order] + [win[:, ATTN_WIDTH:]], axis=1)
    wout = jnp.concatenate([wout[head(h)] for h in order] + [wout[ATTN_WIDTH:]], axis=0)
    ga = regroup(ga, 1)

    wout_chunks = jnp.stack([wout[:, c * OUT_CHUNK:(c + 1) * OUT_CHUNK] for c in range(OUT_CHUNKS)])
    hproj, kw_p, vw_p, hp_r, hp_i = _mixer(x_prompt, meta_tokens, sk, gmix, win, gq2, gk2, sp, ga, gs, wout_chunks)
    y_prompt = _ffn_call(x_prompt.reshape(bp * seq, D_MODEL), hproj.reshape(OUT_CHUNKS, bp * seq, OUT_CHUNK), *ffn_w,
                         tm=512).reshape(bp, seq, D_MODEL)

    n_s = db * dseq
    xs_rows = x_sample.reshape(n_s, D_MODEL)
    q_s, kt_s, vt_s, u_s = _front(xs_rows, gmix, win, gq2, gk2, tm=512)
    to_t = lambda a: jnp.transpose(a, (0, 2, 3, 1)).reshape(db, KV_WIDTH, WINDOW)
    from_t = lambda a: jnp.transpose(a.reshape(db, N_KV_HEADS, HEAD_DIM, WINDOW), (0, 3, 1, 2))[None]
    oa_s, kwt_s, vwt_s = _attn_sample(q_s.reshape(db, dseq, ATTN_WIDTH), kt_s, vt_s, to_t(cache_k_win[li]),
                                      to_t(cache_v_win[li]), sk)
    os_s, hs_r, hs_i = _ssm_sample(u_s, state_ssm_re[li].reshape(db, STATE_COLS),
                                   state_ssm_im[li].reshape(db, STATE_COLS), sp, nseq=db, t_steps=dseq)
    y_sample = _back(xs_rows, oa_s.reshape(n_s, ATTN_WIDTH), os_s, ga, gs, wout, *ffn_w, tm=512).reshape(db, dseq, D_MODEL)

    kv5 = lambda a, n: a.reshape(1, n, WINDOW, N_KV_HEADS, HEAD_DIM)
    st4 = lambda a, n: a.reshape(1, n, N_SSM_GROUPS, SSM_STATE)
    return (y_prompt, y_sample, kv5(kw_p, bp), kv5(vw_p, bp), st4(hp_r, bp), st4(hp_i, bp),
            from_t(kwt_s), from_t(vwt_s), st4(hs_r, db), st4(hs_i, db))
```
